```python
import jax, jax.numpy as jnp
from jax import lax
import numpy as np

D_MODEL = 1024
BATCH = 2
SEQ = 8192
DEPTH = 4

HEAD_DIM = 64
SB_HEADS = D_MODEL // 256
FOX_HEADS = D_MODEL // 128
SGU_GROUPS = D_MODEL // 256
SGU_DIM = 64
SGU_CHUNK = 128
Q_BLOCK = 128
D_FF = 4 * D_MODEL
EPS = 1e-6

SB_W = SB_HEADS * HEAD_DIM
FOX_W = FOX_HEADS * HEAD_DIM
SGU_W = SGU_GROUPS * SGU_DIM
MIX_W = SB_W + FOX_W + SGU_W
IN_SIZES = [SB_W, SB_W, SB_W, FOX_W, FOX_W, FOX_W, FOX_HEADS, SGU_W, SGU_W]
IN_W = sum(IN_SIZES)
IN_SPLITS = [int(s) for s in np.cumsum(IN_SIZES)[:-1]]

kernel_name = "hybrid_sb_fox_sgu_adaln"


def rmsnorm(x, g):
    xf = x.astype(jnp.float32)
    y = xf * lax.rsqrt(jnp.mean(xf * xf, axis=-1, keepdims=True) + EPS)
    return (y * g).astype(x.dtype)


def modulate(h, shift, scale):
    return h * (1.0 + scale[:, None, :]) + shift[:, None, :]


def split_heads(t, n_heads):
    b, s, _ = t.shape
    return t.reshape(b, s, n_heads, -1).transpose(0, 2, 1, 3)


def merge_heads(t):
    b, h, s, d = t.shape
    return t.transpose(0, 2, 1, 3).reshape(b, s, h * d)


def to_blocks(t):
    b, h, s, d = t.shape
    return jnp.moveaxis(t.reshape(b, h, s // Q_BLOCK, Q_BLOCK, d), 2, 0)


def from_blocks(o):
    n, b, h, q, d = o.shape
    return jnp.moveaxis(o, 0, 2).reshape(b, h, n * q, d)


def stick_breaking_attention(q, k, v):
    s_len = k.shape[2]
    scale = HEAD_DIM ** -0.5
    s_pos = jnp.arange(s_len)

    def block(args):
        qb, i = args
        z = jnp.einsum('bhqd,bhsd->bhqs', qb, k).astype(jnp.float32) * scale
        t_pos = i * Q_BLOCK + jnp.arange(Q_BLOCK)
        mask = s_pos[None, :] < t_pos[:, None]
        log_1mb = jnp.where(mask, jax.nn.log_sigmoid(-z), 0.0)
        between = lax.cumsum(log_1mb, axis=3, reverse=True) - log_1mb
        a = jnp.where(mask, jnp.exp(jax.nn.log_sigmoid(z) + between), 0.0)
        return jnp.einsum('bhqs,bhsd->bhqd', a.astype(v.dtype), v)

    o = lax.map(block, (to_blocks(q), jnp.arange(s_len // Q_BLOCK)))
    return from_blocks(o)


def forgetting_attention(q, k, v, log_f):
    b, h, s_len, _ = k.shape
    n_blk = s_len // Q_BLOCK
    scale = HEAD_DIM ** -0.5
    cum_f = jnp.cumsum(log_f, axis=-1)
    cum_f_blocks = jnp.moveaxis(cum_f.reshape(b, h, n_blk, Q_BLOCK), 2, 0)
    s_pos = jnp.arange(s_len)

    def block(args):
        qb, fq, i = args
        z = jnp.einsum('bhqd,bhsd->bhqs', qb, k).astype(jnp.float32) * scale
        z = z + fq[..., :, None] - cum_f[:, :, None, :]
        t_pos = i * Q_BLOCK + jnp.arange(Q_BLOCK)
        mask = s_pos[None, :] <= t_pos[:, None]
        p = jax.nn.softmax(jnp.where(mask, z, -jnp.inf), axis=-1)
        return jnp.einsum('bhqs,bhsd->bhqd', p.astype(v.dtype), v)

    o = lax.map(block, (to_blocks(q), cum_f_blocks, jnp.arange(n_blk)))
    return from_blocks(o)


def spatial_gating(u, vv, norm_g, w_s, b_s):
    b, s_len, _ = u.shape
    n_chunk = s_len // SGU_CHUNK
    vv = rmsnorm(vv.reshape(b, s_len, SGU_GROUPS, SGU_DIM), norm_g)
    vv = vv.reshape(b, n_chunk, SGU_CHUNK, SGU_GROUPS, SGU_DIM)
    w = jnp.tril(w_s)
    mixed = jnp.einsum('gts,bnsgd->bntgd', w, vv) + b_s.T[None, None, :, :, None]
    return u * mixed.reshape(b, s_len, SGU_W)


def setup_inputs(seed: int = 0) -> dict:
    key = jax.random.key(seed)
    ks = jax.random.split(key, 16)
    nrm = jax.random.normal
    f32 = jnp.float32
    return {
        "x": nrm(ks[0], (BATCH, SEQ, D_MODEL), f32),
        "c": nrm(ks[1], (BATCH, D_MODEL), f32),
        "ada_w": nrm(ks[2], (DEPTH, D_MODEL, 6 * D_MODEL), f32) * (0.5 * D_MODEL ** -0.5),
        "ada_b": nrm(ks[3], (DEPTH, 6 * D_MODEL), f32) * 0.02,
        "norm1_g": 1.0 + 0.01 * nrm(ks[4], (DEPTH, D_MODEL), f32),
        "norm2_g": 1.0 + 0.01 * nrm(ks[5], (DEPTH, D_MODEL), f32),
        "w_in": nrm(ks[6], (DEPTH, D_MODEL, IN_W), f32) * D_MODEL ** -0.5,
        "b_forget": 2.0 + 0.5 * nrm(ks[7], (DEPTH, FOX_HEADS), f32),
        "q_norm_g": 1.0 + 0.01 * nrm(ks[8], (DEPTH, HEAD_DIM), f32),
        "k_norm_g": 1.0 + 0.01 * nrm(ks[9], (DEPTH, HEAD_DIM), f32),
        "sgu_norm_g": 1.0 + 0.01 * nrm(ks[10], (DEPTH, SGU_GROUPS, SGU_DIM), f32),
        "sgu_w": nrm(ks[11], (DEPTH, SGU_GROUPS, SGU_CHUNK, SGU_CHUNK), f32) * SGU_CHUNK ** -0.5,
        "sgu_b": 1.0 + 0.1 * nrm(ks[12], (DEPTH, SGU_GROUPS, SGU_CHUNK), f32),
        "w_out": nrm(ks[13], (DEPTH, MIX_W, D_MODEL), f32) * MIX_W ** -0.5,
        "mlp_w1": nrm(ks[14], (DEPTH, D_MODEL, D_FF), f32) * D_MODEL ** -0.5,
        "mlp_w2": nrm(ks[15], (DEPTH, D_FF, D_MODEL), f32) * D_FF ** -0.5,
    }


def reference(x, c, ada_w, ada_b, norm1_g, norm2_g, w_in, b_forget, q_norm_g, k_norm_g,
              sgu_norm_g, sgu_w, sgu_b, w_out, mlp_w1, mlp_w2):
    cond = jax.nn.silu(c)
    for l in range(DEPTH):
        mod = cond @ ada_w[l] + ada_b[l]
        sh1, sc1, g1, sh2, sc2, g2 = jnp.split(mod, 6, axis=-1)

        h = modulate(rmsnorm(x, norm1_g[l]), sh1, sc1)
        proj = h @ w_in[l]
        qa, ka, va, qb, kb, vb, fl, uc, vc = jnp.split(proj, IN_SPLITS, axis=-1)

        o_sb = stick_breaking_attention(split_heads(qa, SB_HEADS), split_heads(ka, SB_HEADS),
                                        split_heads(va, SB_HEADS))

        q_fox = rmsnorm(split_heads(qb, FOX_HEADS), q_norm_g[l])
        k_fox = rmsnorm(split_heads(kb, FOX_HEADS), k_norm_g[l])
        log_f = jax.nn.log_sigmoid(fl.astype(jnp.float32) + b_forget[l].astype(jnp.float32))
        o_fox = forgetting_attention(q_fox, k_fox, split_heads(vb, FOX_HEADS),
                                     log_f.transpose(0, 2, 1))

        o_sgu = spatial_gating(jax.nn.gelu(uc), jax.nn.gelu(vc), sgu_norm_g[l], sgu_w[l], sgu_b[l])

        mixed = jnp.concatenate([merge_heads(o_sb), merge_heads(o_fox), o_sgu], axis=-1)
        x = x + g1[:, None, :] * (mixed @ w_out[l])

        h = modulate(rmsnorm(x, norm2_g[l]), sh2, sc2)
        x = x + g2[:, None, :] * (jnp.square(jax.nn.relu(h @ mlp_w1[l])) @ mlp_w2[l])
    return x
```

```python
import functools
import math

import jax
import jax.numpy as jnp
from jax import lax
from jax.experimental import pallas as pl
from jax.experimental.pallas import tpu as pltpu

HEAD_DIM = 64
LANES = 128
EPS = 1e-6
NEG_BIG = -1e30
VMEM_LIMIT = 56 * 1024 * 1024

F32 = jnp.float32
BF16 = jnp.bfloat16


def _dot(a, b):
    return jnp.dot(a, b, preferred_element_type=F32)


def _dot_nt(a, b):
    return lax.dot_general(a, b, (((1,), (1,)), ((), ())), preferred_element_type=F32)


def _split2(x):
    hi = x.astype(BF16)
    lo = (x - hi.astype(F32)).astype(BF16)
    return hi, lo


def _split3(x):
    hi = x.astype(BF16)
    r = x - hi.astype(F32)
    mid = r.astype(BF16)
    lo = (r - mid.astype(F32)).astype(BF16)
    return hi, mid, lo


def _group_mean_matrix():
    r = lax.broadcasted_iota(jnp.int32, (LANES, LANES), 0) >> 6
    c = lax.broadcasted_iota(jnp.int32, (LANES, LANES), 1) >> 6
    return jnp.where(r == c, 1.0 / HEAD_DIM, 0.0).astype(BF16)


def _head_rmsnorm(t, gmat, gain):
    hi, lo = _split2(t * t)
    ms = _dot(hi, gmat) + _dot(lo, gmat)
    return t * lax.rsqrt(ms + EPS) * gain


def _gelu_tanh(x):
    c = math.sqrt(2.0 / math.pi)
    return x * (0.5 * (1.0 + jnp.tanh(c * (x + 0.044715 * (x * x * x)))))


def _log_sigmoid(x):
    return jnp.minimum(x, 0.0) - jnp.log(1.0 + jnp.exp(-jnp.abs(x)))


def _mod_kernel(ct_ref, w_ref, b_ref, o_ref):
    ct = ct_ref[...]
    cond = ct * (1.0 / (1.0 + jnp.exp(-ct)))
    w = w_ref[0]
    rows = [jnp.sum(cond[:, b:b + 1] * w, axis=0, keepdims=True) for b in range(ct.shape[1])]
    o_ref[0] = jnp.concatenate(rows, axis=0) + b_ref[0]


def _modulation(c, ada_w, ada_b):
    depth, d, n = ada_w.shape
    bsz = c.shape[0]
    tn = 1536
    return pl.pallas_call(
        _mod_kernel,
        grid=(depth, n // tn),
        in_specs=[
            pl.BlockSpec((d, bsz), lambda l, j: (0, 0)),
            pl.BlockSpec((1, d, tn), lambda l, j: (l, 0, j)),
            pl.BlockSpec((1, 1, tn), lambda l, j: (l, 0, j)),
        ],
        out_specs=pl.BlockSpec((1, bsz, tn), lambda l, j: (l, 0, j)),
        out_shape=jax.ShapeDtypeStruct((depth, bsz, n), F32),
        compiler_params=pltpu.CompilerParams(
            dimension_semantics=("arbitrary", "arbitrary"), vmem_limit_bytes=VMEM_LIMIT),
        name="adaln_mod",
    )(c.T, ada_w, ada_b.reshape(depth, 1, n))


def _in_kernel(x_ref, mod_ref, g1_ref, w_ref, wfh_ref, wfl_ref, bf_ref, qg_ref, kg_ref,
               sg_ref, sw_ref, sb_ref,
               qa_ref, ka_ref, va_ref, qb_ref, kb_ref, vb_ref, f_ref, og_ref,
               carry_ref, *, sb_w, fox_w, sgu_w, chunk):
    tm = x_ref.shape[1]
    x = x_ref[0]
    ms = jnp.mean(x * x, axis=-1, keepdims=True)
    h = x * lax.rsqrt(ms + EPS) * g1_ref[...]
    h = h * (1.0 + mod_ref[0, 1:2, :]) + mod_ref[0, 0:1, :]
    hb = h.astype(BF16)
    scale = HEAD_DIM ** -0.5

    o = 0
    qa_ref[0] = (_dot(hb, w_ref[:, o:o + sb_w]) * scale).astype(BF16); o += sb_w
    ka_ref[0] = _dot(hb, w_ref[:, o:o + sb_w]).astype(BF16); o += sb_w
    va_ref[0] = _dot(hb, w_ref[:, o:o + sb_w]).astype(BF16); o += sb_w

    gmat = _group_mean_matrix()
    for j in range(fox_w // LANES):
        t = _dot(hb, w_ref[:, o + j * LANES:o + (j + 1) * LANES])
        qb_ref[0, :, j * LANES:(j + 1) * LANES] = (
            _head_rmsnorm(t, gmat, qg_ref[...]) * scale).astype(BF16)
    o += fox_w
    for j in range(fox_w // LANES):
        t = _dot(hb, w_ref[:, o + j * LANES:o + (j + 1) * LANES])
        kb_ref[0, :, j * LANES:(j + 1) * LANES] = _head_rmsnorm(t, gmat, kg_ref[...]).astype(BF16)
    o += fox_w
    vb_ref[0] = _dot(hb, w_ref[:, o:o + fox_w]).astype(BF16); o += fox_w

    @pl.when(pl.program_id(1) == 0)
    def _():
        carry_ref[...] = jnp.zeros_like(carry_ref)

    h_hi, h_lo = _split2(h)
    fl = _dot(h_hi, wfh_ref[...]) + _dot(h_lo, wfh_ref[...]) + _dot(h_hi, wfl_ref[...])
    logf = _log_sigmoid(fl + bf_ref[...])
    r = lax.broadcasted_iota(jnp.int32, (tm, tm), 0)
    cc = lax.broadcasted_iota(jnp.int32, (tm, tm), 1)
    ltri = jnp.where(r >= cc, 1.0, 0.0).astype(BF16)
    l1, l2, l3 = _split3(logf)
    cum = _dot(ltri, l1) + _dot(ltri, l2) + _dot(ltri, l3) + carry_ref[0:1, :]
    carry_ref[0:1, :] = cum[tm - 1:tm, :]
    f_ref[0] = cum.T[0:f_ref.shape[1], :]

    gu = _gelu_tanh(_dot(hb, w_ref[:, o:o + sgu_w])); o += sgu_w
    gv = _gelu_tanh(_dot(hb, w_ref[:, o:o + sgu_w])); o += sgu_w
    lane = lax.broadcasted_iota(jnp.int32, (chunk, LANES), 1)
    rr = lax.broadcasted_iota(jnp.int32, (chunk, chunk), 0)
    cs = lax.broadcasted_iota(jnp.int32, (chunk, chunk), 1)
    wt = [jnp.where(rr >= cs, sw_ref[g], 0.0).astype(BF16) for g in range(sw_ref.shape[0])]
    for p in range(sgu_w // LANES):
        vn = _head_rmsnorm(gv[:, p * LANES:(p + 1) * LANES], gmat,
                           sg_ref[:, p * LANES:(p + 1) * LANES]).astype(BF16)
        for ci in range(tm // chunk):
            vblk = vn[ci * chunk:(ci + 1) * chunk, :]
            mixed = jnp.where(lane < HEAD_DIM, _dot(wt[2 * p], vblk), _dot(wt[2 * p + 1], vblk))
            mixed = mixed + sb_ref[:, p * LANES:(p + 1) * LANES]
            og_ref[0, ci * chunk:(ci + 1) * chunk, p * LANES:(p + 1) * LANES] = (
                gu[ci * chunk:(ci + 1) * chunk, p * LANES:(p + 1) * LANES] * mixed).astype(BF16)


def _in_proj(x, mod_l, g1, w, wf_hi, wf_lo, bf, qg, kg, sg, sw, sb, *, tm, sb_w, fox_w, sgu_w,
             fox_heads):
    bsz, s, d = x.shape
    chunk = sw.shape[-1]
    wn = w.shape[1]
    const2 = lambda b, i: (0, 0)
    tok = lambda b, i: (b, i, 0)
    kern = functools.partial(_in_kernel, sb_w=sb_w, fox_w=fox_w, sgu_w=sgu_w, chunk=chunk)
    out_shape = (
        [jax.ShapeDtypeStruct((bsz, s, sb_w), BF16)] * 3
        + [jax.ShapeDtypeStruct((bsz, s, fox_w), BF16)] * 3
        + [jax.ShapeDtypeStruct((bsz, fox_heads, s), F32),
           jax.ShapeDtypeStruct((bsz, s, sgu_w), BF16)])
    out_specs = (
        [pl.BlockSpec((1, tm, sb_w), tok)] * 3 + [pl.BlockSpec((1, tm, fox_w), tok)] * 3
        + [pl.BlockSpec((1, fox_heads, tm), lambda b, i: (b, 0, i)),
           pl.BlockSpec((1, tm, sgu_w), tok)])
    return pl.pallas_call(
        kern,
        grid=(bsz, s // tm),
        in_specs=[
            pl.BlockSpec((1, tm, d), tok),
            pl.BlockSpec((1, 6, d), lambda b, i: (b, 0, 0)),
            pl.BlockSpec((1, d), const2),
            pl.BlockSpec((d, wn), const2),
            pl.BlockSpec((d, LANES), const2),
            pl.BlockSpec((d, LANES), const2),
            pl.BlockSpec((1, LANES), const2),
            pl.BlockSpec((1, LANES), const2),
            pl.BlockSpec((1, LANES), const2),
            pl.BlockSpec((1, sgu_w), const2),
            pl.BlockSpec(sw.shape, lambda b, i: (0, 0, 0)),
            pl.BlockSpec((chunk, sgu_w), const2),
        ],
        out_specs=out_specs,
        out_shape=out_shape,
        scratch_shapes=[pltpu.VMEM((8, LANES), F32)],
        compiler_params=pltpu.CompilerParams(
            dimension_semantics=("arbitrary", "arbitrary"), vmem_limit_bytes=VMEM_LIMIT),
        name="in_proj",
    )(x, mod_l, g1, w, wf_hi, wf_lo, bf, qg, kg, sg, sw, sb)


def _sb_kernel(q_ref, k_ref, v_ref, o_ref, acc_ref, c_ref, *, tk):
    tq = q_ref.shape[1]
    qi = pl.program_id(2)
    q = q_ref[0]
    lane = lax.broadcasted_iota(jnp.int32, (1, LANES), 1)
    qh = (jnp.where(lane < HEAD_DIM, q, 0).astype(BF16), jnp.where(lane >= HEAD_DIM, q, 0).astype(BF16))
    row = lax.broadcasted_iota(jnp.int32, (tq, tk), 0)
    col = lax.broadcasted_iota(jnp.int32, (tq, tk), 1)
    causal = col < row
    jr = lax.broadcasted_iota(jnp.int32, (tk, tk), 0)
    js = lax.broadcasted_iota(jnp.int32, (tk, tk), 1)
    suffix = jnp.where(jr >= js, 1.0, 0.0).astype(BF16)

    def step(j, diag):
        start = pl.multiple_of(j * tk, tk)
        k = k_ref[0, pl.ds(start, tk), :]
        v = v_ref[0, pl.ds(start, tk), :]
        for h in range(2):
            z = _dot_nt(qh[h], k)
            l1mb = -(jnp.maximum(z, 0.0) + jnp.log(1.0 + jnp.exp(-jnp.abs(z))))
            if diag:
                l1mb = jnp.where(causal, l1mb, 0.0)
            hi, lo = _split2(l1mb)
            sfx = _dot(hi, suffix) + _dot(lo, suffix)
            if diag:
                a = jnp.where(causal, jnp.exp(z + sfx), 0.0)
                c_ref[h] = jnp.broadcast_to(sfx[:, 0:1], (tq, LANES))
                acc_ref[h] = _dot(a.astype(BF16), v)
            else:
                c = c_ref[h]
                a = jnp.exp(z + sfx + jnp.concatenate([c] * (tk // LANES), axis=1))
                c_ref[h] = c + jnp.broadcast_to(sfx[:, 0:1], (tq, LANES))
                acc_ref[h] += _dot(a.astype(BF16), v)

    step(qi, True)

    def body(i, carry):
        step(qi - 1 - i, False)
        return carry

    lax.fori_loop(0, qi, body, 0)
    o_ref[0] = jnp.where(lane < HEAD_DIM, acc_ref[0], acc_ref[1]).astype(o_ref.dtype)


def _sb_attention(q, k, v, *, tq):
    bsz, s, w = q.shape
    kern = functools.partial(_sb_kernel, tk=tq)
    return pl.pallas_call(
        kern,
        grid=(bsz, w // LANES, s // tq),
        in_specs=[
            pl.BlockSpec((1, tq, LANES), lambda b, p, i: (b, i, p)),
            pl.BlockSpec((1, s, LANES), lambda b, p, i: (b, 0, p)),
            pl.BlockSpec((1, s, LANES), lambda b, p, i: (b, 0, p)),
        ],
        out_specs=pl.BlockSpec((1, tq, LANES), lambda b, p, i: (b, i, p)),
        out_shape=jax.ShapeDtypeStruct((bsz, s, w), BF16),
        scratch_shapes=[pltpu.VMEM((2, tq, LANES), F32), pltpu.VMEM((2, tq, LANES), F32)],
        compiler_params=pltpu.CompilerParams(
            dimension_semantics=("arbitrary", "arbitrary", "arbitrary"),
            vmem_limit_bytes=VMEM_LIMIT),
        name="sb_attn",
    )(q, k, v)


def _fox_kernel(q_ref, k_ref, v_ref, f_ref, o_ref, acc_ref, m_ref, l_ref, *, tk):
    tq = q_ref.shape[1]
    qi = pl.program_id(2)
    q = q_ref[0]
    lane = lax.broadcasted_iota(jnp.int32, (1, LANES), 1)
    qh = (jnp.where(lane < HEAD_DIM, q, 0).astype(BF16), jnp.where(lane >= HEAD_DIM, q, 0).astype(BF16))
    row = lax.broadcasted_iota(jnp.int32, (tq, tk), 0)
    col = lax.broadcasted_iota(jnp.int32, (tq, tk), 1)
    causal = col <= row
    last_tile = pl.multiple_of(qi * tq + (tq - LANES), LANES)

    def step(j, diag):
        start = pl.multiple_of(j * tk, tk)
        k = k_ref[0, pl.ds(start, tk), :]
        v = v_ref[0, pl.ds(start, tk), :]
        for h in range(2):
            fref = f_ref[0, h, :, pl.ds(last_tile, LANES)][:, LANES - 1:LANES]
            frow = f_ref[0, h, :, pl.ds(start, tk)] - fref
            s = _dot_nt(qh[h], k) - frow
            if diag:
                s = jnp.where(causal, s, NEG_BIG)
                m_new = jnp.max(s, axis=-1, keepdims=True)
                p = jnp.exp(s - m_new)
                l_ref[h] = jnp.broadcast_to(jnp.sum(p, axis=-1, keepdims=True), (tq, LANES))
                acc_ref[h] = _dot(p.astype(BF16), v)
            else:
                m_old = m_ref[h][:, 0:1]
                m_new = jnp.maximum(m_old, jnp.max(s, axis=-1, keepdims=True))
                alpha = jnp.exp(m_old - m_new)
                p = jnp.exp(s - m_new)
                l_ref[h] = alpha * l_ref[h] + jnp.sum(p, axis=-1, keepdims=True)
                acc_ref[h] = alpha * acc_ref[h] + _dot(p.astype(BF16), v)
            m_ref[h] = jnp.broadcast_to(m_new, (tq, LANES))

    step(qi, True)

    def body(i, carry):
        step(qi - 1 - i, False)
        return carry

    lax.fori_loop(0, qi, body, 0)
    o = jnp.where(lane < HEAD_DIM, acc_ref[0] / l_ref[0], acc_ref[1] / l_ref[1])
    o_ref[0] = o.astype(o_ref.dtype)


def _fox_attention(q, k, v, f, *, tq):
    bsz, s, w = q.shape
    f4 = f.reshape(bsz, f.shape[1], 1, s)
    kern = functools.partial(_fox_kernel, tk=tq)
    return pl.pallas_call(
        kern,
        grid=(bsz, w // LANES, s // tq),
        in_specs=[
            pl.BlockSpec((1, tq, LANES), lambda b, p, i: (b, i, p)),
            pl.BlockSpec((1, s, LANES), lambda b, p, i: (b, 0, p)),
            pl.BlockSpec((1, s, LANES), lambda b, p, i: (b, 0, p)),
            pl.BlockSpec((1, 2, 1, s), lambda b, p, i: (b, p, 0, 0)),
        ],
        out_specs=pl.BlockSpec((1, tq, LANES), lambda b, p, i: (b, i, p)),
        out_shape=jax.ShapeDtypeStruct((bsz, s, w), BF16),
        scratch_shapes=[pltpu.VMEM((2, tq, LANES), F32)] * 3,
        compiler_params=pltpu.CompilerParams(
            dimension_semantics=("arbitrary", "arbitrary", "arbitrary"),
            vmem_limit_bytes=VMEM_LIMIT),
        name="fox_attn",
    )(q, k, v, f4)


def _out_kernel(x_ref, osb_ref, ofox_ref, osgu_ref, mod_ref, g2_ref, wo_ref, w1_ref, w2_ref,
                o_ref, *, ff_chunk):
    sb_w = osb_ref.shape[2]
    fox_w = ofox_ref.shape[2]
    x = x_ref[0]
    mix = (_dot(osb_ref[0], wo_ref[0:sb_w, :])
           + _dot(ofox_ref[0], wo_ref[sb_w:sb_w + fox_w, :])
           + _dot(osgu_ref[0], wo_ref[sb_w + fox_w:, :]))
    x1 = x + mod_ref[0, 2:3, :] * mix
    ms = jnp.mean(x1 * x1, axis=-1, keepdims=True)
    h = x1 * lax.rsqrt(ms + EPS) * g2_ref[...]
    hb = (h * (1.0 + mod_ref[0, 4:5, :]) + mod_ref[0, 3:4, :]).astype(BF16)
    d_ff = w1_ref.shape[1]
    acc = None
    for c in range(d_ff // ff_chunk):
        hid = jnp.maximum(_dot(hb, w1_ref[:, c * ff_chunk:(c + 1) * ff_chunk]), 0.0)
        part = _dot((hid * hid).astype(BF16), w2_ref[c * ff_chunk:(c + 1) * ff_chunk, :])
        acc = part if acc is None else acc + part
    o_ref[0] = x1 + mod_ref[0, 5:6, :] * acc


def _out_mlp(x, osb, ofox, osgu, mod_l, g2, wo, w1, w2, *, tm):
    bsz, s, d = x.shape
    tok = lambda b, i: (b, i, 0)
    const2 = lambda b, i: (0, 0)
    single = pl.Buffered(1)
    kern = functools.partial(_out_kernel, ff_chunk=1024)
    return pl.pallas_call(
        kern,
        grid=(bsz, s // tm),
        in_specs=[
            pl.BlockSpec((1, tm, d), tok),
            pl.BlockSpec((1, tm, osb.shape[2]), tok),
            pl.BlockSpec((1, tm, ofox.shape[2]), tok),
            pl.BlockSpec((1, tm, osgu.shape[2]), tok),
            pl.BlockSpec((1, 6, d), lambda b, i: (b, 0, 0)),
            pl.BlockSpec((1, d), const2),
            pl.BlockSpec(wo.shape, const2, pipeline_mode=single),
            pl.BlockSpec(w1.shape, const2, pipeline_mode=single),
            pl.BlockSpec(w2.shape, const2, pipeline_mode=single),
        ],
        out_specs=pl.BlockSpec((1, tm, d), tok),
        out_shape=jax.ShapeDtypeStruct((bsz, s, d), F32),
        compiler_params=pltpu.CompilerParams(
            dimension_semantics=("arbitrary", "arbitrary"), vmem_limit_bytes=VMEM_LIMIT),
        name="out_mlp",
    )(x, osb, ofox, osgu, mod_l, g2, wo, w1, w2)


def kernel(x, c, ada_w, ada_b, norm1_g, norm2_g, w_in, b_forget, q_norm_g, k_norm_g, sgu_norm_g,
           sgu_w, sgu_b, w_out, mlp_w1, mlp_w2):
    depth, d, _ = ada_w.shape
    bsz, s, _ = x.shape
    fox_heads = b_forget.shape[1]
    fox_w = fox_heads * HEAD_DIM
    sgu_groups, chunk = sgu_b.shape[1], sgu_b.shape[2]
    sgu_wd = sgu_groups * sgu_norm_g.shape[2]
    sb_w = (w_in.shape[2] - 3 * fox_w - fox_heads - 2 * sgu_wd) // 3
    f_lo = 3 * sb_w + 3 * fox_w

    mod = _modulation(c, ada_w, ada_b).reshape(depth, bsz, 6, d)

    for l in range(depth):
        w = jnp.concatenate([w_in[l][:, :f_lo], w_in[l][:, f_lo + fox_heads:]], axis=1).astype(BF16)
        wf = jnp.pad(w_in[l][:, f_lo:f_lo + fox_heads], ((0, 0), (0, LANES - fox_heads)))
        wf_hi = wf.astype(BF16)
        wf_lo = (wf - wf_hi.astype(F32)).astype(BF16)
        bf = jnp.pad(b_forget[l], (0, LANES - fox_heads)).reshape(1, LANES)
        qg = jnp.tile(q_norm_g[l], LANES // HEAD_DIM).reshape(1, LANES)
        kg = jnp.tile(k_norm_g[l], LANES // HEAD_DIM).reshape(1, LANES)
        sg = sgu_norm_g[l].reshape(1, sgu_wd)
        sb = jnp.repeat(sgu_b[l].T, sgu_norm_g.shape[2], axis=1)

        qa, ka, va, qb, kb, vb, f, osgu = _in_proj(
            x, mod[l], norm1_g[l].reshape(1, d), w, wf_hi, wf_lo, bf, qg, kg, sg, sgu_w[l], sb,
            tm=512, sb_w=sb_w, fox_w=fox_w, sgu_w=sgu_wd, fox_heads=fox_heads)
        osb = _sb_attention(qa, ka, va, tq=256)
        ofox = _fox_attention(qb, kb, vb, f, tq=256)
        x = _out_mlp(x, osb, ofox, osgu, mod[l], norm2_g[l].reshape(1, d),
                     w_out[l].astype(BF16), mlp_w1[l].astype(BF16), mlp_w2[l].astype(BF16), tm=512)
    return x
```

```python
import functools
import math

import jax
import jax.numpy as jnp
from jax import lax
from jax.experimental import pallas as pl
from jax.experimental.pallas import tpu as pltpu

HEAD_DIM = 64
LANES = 128
EPS = 1e-6
NEG_BIG = -1e30
EXP_ZERO = -104.0
VMEM_LIMIT = 56 * 1024 * 1024

F32 = jnp.float32
BF16 = jnp.bfloat16


def _dot(a, b):
    return jnp.dot(a, b, preferred_element_type=F32)


def _dot_nt(a, b):
    return lax.dot_general(a, b, (((1,), (1,)), ((), ())), preferred_element_type=F32)


def _split2(x):
    hi = x.astype(BF16)
    lo = (x - hi.astype(F32)).astype(BF16)
    return hi, lo


def _split3(x):
    hi = x.astype(BF16)
    r = x - hi.astype(F32)
    mid = r.astype(BF16)
    lo = (r - mid.astype(F32)).astype(BF16)
    return hi, mid, lo


def _group_mean_matrix():
    r = lax.broadcasted_iota(jnp.int32, (LANES, LANES), 0) >> 6
    c = lax.broadcasted_iota(jnp.int32, (LANES, LANES), 1) >> 6
    return jnp.where(r == c, 1.0 / HEAD_DIM, 0.0).astype(BF16)


def _head_rmsnorm(t, gmat, gain):
    hi, lo = _split2(t * t)
    ms = _dot(hi, gmat) + _dot(lo, gmat)
    return t * lax.rsqrt(ms + EPS) * gain


def _gelu_tanh(x):
    c = math.sqrt(2.0 / math.pi)
    return x * (0.5 * (1.0 + jnp.tanh(c * (x + 0.044715 * (x * x * x)))))


def _log_sigmoid(x):
    return jnp.minimum(x, 0.0) - jnp.log(1.0 + jnp.exp(-jnp.abs(x)))


def _mod_kernel(ct_ref, w_ref, b_ref, o_ref):
    ct = ct_ref[...]
    cond = ct * (1.0 / (1.0 + jnp.exp(-ct)))
    w = w_ref[0]
    rows = [jnp.sum(cond[:, b:b + 1] * w, axis=0, keepdims=True) for b in range(ct.shape[1])]
    o_ref[0] = jnp.concatenate(rows, axis=0) + b_ref[0]


def _modulation(c, ada_w, ada_b):
    depth, d, n = ada_w.shape
    bsz = c.shape[0]
    tn = 1536
    return pl.pallas_call(
        _mod_kernel,
        grid=(depth, n // tn),
        in_specs=[
            pl.BlockSpec((d, bsz), lambda l, j: (0, 0)),
            pl.BlockSpec((1, d, tn), lambda l, j: (l, 0, j)),
            pl.BlockSpec((1, 1, tn), lambda l, j: (l, 0, j)),
        ],
        out_specs=pl.BlockSpec((1, bsz, tn), lambda l, j: (l, 0, j)),
        out_shape=jax.ShapeDtypeStruct((depth, bsz, n), F32),
        compiler_params=pltpu.CompilerParams(
            dimension_semantics=("arbitrary", "arbitrary"), vmem_limit_bytes=VMEM_LIMIT),
        name="adaln_mod",
    )(c.T, ada_w, ada_b.reshape(depth, 1, n))


def _in_kernel(x_ref, mod_ref, g1_ref, w_ref, wfh_ref, wfl_ref, bf_ref, qg_ref, kg_ref,
               sg_ref, sw_ref, sb_ref,
               qa_ref, ka_ref, vat_ref, qf_ref, kf_ref, vft_ref, og_ref, f_ref,
               carry_ref, *, sb_w, fox_w, sgu_w, chunk):
    tm = x_ref.shape[1]
    x = x_ref[0]
    ms = jnp.mean(x * x, axis=-1, keepdims=True)
    h = x * lax.rsqrt(ms + EPS) * g1_ref[...]
    h = h * (1.0 + mod_ref[0, 1:2, :]) + mod_ref[0, 0:1, :]
    hb = h.astype(BF16)
    scale = HEAD_DIM ** -0.5

    o = 0
    qa_ref[0] = (_dot(hb, w_ref[:, o:o + sb_w]) * scale).astype(BF16); o += sb_w
    ka_ref[0] = _dot(hb, w_ref[:, o:o + sb_w]).astype(BF16); o += sb_w
    vat_ref[0] = _dot(hb, w_ref[:, o:o + sb_w]).T.astype(BF16); o += sb_w

    @pl.when(pl.program_id(1) == 0)
    def _():
        carry_ref[...] = jnp.zeros_like(carry_ref)

    h_hi, h_lo = _split2(h)
    fl = _dot(h_hi, wfh_ref[...]) + _dot(h_lo, wfh_ref[...]) + _dot(h_hi, wfl_ref[...])
    logf = _log_sigmoid(fl + bf_ref[...])
    r = lax.broadcasted_iota(jnp.int32, (tm, tm), 0)
    cc = lax.broadcasted_iota(jnp.int32, (tm, tm), 1)
    ltri = jnp.where(r >= cc, 1.0, 0.0).astype(BF16)
    l1, l2, l3 = _split3(logf)
    cum = _dot(ltri, l1) + _dot(ltri, l2) + _dot(ltri, l3) + carry_ref[0:1, :]
    carry_ref[0:1, :] = cum[tm - 1:tm, :]
    f_ref[0] = cum.T[0:f_ref.shape[1], :]
    cum3 = jnp.concatenate(_split3(cum), axis=1)
    er = lax.broadcasted_iota(jnp.int32, (3 * LANES, LANES), 0)
    ec = lax.broadcasted_iota(jnp.int32, (3 * LANES, LANES), 1)

    gmat = _group_mean_matrix()
    lane = lax.broadcasted_iota(jnp.int32, (1, LANES), 1)
    is_head = lane < HEAD_DIM
    q_extra = jnp.where((lane >= HEAD_DIM) & (lane < HEAD_DIM + 3), -1.0, 0.0)
    heads_per_tile = LANES // HEAD_DIM
    for j in range(fox_w // LANES):
        t = _dot(hb, w_ref[:, o + j * LANES:o + (j + 1) * LANES])
        qn = _head_rmsnorm(t, gmat, qg_ref[...]) * scale
        for s in range(heads_per_tile):
            hh = heads_per_tile * j + s
            qs = qn if s == 0 else pltpu.roll(qn, LANES - s * HEAD_DIM, 1)
            qf_ref[0, :, hh * LANES:(hh + 1) * LANES] = jnp.where(is_head, qs, q_extra).astype(BF16)
    o += fox_w
    for j in range(fox_w // LANES):
        t = _dot(hb, w_ref[:, o + j * LANES:o + (j + 1) * LANES])
        kn = _head_rmsnorm(t, gmat, kg_ref[...])
        for s in range(heads_per_tile):
            hh = heads_per_tile * j + s
            ks = kn if s == 0 else pltpu.roll(kn, LANES - s * HEAD_DIM, 1)
            sel = jnp.where((er == hh) & (ec == HEAD_DIM), 1.0, 0.0)
            sel = sel + jnp.where((er == LANES + hh) & (ec == HEAD_DIM + 1), 1.0, 0.0)
            sel = sel + jnp.where((er == 2 * LANES + hh) & (ec == HEAD_DIM + 2), 1.0, 0.0)
            extra = _dot(cum3, sel.astype(BF16))
            kf_ref[0, :, hh * LANES:(hh + 1) * LANES] = jnp.where(is_head, ks, extra).astype(BF16)
    o += fox_w
    vt = _dot(hb, w_ref[:, o:o + fox_w]).T.astype(BF16); o += fox_w
    ones = jnp.ones((HEAD_DIM, tm), BF16)
    for hh in range(fox_w // HEAD_DIM):
        vft_ref[0, hh * LANES:hh * LANES + HEAD_DIM, :] = vt[hh * HEAD_DIM:(hh + 1) * HEAD_DIM, :]
        vft_ref[0, hh * LANES + HEAD_DIM:(hh + 1) * LANES, :] = ones

    gu = _gelu_tanh(_dot(hb, w_ref[:, o:o + sgu_w])); o += sgu_w
    gv = _gelu_tanh(_dot(hb, w_ref[:, o:o + sgu_w])); o += sgu_w
    lane_c = lax.broadcasted_iota(jnp.int32, (chunk, LANES), 1)
    rr = lax.broadcasted_iota(jnp.int32, (chunk, chunk), 0)
    cs = lax.broadcasted_iota(jnp.int32, (chunk, chunk), 1)
    wt = [jnp.where(rr >= cs, sw_ref[g], 0.0).astype(BF16) for g in range(sw_ref.shape[0])]
    for p in range(sgu_w // LANES):
        vn = _head_rmsnorm(gv[:, p * LANES:(p + 1) * LANES], gmat,
                           sg_ref[:, p * LANES:(p + 1) * LANES]).astype(BF16)
        for ci in range(tm // chunk):
            vblk = vn[ci * chunk:(ci + 1) * chunk, :]
            mixed = jnp.where(lane_c < HEAD_DIM, _dot(wt[2 * p], vblk), _dot(wt[2 * p + 1], vblk))
            mixed = mixed + sb_ref[:, p * LANES:(p + 1) * LANES]
            og_ref[0, ci * chunk:(ci + 1) * chunk, p * LANES:(p + 1) * LANES] = (
                gu[ci * chunk:(ci + 1) * chunk, p * LANES:(p + 1) * LANES] * mixed).astype(BF16)


def _in_proj(x, mod_l, g1, w, wf_hi, wf_lo, bf, qg, kg, sg, sw, sb, *, tm, sb_w, fox_w, sgu_w):
    bsz, s, d = x.shape
    chunk = sw.shape[-1]
    wn = w.shape[1]
    fox_heads = fox_w // HEAD_DIM
    const2 = lambda b, i: (0, 0)
    tok = lambda b, i: (b, i, 0)
    tok_t = lambda b, i: (b, 0, i)
    kern = functools.partial(_in_kernel, sb_w=sb_w, fox_w=fox_w, sgu_w=sgu_w, chunk=chunk)
    out_shape = [
        jax.ShapeDtypeStruct((bsz, s, sb_w), BF16),
        jax.ShapeDtypeStruct((bsz, s, sb_w), BF16),
        jax.ShapeDtypeStruct((bsz, sb_w, s), BF16),
        jax.ShapeDtypeStruct((bsz, s, fox_heads * LANES), BF16),
        jax.ShapeDtypeStruct((bsz, s, fox_heads * LANES), BF16),
        jax.ShapeDtypeStruct((bsz, fox_heads * LANES, s), BF16),
        jax.ShapeDtypeStruct((bsz, s, sgu_w), BF16),
        jax.ShapeDtypeStruct((bsz, fox_heads, s), F32),
    ]
    out_specs = [
        pl.BlockSpec((1, tm, sb_w), tok),
        pl.BlockSpec((1, tm, sb_w), tok),
        pl.BlockSpec((1, sb_w, tm), tok_t),
        pl.BlockSpec((1, tm, fox_heads * LANES), tok),
        pl.BlockSpec((1, tm, fox_heads * LANES), tok),
        pl.BlockSpec((1, fox_heads * LANES, tm), tok_t),
        pl.BlockSpec((1, tm, sgu_w), tok),
        pl.BlockSpec((1, fox_heads, tm), tok_t),
    ]
    return pl.pallas_call(
        kern,
        grid=(bsz, s // tm),
        in_specs=[
            pl.BlockSpec((1, tm, d), tok),
            pl.BlockSpec((1, 6, d), lambda b, i: (b, 0, 0)),
            pl.BlockSpec((1, d), const2),
            pl.BlockSpec((d, wn), const2),
            pl.BlockSpec((d, LANES), const2),
            pl.BlockSpec((d, LANES), const2),
            pl.BlockSpec((1, LANES), const2),
            pl.BlockSpec((1, LANES), const2),
            pl.BlockSpec((1, LANES), const2),
            pl.BlockSpec((1, sgu_w), const2),
            pl.BlockSpec(sw.shape, lambda b, i: (0, 0, 0)),
            pl.BlockSpec((chunk, sgu_w), const2),
        ],
        out_specs=out_specs,
        out_shape=out_shape,
        scratch_shapes=[pltpu.VMEM((8, LANES), F32)],
        compiler_params=pltpu.CompilerParams(
            dimension_semantics=("arbitrary", "arbitrary"), vmem_limit_bytes=VMEM_LIMIT),
        name="in_proj",
    )(x, mod_l, g1, w, wf_hi, wf_lo, bf, qg, kg, sg, sw, sb)


def _sb_kernel(q_ref, k_ref, vt_ref, o_ref, acc_ref, c_ref, *, tk):
    tq = q_ref.shape[1]
    qi = pl.program_id(2)
    q = q_ref[0]
    lane = lax.broadcasted_iota(jnp.int32, (1, LANES), 1)
    qh = (jnp.where(lane < HEAD_DIM, q, 0).astype(BF16), jnp.where(lane >= HEAD_DIM, q, 0).astype(BF16))
    key = lax.broadcasted_iota(jnp.int32, (tk, tq), 0)
    qry = lax.broadcasted_iota(jnp.int32, (tk, tq), 1)
    causal = key < qry
    js = lax.broadcasted_iota(jnp.int32, (tk, tk), 0)
    jr = lax.broadcasted_iota(jnp.int32, (tk, tk), 1)
    suffix = jnp.where(jr >= js, 1.0, 0.0).astype(BF16)

    def step(j, diag):
        start = pl.multiple_of(j * tk, tk)
        k = k_ref[0, pl.ds(start, tk), :]
        vt = vt_ref[0, :, pl.ds(start, tk)]
        for h in range(2):
            z = _dot_nt(k, qh[h])
            l1mb = -(jnp.maximum(z, 0.0) + jnp.log(1.0 + jnp.exp(-jnp.abs(z))))
            if diag:
                l1mb = jnp.where(causal, l1mb, 0.0)
            hi, lo = _split2(l1mb)
            sfx = _dot(suffix, hi) + _dot(suffix, lo)
            if diag:
                a = jnp.where(causal, jnp.exp(z + sfx), 0.0)
                c_ref[h] = sfx[0:1, :]
                acc_ref[h] = _dot(vt, a.astype(BF16))
            else:
                c = c_ref[h]
                a = jnp.exp(z + sfx + c)
                c_ref[h] = c + sfx[0:1, :]
                acc_ref[h] += _dot(vt, a.astype(BF16))

    def carry_max():
        cm = jnp.maximum(jnp.max(c_ref[0], axis=1, keepdims=True),
                         jnp.max(c_ref[1], axis=1, keepdims=True))
        return cm[0, 0]

    step(qi, True)

    def cond(carry):
        i, cmax = carry
        return jnp.logical_and(i < qi, cmax > EXP_ZERO)

    def body(carry):
        i, _ = carry
        step(qi - 1 - i, False)
        return i + 1, carry_max()

    lax.while_loop(cond, body, (jnp.int32(0), carry_max()))
    chan = lax.broadcasted_iota(jnp.int32, (LANES, tq), 0)
    o_ref[0] = jnp.where(chan < HEAD_DIM, acc_ref[0], acc_ref[1]).T.astype(o_ref.dtype)


def _sb_attention(q, k, vt, *, tq):
    bsz, s, w = q.shape
    kern = functools.partial(_sb_kernel, tk=tq)
    return pl.pallas_call(
        kern,
        grid=(bsz, w // LANES, s // tq),
        in_specs=[
            pl.BlockSpec((1, tq, LANES), lambda b, p, i: (b, i, p)),
            pl.BlockSpec((1, s, LANES), lambda b, p, i: (b, 0, p)),
            pl.BlockSpec((1, LANES, s), lambda b, p, i: (b, p, 0)),
        ],
        out_specs=pl.BlockSpec((1, tq, LANES), lambda b, p, i: (b, i, p)),
        out_shape=jax.ShapeDtypeStruct((bsz, s, w), BF16),
        scratch_shapes=[pltpu.VMEM((2, LANES, tq), F32), pltpu.VMEM((2, 1, tq), F32)],
        compiler_params=pltpu.CompilerParams(
            dimension_semantics=("arbitrary", "arbitrary", "arbitrary"),
            vmem_limit_bytes=VMEM_LIMIT),
        name="sb_attn",
    )(q, k, vt)


def _fox_kernel(fend_ref, thr_ref, q_ref, k_ref, vt_ref, o_ref, acc_ref, m_ref, *, tk, n_heads):
    tq = q_ref.shape[1]
    qi = pl.program_id(2)
    n_blk = pl.num_programs(2)

    def first_dead_block(head):
        base = head * n_blk
        f_q = fend_ref[base + jnp.maximum(qi - 1, 0)]

        def alive(j):
            jc = jnp.maximum(j, 0)
            return jnp.logical_and(j >= 0, f_q - fend_ref[base + jc] >= -thr_ref[0])

        return lax.while_loop(alive, lambda j: j - 1, qi - 1)

    head0 = (pl.program_id(0) * n_heads + 2 * pl.program_id(1))
    j_dead = jnp.minimum(first_dead_block(head0), first_dead_block(head0 + 1))
    key = lax.broadcasted_iota(jnp.int32, (tk, tq), 0)
    qry = lax.broadcasted_iota(jnp.int32, (tk, tq), 1)
    causal = key <= qry

    def step(j, diag):
        start = pl.multiple_of(j * tk, tk)
        for h in range(2):
            k = k_ref[0, pl.ds(start, tk), h * LANES:(h + 1) * LANES]
            vt = vt_ref[0, h * LANES:(h + 1) * LANES, pl.ds(start, tk)]
            s = _dot_nt(k, q_ref[0, :, h * LANES:(h + 1) * LANES])
            if diag:
                s = jnp.where(causal, s, NEG_BIG)
                m_new = jnp.max(s, axis=0, keepdims=True)
                p = jnp.exp(s - m_new)
                acc_ref[h] = _dot(vt, p.astype(BF16))
            else:
                m_old = m_ref[h]
                m_new = jnp.maximum(m_old, jnp.max(s, axis=0, keepdims=True))
                alpha = jnp.exp(m_old - m_new)
                p = jnp.exp(s - m_new)
                acc_ref[h] = alpha * acc_ref[h] + _dot(vt, p.astype(BF16))
            m_ref[h] = m_new

    step(qi, True)

    def body(i, carry):
        step(qi - 1 - i, False)
        return carry

    lax.fori_loop(0, qi - 1 - j_dead, body, 0)
    outs = []
    for h in range(2):
        acc = acc_ref[h]
        outs.append(acc[0:HEAD_DIM, :] / acc[HEAD_DIM:HEAD_DIM + 1, :])
    o_ref[0] = jnp.concatenate(outs, axis=0).T.astype(o_ref.dtype)


def _fox_attention(q, k, vt, f, qk_bound, *, tq):
    bsz, s, w = q.shape
    pair = 2 * LANES
    n_heads = w // LANES
    f_end = f[:, :, tq - 1::tq].reshape(-1)
    thr = (2.0 * qk_bound - EXP_ZERO).reshape(1).astype(F32)
    kern = functools.partial(_fox_kernel, tk=tq, n_heads=n_heads)
    grid_spec = pltpu.PrefetchScalarGridSpec(
        num_scalar_prefetch=2,
        grid=(bsz, w // pair, s // tq),
        in_specs=[
            pl.BlockSpec((1, tq, pair), lambda b, p, i, fe, th: (b, i, p)),
            pl.BlockSpec((1, s, pair), lambda b, p, i, fe, th: (b, 0, p)),
            pl.BlockSpec((1, pair, s), lambda b, p, i, fe, th: (b, p, 0)),
        ],
        out_specs=pl.BlockSpec((1, tq, LANES), lambda b, p, i, fe, th: (b, i, p)),
        scratch_shapes=[pltpu.VMEM((2, LANES, tq), F32), pltpu.VMEM((2, 1, tq), F32)],
    )
    return pl.pallas_call(
        kern,
        grid_spec=grid_spec,
        out_shape=jax.ShapeDtypeStruct((bsz, s, w // 2), BF16),
        compiler_params=pltpu.CompilerParams(
            dimension_semantics=("arbitrary", "arbitrary", "arbitrary"),
            vmem_limit_bytes=VMEM_LIMIT),
        name="fox_attn",
    )(f_end, thr, q, k, vt)


def _out_kernel(x_ref, osb_ref, ofox_ref, osgu_ref, mod_ref, g2_ref, wo_ref, w1_ref, w2_ref,
                o_ref, *, ff_chunk):
    sb_w = osb_ref.shape[2]
    fox_w = ofox_ref.shape[2]
    x = x_ref[0]
    mix = (_dot(osb_ref[0], wo_ref[0:sb_w, :])
           + _dot(ofox_ref[0], wo_ref[sb_w:sb_w + fox_w, :])
           + _dot(osgu_ref[0], wo_ref[sb_w + fox_w:, :]))
    x1 = x + mod_ref[0, 2:3, :] * mix
    ms = jnp.mean(x1 * x1, axis=-1, keepdims=True)
    h = x1 * lax.rsqrt(ms + EPS) * g2_ref[...]
    hb = (h * (1.0 + mod_ref[0, 4:5, :]) + mod_ref[0, 3:4, :]).astype(BF16)
    d_ff = w1_ref.shape[1]
    acc = None
    for c in range(d_ff // ff_chunk):
        hid = jnp.maximum(_dot(hb, w1_ref[:, c * ff_chunk:(c + 1) * ff_chunk]), 0.0)
        part = _dot((hid * hid).astype(BF16), w2_ref[c * ff_chunk:(c + 1) * ff_chunk, :])
        acc = part if acc is None else acc + part
    o_ref[0] = x1 + mod_ref[0, 5:6, :] * acc


def _out_mlp(x, osb, ofox, osgu, mod_l, g2, wo, w1, w2, *, tm):
    bsz, s, d = x.shape
    tok = lambda b, i: (b, i, 0)
    const2 = lambda b, i: (0, 0)
    single = pl.Buffered(1)
    kern = functools.partial(_out_kernel, ff_chunk=1024)
    return pl.pallas_call(
        kern,
        grid=(bsz, s // tm),
        in_specs=[
            pl.BlockSpec((1, tm, d), tok),
            pl.BlockSpec((1, tm, osb.shape[2]), tok),
            pl.BlockSpec((1, tm, ofox.shape[2]), tok),
            pl.BlockSpec((1, tm, osgu.shape[2]), tok),
            pl.BlockSpec((1, 6, d), lambda b, i: (b, 0, 0)),
            pl.BlockSpec((1, d), const2),
            pl.BlockSpec(wo.shape, const2, pipeline_mode=single),
            pl.BlockSpec(w1.shape, const2, pipeline_mode=single),
            pl.BlockSpec(w2.shape, const2, pipeline_mode=single),
        ],
        out_specs=pl.BlockSpec((1, tm, d), tok),
        out_shape=jax.ShapeDtypeStruct((bsz, s, d), F32),
        compiler_params=pltpu.CompilerParams(
            dimension_semantics=("arbitrary", "arbitrary"), vmem_limit_bytes=VMEM_LIMIT),
        name="out_mlp",
    )(x, osb, ofox, osgu, mod_l, g2, wo, w1, w2)


def kernel(x, c, ada_w, ada_b, norm1_g, norm2_g, w_in, b_forget, q_norm_g, k_norm_g, sgu_norm_g,
           sgu_w, sgu_b, w_out, mlp_w1, mlp_w2):
    depth, d, _ = ada_w.shape
    bsz, s, _ = x.shape
    fox_heads = b_forget.shape[1]
    fox_w = fox_heads * HEAD_DIM
    sgu_groups, chunk = sgu_b.shape[1], sgu_b.shape[2]
    sgu_wd = sgu_groups * sgu_norm_g.shape[2]
    sb_w = (w_in.shape[2] - 3 * fox_w - fox_heads - 2 * sgu_wd) // 3
    f_lo = 3 * sb_w + 3 * fox_w

    mod = _modulation(c, ada_w, ada_b).reshape(depth, bsz, 6, d)

    for l in range(depth):
        w = jnp.concatenate([w_in[l][:, :f_lo], w_in[l][:, f_lo + fox_heads:]], axis=1).astype(BF16)
        wf = jnp.pad(w_in[l][:, f_lo:f_lo + fox_heads], ((0, 0), (0, LANES - fox_heads)))
        wf_hi = wf.astype(BF16)
        wf_lo = (wf - wf_hi.astype(F32)).astype(BF16)
        bf = jnp.pad(b_forget[l], (0, LANES - fox_heads)).reshape(1, LANES)
        qg = jnp.tile(q_norm_g[l], LANES // HEAD_DIM).reshape(1, LANES)
        kg = jnp.tile(k_norm_g[l], LANES // HEAD_DIM).reshape(1, LANES)
        sg = sgu_norm_g[l].reshape(1, sgu_wd)
        sb = jnp.repeat(sgu_b[l].T, sgu_norm_g.shape[2], axis=1)

        qk_bound = (1.03 * HEAD_DIM ** 0.5) * jnp.max(jnp.abs(q_norm_g[l])) * jnp.max(jnp.abs(k_norm_g[l]))
        qa, ka, vat, qf, kf, vft, osgu, f = _in_proj(
            x, mod[l], norm1_g[l].reshape(1, d), w, wf_hi, wf_lo, bf, qg, kg, sg, sgu_w[l], sb,
            tm=512, sb_w=sb_w, fox_w=fox_w, sgu_w=sgu_wd)
        osb = _sb_attention(qa, ka, vat, tq=256)
        ofox = _fox_attention(qf, kf, vft, f, qk_bound, tq=256)
        x = _out_mlp(x, osb, ofox, osgu, mod[l], norm2_g[l].reshape(1, d),
                     w_out[l].astype(BF16), mlp_w1[l].astype(BF16), mlp_w2[l].astype(BF16), tm=512)
    return x
```

```python
import functools
import math

import jax
import jax.numpy as jnp
from jax import lax
from jax.experimental import pallas as pl
from jax.experimental.pallas import tpu as pltpu

HEAD_DIM = 64
LANES = 128
EPS = 1e-6
NEG_BIG = -1e30
EXP_ZERO = -104.0
VMEM_LIMIT = 56 * 1024 * 1024

F32 = jnp.float32
BF16 = jnp.bfloat16


def _dot(a, b):
    return jnp.dot(a, b, preferred_element_type=F32)


def _dot_nt(a, b):
    return lax.dot_general(a, b, (((1,), (1,)), ((), ())), preferred_element_type=F32)


def _split2(x):
    hi = x.astype(BF16)
    lo = (x - hi.astype(F32)).astype(BF16)
    return hi, lo


def _split3(x):
    hi = x.astype(BF16)
    r = x - hi.astype(F32)
    mid = r.astype(BF16)
    lo = (r - mid.astype(F32)).astype(BF16)
    return hi, mid, lo


def _group_mean_matrix():
    r = lax.broadcasted_iota(jnp.int32, (LANES, LANES), 0) >> 6
    c = lax.broadcasted_iota(jnp.int32, (LANES, LANES), 1) >> 6
    return jnp.where(r == c, 1.0 / HEAD_DIM, 0.0).astype(BF16)


def _head_rmsnorm(t, gmat, gain):
    hi, lo = _split2(t * t)
    ms = _dot(hi, gmat) + _dot(lo, gmat)
    return t * lax.rsqrt(ms + EPS) * gain


def _gelu_tanh(x):
    c = math.sqrt(2.0 / math.pi)
    return x * (0.5 * (1.0 + jnp.tanh(c * (x + 0.044715 * (x * x * x)))))


def _log_sigmoid(x):
    return jnp.minimum(x, 0.0) - jnp.log(1.0 + jnp.exp(-jnp.abs(x)))


def _mod_kernel(ct_ref, w_ref, b_ref, o_ref):
    ct = ct_ref[...]
    cond = ct * (1.0 / (1.0 + jnp.exp(-ct)))
    w = w_ref[0]
    rows = [jnp.sum(cond[:, b:b + 1] * w, axis=0, keepdims=True) for b in range(ct.shape[1])]
    o_ref[0] = jnp.concatenate(rows, axis=0) + b_ref[0]


def _modulation(c, ada_w, ada_b):
    depth, d, n = ada_w.shape
    bsz = c.shape[0]
    tn = 1536
    return pl.pallas_call(
        _mod_kernel,
        grid=(depth, n // tn),
        in_specs=[
            pl.BlockSpec((d, bsz), lambda l, j: (0, 0)),
            pl.BlockSpec((1, d, tn), lambda l, j: (l, 0, j)),
            pl.BlockSpec((1, 1, tn), lambda l, j: (l, 0, j)),
        ],
        out_specs=pl.BlockSpec((1, bsz, tn), lambda l, j: (l, 0, j)),
        out_shape=jax.ShapeDtypeStruct((depth, bsz, n), F32),
        compiler_params=pltpu.CompilerParams(
            dimension_semantics=("arbitrary", "arbitrary"), vmem_limit_bytes=VMEM_LIMIT),
        name="adaln_mod",
    )(c.T, ada_w, ada_b.reshape(depth, 1, n))


def _in_kernel(x_ref, mod_ref, g1_ref, w_ref, wfh_ref, wfl_ref, bf_ref, qg_ref, kg_ref,
               sg_ref, sw_ref, sb_ref,
               qa_ref, ka_ref, vat_ref, qf_ref, kf_ref, vft_ref, og_ref, f_ref,
               carry_ref, *, sb_w, fox_w, sgu_w, chunk):
    tm = x_ref.shape[1]
    x = x_ref[0]
    ms = jnp.mean(x * x, axis=-1, keepdims=True)
    h = x * lax.rsqrt(ms + EPS) * g1_ref[...]
    h = h * (1.0 + mod_ref[0, 1:2, :]) + mod_ref[0, 0:1, :]
    hb = h.astype(BF16)
    scale = HEAD_DIM ** -0.5

    o = 0
    qa_ref[0] = (_dot(hb, w_ref[:, o:o + sb_w]) * scale).astype(BF16); o += sb_w
    ka_ref[0] = _dot(hb, w_ref[:, o:o + sb_w]).astype(BF16); o += sb_w
    vat_ref[0] = _dot(hb, w_ref[:, o:o + sb_w]).T.astype(BF16); o += sb_w

    @pl.when(pl.program_id(1) == 0)
    def _():
        carry_ref[...] = jnp.zeros_like(carry_ref)

    h_hi, h_lo = _split2(h)
    fl = _dot(h_hi, wfh_ref[...]) + _dot(h_lo, wfh_ref[...]) + _dot(h_hi, wfl_ref[...])
    logf = _log_sigmoid(fl + bf_ref[...])
    r = lax.broadcasted_iota(jnp.int32, (tm, tm), 0)
    cc = lax.broadcasted_iota(jnp.int32, (tm, tm), 1)
    ltri = jnp.where(r >= cc, 1.0, 0.0).astype(BF16)
    l1, l2, l3 = _split3(logf)
    cum = _dot(ltri, l1) + _dot(ltri, l2) + _dot(ltri, l3) + carry_ref[0:1, :]
    carry_ref[0:1, :] = cum[tm - 1:tm, :]
    f_ref[0] = cum.T[0:f_ref.shape[1], :]
    cum3 = jnp.concatenate(_split3(cum), axis=1)
    er = lax.broadcasted_iota(jnp.int32, (3 * LANES, LANES), 0)
    ec = lax.broadcasted_iota(jnp.int32, (3 * LANES, LANES), 1)

    gmat = _group_mean_matrix()
    lane = lax.broadcasted_iota(jnp.int32, (1, LANES), 1)
    is_head = lane < HEAD_DIM
    q_extra = jnp.where((lane >= HEAD_DIM) & (lane < HEAD_DIM + 3), -1.0, 0.0)
    heads_per_tile = LANES // HEAD_DIM
    for j in range(fox_w // LANES):
        t = _dot(hb, w_ref[:, o + j * LANES:o + (j + 1) * LANES])
        qn = _head_rmsnorm(t, gmat, qg_ref[...]) * scale
        for s in range(heads_per_tile):
            hh = heads_per_tile * j + s
            qs = qn if s == 0 else pltpu.roll(qn, LANES - s * HEAD_DIM, 1)
            qf_ref[0, :, hh * LANES:(hh + 1) * LANES] = jnp.where(is_head, qs, q_extra).astype(BF16)
    o += fox_w
    for j in range(fox_w // LANES):
        t = _dot(hb, w_ref[:, o + j * LANES:o + (j + 1) * LANES])
        kn = _head_rmsnorm(t, gmat, kg_ref[...])
        for s in range(heads_per_tile):
            hh = heads_per_tile * j + s
            ks = kn if s == 0 else pltpu.roll(kn, LANES - s * HEAD_DIM, 1)
            sel = jnp.where((er == hh) & (ec == HEAD_DIM), 1.0, 0.0)
            sel = sel + jnp.where((er == LANES + hh) & (ec == HEAD_DIM + 1), 1.0, 0.0)
            sel = sel + jnp.where((er == 2 * LANES + hh) & (ec == HEAD_DIM + 2), 1.0, 0.0)
            extra = _dot(cum3, sel.astype(BF16))
            kf_ref[0, :, hh * LANES:(hh + 1) * LANES] = jnp.where(is_head, ks, extra).astype(BF16)
    o += fox_w
    vt = _dot(hb, w_ref[:, o:o + fox_w]).T.astype(BF16); o += fox_w
    ones = jnp.ones((HEAD_DIM, tm), BF16)
    for hh in range(fox_w // HEAD_DIM):
        vft_ref[0, hh * LANES:hh * LANES + HEAD_DIM, :] = vt[hh * HEAD_DIM:(hh + 1) * HEAD_DIM, :]
        vft_ref[0, hh * LANES + HEAD_DIM:(hh + 1) * LANES, :] = ones

    gu = _gelu_tanh(_dot(hb, w_ref[:, o:o + sgu_w])); o += sgu_w
    gv = _gelu_tanh(_dot(hb, w_ref[:, o:o + sgu_w])); o += sgu_w
    lane_c = lax.broadcasted_iota(jnp.int32, (chunk, LANES), 1)
    rr = lax.broadcasted_iota(jnp.int32, (chunk, chunk), 0)
    cs = lax.broadcasted_iota(jnp.int32, (chunk, chunk), 1)
    wt = [jnp.where(rr >= cs, sw_ref[g], 0.0).astype(BF16) for g in range(sw_ref.shape[0])]
    for p in range(sgu_w // LANES):
        vn = _head_rmsnorm(gv[:, p * LANES:(p + 1) * LANES], gmat,
                           sg_ref[:, p * LANES:(p + 1) * LANES]).astype(BF16)
        for ci in range(tm // chunk):
            vblk = vn[ci * chunk:(ci + 1) * chunk, :]
            mixed = jnp.where(lane_c < HEAD_DIM, _dot(wt[2 * p], vblk), _dot(wt[2 * p + 1], vblk))
            mixed = mixed + sb_ref[:, p * LANES:(p + 1) * LANES]
            og_ref[0, ci * chunk:(ci + 1) * chunk, p * LANES:(p + 1) * LANES] = (
                gu[ci * chunk:(ci + 1) * chunk, p * LANES:(p + 1) * LANES] * mixed).astype(BF16)


def _in_proj(x, mod_l, g1, w, wf_hi, wf_lo, bf, qg, kg, sg, sw, sb, *, tm, sb_w, fox_w, sgu_w):
    bsz, s, d = x.shape
    chunk = sw.shape[-1]
    wn = w.shape[1]
    fox_heads = fox_w // HEAD_DIM
    const2 = lambda b, i: (0, 0)
    tok = lambda b, i: (b, i, 0)
    tok_t = lambda b, i: (b, 0, i)
    kern = functools.partial(_in_kernel, sb_w=sb_w, fox_w=fox_w, sgu_w=sgu_w, chunk=chunk)
    out_shape = [
        jax.ShapeDtypeStruct((bsz, s, sb_w), BF16),
        jax.ShapeDtypeStruct((bsz, s, sb_w), BF16),
        jax.ShapeDtypeStruct((bsz, sb_w, s), BF16),
        jax.ShapeDtypeStruct((bsz, s, fox_heads * LANES), BF16),
        jax.ShapeDtypeStruct((bsz, s, fox_heads * LANES), BF16),
        jax.ShapeDtypeStruct((bsz, fox_heads * LANES, s), BF16),
        jax.ShapeDtypeStruct((bsz, s, sgu_w), BF16),
        jax.ShapeDtypeStruct((bsz, fox_heads, s), F32),
    ]
    out_specs = [
        pl.BlockSpec((1, tm, sb_w), tok),
        pl.BlockSpec((1, tm, sb_w), tok),
        pl.BlockSpec((1, sb_w, tm), tok_t),
        pl.BlockSpec((1, tm, fox_heads * LANES), tok),
        pl.BlockSpec((1, tm, fox_heads * LANES), tok),
        pl.BlockSpec((1, fox_heads * LANES, tm), tok_t),
        pl.BlockSpec((1, tm, sgu_w), tok),
        pl.BlockSpec((1, fox_heads, tm), tok_t),
    ]
    return pl.pallas_call(
        kern,
        grid=(bsz, s // tm),
        in_specs=[
            pl.BlockSpec((1, tm, d), tok),
            pl.BlockSpec((1, 6, d), lambda b, i: (b, 0, 0)),
            pl.BlockSpec((1, d), const2),
            pl.BlockSpec((d, wn), const2),
            pl.BlockSpec((d, LANES), const2),
            pl.BlockSpec((d, LANES), const2),
            pl.BlockSpec((1, LANES), const2),
            pl.BlockSpec((1, LANES), const2),
            pl.BlockSpec((1, LANES), const2),
            pl.BlockSpec((1, sgu_w), const2),
            pl.BlockSpec(sw.shape, lambda b, i: (0, 0, 0)),
            pl.BlockSpec((chunk, sgu_w), const2),
        ],
        out_specs=out_specs,
        out_shape=out_shape,
        scratch_shapes=[pltpu.VMEM((8, LANES), F32)],
        compiler_params=pltpu.CompilerParams(
            dimension_semantics=("arbitrary", "arbitrary"), vmem_limit_bytes=VMEM_LIMIT),
        name="in_proj",
    )(x, mod_l, g1, w, wf_hi, wf_lo, bf, qg, kg, sg, sw, sb)


def _sb_kernel(q_ref, k_ref, vt_ref, o_ref, acc_ref, c_ref, *, tk):
    tq = q_ref.shape[1]
    qi = pl.program_id(2)
    q = q_ref[0]
    lane = lax.broadcasted_iota(jnp.int32, (1, LANES), 1)
    qh = (jnp.where(lane < HEAD_DIM, q, 0).astype(BF16), jnp.where(lane >= HEAD_DIM, q, 0).astype(BF16))
    key = lax.broadcasted_iota(jnp.int32, (tk, tq), 0)
    qry = lax.broadcasted_iota(jnp.int32, (tk, tq), 1)
    causal = key < qry
    js = lax.broadcasted_iota(jnp.int32, (tk, tk), 0)
    jr = lax.broadcasted_iota(jnp.int32, (tk, tk), 1)
    suffix = jnp.where(jr >= js, 1.0, 0.0).astype(BF16)

    def process(blocks, first):
        starts = [pl.multiple_of(j * tk, tk) for j in blocks]
        units = [(b, h) for b in range(len(blocks)) for h in range(2)]
        z = {(b, h): _dot_nt(k_ref[0, pl.ds(starts[b], tk), :], qh[h]) for b, h in units}
        split = {}
        for b, h in units:
            zz = z[b, h]
            l1mb = -(jnp.maximum(zz, 0.0) + jnp.log(1.0 + jnp.exp(-jnp.abs(zz))))
            if first and b == 0:
                l1mb = jnp.where(causal, l1mb, 0.0)
            split[b, h] = _split2(l1mb)
        sfx = {u: _dot(suffix, split[u][0]) + _dot(suffix, split[u][1]) for u in units}
        a = {}
        for h in range(2):
            c = None if first else c_ref[h]
            for b in range(len(blocks)):
                e = z[b, h] + sfx[b, h]
                if c is not None:
                    e = e + c
                w = jnp.exp(e)
                if first and b == 0:
                    w = jnp.where(causal, w, 0.0)
                a[b, h] = w.astype(BF16)
                c = sfx[b, h][0:1, :] if c is None else c + sfx[b, h][0:1, :]
            c_ref[h] = c
        for h in range(2):
            pv = None
            for b in range(len(blocks)):
                part = _dot(vt_ref[0, :, pl.ds(starts[b], tk)], a[b, h])
                pv = part if pv is None else pv + part
            if first:
                acc_ref[h] = pv
            else:
                acc_ref[h] += pv

    def carry_max():
        cm = jnp.maximum(jnp.max(c_ref[0], axis=1, keepdims=True),
                         jnp.max(c_ref[1], axis=1, keepdims=True))
        return cm[0, 0]

    @pl.when(qi == 0)
    def _():
        process([qi], True)

    @pl.when(qi > 0)
    def _():
        process([qi, qi - 1], True)

    def cond(carry):
        j, cmax = carry
        return jnp.logical_and(j >= 0, cmax > EXP_ZERO)

    def body(carry):
        j, _ = carry
        process([j], False)
        return j - 1, carry_max()

    lax.while_loop(cond, body, (qi - 2, carry_max()))
    chan = lax.broadcasted_iota(jnp.int32, (LANES, tq), 0)
    o_ref[0] = jnp.where(chan < HEAD_DIM, acc_ref[0], acc_ref[1]).T.astype(o_ref.dtype)


def _sb_attention(q, k, vt, *, tq):
    bsz, s, w = q.shape
    kern = functools.partial(_sb_kernel, tk=tq)
    return pl.pallas_call(
        kern,
        grid=(bsz, w // LANES, s // tq),
        in_specs=[
            pl.BlockSpec((1, tq, LANES), lambda b, p, i: (b, i, p)),
            pl.BlockSpec((1, s, LANES), lambda b, p, i: (b, 0, p)),
            pl.BlockSpec((1, LANES, s), lambda b, p, i: (b, p, 0)),
        ],
        out_specs=pl.BlockSpec((1, tq, LANES), lambda b, p, i: (b, i, p)),
        out_shape=jax.ShapeDtypeStruct((bsz, s, w), BF16),
        scratch_shapes=[pltpu.VMEM((2, LANES, tq), F32), pltpu.VMEM((2, 1, tq), F32)],
        compiler_params=pltpu.CompilerParams(
            dimension_semantics=("arbitrary", "arbitrary", "arbitrary"),
            vmem_limit_bytes=VMEM_LIMIT),
        name="sb_attn",
    )(q, k, vt)


def _fox_kernel(fend_ref, thr_ref, q_ref, k_ref, vt_ref, o_ref, acc_ref, m_ref, *, tk, n_heads):
    tq = q_ref.shape[1]
    qi = pl.program_id(2)
    n_blk = pl.num_programs(2)

    def first_dead_block(head):
        base = head * n_blk
        f_q = fend_ref[base + jnp.maximum(qi - 1, 0)]

        def alive(j):
            jc = jnp.maximum(j, 0)
            return jnp.logical_and(j >= 0, f_q - fend_ref[base + jc] >= -thr_ref[0])

        return lax.while_loop(alive, lambda j: j - 1, qi - 1)

    head0 = (pl.program_id(0) * n_heads + 2 * pl.program_id(1))
    j_dead = jnp.minimum(first_dead_block(head0), first_dead_block(head0 + 1))
    key = lax.broadcasted_iota(jnp.int32, (tk, tq), 0)
    qry = lax.broadcasted_iota(jnp.int32, (tk, tq), 1)
    causal = key <= qry

    def process(blocks, first):
        starts = [pl.multiple_of(j * tk, tk) for j in blocks]
        units = [(b, h) for b in range(len(blocks)) for h in range(2)]
        s = {(b, h): _dot_nt(k_ref[0, pl.ds(starts[b], tk), h * LANES:(h + 1) * LANES],
                             q_ref[0, :, h * LANES:(h + 1) * LANES]) for b, h in units}
        p = {}
        alpha = {}
        for h in range(2):
            if first:
                s[0, h] = jnp.where(causal, s[0, h], NEG_BIG)
            m_new = None if first else m_ref[h]
            for b in range(len(blocks)):
                mb = jnp.max(s[b, h], axis=0, keepdims=True)
                m_new = mb if m_new is None else jnp.maximum(m_new, mb)
            if not first:
                alpha[h] = jnp.exp(m_ref[h] - m_new)
            for b in range(len(blocks)):
                p[b, h] = jnp.exp(s[b, h] - m_new).astype(BF16)
            m_ref[h] = m_new
        for h in range(2):
            pv = None
            for b in range(len(blocks)):
                part = _dot(vt_ref[0, h * LANES:(h + 1) * LANES, pl.ds(starts[b], tk)], p[b, h])
                pv = part if pv is None else pv + part
            acc_ref[h] = pv if first else alpha[h] * acc_ref[h] + pv

    @pl.when(qi == 0)
    def _():
        process([qi], True)

    @pl.when(qi > 0)
    def _():
        process([qi, qi - 1], True)

    n_left = jnp.maximum(qi - 2 - j_dead, 0)
    odd = n_left % 2
    has_dead_below = j_dead >= 0
    n_pairs = n_left // 2 + jnp.where(has_dead_below, odd, 0)

    def body(i, carry):
        j = qi - 2 - 2 * i
        process([j, j - 1], False)
        return carry

    lax.fori_loop(0, n_pairs, body, 0)

    @pl.when(jnp.logical_and(odd == 1, jnp.logical_not(has_dead_below)))
    def _():
        process([0], False)

    outs = []
    for h in range(2):
        acc = acc_ref[h]
        outs.append(acc[0:HEAD_DIM, :] / acc[HEAD_DIM:HEAD_DIM + 1, :])
    o_ref[0] = jnp.concatenate(outs, axis=0).T.astype(o_ref.dtype)


def _fox_attention(q, k, vt, f, qk_bound, *, tq):
    bsz, s, w = q.shape
    pair = 2 * LANES
    n_heads = w // LANES
    f_end = f[:, :, tq - 1::tq].reshape(-1)
    thr = (2.0 * qk_bound - EXP_ZERO).reshape(1).astype(F32)
    kern = functools.partial(_fox_kernel, tk=tq, n_heads=n_heads)
    grid_spec = pltpu.PrefetchScalarGridSpec(
        num_scalar_prefetch=2,
        grid=(bsz, w // pair, s // tq),
        in_specs=[
            pl.BlockSpec((1, tq, pair), lambda b, p, i, fe, th: (b, i, p)),
            pl.BlockSpec((1, s, pair), lambda b, p, i, fe, th: (b, 0, p)),
            pl.BlockSpec((1, pair, s), lambda b, p, i, fe, th: (b, p, 0)),
        ],
        out_specs=pl.BlockSpec((1, tq, LANES), lambda b, p, i, fe, th: (b, i, p)),
        scratch_shapes=[pltpu.VMEM((2, LANES, tq), F32), pltpu.VMEM((2, 1, tq), F32)],
    )
    return pl.pallas_call(
        kern,
        grid_spec=grid_spec,
        out_shape=jax.ShapeDtypeStruct((bsz, s, w // 2), BF16),
        compiler_params=pltpu.CompilerParams(
            dimension_semantics=("arbitrary", "arbitrary", "arbitrary"),
            vmem_limit_bytes=VMEM_LIMIT),
        name="fox_attn",
    )(f_end, thr, q, k, vt)


def _out_kernel(x_ref, osb_ref, ofox_ref, osgu_ref, mod_ref, g2_ref, wo_ref, w1_ref, w2_ref,
                o_ref, *, ff_chunk):
    sb_w = osb_ref.shape[2]
    fox_w = ofox_ref.shape[2]
    x = x_ref[0]
    mix = (_dot(osb_ref[0], wo_ref[0:sb_w, :])
           + _dot(ofox_ref[0], wo_ref[sb_w:sb_w + fox_w, :])
           + _dot(osgu_ref[0], wo_ref[sb_w + fox_w:, :]))
    x1 = x + mod_ref[0, 2:3, :] * mix
    ms = jnp.mean(x1 * x1, axis=-1, keepdims=True)
    h = x1 * lax.rsqrt(ms + EPS) * g2_ref[...]
    hb = (h * (1.0 + mod_ref[0, 4:5, :]) + mod_ref[0, 3:4, :]).astype(BF16)
    d_ff = w1_ref.shape[1]
    acc = None
    for c in range(d_ff // ff_chunk):
        hid = jnp.maximum(_dot(hb, w1_ref[:, c * ff_chunk:(c + 1) * ff_chunk]), 0.0)
        part = _dot((hid * hid).astype(BF16), w2_ref[c * ff_chunk:(c + 1) * ff_chunk, :])
        acc = part if acc is None else acc + part
    o_ref[0] = x1 + mod_ref[0, 5:6, :] * acc


def _out_mlp(x, osb, ofox, osgu, mod_l, g2, wo, w1, w2, *, tm):
    bsz, s, d = x.shape
    tok = lambda b, i: (b, i, 0)
    const2 = lambda b, i: (0, 0)
    single = pl.Buffered(1)
    kern = functools.partial(_out_kernel, ff_chunk=1024)
    return pl.pallas_call(
        kern,
        grid=(bsz, s // tm),
        in_specs=[
            pl.BlockSpec((1, tm, d), tok),
            pl.BlockSpec((1, tm, osb.shape[2]), tok),
            pl.BlockSpec((1, tm, ofox.shape[2]), tok),
            pl.BlockSpec((1, tm, osgu.shape[2]), tok),
            pl.BlockSpec((1, 6, d), lambda b, i: (b, 0, 0)),
            pl.BlockSpec((1, d), const2),
            pl.BlockSpec(wo.shape, const2, pipeline_mode=single),
            pl.BlockSpec(w1.shape, const2, pipeline_mode=single),
            pl.BlockSpec(w2.shape, const2, pipeline_mode=single),
        ],
        out_specs=pl.BlockSpec((1, tm, d), tok),
        out_shape=jax.ShapeDtypeStruct((bsz, s, d), F32),
        compiler_params=pltpu.CompilerParams(
            dimension_semantics=("arbitrary", "arbitrary"), vmem_limit_bytes=VMEM_LIMIT),
        name="out_mlp",
    )(x, osb, ofox, osgu, mod_l, g2, wo, w1, w2)


def kernel(x, c, ada_w, ada_b, norm1_g, norm2_g, w_in, b_forget, q_norm_g, k_norm_g, sgu_norm_g,
           sgu_w, sgu_b, w_out, mlp_w1, mlp_w2):
    depth, d, _ = ada_w.shape
    bsz, s, _ = x.shape
    fox_heads = b_forget.shape[1]
    fox_w = fox_heads * HEAD_DIM
    sgu_groups, chunk = sgu_b.shape[1], sgu_b.shape[2]
    sgu_wd = sgu_groups * sgu_norm_g.shape[2]
    sb_w = (w_in.shape[2] - 3 * fox_w - fox_heads - 2 * sgu_wd) // 3
    f_lo = 3 * sb_w + 3 * fox_w

    mod = _modulation(c, ada_w, ada_b).reshape(depth, bsz, 6, d)

    for l in range(depth):
        w = jnp.concatenate([w_in[l][:, :f_lo], w_in[l][:, f_lo + fox_heads:]], axis=1).astype(BF16)
        wf = jnp.pad(w_in[l][:, f_lo:f_lo + fox_heads], ((0, 0), (0, LANES - fox_heads)))
        wf_hi = wf.astype(BF16)
        wf_lo = (wf - wf_hi.astype(F32)).astype(BF16)
        bf = jnp.pad(b_forget[l], (0, LANES - fox_heads)).reshape(1, LANES)
        qg = jnp.tile(q_norm_g[l], LANES // HEAD_DIM).reshape(1, LANES)
        kg = jnp.tile(k_norm_g[l], LANES // HEAD_DIM).reshape(1, LANES)
        sg = sgu_norm_g[l].reshape(1, sgu_wd)
        sb = jnp.repeat(sgu_b[l].T, sgu_norm_g.shape[2], axis=1)

        qk_bound = (1.03 * HEAD_DIM ** 0.5) * jnp.max(jnp.abs(q_norm_g[l])) * jnp.max(jnp.abs(k_norm_g[l]))
        qa, ka, vat, qf, kf, vft, osgu, f = _in_proj(
            x, mod[l], norm1_g[l].reshape(1, d), w, wf_hi, wf_lo, bf, qg, kg, sg, sgu_w[l], sb,
            tm=512, sb_w=sb_w, fox_w=fox_w, sgu_w=sgu_wd)
        osb = _sb_attention(qa, ka, vat, tq=256)
        ofox = _fox_attention(qf, kf, vft, f, qk_bound, tq=256)
        x = _out_mlp(x, osb, ofox, osgu, mod[l], norm2_g[l].reshape(1, d),
                     w_out[l].astype(BF16), mlp_w1[l].astype(BF16), mlp_w2[l].astype(BF16), tm=512)
    return x
```

```python
import functools
import math

import jax
import jax.numpy as jnp
from jax import lax
from jax.experimental import pallas as pl
from jax.experimental.pallas import tpu as pltpu

HEAD_DIM = 64
LANES = 128
MXU_DIM = 256
EPS = 1e-6
NEG_BIG = -1e30
EXP_ZERO = -104.0
MAX_UNSTABILISED_LOGIT = 40.0
VMEM_LIMIT = 56 * 1024 * 1024

F32 = jnp.float32
BF16 = jnp.bfloat16


def _dot(a, b):
    return jnp.dot(a, b, preferred_element_type=F32)


def _dot_nt(a, b):
    return lax.dot_general(a, b, (((1,), (1,)), ((), ())), preferred_element_type=F32)


def _split2(x):
    hi = x.astype(BF16)
    lo = (x - hi.astype(F32)).astype(BF16)
    return hi, lo


def _split3(x):
    hi = x.astype(BF16)
    r = x - hi.astype(F32)
    mid = r.astype(BF16)
    lo = (r - mid.astype(F32)).astype(BF16)
    return hi, mid, lo


def _group_mean_matrix():
    r = lax.broadcasted_iota(jnp.int32, (MXU_DIM, MXU_DIM), 0) >> 6
    c = lax.broadcasted_iota(jnp.int32, (MXU_DIM, MXU_DIM), 1) >> 6
    return jnp.where(r == c, 1.0 / HEAD_DIM, 0.0).astype(BF16)


def _head_rmsnorm(t, gmat, gain):
    ms = _dot((t * t).astype(BF16), gmat)
    return t * lax.rsqrt(ms + EPS) * gain


def _gelu_tanh(x):
    c = math.sqrt(2.0 / math.pi)
    return x * (0.5 * (1.0 + jnp.tanh(c * (x + 0.044715 * (x * x * x)))))


def _log_sigmoid(x):
    return jnp.minimum(x, 0.0) - jnp.log(1.0 + jnp.exp(-jnp.abs(x)))


def _mod_kernel(ct_ref, w_ref, b_ref, o_ref):
    ct = ct_ref[...]
    cond = ct * (1.0 / (1.0 + jnp.exp(-ct)))
    w = w_ref[0]
    rows = [jnp.sum(cond[:, b:b + 1] * w, axis=0, keepdims=True) for b in range(ct.shape[1])]
    o_ref[0] = jnp.concatenate(rows, axis=0) + b_ref[0]


def _modulation(c, ada_w, ada_b):
    depth, d, n = ada_w.shape
    bsz = c.shape[0]
    tn = 1536
    return pl.pallas_call(
        _mod_kernel,
        grid=(depth, n // tn),
        in_specs=[
            pl.BlockSpec((d, bsz), lambda l, j: (0, 0)),
            pl.BlockSpec((1, d, tn), lambda l, j: (l, 0, j)),
            pl.BlockSpec((1, 1, tn), lambda l, j: (l, 0, j)),
        ],
        out_specs=pl.BlockSpec((1, bsz, tn), lambda l, j: (l, 0, j)),
        out_shape=jax.ShapeDtypeStruct((depth, bsz, n), F32),
        compiler_params=pltpu.CompilerParams(
            dimension_semantics=("arbitrary", "arbitrary"), vmem_limit_bytes=VMEM_LIMIT),
        name="adaln_mod",
    )(c.T, ada_w, ada_b.reshape(depth, 1, n))


def _in_kernel(x_ref, mod_ref, g1_ref, w_ref, wf_ref, bf_ref, qx_ref, qg_ref, kg_ref,
               sg_ref, sw_ref, sb_ref,
               qa_ref, ka_ref, vat_ref, qf_ref, kf_ref, vft_ref, og_ref, f_ref,
               carry_ref, *, sb_w, fox_w, sgu_w, chunk):
    tm = x_ref.shape[1]
    x = x_ref[0]
    ms = jnp.mean(x * x, axis=-1, keepdims=True)
    h = x * lax.rsqrt(ms + EPS) * g1_ref[...]
    h = h * (1.0 + mod_ref[0, 1:2, :]) + mod_ref[0, 0:1, :]
    hb = h.astype(BF16)
    scale = HEAD_DIM ** -0.5

    o = 0
    qa_ref[0] = (_dot(hb, w_ref[:, o:o + sb_w]) * scale).astype(BF16); o += sb_w
    ka_ref[0] = _dot(hb, w_ref[:, o:o + sb_w]).astype(BF16); o += sb_w
    vat_ref[0] = _dot(hb, w_ref[:, o:o + sb_w]).T.astype(BF16); o += sb_w

    @pl.when(pl.program_id(1) == 0)
    def _():
        carry_ref[...] = jnp.zeros_like(carry_ref)

    n_heads = f_ref.shape[1]
    lane = lax.broadcasted_iota(jnp.int32, (1, LANES), 1)
    head_lane = lane < n_heads

    def pack3(parts):
        a, b, c = (p.astype(F32) for p in parts)
        return (a + pltpu.roll(b, 8, 1) + pltpu.roll(c, 16, 1)).astype(BF16)

    h_hi, h_lo = _split2(h)
    t2 = _dot(h_hi, wf_ref[...])
    fl = t2[:, :LANES] + t2[:, LANES:] + _dot(h_lo, wf_ref[:, :LANES])
    logf = jnp.where(head_lane, _log_sigmoid(fl + bf_ref[...]), 0.0)
    r = lax.broadcasted_iota(jnp.int32, (tm, tm), 0)
    cc = lax.broadcasted_iota(jnp.int32, (tm, tm), 1)
    ltri = jnp.where(r >= cc, 1.0, 0.0).astype(BF16)
    cp = _dot(ltri, pack3(_split3(logf)))
    cum = cp + pltpu.roll(cp, LANES - 8, 1) + pltpu.roll(cp, LANES - 16, 1)
    cum = jnp.where(head_lane, cum, 0.0) + carry_ref[0:1, :]
    carry_ref[0:1, :] = cum[tm - 1:tm, :]
    f_ref[0] = cum.T[0:n_heads, :]
    er = lax.broadcasted_iota(jnp.int32, (LANES, 2 * n_heads * LANES), 0)
    ec = lax.broadcasted_iota(jnp.int32, (LANES, 2 * n_heads * LANES), 1)
    tile = ec >> 7
    is_q = jnp.where(tile >= n_heads, 1, 0)
    part = (ec & (LANES - 1)) - HEAD_DIM - 3 * is_q
    sel = jnp.where((part >= 0) & (part < 3) & (er == part * 8 + tile - n_heads * is_q), 1.0, 0.0)
    extras = _dot(pack3(_split3(cum)), sel.astype(BF16))

    gmat = _group_mean_matrix()
    is_head = lane < HEAD_DIM
    q_extra = qx_ref[...]
    k_extra = jnp.where((lane >= HEAD_DIM + 3) & (lane < HEAD_DIM + 7), 1.0, 0.0)
    heads_per_mxu = MXU_DIM // HEAD_DIM

    def head_tile(tn, s):
        half = tn[:, (s // 2) * LANES:(s // 2 + 1) * LANES]
        return half if s % 2 == 0 else pltpu.roll(half, HEAD_DIM, 1)

    for j in range(fox_w // MXU_DIM):
        t = _dot(hb, w_ref[:, o + j * MXU_DIM:o + (j + 1) * MXU_DIM])
        qn = _head_rmsnorm(t, gmat, qg_ref[...]) * scale
        for s in range(heads_per_mxu):
            hh = heads_per_mxu * j + s
            xq = extras[:, (n_heads + hh) * LANES:(n_heads + hh + 1) * LANES] + q_extra
            qf_ref[0, :, hh * LANES:(hh + 1) * LANES] = jnp.where(
                is_head, head_tile(qn, s), xq).astype(BF16)
    o += fox_w
    for j in range(fox_w // MXU_DIM):
        t = _dot(hb, w_ref[:, o + j * MXU_DIM:o + (j + 1) * MXU_DIM])
        kn = _head_rmsnorm(t, gmat, kg_ref[...])
        for s in range(heads_per_mxu):
            hh = heads_per_mxu * j + s
            xk = extras[:, hh * LANES:(hh + 1) * LANES] + k_extra
            kf_ref[0, :, hh * LANES:(hh + 1) * LANES] = jnp.where(
                is_head, head_tile(kn, s), xk).astype(BF16)
    o += fox_w
    vt = _dot(hb, w_ref[:, o:o + fox_w]).T.astype(BF16); o += fox_w
    ones = jnp.ones((HEAD_DIM, tm), BF16)
    for hh in range(fox_w // HEAD_DIM):
        vft_ref[0, hh * LANES:hh * LANES + HEAD_DIM, :] = vt[hh * HEAD_DIM:(hh + 1) * HEAD_DIM, :]
        vft_ref[0, hh * LANES + HEAD_DIM:(hh + 1) * LANES, :] = ones

    gu = _gelu_tanh(_dot(hb, w_ref[:, o:o + sgu_w])); o += sgu_w
    gv = _gelu_tanh(_dot(hb, w_ref[:, o:o + sgu_w])); o += sgu_w
    lane_c = lax.broadcasted_iota(jnp.int32, (chunk, LANES), 1)
    rr = lax.broadcasted_iota(jnp.int32, (chunk, chunk), 0)
    cs = lax.broadcasted_iota(jnp.int32, (chunk, chunk), 1)
    wt = [jnp.where(rr >= cs, sw_ref[g], 0.0).astype(BF16) for g in range(sw_ref.shape[0])]
    vn_all = _head_rmsnorm(gv, gmat, sg_ref[...]).astype(BF16)
    for p in range(sgu_w // LANES):
        vn = vn_all[:, p * LANES:(p + 1) * LANES]
        for ci in range(tm // chunk):
            vblk = vn[ci * chunk:(ci + 1) * chunk, :]
            mixed = jnp.where(lane_c < HEAD_DIM, _dot(wt[2 * p], vblk), _dot(wt[2 * p + 1], vblk))
            mixed = mixed + sb_ref[:, p * LANES:(p + 1) * LANES]
            og_ref[0, ci * chunk:(ci + 1) * chunk, p * LANES:(p + 1) * LANES] = (
                gu[ci * chunk:(ci + 1) * chunk, p * LANES:(p + 1) * LANES] * mixed).astype(BF16)


def _in_proj(x, mod_l, g1, w, wf, bf, qx, qg, kg, sg, sw, sb, *, tm, sb_w, fox_w, sgu_w):
    bsz, s, d = x.shape
    chunk = sw.shape[-1]
    wn = w.shape[1]
    fox_heads = fox_w // HEAD_DIM
    assert fox_heads <= 8 and fox_w % MXU_DIM == 0 and sgu_w == MXU_DIM
    const2 = lambda b, i: (0, 0)
    tok = lambda b, i: (b, i, 0)
    tok_t = lambda b, i: (b, 0, i)
    kern = functools.partial(_in_kernel, sb_w=sb_w, fox_w=fox_w, sgu_w=sgu_w, chunk=chunk)
    out_shape = [
        jax.ShapeDtypeStruct((bsz, s, sb_w), BF16),
        jax.ShapeDtypeStruct((bsz, s, sb_w), BF16),
        jax.ShapeDtypeStruct((bsz, sb_w, s), BF16),
        jax.ShapeDtypeStruct((bsz, s, fox_heads * LANES), BF16),
        jax.ShapeDtypeStruct((bsz, s, fox_heads * LANES), BF16),
        jax.ShapeDtypeStruct((bsz, fox_heads * LANES, s), BF16),
        jax.ShapeDtypeStruct((bsz, s, sgu_w), BF16),
        jax.ShapeDtypeStruct((bsz, fox_heads, s), F32),
    ]
    out_specs = [
        pl.BlockSpec((1, tm, sb_w), tok),
        pl.BlockSpec((1, tm, sb_w), tok),
        pl.BlockSpec((1, sb_w, tm), tok_t),
        pl.BlockSpec((1, tm, fox_heads * LANES), tok),
        pl.BlockSpec((1, tm, fox_heads * LANES), tok),
        pl.BlockSpec((1, fox_heads * LANES, tm), tok_t),
        pl.BlockSpec((1, tm, sgu_w), tok),
        pl.BlockSpec((1, fox_heads, tm), tok_t),
    ]
    return pl.pallas_call(
        kern,
        grid=(bsz, s // tm),
        in_specs=[
            pl.BlockSpec((1, tm, d), tok),
            pl.BlockSpec((1, 6, d), lambda b, i: (b, 0, 0)),
            pl.BlockSpec((1, d), const2),
            pl.BlockSpec((d, wn), const2),
            pl.BlockSpec((d, 2 * LANES), const2),
            pl.BlockSpec((1, LANES), const2),
            pl.BlockSpec((1, LANES), const2),
            pl.BlockSpec((1, MXU_DIM), const2),
            pl.BlockSpec((1, MXU_DIM), const2),
            pl.BlockSpec((1, sgu_w), const2),
            pl.BlockSpec(sw.shape, lambda b, i: (0, 0, 0)),
            pl.BlockSpec((chunk, sgu_w), const2),
        ],
        out_specs=out_specs,
        out_shape=out_shape,
        scratch_shapes=[pltpu.VMEM((8, LANES), F32)],
        compiler_params=pltpu.CompilerParams(
            dimension_semantics=("arbitrary", "arbitrary"), vmem_limit_bytes=VMEM_LIMIT),
        name="in_proj",
    )(x, mod_l, g1, w, wf, bf, qx, qg, kg, sg, sw, sb)


def _sb_kernel(q_ref, k_ref, vt_ref, o_ref, acc_ref, c_ref, *, tk):
    tq = q_ref.shape[1]
    qi = pl.program_id(2)
    q = q_ref[0]
    lane = lax.broadcasted_iota(jnp.int32, (1, LANES), 1)
    qh = (jnp.where(lane < HEAD_DIM, q, 0).astype(BF16), jnp.where(lane >= HEAD_DIM, q, 0).astype(BF16))
    key = lax.broadcasted_iota(jnp.int32, (tk, tq), 0)
    qry = lax.broadcasted_iota(jnp.int32, (tk, tq), 1)
    causal = key < qry
    js = lax.broadcasted_iota(jnp.int32, (tk, tk), 0)
    jr = lax.broadcasted_iota(jnp.int32, (tk, tk), 1)
    suffix = jnp.where(jr >= js, 1.0, 0.0).astype(BF16)

    def process(blocks, first):
        starts = [pl.multiple_of(j * tk, tk) for j in blocks]
        units = [(b, h) for b in range(len(blocks)) for h in range(2)]
        z = {(b, h): _dot_nt(k_ref[0, pl.ds(starts[b], tk), :], qh[h]) for b, h in units}
        split = {}
        for b, h in units:
            zz = z[b, h]
            l1mb = -(jnp.maximum(zz, 0.0) + jnp.log(1.0 + jnp.exp(-jnp.abs(zz))))
            if first and b == 0:
                l1mb = jnp.where(causal, l1mb, 0.0)
            split[b, h] = _split2(l1mb)
        sfx = {u: _dot(suffix, split[u][0]) + _dot(suffix, split[u][1]) for u in units}
        a = {}
        for h in range(2):
            c = None if first else c_ref[h]
            for b in range(len(blocks)):
                e = z[b, h] + sfx[b, h]
                if c is not None:
                    e = e + c
                w = jnp.exp(e)
                if first and b == 0:
                    w = jnp.where(causal, w, 0.0)
                a[b, h] = w.astype(BF16)
                c = sfx[b, h][0:1, :] if c is None else c + sfx[b, h][0:1, :]
            c_ref[h] = c
        for h in range(2):
            pv = None
            for b in range(len(blocks)):
                part = _dot(vt_ref[0, :, pl.ds(starts[b], tk)], a[b, h])
                pv = part if pv is None else pv + part
            if first:
                acc_ref[h] = pv
            else:
                acc_ref[h] += pv

    def carry_max():
        cm = jnp.maximum(jnp.max(c_ref[0], axis=1, keepdims=True),
                         jnp.max(c_ref[1], axis=1, keepdims=True))
        return cm[0, 0]

    @pl.when(qi == 0)
    def _():
        process([qi], True)

    @pl.when(qi > 0)
    def _():
        process([qi, qi - 1], True)

    def cond(carry):
        j, cmax = carry
        return jnp.logical_and(j >= 0, cmax > EXP_ZERO)

    def body(carry):
        j, _ = carry
        process([j], False)
        return j - 1, carry_max()

    lax.while_loop(cond, body, (qi - 2, carry_max()))
    chan = lax.broadcasted_iota(jnp.int32, (LANES, tq), 0)
    o_ref[0] = jnp.where(chan < HEAD_DIM, acc_ref[0], acc_ref[1]).T.astype(o_ref.dtype)


def _sb_attention(q, k, vt, *, tq):
    bsz, s, w = q.shape
    kern = functools.partial(_sb_kernel, tk=tq)
    return pl.pallas_call(
        kern,
        grid=(bsz, w // LANES, s // tq),
        in_specs=[
            pl.BlockSpec((1, tq, LANES), lambda b, p, i: (b, i, p)),
            pl.BlockSpec((1, s, LANES), lambda b, p, i: (b, 0, p)),
            pl.BlockSpec((1, LANES, s), lambda b, p, i: (b, p, 0)),
        ],
        out_specs=pl.BlockSpec((1, tq, LANES), lambda b, p, i: (b, i, p)),
        out_shape=jax.ShapeDtypeStruct((bsz, s, w), BF16),
        scratch_shapes=[pltpu.VMEM((2, LANES, tq), F32), pltpu.VMEM((2, 1, tq), F32)],
        compiler_params=pltpu.CompilerParams(
            dimension_semantics=("arbitrary", "arbitrary", "arbitrary"),
            vmem_limit_bytes=VMEM_LIMIT),
        name="sb_attn",
    )(q, k, vt)


def _fox_kernel(fend_ref, par_ref, q_ref, k_ref, vt_ref, o_ref, acc_ref, m_ref, p_ref, *, tk, n_heads):
    tq = q_ref.shape[1]
    qi = pl.program_id(2)
    n_blk = pl.num_programs(2)

    def first_dead_block(head):
        base = head * n_blk
        f_q = fend_ref[base + jnp.maximum(qi - 1, 0)]

        def alive(j):
            jc = jnp.maximum(j, 0)
            return jnp.logical_and(j >= 0, f_q - fend_ref[base + jc] >= -par_ref[0])

        return lax.while_loop(alive, lambda j: j - 1, qi - 1)

    head0 = (pl.program_id(0) * n_heads + 2 * pl.program_id(1))
    j_dead = jnp.minimum(first_dead_block(head0), first_dead_block(head0 + 1))
    key = lax.broadcasted_iota(jnp.int32, (tk, tq), 0)
    qry = lax.broadcasted_iota(jnp.int32, (tk, tq), 1)
    causal = key <= qry

    def process(blocks, first):
        starts = [pl.multiple_of(j * tk, tk) for j in blocks]
        units = [(b, h) for b in range(len(blocks)) for h in range(2)]
        s = {(b, h): _dot_nt(k_ref[0, pl.ds(starts[b], tk), h * LANES:(h + 1) * LANES],
                             q_ref[0, :, h * LANES:(h + 1) * LANES]) for b, h in units}
        p = {}
        alpha = {}
        for h in range(2):
            if first:
                s[0, h] = jnp.where(causal, s[0, h], NEG_BIG)
            m_new = None if first else m_ref[h]
            for b in range(len(blocks)):
                mb = jnp.max(s[b, h], axis=0, keepdims=True)
                m_new = mb if m_new is None else jnp.maximum(m_new, mb)
            if not first:
                alpha[h] = jnp.exp(m_ref[h] - m_new)
            for b in range(len(blocks)):
                p[b, h] = jnp.exp(s[b, h] - m_new).astype(BF16)
            m_ref[h] = m_new
        for h in range(2):
            pv = None
            for b in range(len(blocks)):
                part = _dot(vt_ref[0, h * LANES:(h + 1) * LANES, pl.ds(starts[b], tk)], p[b, h])
                pv = part if pv is None else pv + part
            acc_ref[h] = pv if first else alpha[h] * acc_ref[h] + pv

    n_left = jnp.maximum(qi - 2 - j_dead, 0)
    odd = n_left % 2
    has_dead_below = j_dead >= 0
    n_pairs = n_left // 2 + jnp.where(has_dead_below, odd, 0)
    lone_block0 = jnp.logical_and(odd == 1, jnp.logical_not(has_dead_below))

    def online_path():
        @pl.when(qi == 0)
        def _():
            process([qi], True)

        @pl.when(qi > 0)
        def _():
            process([qi, qi - 1], True)

        def body(i, carry):
            j = qi - 2 - 2 * i
            process([j, j - 1], False)
            return carry

        lax.fori_loop(0, n_pairs, body, 0)

        @pl.when(lone_block0)
        def _():
            process([0], False)

    win = 2 * tk
    wrow = lax.broadcasted_iota(jnp.int32, (win, tq), 0)
    wcol = lax.broadcasted_iota(jnp.int32, (win, tq), 1)

    def window(i):
        jl = qi - 1 - 2 * i
        j_lo = jnp.maximum(jl, 0)
        return j_lo, pl.multiple_of(j_lo * tk, tk), jnp.where(jl < 0, tk, win)

    def stage_a(i, first):
        j_lo, start, row_lim = window(i)
        keep = wrow < row_lim
        if first:
            keep = jnp.logical_and(keep, wrow <= wcol + (qi - j_lo) * tk)
        slot = i & 1
        for h in range(2):
            s = _dot_nt(k_ref[0, pl.ds(start, win), h * LANES:(h + 1) * LANES],
                        q_ref[0, :, h * LANES:(h + 1) * LANES])
            p_ref[slot, h] = jnp.exp(jnp.where(keep, s, NEG_BIG)).astype(BF16)

    def stage_c(i):
        _, start, _ = window(i)
        slot = i & 1
        for h in range(2):
            acc_ref[h] += _dot(vt_ref[0, h * LANES:(h + 1) * LANES, pl.ds(start, win)], p_ref[slot, h])

    def bounded_path():
        acc_ref[...] = jnp.zeros_like(acc_ref)
        n_stages = 1 + n_pairs + jnp.where(lone_block0, 1, 0)
        stage_a(0, True)

        def body(i, carry):
            stage_c(i - 1)
            stage_a(i, False)
            return carry

        lax.fori_loop(1, n_stages, body, 0)
        stage_c(n_stages - 1)

    bounded = par_ref[1] > 0.5
    pl.when(bounded)(bounded_path)
    pl.when(jnp.logical_not(bounded))(online_path)

    outs = []
    for h in range(2):
        acc = acc_ref[h]
        outs.append(acc[0:HEAD_DIM, :] / acc[HEAD_DIM:HEAD_DIM + 1, :])
    o_ref[0] = jnp.concatenate(outs, axis=0).T.astype(o_ref.dtype)


def _fox_attention(q, k, vt, f, qk_bound, *, tq):
    bsz, s, w = q.shape
    pair = 2 * LANES
    n_heads = w // LANES
    assert s >= 2 * tq
    f_end = f[:, :, tq - 1::tq].reshape(-1)
    par = jnp.stack([2.0 * qk_bound - EXP_ZERO,
                     jnp.where(qk_bound <= MAX_UNSTABILISED_LOGIT, 1.0, 0.0)]).astype(F32)
    kern = functools.partial(_fox_kernel, tk=tq, n_heads=n_heads)
    grid_spec = pltpu.PrefetchScalarGridSpec(
        num_scalar_prefetch=2,
        grid=(bsz, w // pair, s // tq),
        in_specs=[
            pl.BlockSpec((1, tq, pair), lambda b, p, i, fe, th: (b, i, p)),
            pl.BlockSpec((1, s, pair), lambda b, p, i, fe, th: (b, 0, p)),
            pl.BlockSpec((1, pair, s), lambda b, p, i, fe, th: (b, p, 0)),
        ],
        out_specs=pl.BlockSpec((1, tq, LANES), lambda b, p, i, fe, th: (b, i, p)),
        scratch_shapes=[pltpu.VMEM((2, LANES, tq), F32), pltpu.VMEM((2, 1, tq), F32),
                        pltpu.VMEM((2, 2, 2 * tq, tq), BF16)],
    )
    return pl.pallas_call(
        kern,
        grid_spec=grid_spec,
        out_shape=jax.ShapeDtypeStruct((bsz, s, w // 2), BF16),
        compiler_params=pltpu.CompilerParams(
            dimension_semantics=("arbitrary", "arbitrary", "arbitrary"),
            vmem_limit_bytes=VMEM_LIMIT),
        name="fox_attn",
    )(f_end, par, q, k, vt)


def _out_kernel(x_ref, osb_ref, ofox_ref, osgu_ref, mod_ref, g2_ref, wo_ref, w1_ref, w2_ref,
                o_ref, *, ff_chunk):
    sb_w = osb_ref.shape[2]
    fox_w = ofox_ref.shape[2]
    x = x_ref[0]
    mix = (_dot(osb_ref[0], wo_ref[0:sb_w, :])
           + _dot(ofox_ref[0], wo_ref[sb_w:sb_w + fox_w, :])
           + _dot(osgu_ref[0], wo_ref[sb_w + fox_w:, :]))
    x1 = x + mod_ref[0, 2:3, :] * mix
    ms = jnp.mean(x1 * x1, axis=-1, keepdims=True)
    h = x1 * lax.rsqrt(ms + EPS) * g2_ref[...]
    hb = (h * (1.0 + mod_ref[0, 4:5, :]) + mod_ref[0, 3:4, :]).astype(BF16)
    d_ff = w1_ref.shape[1]
    acc = None
    for c in range(d_ff // ff_chunk):
        hid = jnp.maximum(_dot(hb, w1_ref[:, c * ff_chunk:(c + 1) * ff_chunk]), 0.0)
        part = _dot((hid * hid).astype(BF16), w2_ref[c * ff_chunk:(c + 1) * ff_chunk, :])
        acc = part if acc is None else acc + part
    o_ref[0] = x1 + mod_ref[0, 5:6, :] * acc


def _out_mlp(x, osb, ofox, osgu, mod_l, g2, wo, w1, w2, *, tm):
    bsz, s, d = x.shape
    tok = lambda b, i: (b, i, 0)
    const2 = lambda b, i: (0, 0)
    single = pl.Buffered(1)
    kern = functools.partial(_out_kernel, ff_chunk=1024)
    return pl.pallas_call(
        kern,
        grid=(bsz, s // tm),
        in_specs=[
            pl.BlockSpec((1, tm, d), tok),
            pl.BlockSpec((1, tm, osb.shape[2]), tok),
            pl.BlockSpec((1, tm, ofox.shape[2]), tok),
            pl.BlockSpec((1, tm, osgu.shape[2]), tok),
            pl.BlockSpec((1, 6, d), lambda b, i: (b, 0, 0)),
            pl.BlockSpec((1, d), const2),
            pl.BlockSpec(wo.shape, const2, pipeline_mode=single),
            pl.BlockSpec(w1.shape, const2, pipeline_mode=single),
            pl.BlockSpec(w2.shape, const2, pipeline_mode=single),
        ],
        out_specs=pl.BlockSpec((1, tm, d), tok),
        out_shape=jax.ShapeDtypeStruct((bsz, s, d), F32),
        compiler_params=pltpu.CompilerParams(
            dimension_semantics=("arbitrary", "arbitrary"), vmem_limit_bytes=VMEM_LIMIT),
        name="out_mlp",
    )(x, osb, ofox, osgu, mod_l, g2, wo, w1, w2)


def kernel(x, c, ada_w, ada_b, norm1_g, norm2_g, w_in, b_forget, q_norm_g, k_norm_g, sgu_norm_g,
           sgu_w, sgu_b, w_out, mlp_w1, mlp_w2):
    depth, d, _ = ada_w.shape
    bsz, s, _ = x.shape
    fox_heads = b_forget.shape[1]
    fox_w = fox_heads * HEAD_DIM
    sgu_groups, chunk = sgu_b.shape[1], sgu_b.shape[2]
    sgu_wd = sgu_groups * sgu_norm_g.shape[2]
    sb_w = (w_in.shape[2] - 3 * fox_w - fox_heads - 2 * sgu_wd) // 3
    f_lo = 3 * sb_w + 3 * fox_w

    mod = _modulation(c, ada_w, ada_b).reshape(depth, bsz, 6, d)

    for l in range(depth):
        w = jnp.concatenate([w_in[l][:, :f_lo], w_in[l][:, f_lo + fox_heads:]], axis=1).astype(BF16)
        wf = jnp.pad(w_in[l][:, f_lo:f_lo + fox_heads], ((0, 0), (0, LANES - fox_heads)))
        wf_hi = wf.astype(BF16)
        wf = jnp.concatenate([wf_hi, (wf - wf_hi.astype(F32)).astype(BF16)], axis=1)
        bf = jnp.pad(b_forget[l], (0, LANES - fox_heads)).reshape(1, LANES)
        qg = jnp.tile(q_norm_g[l], MXU_DIM // HEAD_DIM).reshape(1, MXU_DIM)
        kg = jnp.tile(k_norm_g[l], MXU_DIM // HEAD_DIM).reshape(1, MXU_DIM)
        sg = sgu_norm_g[l].reshape(1, sgu_wd)
        sb = jnp.repeat(sgu_b[l].T, sgu_norm_g.shape[2], axis=1)

        qk_bound = (1.03 * HEAD_DIM ** 0.5) * jnp.max(jnp.abs(q_norm_g[l])) * jnp.max(jnp.abs(k_norm_g[l]))
        qx = jnp.zeros((LANES,), F32).at[HEAD_DIM:HEAD_DIM + 3].set(-1.0).at[HEAD_DIM + 6].set(-qk_bound)
        qa, ka, vat, qf, kf, vft, osgu, f = _in_proj(
            x, mod[l], norm1_g[l].reshape(1, d), w, wf, bf, qx.reshape(1, LANES), qg, kg, sg, sgu_w[l], sb,
            tm=512, sb_w=sb_w, fox_w=fox_w, sgu_w=sgu_wd)
        osb = _sb_attention(qa, ka, vat, tq=256)
        ofox = _fox_attention(qf, kf, vft, f, qk_bound, tq=256)
        x = _out_mlp(x, osb, ofox, osgu, mod[l], norm2_g[l].reshape(1, d),
                     w_out[l].astype(BF16), mlp_w1[l].astype(BF16), mlp_w2[l].astype(BF16), tm=512)
    return x
```

```python
import functools
import math

import jax
import jax.numpy as jnp
from jax import lax
from jax.experimental import pallas as pl
from jax.experimental.pallas import tpu as pltpu

HEAD_DIM = 64
LANES = 128
MXU_DIM = 256
EPS = 1e-6
NEG_BIG = -1e30
EXP_ZERO = -104.0
MAX_UNSTABILISED_LOGIT = 40.0
VMEM_LIMIT = 56 * 1024 * 1024

F32 = jnp.float32
BF16 = jnp.bfloat16


def _dot(a, b):
    return jnp.dot(a, b, preferred_element_type=F32)


def _dot_nt(a, b):
    return lax.dot_general(a, b, (((1,), (1,)), ((), ())), preferred_element_type=F32)


def _split2(x):
    hi = x.astype(BF16)
    lo = (x - hi.astype(F32)).astype(BF16)
    return hi, lo


def _split3(x):
    hi = x.astype(BF16)
    r = x - hi.astype(F32)
    mid = r.astype(BF16)
    lo = (r - mid.astype(F32)).astype(BF16)
    return hi, mid, lo


def _group_mean_matrix():
    r = lax.broadcasted_iota(jnp.int32, (MXU_DIM, MXU_DIM), 0) >> 6
    c = lax.broadcasted_iota(jnp.int32, (MXU_DIM, MXU_DIM), 1) >> 6
    return jnp.where(r == c, 1.0 / HEAD_DIM, 0.0).astype(BF16)


def _head_rmsnorm(t, gmat, gain):
    ms = _dot((t * t).astype(BF16), gmat)
    return t * lax.rsqrt(ms + EPS) * gain


def _gelu_tanh(x):
    c = math.sqrt(2.0 / math.pi)
    return x * (0.5 * (1.0 + jnp.tanh(c * (x + 0.044715 * (x * x * x)))))


def _log_sigmoid(x):
    return jnp.minimum(x, 0.0) - jnp.log(1.0 + jnp.exp(-jnp.abs(x)))


def _mod_kernel(ct_ref, w_ref, b_ref, o_ref):
    ct = ct_ref[...]
    cond = ct * (1.0 / (1.0 + jnp.exp(-ct)))
    w = w_ref[0]
    rows = [jnp.sum(cond[:, b:b + 1] * w, axis=0, keepdims=True) for b in range(ct.shape[1])]
    o_ref[0] = jnp.concatenate(rows, axis=0) + b_ref[0]


def _modulation(c, ada_w, ada_b):
    depth, d, n = ada_w.shape
    bsz = c.shape[0]
    tn = 1536
    return pl.pallas_call(
        _mod_kernel,
        grid=(depth, n // tn),
        in_specs=[
            pl.BlockSpec((d, bsz), lambda l, j: (0, 0)),
            pl.BlockSpec((1, d, tn), lambda l, j: (l, 0, j)),
            pl.BlockSpec((1, 1, tn), lambda l, j: (l, 0, j)),
        ],
        out_specs=pl.BlockSpec((1, bsz, tn), lambda l, j: (l, 0, j)),
        out_shape=jax.ShapeDtypeStruct((depth, bsz, n), F32),
        compiler_params=pltpu.CompilerParams(
            dimension_semantics=("arbitrary", "arbitrary"), vmem_limit_bytes=VMEM_LIMIT),
        name="adaln_mod",
    )(c.T, ada_w, ada_b.reshape(depth, 1, n))


def _in_kernel(x_ref, mod_ref, g1_ref, w_ref, wf_ref, bf_ref, qx_ref, qg_ref, kg_ref,
               sg_ref, sw_ref, sb_ref,
               qa_ref, ka_ref, vat_ref, qf_ref, kf_ref, vft_ref, og_ref, f_ref,
               carry_ref, proj_ref, *, sb_w, fox_w, sgu_w, chunk):
    @pl.when(pl.program_id(1) == 0)
    def _():
        carry_ref[...] = jnp.zeros_like(carry_ref)

    tm = x_ref.shape[1]
    x = x_ref[0]
    ms = jnp.mean(x * x, axis=-1, keepdims=True)
    h = x * lax.rsqrt(ms + EPS) * g1_ref[...]
    h = h * (1.0 + mod_ref[0, 1:2, :]) + mod_ref[0, 0:1, :]
    hb = h.astype(BF16)
    scale = HEAD_DIM ** -0.5
    proj_ref[...] = _dot(hb, w_ref[...])

    o = 0
    qa_ref[0] = (proj_ref[:, o:o + sb_w] * scale).astype(BF16); o += sb_w
    ka_ref[0] = proj_ref[:, o:o + sb_w].astype(BF16); o += sb_w
    vat_ref[0] = proj_ref[:, o:o + sb_w].T.astype(BF16); o += sb_w

    n_heads = f_ref.shape[1]
    lane = lax.broadcasted_iota(jnp.int32, (1, LANES), 1)
    head_lane = lane < n_heads

    def pack3(parts):
        a, b, c = (p.astype(F32) for p in parts)
        return (a + pltpu.roll(b, 8, 1) + pltpu.roll(c, 16, 1)).astype(BF16)

    h_hi, h_lo = _split2(h)
    t2 = _dot(h_hi, wf_ref[...])
    fl = t2[:, :LANES] + t2[:, LANES:] + _dot(h_lo, wf_ref[:, :LANES])
    logf = jnp.where(head_lane, _log_sigmoid(fl + bf_ref[...]), 0.0)
    r = lax.broadcasted_iota(jnp.int32, (tm, tm), 0)
    cc = lax.broadcasted_iota(jnp.int32, (tm, tm), 1)
    ltri = jnp.where(r >= cc, 1.0, 0.0).astype(BF16)
    cp = _dot(ltri, pack3(_split3(logf)))
    cum = cp + pltpu.roll(cp, LANES - 8, 1) + pltpu.roll(cp, LANES - 16, 1)
    cum = jnp.where(head_lane, cum, 0.0) + carry_ref[0:1, :]
    carry_ref[0:1, :] = cum[tm - 1:tm, :]
    f_ref[0] = cum.T[0:n_heads, :]
    er = lax.broadcasted_iota(jnp.int32, (LANES, 2 * n_heads * LANES), 0)
    ec = lax.broadcasted_iota(jnp.int32, (LANES, 2 * n_heads * LANES), 1)
    tile = ec >> 7
    is_q = jnp.where(tile >= n_heads, 1, 0)
    part = (ec & (LANES - 1)) - HEAD_DIM - 3 * is_q
    sel = jnp.where((part >= 0) & (part < 3) & (er == part * 8 + tile - n_heads * is_q), 1.0, 0.0)
    extras = _dot(pack3(_split3(cum)), sel.astype(BF16))

    gmat = _group_mean_matrix()
    is_head = lane < HEAD_DIM
    q_extra = qx_ref[...]
    k_extra = jnp.where((lane >= HEAD_DIM + 3) & (lane < HEAD_DIM + 7), 1.0, 0.0)
    heads_per_mxu = MXU_DIM // HEAD_DIM

    def head_tile(tn, s):
        half = tn[:, (s // 2) * LANES:(s // 2 + 1) * LANES]
        return half if s % 2 == 0 else pltpu.roll(half, HEAD_DIM, 1)

    for j in range(fox_w // MXU_DIM):
        t = proj_ref[:, o + j * MXU_DIM:o + (j + 1) * MXU_DIM]
        qn = _head_rmsnorm(t, gmat, qg_ref[...]) * scale
        for s in range(heads_per_mxu):
            hh = heads_per_mxu * j + s
            xq = extras[:, (n_heads + hh) * LANES:(n_heads + hh + 1) * LANES] + q_extra
            qf_ref[0, :, hh * LANES:(hh + 1) * LANES] = jnp.where(
                is_head, head_tile(qn, s), xq).astype(BF16)
    o += fox_w
    for j in range(fox_w // MXU_DIM):
        t = proj_ref[:, o + j * MXU_DIM:o + (j + 1) * MXU_DIM]
        kn = _head_rmsnorm(t, gmat, kg_ref[...])
        for s in range(heads_per_mxu):
            hh = heads_per_mxu * j + s
            xk = extras[:, hh * LANES:(hh + 1) * LANES] + k_extra
            kf_ref[0, :, hh * LANES:(hh + 1) * LANES] = jnp.where(
                is_head, head_tile(kn, s), xk).astype(BF16)
    o += fox_w
    vt = proj_ref[:, o:o + fox_w].T.astype(BF16); o += fox_w
    ones = jnp.ones((HEAD_DIM, tm), BF16)
    for hh in range(fox_w // HEAD_DIM):
        vft_ref[0, hh * LANES:hh * LANES + HEAD_DIM, :] = vt[hh * HEAD_DIM:(hh + 1) * HEAD_DIM, :]
        vft_ref[0, hh * LANES + HEAD_DIM:(hh + 1) * LANES, :] = ones

    gu = _gelu_tanh(proj_ref[:, o:o + sgu_w]); o += sgu_w
    gv = _gelu_tanh(proj_ref[:, o:o + sgu_w]); o += sgu_w
    lane_c = lax.broadcasted_iota(jnp.int32, (chunk, LANES), 1)
    rr = lax.broadcasted_iota(jnp.int32, (chunk, chunk), 0)
    cs = lax.broadcasted_iota(jnp.int32, (chunk, chunk), 1)
    wt = [jnp.where(rr >= cs, sw_ref[g], 0.0).astype(BF16) for g in range(sw_ref.shape[0])]
    vn_all = _head_rmsnorm(gv, gmat, sg_ref[...]).astype(BF16)
    for p in range(sgu_w // LANES):
        vn = vn_all[:, p * LANES:(p + 1) * LANES]
        for ci in range(tm // chunk):
            vblk = vn[ci * chunk:(ci + 1) * chunk, :]
            mixed = jnp.where(lane_c < HEAD_DIM, _dot(wt[2 * p], vblk), _dot(wt[2 * p + 1], vblk))
            mixed = mixed + sb_ref[:, p * LANES:(p + 1) * LANES]
            og_ref[0, ci * chunk:(ci + 1) * chunk, p * LANES:(p + 1) * LANES] = (
                gu[ci * chunk:(ci + 1) * chunk, p * LANES:(p + 1) * LANES] * mixed).astype(BF16)


def _in_proj(x, mod_l, g1, w, wf, bf, qx, qg, kg, sg, sw, sb, *, tm, sb_w, fox_w, sgu_w):
    bsz, s, d = x.shape
    chunk = sw.shape[-1]
    wn = w.shape[1]
    fox_heads = fox_w // HEAD_DIM
    assert fox_heads <= 8 and fox_w % MXU_DIM == 0 and sgu_w == MXU_DIM
    const2 = lambda b, i: (0, 0)
    tok = lambda b, i: (b, i, 0)
    tok_t = lambda b, i: (b, 0, i)
    kern = functools.partial(_in_kernel, sb_w=sb_w, fox_w=fox_w, sgu_w=sgu_w, chunk=chunk)
    out_shape = [
        jax.ShapeDtypeStruct((bsz, s, sb_w), BF16),
        jax.ShapeDtypeStruct((bsz, s, sb_w), BF16),
        jax.ShapeDtypeStruct((bsz, sb_w, s), BF16),
        jax.ShapeDtypeStruct((bsz, s, fox_heads * LANES), BF16),
        jax.ShapeDtypeStruct((bsz, s, fox_heads * LANES), BF16),
        jax.ShapeDtypeStruct((bsz, fox_heads * LANES, s), BF16),
        jax.ShapeDtypeStruct((bsz, s, sgu_w), BF16),
        jax.ShapeDtypeStruct((bsz, fox_heads, s), F32),
    ]
    out_specs = [
        pl.BlockSpec((1, tm, sb_w), tok),
        pl.BlockSpec((1, tm, sb_w), tok),
        pl.BlockSpec((1, sb_w, tm), tok_t),
        pl.BlockSpec((1, tm, fox_heads * LANES), tok),
        pl.BlockSpec((1, tm, fox_heads * LANES), tok),
        pl.BlockSpec((1, fox_heads * LANES, tm), tok_t),
        pl.BlockSpec((1, tm, sgu_w), tok),
        pl.BlockSpec((1, fox_heads, tm), tok_t),
    ]
    return pl.pallas_call(
        kern,
        grid=(bsz, s // tm),
        in_specs=[
            pl.BlockSpec((1, tm, d), tok),
            pl.BlockSpec((1, 6, d), lambda b, i: (b, 0, 0)),
            pl.BlockSpec((1, d), const2),
            pl.BlockSpec((d, wn), const2),
            pl.BlockSpec((d, 2 * LANES), const2),
            pl.BlockSpec((1, LANES), const2),
            pl.BlockSpec((1, LANES), const2),
            pl.BlockSpec((1, MXU_DIM), const2),
            pl.BlockSpec((1, MXU_DIM), const2),
            pl.BlockSpec((1, sgu_w), const2),
            pl.BlockSpec(sw.shape, lambda b, i: (0, 0, 0)),
            pl.BlockSpec((chunk, sgu_w), const2),
        ],
        out_specs=out_specs,
        out_shape=out_shape,
        scratch_shapes=[pltpu.VMEM((8, LANES), F32), pltpu.VMEM((tm, wn), F32)],
        compiler_params=pltpu.CompilerParams(
            dimension_semantics=("arbitrary", "arbitrary"), vmem_limit_bytes=VMEM_LIMIT),
        name="in_proj",
    )(x, mod_l, g1, w, wf, bf, qx, qg, kg, sg, sw, sb)


def _sb_kernel(q_ref, k_ref, vt_ref, o_ref, acc_ref, c_ref, *, tk):
    tq = q_ref.shape[1]
    qi = pl.program_id(2)
    q = q_ref[0]
    lane = lax.broadcasted_iota(jnp.int32, (1, LANES), 1)
    qh = (jnp.where(lane < HEAD_DIM, q, 0).astype(BF16), jnp.where(lane >= HEAD_DIM, q, 0).astype(BF16))
    key = lax.broadcasted_iota(jnp.int32, (tk, tq), 0)
    qry = lax.broadcasted_iota(jnp.int32, (tk, tq), 1)
    causal = key < qry
    js = lax.broadcasted_iota(jnp.int32, (tk, tk), 0)
    jr = lax.broadcasted_iota(jnp.int32, (tk, tk), 1)
    after = jnp.where(jr > js, 1.0, 0.0).astype(BF16)

    def process(blocks, first):
        starts = [pl.multiple_of(j * tk, tk) for j in blocks]
        units = [(b, h) for b in range(len(blocks)) for h in range(2)]
        z = {(b, h): _dot_nt(k_ref[0, pl.ds(starts[b], tk), :], qh[h]) for b, h in units}
        l1mb, head = {}, {}
        for b, h in units:
            zz = z[b, h]
            lg = -(jnp.maximum(zz, 0.0) + jnp.log(1.0 + jnp.exp(-jnp.abs(zz))))
            if first and b == 0:
                lg = jnp.where(causal, lg, 0.0)
            l1mb[b, h] = lg.astype(BF16)
            head[b, h] = (zz + lg, lg[0:1, :])
        between = {u: _dot(after, l1mb[u]) for u in units}
        a = {}
        for h in range(2):
            c = None if first else c_ref[h]
            for b in range(len(blocks)):
                e = head[b, h][0] + between[b, h]
                if c is not None:
                    e = e + c
                w = jnp.exp(e)
                if first and b == 0:
                    w = jnp.where(causal, w, 0.0)
                a[b, h] = w.astype(BF16)
                block_sum = between[b, h][0:1, :] + head[b, h][1]
                c = block_sum if c is None else c + block_sum
            c_ref[h] = c
        for h in range(2):
            pv = None
            for b in range(len(blocks)):
                part = _dot(vt_ref[0, :, pl.ds(starts[b], tk)], a[b, h])
                pv = part if pv is None else pv + part
            if first:
                acc_ref[h] = pv
            else:
                acc_ref[h] += pv

    def carry_max():
        cm = jnp.maximum(jnp.max(c_ref[0], axis=1, keepdims=True),
                         jnp.max(c_ref[1], axis=1, keepdims=True))
        return cm[0, 0]

    @pl.when(qi == 0)
    def _():
        process([qi], True)

    @pl.when(qi > 0)
    def _():
        process([qi, qi - 1], True)

    def cond(carry):
        j, cmax = carry
        return jnp.logical_and(j >= 0, cmax > EXP_ZERO)

    def body(carry):
        j, _ = carry
        process([j], False)
        return j - 1, carry_max()

    lax.while_loop(cond, body, (qi - 2, carry_max()))
    chan = lax.broadcasted_iota(jnp.int32, (LANES, tq), 0)
    o_ref[0] = jnp.where(chan < HEAD_DIM, acc_ref[0], acc_ref[1]).T.astype(o_ref.dtype)


def _sb_attention(q, k, vt, *, tq):
    bsz, s, w = q.shape
    kern = functools.partial(_sb_kernel, tk=tq)
    return pl.pallas_call(
        kern,
        grid=(bsz, w // LANES, s // tq),
        in_specs=[
            pl.BlockSpec((1, tq, LANES), lambda b, p, i: (b, i, p)),
            pl.BlockSpec((1, s, LANES), lambda b, p, i: (b, 0, p)),
            pl.BlockSpec((1, LANES, s), lambda b, p, i: (b, p, 0)),
        ],
        out_specs=pl.BlockSpec((1, tq, LANES), lambda b, p, i: (b, i, p)),
        out_shape=jax.ShapeDtypeStruct((bsz, s, w), BF16),
        scratch_shapes=[pltpu.VMEM((2, LANES, tq), F32), pltpu.VMEM((2, 1, tq), F32)],
        compiler_params=pltpu.CompilerParams(
            dimension_semantics=("arbitrary", "arbitrary", "arbitrary"),
            vmem_limit_bytes=VMEM_LIMIT),
        name="sb_attn",
    )(q, k, vt)


def _fox_kernel(fend_ref, par_ref, q_ref, k_ref, vt_ref, o_ref, acc_ref, m_ref, p_ref, *, tk, n_heads):
    tq = q_ref.shape[1]
    qi = pl.program_id(2)
    n_blk = pl.num_programs(2)

    def first_dead_block(head):
        base = head * n_blk
        f_q = fend_ref[base + jnp.maximum(qi - 1, 0)]

        def alive(j):
            jc = jnp.maximum(j, 0)
            return jnp.logical_and(j >= 0, f_q - fend_ref[base + jc] >= -par_ref[0])

        return lax.while_loop(alive, lambda j: j - 1, qi - 1)

    head0 = (pl.program_id(0) * n_heads + 2 * pl.program_id(1))
    j_dead = jnp.minimum(first_dead_block(head0), first_dead_block(head0 + 1))
    key = lax.broadcasted_iota(jnp.int32, (tk, tq), 0)
    qry = lax.broadcasted_iota(jnp.int32, (tk, tq), 1)
    causal = key <= qry

    def process(blocks, first):
        starts = [pl.multiple_of(j * tk, tk) for j in blocks]
        units = [(b, h) for b in range(len(blocks)) for h in range(2)]
        s = {(b, h): _dot_nt(k_ref[0, pl.ds(starts[b], tk), h * LANES:(h + 1) * LANES],
                             q_ref[0, :, h * LANES:(h + 1) * LANES]) for b, h in units}
        p = {}
        alpha = {}
        for h in range(2):
            if first:
                s[0, h] = jnp.where(causal, s[0, h], NEG_BIG)
            m_new = None if first else m_ref[h]
            for b in range(len(blocks)):
                mb = jnp.max(s[b, h], axis=0, keepdims=True)
                m_new = mb if m_new is None else jnp.maximum(m_new, mb)
            if not first:
                alpha[h] = jnp.exp(m_ref[h] - m_new)
            for b in range(len(blocks)):
                p[b, h] = jnp.exp(s[b, h] - m_new).astype(BF16)
            m_ref[h] = m_new
        for h in range(2):
            pv = None
            for b in range(len(blocks)):
                part = _dot(vt_ref[0, h * LANES:(h + 1) * LANES, pl.ds(starts[b], tk)], p[b, h])
                pv = part if pv is None else pv + part
            acc_ref[h] = pv if first else alpha[h] * acc_ref[h] + pv

    n_left = jnp.maximum(qi - 2 - j_dead, 0)
    odd = n_left % 2
    has_dead_below = j_dead >= 0
    n_pairs = n_left // 2 + jnp.where(has_dead_below, odd, 0)
    lone_block0 = jnp.logical_and(odd == 1, jnp.logical_not(has_dead_below))

    def online_path():
        @pl.when(qi == 0)
        def _():
            process([qi], True)

        @pl.when(qi > 0)
        def _():
            process([qi, qi - 1], True)

        def body(i, carry):
            j = qi - 2 - 2 * i
            process([j, j - 1], False)
            return carry

        lax.fori_loop(0, n_pairs, body, 0)

        @pl.when(lone_block0)
        def _():
            process([0], False)

    win = 2 * tk
    wrow = lax.broadcasted_iota(jnp.int32, (win, tq), 0)
    wcol = lax.broadcasted_iota(jnp.int32, (win, tq), 1)

    def window(i):
        jl = qi - 1 - 2 * i
        j_lo = jnp.maximum(jl, 0)
        return j_lo, pl.multiple_of(j_lo * tk, tk), jnp.where(jl < 0, tk, win)

    def stage_a(i, first):
        j_lo, start, row_lim = window(i)
        keep = wrow < row_lim
        if first:
            keep = jnp.logical_and(keep, wrow <= wcol + (qi - j_lo) * tk)
        slot = i & 1
        for h in range(2):
            s = _dot_nt(k_ref[0, pl.ds(start, win), h * LANES:(h + 1) * LANES],
                        q_ref[0, :, h * LANES:(h + 1) * LANES])
            p_ref[slot, h] = jnp.exp(jnp.where(keep, s, NEG_BIG)).astype(BF16)

    def stage_c(i):
        _, start, _ = window(i)
        slot = i & 1
        for h in range(2):
            acc_ref[h] += _dot(vt_ref[0, h * LANES:(h + 1) * LANES, pl.ds(start, win)], p_ref[slot, h])

    def bounded_path():
        acc_ref[...] = jnp.zeros_like(acc_ref)
        n_stages = 1 + n_pairs + jnp.where(lone_block0, 1, 0)
        stage_a(0, True)

        def body(i, carry):
            stage_c(i - 1)
            stage_a(i, False)
            return carry

        lax.fori_loop(1, n_stages, body, 0)
        stage_c(n_stages - 1)

    bounded = par_ref[1] > 0.5
    pl.when(bounded)(bounded_path)
    pl.when(jnp.logical_not(bounded))(online_path)

    outs = []
    for h in range(2):
        acc = acc_ref[h]
        outs.append(acc[0:HEAD_DIM, :] / acc[HEAD_DIM:HEAD_DIM + 1, :])
    o_ref[0] = jnp.concatenate(outs, axis=0).T.astype(o_ref.dtype)


def _fox_attention(q, k, vt, f, qk_bound, *, tq):
    bsz, s, w = q.shape
    pair = 2 * LANES
    n_heads = w // LANES
    assert s >= 2 * tq
    f_end = f[:, :, tq - 1::tq].reshape(-1)
    par = jnp.stack([2.0 * qk_bound - EXP_ZERO,
                     jnp.where(qk_bound <= MAX_UNSTABILISED_LOGIT, 1.0, 0.0)]).astype(F32)
    kern = functools.partial(_fox_kernel, tk=tq, n_heads=n_heads)
    grid_spec = pltpu.PrefetchScalarGridSpec(
        num_scalar_prefetch=2,
        grid=(bsz, w // pair, s // tq),
        in_specs=[
            pl.BlockSpec((1, tq, pair), lambda b, p, i, fe, th: (b, i, p)),
            pl.BlockSpec((1, s, pair), lambda b, p, i, fe, th: (b, 0, p)),
            pl.BlockSpec((1, pair, s), lambda b, p, i, fe, th: (b, p, 0)),
        ],
        out_specs=pl.BlockSpec((1, tq, LANES), lambda b, p, i, fe, th: (b, i, p)),
        scratch_shapes=[pltpu.VMEM((2, LANES, tq), F32), pltpu.VMEM((2, 1, tq), F32),
                        pltpu.VMEM((2, 2, 2 * tq, tq), BF16)],
    )
    return pl.pallas_call(
        kern,
        grid_spec=grid_spec,
        out_shape=jax.ShapeDtypeStruct((bsz, s, w // 2), BF16),
        compiler_params=pltpu.CompilerParams(
            dimension_semantics=("arbitrary", "arbitrary", "arbitrary"),
            vmem_limit_bytes=VMEM_LIMIT),
        name="fox_attn",
    )(f_end, par, q, k, vt)


def _out_kernel(x_ref, osb_ref, ofox_ref, osgu_ref, mod_ref, g2_ref, wo_ref, w1_ref, w2_ref,
                o_ref, *, ff_chunk):
    sb_w = osb_ref.shape[2]
    fox_w = ofox_ref.shape[2]
    x = x_ref[0]
    mix = (_dot(osb_ref[0], wo_ref[0:sb_w, :])
           + _dot(ofox_ref[0], wo_ref[sb_w:sb_w + fox_w, :])
           + _dot(osgu_ref[0], wo_ref[sb_w + fox_w:, :]))
    x1 = x + mod_ref[0, 2:3, :] * mix
    ms = jnp.mean(x1 * x1, axis=-1, keepdims=True)
    h = x1 * lax.rsqrt(ms + EPS) * g2_ref[...]
    hb = (h * (1.0 + mod_ref[0, 4:5, :]) + mod_ref[0, 3:4, :]).astype(BF16)
    d_ff = w1_ref.shape[1]
    acc = None
    for c in range(d_ff // ff_chunk):
        hid = jnp.maximum(_dot(hb, w1_ref[:, c * ff_chunk:(c + 1) * ff_chunk]), 0.0)
        part = _dot((hid * hid).astype(BF16), w2_ref[c * ff_chunk:(c + 1) * ff_chunk, :])
        acc = part if acc is None else acc + part
    o_ref[0] = x1 + mod_ref[0, 5:6, :] * acc


def _out_mlp(x, osb, ofox, osgu, mod_l, g2, wo, w1, w2, *, tm):
    bsz, s, d = x.shape
    tok = lambda b, i: (b, i, 0)
    const2 = lambda b, i: (0, 0)
    single = pl.Buffered(1)
    kern = functools.partial(_out_kernel, ff_chunk=1024)
    return pl.pallas_call(
        kern,
        grid=(bsz, s // tm),
        in_specs=[
            pl.BlockSpec((1, tm, d), tok),
            pl.BlockSpec((1, tm, osb.shape[2]), tok),
            pl.BlockSpec((1, tm, ofox.shape[2]), tok),
            pl.BlockSpec((1, tm, osgu.shape[2]), tok),
            pl.BlockSpec((1, 6, d), lambda b, i: (b, 0, 0)),
            pl.BlockSpec((1, d), const2),
            pl.BlockSpec(wo.shape, const2, pipeline_mode=single),
            pl.BlockSpec(w1.shape, const2, pipeline_mode=single),
            pl.BlockSpec(w2.shape, const2, pipeline_mode=single),
        ],
        out_specs=pl.BlockSpec((1, tm, d), tok),
        out_shape=jax.ShapeDtypeStruct((bsz, s, d), F32),
        compiler_params=pltpu.CompilerParams(
            dimension_semantics=("arbitrary", "arbitrary"), vmem_limit_bytes=VMEM_LIMIT),
        name="out_mlp",
    )(x, osb, ofox, osgu, mod_l, g2, wo, w1, w2)


def kernel(x, c, ada_w, ada_b, norm1_g, norm2_g, w_in, b_forget, q_norm_g, k_norm_g, sgu_norm_g,
           sgu_w, sgu_b, w_out, mlp_w1, mlp_w2):
    depth, d, _ = ada_w.shape
    bsz, s, _ = x.shape
    fox_heads = b_forget.shape[1]
    fox_w = fox_heads * HEAD_DIM
    sgu_groups, chunk = sgu_b.shape[1], sgu_b.shape[2]
    sgu_wd = sgu_groups * sgu_norm_g.shape[2]
    sb_w = (w_in.shape[2] - 3 * fox_w - fox_heads - 2 * sgu_wd) // 3
    f_lo = 3 * sb_w + 3 * fox_w

    mod = _modulation(c, ada_w, ada_b).reshape(depth, bsz, 6, d)

    for l in range(depth):
        w = jnp.concatenate([w_in[l][:, :f_lo], w_in[l][:, f_lo + fox_heads:]], axis=1).astype(BF16)
        wf = jnp.pad(w_in[l][:, f_lo:f_lo + fox_heads], ((0, 0), (0, LANES - fox_heads)))
        wf_hi = wf.astype(BF16)
        wf = jnp.concatenate([wf_hi, (wf - wf_hi.astype(F32)).astype(BF16)], axis=1)
        bf = jnp.pad(b_forget[l], (0, LANES - fox_heads)).reshape(1, LANES)
        qg = jnp.tile(q_norm_g[l], MXU_DIM // HEAD_DIM).reshape(1, MXU_DIM)
        kg = jnp.tile(k_norm_g[l], MXU_DIM // HEAD_DIM).reshape(1, MXU_DIM)
        sg = sgu_norm_g[l].reshape(1, sgu_wd)
        sb = jnp.repeat(sgu_b[l].T, sgu_norm_g.shape[2], axis=1)

        qk_bound = (1.03 * HEAD_DIM ** 0.5) * jnp.max(jnp.abs(q_norm_g[l])) * jnp.max(jnp.abs(k_norm_g[l]))
        qx = jnp.zeros((LANES,), F32).at[HEAD_DIM:HEAD_DIM + 3].set(-1.0).at[HEAD_DIM + 6].set(-qk_bound)
        qa, ka, vat, qf, kf, vft, osgu, f = _in_proj(
            x, mod[l], norm1_g[l].reshape(1, d), w, wf, bf, qx.reshape(1, LANES), qg, kg, sg, sgu_w[l], sb,
            tm=512, sb_w=sb_w, fox_w=fox_w, sgu_w=sgu_wd)
        osb = _sb_attention(qa, ka, vat, tq=256)
        ofox = _fox_attention(qf, kf, vft, f, qk_bound, tq=256)
        x = _out_mlp(x, osb, ofox, osgu, mod[l], norm2_g[l].reshape(1, d),
                     w_out[l].astype(BF16), mlp_w1[l].astype(BF16), mlp_w2[l].astype(BF16), tm=512)
    return x
```

```python
import functools
import math

import jax
import jax.numpy as jnp
from jax import lax
from jax.experimental import pallas as pl
from jax.experimental.pallas import tpu as pltpu

HEAD_DIM = 64
LANES = 128
MXU_DIM = 256
EPS = 1e-6
NEG_BIG = -1e30
EXP_ZERO = -104.0
MAX_UNSTABILISED_LOGIT = 40.0
VMEM_LIMIT = 56 * 1024 * 1024

F32 = jnp.float32
BF16 = jnp.bfloat16


def _dot(a, b):
    return jnp.dot(a, b, preferred_element_type=F32)


def _dot_nt(a, b):
    return lax.dot_general(a, b, (((1,), (1,)), ((), ())), preferred_element_type=F32)


def _split2(x):
    hi = x.astype(BF16)
    lo = (x - hi.astype(F32)).astype(BF16)
    return hi, lo


def _split3(x):
    hi = x.astype(BF16)
    r = x - hi.astype(F32)
    mid = r.astype(BF16)
    lo = (r - mid.astype(F32)).astype(BF16)
    return hi, mid, lo


def _group_mean_matrix():
    r = lax.broadcasted_iota(jnp.int32, (MXU_DIM, MXU_DIM), 0) >> 6
    c = lax.broadcasted_iota(jnp.int32, (MXU_DIM, MXU_DIM), 1) >> 6
    return jnp.where(r == c, 1.0 / HEAD_DIM, 0.0).astype(BF16)


def _head_rmsnorm(t, gmat, gain):
    ms = _dot((t * t).astype(BF16), gmat)
    return t * lax.rsqrt(ms + EPS) * gain


def _gelu_tanh(x):
    c = math.sqrt(2.0 / math.pi)
    return x * (0.5 * (1.0 + jnp.tanh(c * (x + 0.044715 * (x * x * x)))))


def _log_sigmoid(x):
    return jnp.minimum(x, 0.0) - jnp.log(1.0 + jnp.exp(-jnp.abs(x)))


def _mod_kernel(ct_ref, w_ref, b_ref, o_ref):
    ct = ct_ref[...]
    cond = ct * (1.0 / (1.0 + jnp.exp(-ct)))
    w = w_ref[0]
    rows = [jnp.sum(cond[:, b:b + 1] * w, axis=0, keepdims=True) for b in range(ct.shape[1])]
    o_ref[0] = jnp.concatenate(rows, axis=0) + b_ref[0]


def _modulation(c, ada_w, ada_b):
    depth, d, n = ada_w.shape
    bsz = c.shape[0]
    tn = 1536
    return pl.pallas_call(
        _mod_kernel,
        grid=(depth, n // tn),
        in_specs=[
            pl.BlockSpec((d, bsz), lambda l, j: (0, 0)),
            pl.BlockSpec((1, d, tn), lambda l, j: (l, 0, j)),
            pl.BlockSpec((1, 1, tn), lambda l, j: (l, 0, j)),
        ],
        out_specs=pl.BlockSpec((1, bsz, tn), lambda l, j: (l, 0, j)),
        out_shape=jax.ShapeDtypeStruct((depth, bsz, n), F32),
        compiler_params=pltpu.CompilerParams(
            dimension_semantics=("arbitrary", "arbitrary"), vmem_limit_bytes=VMEM_LIMIT),
        name="adaln_mod",
    )(c.T, ada_w, ada_b.reshape(depth, 1, n))


def _in_kernel(x_ref, mod_ref, g1_ref, w_ref, wf_ref, bf_ref, qx_ref, qg_ref, kg_ref,
               sg_ref, sw_ref, sb_ref,
               qa_ref, ka_ref, vat_ref, qf_ref, kf_ref, vft_ref, og_ref, f_ref,
               carry_ref, proj_ref, *, sb_w, fox_w, sgu_w, chunk):
    @pl.when(pl.program_id(1) == 0)
    def _():
        carry_ref[...] = jnp.zeros_like(carry_ref)

    tm = x_ref.shape[1]
    x = x_ref[0]
    ms = jnp.mean(x * x, axis=-1, keepdims=True)
    h = x * lax.rsqrt(ms + EPS) * g1_ref[...]
    h = h * (1.0 + mod_ref[0, 1:2, :]) + mod_ref[0, 0:1, :]
    hb = h.astype(BF16)
    scale = HEAD_DIM ** -0.5
    proj_ref[...] = _dot(hb, w_ref[...])

    o = 0
    qa_ref[0] = (proj_ref[:, o:o + sb_w] * scale).astype(BF16); o += sb_w
    ka_ref[0] = proj_ref[:, o:o + sb_w].astype(BF16); o += sb_w
    vat_ref[0] = proj_ref[:, o:o + sb_w].T.astype(BF16); o += sb_w

    n_heads = f_ref.shape[1]
    lane = lax.broadcasted_iota(jnp.int32, (1, LANES), 1)
    head_lane = lane < n_heads

    def pack3(parts):
        a, b, c = (p.astype(F32) for p in parts)
        return (a + pltpu.roll(b, 8, 1) + pltpu.roll(c, 16, 1)).astype(BF16)

    h_hi, h_lo = _split2(h)
    t2 = _dot(h_hi, wf_ref[...])
    fl = t2[:, :LANES] + t2[:, LANES:] + _dot(h_lo, wf_ref[:, :LANES])
    logf = jnp.where(head_lane, _log_sigmoid(fl + bf_ref[...]), 0.0)
    r = lax.broadcasted_iota(jnp.int32, (tm, tm), 0)
    cc = lax.broadcasted_iota(jnp.int32, (tm, tm), 1)
    ltri = jnp.where(r >= cc, 1.0, 0.0).astype(BF16)
    cp = _dot(ltri, pack3(_split3(logf)))
    cum = cp + pltpu.roll(cp, LANES - 8, 1) + pltpu.roll(cp, LANES - 16, 1)
    cum = jnp.where(head_lane, cum, 0.0) + carry_ref[0:1, :]
    carry_ref[0:1, :] = cum[tm - 1:tm, :]
    f_ref[0] = cum.T[0:n_heads, :]
    er = lax.broadcasted_iota(jnp.int32, (LANES, 2 * n_heads * LANES), 0)
    ec = lax.broadcasted_iota(jnp.int32, (LANES, 2 * n_heads * LANES), 1)
    tile = ec >> 7
    is_q = jnp.where(tile >= n_heads, 1, 0)
    part = (ec & (LANES - 1)) - HEAD_DIM - 3 * is_q
    sel = jnp.where((part >= 0) & (part < 3) & (er == part * 8 + tile - n_heads * is_q), 1.0, 0.0)
    extras = _dot(pack3(_split3(cum)), sel.astype(BF16))

    gmat = _group_mean_matrix()
    is_head = lane < HEAD_DIM
    q_extra = qx_ref[...]
    k_extra = jnp.where((lane >= HEAD_DIM + 3) & (lane < HEAD_DIM + 7), 1.0, 0.0)
    heads_per_mxu = MXU_DIM // HEAD_DIM

    def head_tile(tn, s):
        half = tn[:, (s // 2) * LANES:(s // 2 + 1) * LANES]
        return half if s % 2 == 0 else pltpu.roll(half, HEAD_DIM, 1)

    for j in range(fox_w // MXU_DIM):
        t = proj_ref[:, o + j * MXU_DIM:o + (j + 1) * MXU_DIM]
        qn = _head_rmsnorm(t, gmat, qg_ref[...]) * scale
        for s in range(heads_per_mxu):
            hh = heads_per_mxu * j + s
            xq = extras[:, (n_heads + hh) * LANES:(n_heads + hh + 1) * LANES] + q_extra
            qf_ref[0, :, hh * LANES:(hh + 1) * LANES] = jnp.where(
                is_head, head_tile(qn, s), xq).astype(BF16)
    o += fox_w
    for j in range(fox_w // MXU_DIM):
        t = proj_ref[:, o + j * MXU_DIM:o + (j + 1) * MXU_DIM]
        kn = _head_rmsnorm(t, gmat, kg_ref[...])
        for s in range(heads_per_mxu):
            hh = heads_per_mxu * j + s
            xk = extras[:, hh * LANES:(hh + 1) * LANES] + k_extra
            kf_ref[0, :, hh * LANES:(hh + 1) * LANES] = jnp.where(
                is_head, head_tile(kn, s), xk).astype(BF16)
    o += fox_w
    vt = proj_ref[:, o:o + fox_w].T.astype(BF16); o += fox_w
    ones = jnp.ones((HEAD_DIM, tm), BF16)
    for hh in range(fox_w // HEAD_DIM):
        vft_ref[0, hh * LANES:hh * LANES + HEAD_DIM, :] = vt[hh * HEAD_DIM:(hh + 1) * HEAD_DIM, :]
        vft_ref[0, hh * LANES + HEAD_DIM:(hh + 1) * LANES, :] = ones

    gu = _gelu_tanh(proj_ref[:, o:o + sgu_w]); o += sgu_w
    gv = _gelu_tanh(proj_ref[:, o:o + sgu_w]); o += sgu_w
    lane_c = lax.broadcasted_iota(jnp.int32, (chunk, LANES), 1)
    rr = lax.broadcasted_iota(jnp.int32, (chunk, chunk), 0)
    cs = lax.broadcasted_iota(jnp.int32, (chunk, chunk), 1)
    wt = [jnp.where(rr >= cs, sw_ref[g], 0.0).astype(BF16) for g in range(sw_ref.shape[0])]
    vn_all = _head_rmsnorm(gv, gmat, sg_ref[...]).astype(BF16)
    for p in range(sgu_w // LANES):
        vn = vn_all[:, p * LANES:(p + 1) * LANES]
        for ci in range(tm // chunk):
            vblk = vn[ci * chunk:(ci + 1) * chunk, :]
            mixed = jnp.where(lane_c < HEAD_DIM, _dot(wt[2 * p], vblk), _dot(wt[2 * p + 1], vblk))
            mixed = mixed + sb_ref[:, p * LANES:(p + 1) * LANES]
            og_ref[0, ci * chunk:(ci + 1) * chunk, p * LANES:(p + 1) * LANES] = (
                gu[ci * chunk:(ci + 1) * chunk, p * LANES:(p + 1) * LANES] * mixed).astype(BF16)


def _in_proj(x, mod_l, g1, w, wf, bf, qx, qg, kg, sg, sw, sb, *, tm, sb_w, fox_w, sgu_w):
    bsz, s, d = x.shape
    chunk = sw.shape[-1]
    wn = w.shape[1]
    fox_heads = fox_w // HEAD_DIM
    assert fox_heads <= 8 and fox_w % MXU_DIM == 0 and sgu_w == MXU_DIM
    const2 = lambda b, i: (0, 0)
    tok = lambda b, i: (b, i, 0)
    tok_t = lambda b, i: (b, 0, i)
    kern = functools.partial(_in_kernel, sb_w=sb_w, fox_w=fox_w, sgu_w=sgu_w, chunk=chunk)
    out_shape = [
        jax.ShapeDtypeStruct((bsz, s, sb_w), BF16),
        jax.ShapeDtypeStruct((bsz, s, sb_w), BF16),
        jax.ShapeDtypeStruct((bsz, sb_w, s), BF16),
        jax.ShapeDtypeStruct((bsz, s, fox_heads * LANES), BF16),
        jax.ShapeDtypeStruct((bsz, s, fox_heads * LANES), BF16),
        jax.ShapeDtypeStruct((bsz, fox_heads * LANES, s), BF16),
        jax.ShapeDtypeStruct((bsz, s, sgu_w), BF16),
        jax.ShapeDtypeStruct((bsz, fox_heads, s), F32),
    ]
    out_specs = [
        pl.BlockSpec((1, tm, sb_w), tok),
        pl.BlockSpec((1, tm, sb_w), tok),
        pl.BlockSpec((1, sb_w, tm), tok_t),
        pl.BlockSpec((1, tm, fox_heads * LANES), tok),
        pl.BlockSpec((1, tm, fox_heads * LANES), tok),
        pl.BlockSpec((1, fox_heads * LANES, tm), tok_t),
        pl.BlockSpec((1, tm, sgu_w), tok),
        pl.BlockSpec((1, fox_heads, tm), tok_t),
    ]
    return pl.pallas_call(
        kern,
        grid=(bsz, s // tm),
        in_specs=[
            pl.BlockSpec((1, tm, d), tok),
            pl.BlockSpec((1, 6, d), lambda b, i: (b, 0, 0)),
            pl.BlockSpec((1, d), const2),
            pl.BlockSpec((d, wn), const2),
            pl.BlockSpec((d, 2 * LANES), const2),
            pl.BlockSpec((1, LANES), const2),
            pl.BlockSpec((1, LANES), const2),
            pl.BlockSpec((1, MXU_DIM), const2),
            pl.BlockSpec((1, MXU_DIM), const2),
            pl.BlockSpec((1, sgu_w), const2),
            pl.BlockSpec(sw.shape, lambda b, i: (0, 0, 0)),
            pl.BlockSpec((chunk, sgu_w), const2),
        ],
        out_specs=out_specs,
        out_shape=out_shape,
        scratch_shapes=[pltpu.VMEM((8, LANES), F32), pltpu.VMEM((tm, wn), F32)],
        compiler_params=pltpu.CompilerParams(
            dimension_semantics=("arbitrary", "arbitrary"), vmem_limit_bytes=VMEM_LIMIT),
        name="in_proj",
    )(x, mod_l, g1, w, wf, bf, qx, qg, kg, sg, sw, sb)


def _sb_kernel(q_ref, k_ref, vt_ref, o_ref, acc_ref, c_ref, *, tk):
    tq = q_ref.shape[1]
    qi = pl.program_id(2)
    q = q_ref[0]
    lane = lax.broadcasted_iota(jnp.int32, (1, LANES), 1)
    qh = (jnp.where(lane < HEAD_DIM, q, 0).astype(BF16), jnp.where(lane >= HEAD_DIM, q, 0).astype(BF16))
    key = lax.broadcasted_iota(jnp.int32, (tk, tq), 0)
    qry = lax.broadcasted_iota(jnp.int32, (tk, tq), 1)
    causal = key < qry
    js = lax.broadcasted_iota(jnp.int32, (tk, tk), 0)
    jr = lax.broadcasted_iota(jnp.int32, (tk, tk), 1)
    after = jnp.where(jr > js, 1.0, 0.0).astype(BF16)

    def process(blocks, first):
        starts = [pl.multiple_of(j * tk, tk) for j in blocks]
        units = [(b, h) for b in range(len(blocks)) for h in range(2)]
        z = {(b, h): _dot_nt(k_ref[0, pl.ds(starts[b], tk), :], qh[h]) for b, h in units}
        l1mb, head = {}, {}
        for b, h in units:
            zz = z[b, h]
            lg = -(jnp.maximum(zz, 0.0) + jnp.log(1.0 + jnp.exp(-jnp.abs(zz))))
            if first and b == 0:
                lg = jnp.where(causal, lg, 0.0)
            l1mb[b, h] = lg.astype(BF16)
            head[b, h] = (zz + lg, lg[0:1, :])
        between = {u: _dot(after, l1mb[u]) for u in units}
        a = {}
        for h in range(2):
            c = None if first else c_ref[h]
            for b in range(len(blocks)):
                e = head[b, h][0] + between[b, h]
                if c is not None:
                    e = e + c
                w = jnp.exp(e)
                if first and b == 0:
                    w = jnp.where(causal, w, 0.0)
                a[b, h] = w.astype(BF16)
                block_sum = between[b, h][0:1, :] + head[b, h][1]
                c = block_sum if c is None else c + block_sum
            c_ref[h] = c
        for h in range(2):
            pv = None
            for b in range(len(blocks)):
                part = _dot(vt_ref[0, :, pl.ds(starts[b], tk)], a[b, h])
                pv = part if pv is None else pv + part
            if first:
                acc_ref[h] = pv
            else:
                acc_ref[h] += pv

    def carry_max():
        cm = jnp.maximum(jnp.max(c_ref[0], axis=1, keepdims=True),
                         jnp.max(c_ref[1], axis=1, keepdims=True))
        return cm[0, 0]

    @pl.when(qi == 0)
    def _():
        process([qi], True)

    @pl.when(qi > 0)
    def _():
        process([qi, qi - 1], True)

    def cond(carry):
        j, cmax = carry
        return jnp.logical_and(j >= 0, cmax > EXP_ZERO)

    def body(carry):
        j, _ = carry
        process([j], False)
        return j - 1, carry_max()

    lax.while_loop(cond, body, (qi - 2, carry_max()))
    chan = lax.broadcasted_iota(jnp.int32, (LANES, tq), 0)
    o_ref[0] = jnp.where(chan < HEAD_DIM, acc_ref[0], acc_ref[1]).T.astype(o_ref.dtype)


def _sb_attention(q, k, vt, *, tq):
    bsz, s, w = q.shape
    kern = functools.partial(_sb_kernel, tk=tq)
    return pl.pallas_call(
        kern,
        grid=(bsz, w // LANES, s // tq),
        in_specs=[
            pl.BlockSpec((1, tq, LANES), lambda b, p, i: (b, i, p)),
            pl.BlockSpec((1, s, LANES), lambda b, p, i: (b, 0, p)),
            pl.BlockSpec((1, LANES, s), lambda b, p, i: (b, p, 0)),
        ],
        out_specs=pl.BlockSpec((1, tq, LANES), lambda b, p, i: (b, i, p)),
        out_shape=jax.ShapeDtypeStruct((bsz, s, w), BF16),
        scratch_shapes=[pltpu.VMEM((2, LANES, tq), F32), pltpu.VMEM((2, 1, tq), F32)],
        compiler_params=pltpu.CompilerParams(
            dimension_semantics=("arbitrary", "arbitrary", "arbitrary"),
            vmem_limit_bytes=VMEM_LIMIT),
        name="sb_attn",
    )(q, k, vt)


def _fox_kernel(fend_ref, par_ref, q_ref, k_ref, vt_ref, o_ref, acc_ref, m_ref, p_ref, *, tq, n_heads):
    tk = tq
    n_blk = q_ref.shape[1] // tq
    head0 = (pl.program_id(0) * n_heads + 2 * pl.program_id(1))

    def first_dead_block(head, qi):
        base = head * n_blk
        f_q = fend_ref[base + jnp.maximum(qi - 1, 0)]

        def alive(j):
            jc = jnp.maximum(j, 0)
            return jnp.logical_and(j >= 0, f_q - fend_ref[base + jc] >= -par_ref[0])

        return lax.while_loop(alive, lambda j: j - 1, qi - 1)

    def plan(qi):
        j_dead = jnp.minimum(first_dead_block(head0, qi), first_dead_block(head0 + 1, qi))
        n_left = jnp.maximum(qi - 2 - j_dead, 0)
        odd = n_left % 2
        has_dead_below = j_dead >= 0
        n_pairs = n_left // 2 + jnp.where(has_dead_below, odd, 0)
        return n_pairs, jnp.logical_and(odd == 1, jnp.logical_not(has_dead_below))

    def q_tile(qi, h):
        return q_ref[0, pl.ds(pl.multiple_of(qi * tq, tq), tq), h * LANES:(h + 1) * LANES]

    def finalize(qi):
        outs = []
        for h in range(2):
            acc = acc_ref[h]
            outs.append(acc[0:HEAD_DIM, :] / acc[HEAD_DIM:HEAD_DIM + 1, :])
        o_ref[0, pl.ds(pl.multiple_of(qi * tq, tq), tq), :] = (
            jnp.concatenate(outs, axis=0).T.astype(o_ref.dtype))

    key = lax.broadcasted_iota(jnp.int32, (tk, tq), 0)
    qry = lax.broadcasted_iota(jnp.int32, (tk, tq), 1)
    causal = key <= qry

    def process(qi, blocks, first):
        starts = [pl.multiple_of(j * tk, tk) for j in blocks]
        units = [(b, h) for b in range(len(blocks)) for h in range(2)]
        s = {(b, h): _dot_nt(k_ref[0, pl.ds(starts[b], tk), h * LANES:(h + 1) * LANES],
                             q_tile(qi, h)) for b, h in units}
        p = {}
        alpha = {}
        for h in range(2):
            if first:
                s[0, h] = jnp.where(causal, s[0, h], NEG_BIG)
            m_new = None if first else m_ref[h]
            for b in range(len(blocks)):
                mb = jnp.max(s[b, h], axis=0, keepdims=True)
                m_new = mb if m_new is None else jnp.maximum(m_new, mb)
            if not first:
                alpha[h] = jnp.exp(m_ref[h] - m_new)
            for b in range(len(blocks)):
                p[b, h] = jnp.exp(s[b, h] - m_new).astype(BF16)
            m_ref[h] = m_new
        for h in range(2):
            pv = None
            for b in range(len(blocks)):
                part = _dot(vt_ref[0, h * LANES:(h + 1) * LANES, pl.ds(starts[b], tk)], p[b, h])
                pv = part if pv is None else pv + part
            acc_ref[h] = pv if first else alpha[h] * acc_ref[h] + pv

    def online_path():
        def q_block(qi, carry):
            n_pairs, lone_block0 = plan(qi)

            @pl.when(qi == 0)
            def _():
                process(qi, [qi], True)

            @pl.when(qi > 0)
            def _():
                process(qi, [qi, qi - 1], True)

            def body(i, c):
                j = qi - 2 - 2 * i
                process(qi, [j, j - 1], False)
                return c

            lax.fori_loop(0, n_pairs, body, 0)

            @pl.when(lone_block0)
            def _():
                process(qi, [0], False)

            finalize(qi)
            return carry

        lax.fori_loop(0, n_blk, q_block, 0)

    win = 2 * tk
    wrow = lax.broadcasted_iota(jnp.int32, (win, tq), 0)
    wcol = lax.broadcasted_iota(jnp.int32, (win, tq), 1)

    def window(qi, i):
        jl = qi - 1 - 2 * i
        j_lo = jnp.maximum(jl, 0)
        return j_lo, pl.multiple_of(j_lo * tk, tk), jnp.where(jl < 0, tk, win)

    def stage_a(qi, i, slot, first):
        j_lo, start, row_lim = window(qi, i)
        keep = wrow < row_lim
        if first:
            keep = jnp.logical_and(keep, wrow <= wcol + (qi - j_lo) * tk)
        for h in range(2):
            s = _dot_nt(k_ref[0, pl.ds(start, win), h * LANES:(h + 1) * LANES], q_tile(qi, h))
            p_ref[slot, h] = jnp.exp(jnp.where(keep, s, NEG_BIG)).astype(BF16)

    def stage_c(qi, i, slot):
        _, start, _ = window(qi, i)
        for h in range(2):
            acc_ref[h] += _dot(vt_ref[0, h * LANES:(h + 1) * LANES, pl.ds(start, win)], p_ref[slot, h])

    def bounded_path():
        acc_ref[...] = jnp.zeros_like(acc_ref)
        stage_a(0, 0, 0, True)

        def q_block(qi, t):
            n_pairs, lone_block0 = plan(qi)
            n_stages = 1 + n_pairs + jnp.where(lone_block0, 1, 0)

            def body(i, t):
                stage_c(qi, i - 1, t & 1)
                stage_a(qi, i, (t + 1) & 1, False)
                return t + 1

            t = lax.fori_loop(1, n_stages, body, t)
            stage_c(qi, n_stages - 1, t & 1)
            stage_a(jnp.minimum(qi + 1, n_blk - 1), 0, (t + 1) & 1, True)
            finalize(qi)
            acc_ref[...] = jnp.zeros_like(acc_ref)
            return t + 1

        lax.fori_loop(0, n_blk, q_block, jnp.int32(0))

    bounded = par_ref[1] > 0.5
    pl.when(bounded)(bounded_path)
    pl.when(jnp.logical_not(bounded))(online_path)


def _fox_attention(q, k, vt, f, qk_bound, *, tq):
    bsz, s, w = q.shape
    pair = 2 * LANES
    n_heads = w // LANES
    assert s >= 2 * tq
    f_end = f[:, :, tq - 1::tq].reshape(-1)
    par = jnp.stack([2.0 * qk_bound - EXP_ZERO,
                     jnp.where(qk_bound <= MAX_UNSTABILISED_LOGIT, 1.0, 0.0)]).astype(F32)
    kern = functools.partial(_fox_kernel, tq=tq, n_heads=n_heads)
    grid_spec = pltpu.PrefetchScalarGridSpec(
        num_scalar_prefetch=2,
        grid=(bsz, w // pair),
        in_specs=[
            pl.BlockSpec((1, s, pair), lambda b, p, fe, th: (b, 0, p)),
            pl.BlockSpec((1, s, pair), lambda b, p, fe, th: (b, 0, p)),
            pl.BlockSpec((1, pair, s), lambda b, p, fe, th: (b, p, 0)),
        ],
        out_specs=pl.BlockSpec((1, s, LANES), lambda b, p, fe, th: (b, 0, p)),
        scratch_shapes=[pltpu.VMEM((2, LANES, tq), F32), pltpu.VMEM((2, 1, tq), F32),
                        pltpu.VMEM((2, 2, 2 * tq, tq), BF16)],
    )
    return pl.pallas_call(
        kern,
        grid_spec=grid_spec,
        out_shape=jax.ShapeDtypeStruct((bsz, s, w // 2), BF16),
        compiler_params=pltpu.CompilerParams(
            dimension_semantics=("arbitrary", "arbitrary"), vmem_limit_bytes=VMEM_LIMIT),
        name="fox_attn",
    )(f_end, par, q, k, vt)


def _out_kernel(x_ref, osb_ref, ofox_ref, osgu_ref, mod_ref, g2_ref, wo_ref, w1_ref, w2_ref,
                o_ref, *, ff_chunk):
    sb_w = osb_ref.shape[2]
    fox_w = ofox_ref.shape[2]
    x = x_ref[0]
    mix = (_dot(osb_ref[0], wo_ref[0:sb_w, :])
           + _dot(ofox_ref[0], wo_ref[sb_w:sb_w + fox_w, :])
           + _dot(osgu_ref[0], wo_ref[sb_w + fox_w:, :]))
    x1 = x + mod_ref[0, 2:3, :] * mix
    ms = jnp.mean(x1 * x1, axis=-1, keepdims=True)
    h = x1 * lax.rsqrt(ms + EPS) * g2_ref[...]
    hb = (h * (1.0 + mod_ref[0, 4:5, :]) + mod_ref[0, 3:4, :]).astype(BF16)
    d_ff = w1_ref.shape[1]
    acc = None
    for c in range(d_ff // ff_chunk):
        hid = jnp.maximum(_dot(hb, w1_ref[:, c * ff_chunk:(c + 1) * ff_chunk]), 0.0)
        part = _dot((hid * hid).astype(BF16), w2_ref[c * ff_chunk:(c + 1) * ff_chunk, :])
        acc = part if acc is None else acc + part
    o_ref[0] = x1 + mod_ref[0, 5:6, :] * acc


def _out_mlp(x, osb, ofox, osgu, mod_l, g2, wo, w1, w2, *, tm):
    bsz, s, d = x.shape
    tok = lambda b, i: (b, i, 0)
    const2 = lambda b, i: (0, 0)
    single = pl.Buffered(1)
    kern = functools.partial(_out_kernel, ff_chunk=1024)
    return pl.pallas_call(
        kern,
        grid=(bsz, s // tm),
        in_specs=[
            pl.BlockSpec((1, tm, d), tok),
            pl.BlockSpec((1, tm, osb.shape[2]), tok),
            pl.BlockSpec((1, tm, ofox.shape[2]), tok),
            pl.BlockSpec((1, tm, osgu.shape[2]), tok),
            pl.BlockSpec((1, 6, d), lambda b, i: (b, 0, 0)),
            pl.BlockSpec((1, d), const2),
            pl.BlockSpec(wo.shape, const2, pipeline_mode=single),
            pl.BlockSpec(w1.shape, const2, pipeline_mode=single),
            pl.BlockSpec(w2.shape, const2, pipeline_mode=single),
        ],
        out_specs=pl.BlockSpec((1, tm, d), tok),
        out_shape=jax.ShapeDtypeStruct((bsz, s, d), F32),
        compiler_params=pltpu.CompilerParams(
            dimension_semantics=("arbitrary", "arbitrary"), vmem_limit_bytes=VMEM_LIMIT),
        name="out_mlp",
    )(x, osb, ofox, osgu, mod_l, g2, wo, w1, w2)


def kernel(x, c, ada_w, ada_b, norm1_g, norm2_g, w_in, b_forget, q_norm_g, k_norm_g, sgu_norm_g,
           sgu_w, sgu_b, w_out, mlp_w1, mlp_w2):
    depth, d, _ = ada_w.shape
    bsz, s, _ = x.shape
    fox_heads = b_forget.shape[1]
    fox_w = fox_heads * HEAD_DIM
    sgu_groups, chunk = sgu_b.shape[1], sgu_b.shape[2]
    sgu_wd = sgu_groups * sgu_norm_g.shape[2]
    sb_w = (w_in.shape[2] - 3 * fox_w - fox_heads - 2 * sgu_wd) // 3
    f_lo = 3 * sb_w + 3 * fox_w

    mod = _modulation(c, ada_w, ada_b).reshape(depth, bsz, 6, d)

    for l in range(depth):
        w = jnp.concatenate([w_in[l][:, :f_lo], w_in[l][:, f_lo + fox_heads:]], axis=1).astype(BF16)
        wf = jnp.pad(w_in[l][:, f_lo:f_lo + fox_heads], ((0, 0), (0, LANES - fox_heads)))
        wf_hi = wf.astype(BF16)
        wf = jnp.concatenate([wf_hi, (wf - wf_hi.astype(F32)).astype(BF16)], axis=1)
        bf = jnp.pad(b_forget[l], (0, LANES - fox_heads)).reshape(1, LANES)
        qg = jnp.tile(q_norm_g[l], MXU_DIM // HEAD_DIM).reshape(1, MXU_DIM)
        kg = jnp.tile(k_norm_g[l], MXU_DIM // HEAD_DIM).reshape(1, MXU_DIM)
        sg = sgu_norm_g[l].reshape(1, sgu_wd)
        sb = jnp.repeat(sgu_b[l].T, sgu_norm_g.shape[2], axis=1)

        qk_bound = (1.03 * HEAD_DIM ** 0.5) * jnp.max(jnp.abs(q_norm_g[l])) * jnp.max(jnp.abs(k_norm_g[l]))
        qx = jnp.zeros((LANES,), F32).at[HEAD_DIM:HEAD_DIM + 3].set(-1.0).at[HEAD_DIM + 6].set(-qk_bound)
        qa, ka, vat, qf, kf, vft, osgu, f = _in_proj(
            x, mod[l], norm1_g[l].reshape(1, d), w, wf, bf, qx.reshape(1, LANES), qg, kg, sg, sgu_w[l], sb,
            tm=512, sb_w=sb_w, fox_w=fox_w, sgu_w=sgu_wd)
        osb = _sb_attention(qa, ka, vat, tq=256)
        ofox = _fox_attention(qf, kf, vft, f, qk_bound, tq=256)
        x = _out_mlp(x, osb, ofox, osgu, mod[l], norm2_g[l].reshape(1, d),
                     w_out[l].astype(BF16), mlp_w1[l].astype(BF16), mlp_w2[l].astype(BF16), tm=512)
    return x
```

```python
import functools
import math

import jax
import jax.numpy as jnp
from jax import lax
from jax.experimental import pallas as pl
from jax.experimental.pallas import tpu as pltpu

HEAD_DIM = 64
LANES = 128
MXU_DIM = 256
EPS = 1e-6
NEG_BIG = -1e30
EXP_ZERO = -104.0
MAX_UNSTABILISED_LOGIT = 40.0
VMEM_LIMIT = 56 * 1024 * 1024

F32 = jnp.float32
BF16 = jnp.bfloat16


def _dot(a, b):
    return jnp.dot(a, b, preferred_element_type=F32)


def _dot_nt(a, b):
    return lax.dot_general(a, b, (((1,), (1,)), ((), ())), preferred_element_type=F32)


def _block_start(j, size):
    return j * size if isinstance(j, int) else pl.multiple_of(j * size, size)


def _split2(x):
    hi = x.astype(BF16)
    lo = (x - hi.astype(F32)).astype(BF16)
    return hi, lo


def _split3(x):
    hi = x.astype(BF16)
    r = x - hi.astype(F32)
    mid = r.astype(BF16)
    lo = (r - mid.astype(F32)).astype(BF16)
    return hi, mid, lo


def _group_mean_matrix():
    head = jnp.arange(MXU_DIM) // HEAD_DIM
    return jnp.where(head[:, None] == head[None, :], 1.0 / HEAD_DIM, 0.0).astype(BF16)


def _placement_matrix(n_heads):
    row = jnp.arange(LANES)[:, None]
    col = jnp.arange(2 * n_heads * LANES)[None, :]
    tile = col // LANES
    is_q = (tile >= n_heads).astype(jnp.int32)
    part = col % LANES - HEAD_DIM - 3 * is_q
    hit = (part >= 0) & (part < 3) & (row == part * 8 + tile - n_heads * is_q)
    return hit.astype(BF16)


def _head_rmsnorm(t, gmat, gain):
    ms = _dot((t * t).astype(BF16), gmat)
    return t * lax.rsqrt(ms + EPS) * gain


def _gelu_tanh(x):
    c = math.sqrt(2.0 / math.pi)
    return x * (0.5 * (1.0 + jnp.tanh(c * (x + 0.044715 * (x * x * x)))))


def _log_sigmoid(x):
    return jnp.minimum(x, 0.0) - jnp.log(1.0 + jnp.exp(-jnp.abs(x)))


def _mod_kernel(ct_ref, w_ref, b_ref, o_ref):
    ct = ct_ref[...]
    cond = ct * (1.0 / (1.0 + jnp.exp(-ct)))
    w = w_ref[0]
    rows = [jnp.sum(cond[:, b:b + 1] * w, axis=0, keepdims=True) for b in range(ct.shape[1])]
    o_ref[0] = jnp.concatenate(rows, axis=0) + b_ref[0]


def _modulation(c, ada_w, ada_b):
    depth, d, n = ada_w.shape
    bsz = c.shape[0]
    tn = 1536
    return pl.pallas_call(
        _mod_kernel,
        grid=(depth, n // tn),
        in_specs=[
            pl.BlockSpec((d, bsz), lambda l, j: (0, 0)),
            pl.BlockSpec((1, d, tn), lambda l, j: (l, 0, j)),
            pl.BlockSpec((1, 1, tn), lambda l, j: (l, 0, j)),
        ],
        out_specs=pl.BlockSpec((1, bsz, tn), lambda l, j: (l, 0, j)),
        out_shape=jax.ShapeDtypeStruct((depth, bsz, n), F32),
        compiler_params=pltpu.CompilerParams(
            dimension_semantics=("arbitrary", "arbitrary"), vmem_limit_bytes=VMEM_LIMIT),
        name="adaln_mod",
    )(c.T, ada_w, ada_b.reshape(depth, 1, n))


def _in_kernel(x_ref, mod_ref, g1_ref, w_ref, wf_ref, bf_ref, qx_ref, qg_ref, kg_ref,
               sg_ref, sw_ref, sb_ref, ltri_ref, sel_ref, gmat_ref,
               qa_ref, ka_ref, vat_ref, qf_ref, kf_ref, vft_ref, og_ref, f_ref,
               carry_ref, proj_ref, *, sb_w, fox_w, sgu_w, chunk):
    @pl.when(pl.program_id(1) == 0)
    def _():
        carry_ref[...] = jnp.zeros_like(carry_ref)

    tm = x_ref.shape[1]
    x = x_ref[0]
    ms = jnp.mean(x * x, axis=-1, keepdims=True)
    h = x * lax.rsqrt(ms + EPS) * g1_ref[...]
    h = h * (1.0 + mod_ref[0, 1:2, :]) + mod_ref[0, 0:1, :]
    hb = h.astype(BF16)
    scale = HEAD_DIM ** -0.5
    proj_ref[...] = _dot(hb, w_ref[...])

    o = 0
    qa_ref[0] = (proj_ref[:, o:o + sb_w] * scale).astype(BF16); o += sb_w
    ka_ref[0] = proj_ref[:, o:o + sb_w].astype(BF16); o += sb_w
    vat_ref[0] = proj_ref[:, o:o + sb_w].T.astype(BF16); o += sb_w

    n_heads = f_ref.shape[1]
    lane = lax.broadcasted_iota(jnp.int32, (1, LANES), 1)
    head_lane = lane < n_heads

    def pack3(parts):
        a, b, c = (p.astype(F32) for p in parts)
        return (a + pltpu.roll(b, 8, 1) + pltpu.roll(c, 16, 1)).astype(BF16)

    h_hi, h_lo = _split2(h)
    t2 = _dot(h_hi, wf_ref[...])
    fl = t2[:, :LANES] + t2[:, LANES:] + _dot(h_lo, wf_ref[:, :LANES])
    logf = jnp.where(head_lane, _log_sigmoid(fl + bf_ref[...]), 0.0)
    cp = _dot(ltri_ref[...], pack3(_split3(logf)))
    cum = cp + pltpu.roll(cp, LANES - 8, 1) + pltpu.roll(cp, LANES - 16, 1)
    cum = jnp.where(head_lane, cum, 0.0) + carry_ref[0:1, :]
    carry_ref[0:1, :] = cum[tm - 1:tm, :]
    f_ref[0] = cum.T[0:n_heads, :]
    extras = _dot(pack3(_split3(cum)), sel_ref[...])

    gmat = gmat_ref[...]
    is_head = lane < HEAD_DIM
    q_extra = qx_ref[...]
    k_extra = jnp.where((lane >= HEAD_DIM + 3) & (lane < HEAD_DIM + 7), 1.0, 0.0)
    heads_per_mxu = MXU_DIM // HEAD_DIM

    def head_tile(tn, s):
        half = tn[:, (s // 2) * LANES:(s // 2 + 1) * LANES]
        return half if s % 2 == 0 else pltpu.roll(half, HEAD_DIM, 1)

    for j in range(fox_w // MXU_DIM):
        t = proj_ref[:, o + j * MXU_DIM:o + (j + 1) * MXU_DIM]
        qn = _head_rmsnorm(t, gmat, qg_ref[...]) * scale
        for s in range(heads_per_mxu):
            hh = heads_per_mxu * j + s
            xq = extras[:, (n_heads + hh) * LANES:(n_heads + hh + 1) * LANES] + q_extra
            qf_ref[0, :, hh * LANES:(hh + 1) * LANES] = jnp.where(
                is_head, head_tile(qn, s), xq).astype(BF16)
    o += fox_w
    for j in range(fox_w // MXU_DIM):
        t = proj_ref[:, o + j * MXU_DIM:o + (j + 1) * MXU_DIM]
        kn = _head_rmsnorm(t, gmat, kg_ref[...])
        for s in range(heads_per_mxu):
            hh = heads_per_mxu * j + s
            xk = extras[:, hh * LANES:(hh + 1) * LANES] + k_extra
            kf_ref[0, :, hh * LANES:(hh + 1) * LANES] = jnp.where(
                is_head, head_tile(kn, s), xk).astype(BF16)
    o += fox_w
    vt = proj_ref[:, o:o + fox_w].T.astype(BF16); o += fox_w
    ones = jnp.ones((HEAD_DIM, tm), BF16)
    for hh in range(fox_w // HEAD_DIM):
        vft_ref[0, hh * LANES:hh * LANES + HEAD_DIM, :] = vt[hh * HEAD_DIM:(hh + 1) * HEAD_DIM, :]
        vft_ref[0, hh * LANES + HEAD_DIM:(hh + 1) * LANES, :] = ones

    gu = _gelu_tanh(proj_ref[:, o:o + sgu_w]); o += sgu_w
    gv = _gelu_tanh(proj_ref[:, o:o + sgu_w]); o += sgu_w
    lane_c = lax.broadcasted_iota(jnp.int32, (chunk, LANES), 1)
    rr = lax.broadcasted_iota(jnp.int32, (chunk, chunk), 0)
    cs = lax.broadcasted_iota(jnp.int32, (chunk, chunk), 1)
    wt = [jnp.where(rr >= cs, sw_ref[g], 0.0).astype(BF16) for g in range(sw_ref.shape[0])]
    vn_all = _head_rmsnorm(gv, gmat, sg_ref[...]).astype(BF16)
    for p in range(sgu_w // LANES):
        vn = vn_all[:, p * LANES:(p + 1) * LANES]
        for ci in range(tm // chunk):
            vblk = vn[ci * chunk:(ci + 1) * chunk, :]
            mixed = jnp.where(lane_c < HEAD_DIM, _dot(wt[2 * p], vblk), _dot(wt[2 * p + 1], vblk))
            mixed = mixed + sb_ref[:, p * LANES:(p + 1) * LANES]
            og_ref[0, ci * chunk:(ci + 1) * chunk, p * LANES:(p + 1) * LANES] = (
                gu[ci * chunk:(ci + 1) * chunk, p * LANES:(p + 1) * LANES] * mixed).astype(BF16)


def _in_proj(x, mod_l, g1, w, wf, bf, qx, qg, kg, sg, sw, sb, *, tm, sb_w, fox_w, sgu_w):
    bsz, s, d = x.shape
    chunk = sw.shape[-1]
    wn = w.shape[1]
    fox_heads = fox_w // HEAD_DIM
    assert fox_heads <= 8 and fox_w % MXU_DIM == 0 and sgu_w == MXU_DIM
    tok_idx = jnp.arange(tm)
    ltri = (tok_idx[:, None] >= tok_idx[None, :]).astype(BF16)
    const2 = lambda b, i: (0, 0)
    tok = lambda b, i: (b, i, 0)
    tok_t = lambda b, i: (b, 0, i)
    kern = functools.partial(_in_kernel, sb_w=sb_w, fox_w=fox_w, sgu_w=sgu_w, chunk=chunk)
    out_shape = [
        jax.ShapeDtypeStruct((bsz, s, sb_w), BF16),
        jax.ShapeDtypeStruct((bsz, s, sb_w), BF16),
        jax.ShapeDtypeStruct((bsz, sb_w, s), BF16),
        jax.ShapeDtypeStruct((bsz, s, fox_heads * LANES), BF16),
        jax.ShapeDtypeStruct((bsz, s, fox_heads * LANES), BF16),
        jax.ShapeDtypeStruct((bsz, fox_heads * LANES, s), BF16),
        jax.ShapeDtypeStruct((bsz, s, sgu_w), BF16),
        jax.ShapeDtypeStruct((bsz, fox_heads, s), F32),
    ]
    out_specs = [
        pl.BlockSpec((1, tm, sb_w), tok),
        pl.BlockSpec((1, tm, sb_w), tok),
        pl.BlockSpec((1, sb_w, tm), tok_t),
        pl.BlockSpec((1, tm, fox_heads * LANES), tok),
        pl.BlockSpec((1, tm, fox_heads * LANES), tok),
        pl.BlockSpec((1, fox_heads * LANES, tm), tok_t),
        pl.BlockSpec((1, tm, sgu_w), tok),
        pl.BlockSpec((1, fox_heads, tm), tok_t),
    ]
    return pl.pallas_call(
        kern,
        grid=(bsz, s // tm),
        in_specs=[
            pl.BlockSpec((1, tm, d), tok),
            pl.BlockSpec((1, 6, d), lambda b, i: (b, 0, 0)),
            pl.BlockSpec((1, d), const2),
            pl.BlockSpec((d, wn), const2),
            pl.BlockSpec((d, 2 * LANES), const2),
            pl.BlockSpec((1, LANES), const2),
            pl.BlockSpec((1, LANES), const2),
            pl.BlockSpec((1, MXU_DIM), const2),
            pl.BlockSpec((1, MXU_DIM), const2),
            pl.BlockSpec((1, sgu_w), const2),
            pl.BlockSpec(sw.shape, lambda b, i: (0, 0, 0)),
            pl.BlockSpec((chunk, sgu_w), const2),
            pl.BlockSpec((tm, tm), const2),
            pl.BlockSpec((LANES, 2 * fox_heads * LANES), const2),
            pl.BlockSpec((MXU_DIM, MXU_DIM), const2),
        ],
        out_specs=out_specs,
        out_shape=out_shape,
        scratch_shapes=[pltpu.VMEM((8, LANES), F32), pltpu.VMEM((tm, wn), F32)],
        compiler_params=pltpu.CompilerParams(
            dimension_semantics=("arbitrary", "arbitrary"), vmem_limit_bytes=VMEM_LIMIT),
        name="in_proj",
    )(x, mod_l, g1, w, wf, bf, qx, qg, kg, sg, sw, sb, ltri, _placement_matrix(fox_heads),
      _group_mean_matrix())


def _sb_kernel(q_ref, k_ref, vt_ref, after_ref, o_ref, acc_ref, c_ref, *, tq):
    tk = tq
    n_blk = q_ref.shape[1] // tq
    lane = lax.broadcasted_iota(jnp.int32, (1, LANES), 1)
    key = lax.broadcasted_iota(jnp.int32, (tk, tq), 0)
    qry = lax.broadcasted_iota(jnp.int32, (tk, tq), 1)
    causal = key < qry
    after = after_ref[...]

    def process(chains):
        units, starts, qh = [], {}, {}
        for slot, qi, blocks, first in chains:
            q = q_ref[0, pl.ds(_block_start(qi, tq), tq), :]
            qh[slot] = (jnp.where(lane < HEAD_DIM, q, 0).astype(BF16),
                        jnp.where(lane >= HEAD_DIM, q, 0).astype(BF16))
            for b, j in enumerate(blocks):
                starts[slot, b] = _block_start(j, tk)
                units += [(slot, b, h, first and b == 0) for h in range(2)]
        z = {(sl, b, h): _dot_nt(k_ref[0, pl.ds(starts[sl, b], tk), :], qh[sl][h])
             for sl, b, h, _ in units}
        l1mb, head = {}, {}
        for sl, b, h, diag in units:
            zz = z[sl, b, h]
            lg = -(jnp.maximum(zz, 0.0) + jnp.log(1.0 + jnp.exp(-jnp.abs(zz))))
            if diag:
                lg = jnp.where(causal, lg, 0.0)
            l1mb[sl, b, h] = lg.astype(BF16)
            head[sl, b, h] = (zz + lg, lg[0:1, :])
        between = {(sl, b, h): _dot(after, l1mb[sl, b, h]) for sl, b, h, _ in units}
        a = {}
        for slot, qi, blocks, first in chains:
            for h in range(2):
                c = None if first else c_ref[slot, h]
                for b in range(len(blocks)):
                    e = head[slot, b, h][0] + between[slot, b, h]
                    if c is not None:
                        e = e + c
                    w = jnp.exp(e)
                    if first and b == 0:
                        w = jnp.where(causal, w, 0.0)
                    a[slot, b, h] = w.astype(BF16)
                    block_sum = between[slot, b, h][0:1, :] + head[slot, b, h][1]
                    c = block_sum if c is None else c + block_sum
                c_ref[slot, h] = c
        for slot, qi, blocks, first in chains:
            for h in range(2):
                pv = None
                for b in range(len(blocks)):
                    part = _dot(vt_ref[0, :, pl.ds(starts[slot, b], tk)], a[slot, b, h])
                    pv = part if pv is None else pv + part
                if first:
                    acc_ref[slot, h] = pv
                else:
                    acc_ref[slot, h] += pv

    def carry_max(slot):
        cm = jnp.maximum(jnp.max(c_ref[slot, 0], axis=1, keepdims=True),
                         jnp.max(c_ref[slot, 1], axis=1, keepdims=True))
        return cm[0, 0]

    chan = lax.broadcasted_iota(jnp.int32, (LANES, tq), 0)

    def finish(slot, qi):
        def cond(carry):
            j, cmax = carry
            return jnp.logical_and(j >= 0, cmax > EXP_ZERO)

        def body(carry):
            j, _ = carry
            process([(slot, qi, [j], False)])
            return j - 1, carry_max(slot)

        lax.while_loop(cond, body, (jnp.asarray(qi - 2, jnp.int32), carry_max(slot)))
        o_ref[0, pl.ds(_block_start(qi, tq), tq), :] = (
            jnp.where(chan < HEAD_DIM, acc_ref[slot, 0], acc_ref[slot, 1]).T.astype(o_ref.dtype))

    process([(0, 0, [0], True)])
    finish(0, 0)

    def two_blocks(g, carry):
        qa, qb = 1 + 2 * g, 2 + 2 * g
        process([(0, qa, [qa, qa - 1], True), (1, qb, [qb, qb - 1], True)])
        finish(0, qa)
        finish(1, qb)
        return carry

    lax.fori_loop(0, (n_blk - 1) // 2, two_blocks, 0)
    if (n_blk - 1) % 2:
        last = n_blk - 1
        process([(0, last, [last, last - 1], True)])
        finish(0, last)


def _sb_attention(q, k, vt, *, tq):
    bsz, s, w = q.shape
    assert s >= 2 * tq
    idx = jnp.arange(tq)
    after = (idx[None, :] > idx[:, None]).astype(BF16)
    kern = functools.partial(_sb_kernel, tq=tq)
    return pl.pallas_call(
        kern,
        grid=(bsz, w // LANES),
        in_specs=[
            pl.BlockSpec((1, s, LANES), lambda b, p: (b, 0, p)),
            pl.BlockSpec((1, s, LANES), lambda b, p: (b, 0, p)),
            pl.BlockSpec((1, LANES, s), lambda b, p: (b, p, 0)),
            pl.BlockSpec((tq, tq), lambda b, p: (0, 0)),
        ],
        out_specs=pl.BlockSpec((1, s, LANES), lambda b, p: (b, 0, p)),
        out_shape=jax.ShapeDtypeStruct((bsz, s, w), BF16),
        scratch_shapes=[pltpu.VMEM((2, 2, LANES, tq), F32), pltpu.VMEM((2, 2, 1, tq), F32)],
        compiler_params=pltpu.CompilerParams(
            dimension_semantics=("arbitrary", "arbitrary"), vmem_limit_bytes=VMEM_LIMIT),
        name="sb_attn",
    )(q, k, vt, after)


def _fox_kernel(fend_ref, par_ref, q_ref, k_ref, vt_ref, o_ref, acc_ref, m_ref, p_ref, *, tq, n_heads):
    tk = tq
    n_blk = q_ref.shape[1] // tq
    head0 = (pl.program_id(0) * n_heads + 2 * pl.program_id(1))

    def first_dead_block(head, qi):
        base = head * n_blk
        f_q = fend_ref[base + jnp.maximum(qi - 1, 0)]

        def alive(j):
            jc = jnp.maximum(j, 0)
            return jnp.logical_and(j >= 0, f_q - fend_ref[base + jc] >= -par_ref[0])

        return lax.while_loop(alive, lambda j: j - 1, qi - 1)

    def plan(qi):
        j_dead = jnp.minimum(first_dead_block(head0, qi), first_dead_block(head0 + 1, qi))
        n_left = jnp.maximum(qi - 2 - j_dead, 0)
        odd = n_left % 2
        has_dead_below = j_dead >= 0
        n_pairs = n_left // 2 + jnp.where(has_dead_below, odd, 0)
        return n_pairs, jnp.logical_and(odd == 1, jnp.logical_not(has_dead_below))

    def q_tile(qi, h):
        return q_ref[0, pl.ds(pl.multiple_of(qi * tq, tq), tq), h * LANES:(h + 1) * LANES]

    def finalize(qi):
        outs = []
        for h in range(2):
            acc = acc_ref[h]
            outs.append(acc[0:HEAD_DIM, :] / acc[HEAD_DIM:HEAD_DIM + 1, :])
        o_ref[0, pl.ds(pl.multiple_of(qi * tq, tq), tq), :] = (
            jnp.concatenate(outs, axis=0).T.astype(o_ref.dtype))

    key = lax.broadcasted_iota(jnp.int32, (tk, tq), 0)
    qry = lax.broadcasted_iota(jnp.int32, (tk, tq), 1)
    causal = key <= qry

    def process(qi, blocks, first):
        starts = [pl.multiple_of(j * tk, tk) for j in blocks]
        units = [(b, h) for b in range(len(blocks)) for h in range(2)]
        s = {(b, h): _dot_nt(k_ref[0, pl.ds(starts[b], tk), h * LANES:(h + 1) * LANES],
                             q_tile(qi, h)) for b, h in units}
        p = {}
        alpha = {}
        for h in range(2):
            if first:
                s[0, h] = jnp.where(causal, s[0, h], NEG_BIG)
            m_new = None if first else m_ref[h]
            for b in range(len(blocks)):
                mb = jnp.max(s[b, h], axis=0, keepdims=True)
                m_new = mb if m_new is None else jnp.maximum(m_new, mb)
            if not first:
                alpha[h] = jnp.exp(m_ref[h] - m_new)
            for b in range(len(blocks)):
                p[b, h] = jnp.exp(s[b, h] - m_new).astype(BF16)
            m_ref[h] = m_new
        for h in range(2):
            pv = None
            for b in range(len(blocks)):
                part = _dot(vt_ref[0, h * LANES:(h + 1) * LANES, pl.ds(starts[b], tk)], p[b, h])
                pv = part if pv is None else pv + part
            acc_ref[h] = pv if first else alpha[h] * acc_ref[h] + pv

    def online_path():
        def q_block(qi, carry):
            n_pairs, lone_block0 = plan(qi)

            @pl.when(qi == 0)
            def _():
                process(qi, [qi], True)

            @pl.when(qi > 0)
            def _():
                process(qi, [qi, qi - 1], True)

            def body(i, c):
                j = qi - 2 - 2 * i
                process(qi, [j, j - 1], False)
                return c

            lax.fori_loop(0, n_pairs, body, 0)

            @pl.when(lone_block0)
            def _():
                process(qi, [0], False)

            finalize(qi)
            return carry

        lax.fori_loop(0, n_blk, q_block, 0)

    win = 2 * tk
    wrow = lax.broadcasted_iota(jnp.int32, (win, tq), 0)
    wcol = lax.broadcasted_iota(jnp.int32, (win, tq), 1)

    def window(qi, i):
        jl = qi - 1 - 2 * i
        j_lo = jnp.maximum(jl, 0)
        return j_lo, pl.multiple_of(j_lo * tk, tk), jnp.where(jl < 0, tk, win)

    def stage_a(qi, i, slot, first):
        j_lo, start, row_lim = window(qi, i)
        keep = wrow < row_lim
        if first:
            keep = jnp.logical_and(keep, wrow <= wcol + (qi - j_lo) * tk)
        for h in range(2):
            s = _dot_nt(k_ref[0, pl.ds(start, win), h * LANES:(h + 1) * LANES], q_tile(qi, h))
            p_ref[slot, h] = jnp.exp(jnp.where(keep, s, NEG_BIG)).astype(BF16)

    def stage_c(qi, i, slot):
        _, start, _ = window(qi, i)
        for h in range(2):
            acc_ref[h] += _dot(vt_ref[0, h * LANES:(h + 1) * LANES, pl.ds(start, win)], p_ref[slot, h])

    def bounded_path():
        acc_ref[...] = jnp.zeros_like(acc_ref)
        stage_a(0, 0, 0, True)

        def q_block(qi, t):
            n_pairs, lone_block0 = plan(qi)
            n_stages = 1 + n_pairs + jnp.where(lone_block0, 1, 0)

            def body(i, t):
                stage_c(qi, i - 1, t & 1)
                stage_a(qi, i, (t + 1) & 1, False)
                return t + 1

            t = lax.fori_loop(1, n_stages, body, t)
            stage_c(qi, n_stages - 1, t & 1)
            stage_a(jnp.minimum(qi + 1, n_blk - 1), 0, (t + 1) & 1, True)
            finalize(qi)
            acc_ref[...] = jnp.zeros_like(acc_ref)
            return t + 1

        lax.fori_loop(0, n_blk, q_block, jnp.int32(0))

    bounded = par_ref[1] > 0.5
    pl.when(bounded)(bounded_path)
    pl.when(jnp.logical_not(bounded))(online_path)


def _fox_attention(q, k, vt, f, qk_bound, *, tq):
    bsz, s, w = q.shape
    pair = 2 * LANES
    n_heads = w // LANES
    assert s >= 2 * tq
    f_end = f[:, :, tq - 1::tq].reshape(-1)
    par = jnp.stack([2.0 * qk_bound - EXP_ZERO,
                     jnp.where(qk_bound <= MAX_UNSTABILISED_LOGIT, 1.0, 0.0)]).astype(F32)
    kern = functools.partial(_fox_kernel, tq=tq, n_heads=n_heads)
    grid_spec = pltpu.PrefetchScalarGridSpec(
        num_scalar_prefetch=2,
        grid=(bsz, w // pair),
        in_specs=[
            pl.BlockSpec((1, s, pair), lambda b, p, fe, th: (b, 0, p)),
            pl.BlockSpec((1, s, pair), lambda b, p, fe, th: (b, 0, p)),
            pl.BlockSpec((1, pair, s), lambda b, p, fe, th: (b, p, 0)),
        ],
        out_specs=pl.BlockSpec((1, s, LANES), lambda b, p, fe, th: (b, 0, p)),
        scratch_shapes=[pltpu.VMEM((2, LANES, tq), F32), pltpu.VMEM((2, 1, tq), F32),
                        pltpu.VMEM((2, 2, 2 * tq, tq), BF16)],
    )
    return pl.pallas_call(
        kern,
        grid_spec=grid_spec,
        out_shape=jax.ShapeDtypeStruct((bsz, s, w // 2), BF16),
        compiler_params=pltpu.CompilerParams(
            dimension_semantics=("arbitrary", "arbitrary"), vmem_limit_bytes=VMEM_LIMIT),
        name="fox_attn",
    )(f_end, par, q, k, vt)


def _out_kernel(x_ref, osb_ref, ofox_ref, osgu_ref, mod_ref, g2_ref, wo_ref, w1_ref, w2_ref,
                o_ref, *, ff_chunk):
    sb_w = osb_ref.shape[2]
    fox_w = ofox_ref.shape[2]
    x = x_ref[0]
    mix = (_dot(osb_ref[0], wo_ref[0:sb_w, :])
           + _dot(ofox_ref[0], wo_ref[sb_w:sb_w + fox_w, :])
           + _dot(osgu_ref[0], wo_ref[sb_w + fox_w:, :]))
    x1 = x + mod_ref[0, 2:3, :] * mix
    ms = jnp.mean(x1 * x1, axis=-1, keepdims=True)
    h = x1 * lax.rsqrt(ms + EPS) * g2_ref[...]
    hb = (h * (1.0 + mod_ref[0, 4:5, :]) + mod_ref[0, 3:4, :]).astype(BF16)
    d_ff = w1_ref.shape[1]
    acc = None
    for c in range(d_ff // ff_chunk):
        hid = jnp.maximum(_dot(hb, w1_ref[:, c * ff_chunk:(c + 1) * ff_chunk]), 0.0)
        part = _dot((hid * hid).astype(BF16), w2_ref[c * ff_chunk:(c + 1) * ff_chunk, :])
        acc = part if acc is None else acc + part
    o_ref[0] = x1 + mod_ref[0, 5:6, :] * acc


def _out_mlp(x, osb, ofox, osgu, mod_l, g2, wo, w1, w2, *, tm):
    bsz, s, d = x.shape
    tok = lambda b, i: (b, i, 0)
    const2 = lambda b, i: (0, 0)
    single = pl.Buffered(1)
    kern = functools.partial(_out_kernel, ff_chunk=1024)
    return pl.pallas_call(
        kern,
        grid=(bsz, s // tm),
        in_specs=[
            pl.BlockSpec((1, tm, d), tok),
            pl.BlockSpec((1, tm, osb.shape[2]), tok),
            pl.BlockSpec((1, tm, ofox.shape[2]), tok),
            pl.BlockSpec((1, tm, osgu.shape[2]), tok),
            pl.BlockSpec((1, 6, d), lambda b, i: (b, 0, 0)),
            pl.BlockSpec((1, d), const2),
            pl.BlockSpec(wo.shape, const2, pipeline_mode=single),
            pl.BlockSpec(w1.shape, const2, pipeline_mode=single),
            pl.BlockSpec(w2.shape, const2, pipeline_mode=single),
        ],
        out_specs=pl.BlockSpec((1, tm, d), tok),
        out_shape=jax.ShapeDtypeStruct((bsz, s, d), F32),
        compiler_params=pltpu.CompilerParams(
            dimension_semantics=("arbitrary", "arbitrary"), vmem_limit_bytes=VMEM_LIMIT),
        name="out_mlp",
    )(x, osb, ofox, osgu, mod_l, g2, wo, w1, w2)


def kernel(x, c, ada_w, ada_b, norm1_g, norm2_g, w_in, b_forget, q_norm_g, k_norm_g, sgu_norm_g,
           sgu_w, sgu_b, w_out, mlp_w1, mlp_w2):
    depth, d, _ = ada_w.shape
    bsz, s, _ = x.shape
    fox_heads = b_forget.shape[1]
    fox_w = fox_heads * HEAD_DIM
    sgu_groups, chunk = sgu_b.shape[1], sgu_b.shape[2]
    sgu_wd = sgu_groups * sgu_norm_g.shape[2]
    sb_w = (w_in.shape[2] - 3 * fox_w - fox_heads - 2 * sgu_wd) // 3
    f_lo = 3 * sb_w + 3 * fox_w

    mod = _modulation(c, ada_w, ada_b).reshape(depth, bsz, 6, d)

    for l in range(depth):
        w = jnp.concatenate([w_in[l][:, :f_lo], w_in[l][:, f_lo + fox_heads:]], axis=1).astype(BF16)
        wf = jnp.pad(w_in[l][:, f_lo:f_lo + fox_heads], ((0, 0), (0, LANES - fox_heads)))
        wf_hi = wf.astype(BF16)
        wf = jnp.concatenate([wf_hi, (wf - wf_hi.astype(F32)).astype(BF16)], axis=1)
        bf = jnp.pad(b_forget[l], (0, LANES - fox_heads)).reshape(1, LANES)
        qg = jnp.tile(q_norm_g[l], MXU_DIM // HEAD_DIM).reshape(1, MXU_DIM)
        kg = jnp.tile(k_norm_g[l], MXU_DIM // HEAD_DIM).reshape(1, MXU_DIM)
        sg = sgu_norm_g[l].reshape(1, sgu_wd)
        sb = jnp.repeat(sgu_b[l].T, sgu_norm_g.shape[2], axis=1)

        qk_bound = (1.03 * HEAD_DIM ** 0.5) * jnp.max(jnp.abs(q_norm_g[l])) * jnp.max(jnp.abs(k_norm_g[l]))
        qx = jnp.zeros((LANES,), F32).at[HEAD_DIM:HEAD_DIM + 3].set(-1.0).at[HEAD_DIM + 6].set(-qk_bound)
        qa, ka, vat, qf, kf, vft, osgu, f = _in_proj(
            x, mod[l], norm1_g[l].reshape(1, d), w, wf, bf, qx.reshape(1, LANES), qg, kg, sg, sgu_w[l], sb,
            tm=512, sb_w=sb_w, fox_w=fox_w, sgu_w=sgu_wd)
        osb = _sb_attention(qa, ka, vat, tq=256)
        ofox = _fox_attention(qf, kf, vft, f, qk_bound, tq=256)
        x = _out_mlp(x, osb, ofox, osgu, mod[l], norm2_g[l].reshape(1, d),
                     w_out[l].astype(BF16), mlp_w1[l].astype(BF16), mlp_w2[l].astype(BF16), tm=512)
    return x
```

```python
import functools
import math

import jax
import jax.numpy as jnp
from jax import lax
from jax.experimental import pallas as pl
from jax.experimental.pallas import tpu as pltpu

HEAD_DIM = 64
LANES = 128
MXU_DIM = 256
EPS = 1e-6
NEG_BIG = -1e30
EXP_ZERO = -104.0
MAX_UNSTABILISED_LOGIT = 40.0
VMEM_LIMIT = 56 * 1024 * 1024

F32 = jnp.float32
BF16 = jnp.bfloat16


def _dot(a, b):
    return jnp.dot(a, b, preferred_element_type=F32)


def _dot_nt(a, b):
    return lax.dot_general(a, b, (((1,), (1,)), ((), ())), preferred_element_type=F32)


def _block_start(j, size):
    return j * size if isinstance(j, int) else pl.multiple_of(j * size, size)


def _split2(x):
    hi = x.astype(BF16)
    lo = (x - hi.astype(F32)).astype(BF16)
    return hi, lo


def _split3(x):
    hi = x.astype(BF16)
    r = x - hi.astype(F32)
    mid = r.astype(BF16)
    lo = (r - mid.astype(F32)).astype(BF16)
    return hi, mid, lo


def _group_mean_matrix():
    head = jnp.arange(MXU_DIM) // HEAD_DIM
    return jnp.where(head[:, None] == head[None, :], 1.0 / HEAD_DIM, 0.0).astype(BF16)


def _placement_matrix(n_heads):
    row = jnp.arange(LANES)[:, None]
    col = jnp.arange(2 * n_heads * LANES)[None, :]
    tile = col // LANES
    is_q = (tile >= n_heads).astype(jnp.int32)
    part = col % LANES - HEAD_DIM - 3 * is_q
    hit = (part >= 0) & (part < 3) & (row == part * 8 + tile - n_heads * is_q)
    return hit.astype(BF16)


def _head_rmsnorm(t, gmat, gain):
    ms = _dot((t * t).astype(BF16), gmat)
    return t * lax.rsqrt(ms + EPS) * gain


def _gelu_tanh(x):
    c = math.sqrt(2.0 / math.pi)
    return x * (0.5 * (1.0 + jnp.tanh(c * (x + 0.044715 * (x * x * x)))))


def _log_sigmoid(x):
    return jnp.minimum(x, 0.0) - jnp.log(1.0 + jnp.exp(-jnp.abs(x)))


def _mod_kernel(ct_ref, w_ref, b_ref, o_ref):
    ct = ct_ref[...]
    cond = ct * (1.0 / (1.0 + jnp.exp(-ct)))
    w = w_ref[0]
    rows = [jnp.sum(cond[:, b:b + 1] * w, axis=0, keepdims=True) for b in range(ct.shape[1])]
    o_ref[0] = jnp.concatenate(rows, axis=0) + b_ref[0]


def _modulation(c, ada_w, ada_b):
    depth, d, n = ada_w.shape
    bsz = c.shape[0]
    tn = 1536
    return pl.pallas_call(
        _mod_kernel,
        grid=(depth, n // tn),
        in_specs=[
            pl.BlockSpec((d, bsz), lambda l, j: (0, 0)),
            pl.BlockSpec((1, d, tn), lambda l, j: (l, 0, j)),
            pl.BlockSpec((1, 1, tn), lambda l, j: (l, 0, j)),
        ],
        out_specs=pl.BlockSpec((1, bsz, tn), lambda l, j: (l, 0, j)),
        out_shape=jax.ShapeDtypeStruct((depth, bsz, n), F32),
        compiler_params=pltpu.CompilerParams(
            dimension_semantics=("arbitrary", "arbitrary"), vmem_limit_bytes=VMEM_LIMIT),
        name="adaln_mod",
    )(c.T, ada_w, ada_b.reshape(depth, 1, n))


def _in_kernel(x_ref, mod_ref, g1_ref, w_ref, wf_ref, bf_ref, qx_ref, qg_ref, kg_ref,
               sg_ref, sw_ref, sb_ref, ltri_ref, sel_ref, gmat_ref,
               qa_ref, ka_ref, vat_ref, qf_ref, kf_ref, vft_ref, og_ref, f_ref,
               carry_ref, proj_ref, *, sb_w, fox_w, sgu_w, chunk):
    @pl.when(pl.program_id(1) == 0)
    def _():
        carry_ref[...] = jnp.zeros_like(carry_ref)

    tm = x_ref.shape[1]
    x = x_ref[0]
    ms = jnp.mean(x * x, axis=-1, keepdims=True)
    h = x * lax.rsqrt(ms + EPS) * g1_ref[...]
    h = h * (1.0 + mod_ref[0, 1:2, :]) + mod_ref[0, 0:1, :]
    hb = h.astype(BF16)
    scale = HEAD_DIM ** -0.5
    proj_ref[...] = _dot(hb, w_ref[...])

    o = 0
    qa_ref[0] = (proj_ref[:, o:o + sb_w] * scale).astype(BF16); o += sb_w
    ka_ref[0] = proj_ref[:, o:o + sb_w].astype(BF16); o += sb_w
    vat_ref[0] = proj_ref[:, o:o + sb_w].T.astype(BF16); o += sb_w

    n_heads = f_ref.shape[1]
    lane = lax.broadcasted_iota(jnp.int32, (1, LANES), 1)
    head_lane = lane < n_heads

    def pack3(parts):
        a, b, c = (p.astype(F32) for p in parts)
        return (a + pltpu.roll(b, 8, 1) + pltpu.roll(c, 16, 1)).astype(BF16)

    h_hi, h_lo = _split2(h)
    t2 = _dot(h_hi, wf_ref[...])
    fl = t2[:, :LANES] + t2[:, LANES:] + _dot(h_lo, wf_ref[:, :LANES])
    logf = jnp.where(head_lane, _log_sigmoid(fl + bf_ref[...]), 0.0)
    cp = _dot(ltri_ref[...], pack3(_split3(logf)))
    cum = cp + pltpu.roll(cp, LANES - 8, 1) + pltpu.roll(cp, LANES - 16, 1)
    cum = jnp.where(head_lane, cum, 0.0) + carry_ref[0:1, :]
    carry_ref[0:1, :] = cum[tm - 1:tm, :]
    f_ref[0] = cum.T[0:n_heads, :]
    extras = _dot(pack3(_split3(cum)), sel_ref[...])

    gmat = gmat_ref[...]
    is_head = lane < HEAD_DIM
    q_extra = qx_ref[...]
    k_extra = jnp.where((lane >= HEAD_DIM + 3) & (lane < HEAD_DIM + 7), 1.0, 0.0)
    heads_per_mxu = MXU_DIM // HEAD_DIM

    def head_tile(tn, s):
        half = tn[:, (s // 2) * LANES:(s // 2 + 1) * LANES]
        return half if s % 2 == 0 else pltpu.roll(half, HEAD_DIM, 1)

    for j in range(fox_w // MXU_DIM):
        t = proj_ref[:, o + j * MXU_DIM:o + (j + 1) * MXU_DIM]
        qn = _head_rmsnorm(t, gmat, qg_ref[...]) * scale
        for s in range(heads_per_mxu):
            hh = heads_per_mxu * j + s
            xq = extras[:, (n_heads + hh) * LANES:(n_heads + hh + 1) * LANES] + q_extra
            qf_ref[0, :, hh * LANES:(hh + 1) * LANES] = jnp.where(
                is_head, head_tile(qn, s), xq).astype(BF16)
    o += fox_w
    for j in range(fox_w // MXU_DIM):
        t = proj_ref[:, o + j * MXU_DIM:o + (j + 1) * MXU_DIM]
        kn = _head_rmsnorm(t, gmat, kg_ref[...])
        for s in range(heads_per_mxu):
            hh = heads_per_mxu * j + s
            xk = extras[:, hh * LANES:(hh + 1) * LANES] + k_extra
            kf_ref[0, :, hh * LANES:(hh + 1) * LANES] = jnp.where(
                is_head, head_tile(kn, s), xk).astype(BF16)
    o += fox_w
    vt = proj_ref[:, o:o + fox_w].T.astype(BF16); o += fox_w
    ones = jnp.ones((HEAD_DIM, tm), BF16)
    for hh in range(fox_w // HEAD_DIM):
        vft_ref[0, hh * LANES:hh * LANES + HEAD_DIM, :] = vt[hh * HEAD_DIM:(hh + 1) * HEAD_DIM, :]
        vft_ref[0, hh * LANES + HEAD_DIM:(hh + 1) * LANES, :] = ones

    gu = _gelu_tanh(proj_ref[:, o:o + sgu_w]); o += sgu_w
    gv = _gelu_tanh(proj_ref[:, o:o + sgu_w]); o += sgu_w
    lane_c = lax.broadcasted_iota(jnp.int32, (chunk, LANES), 1)
    rr = lax.broadcasted_iota(jnp.int32, (chunk, chunk), 0)
    cs = lax.broadcasted_iota(jnp.int32, (chunk, chunk), 1)
    wt = [jnp.where(rr >= cs, sw_ref[g], 0.0).astype(BF16) for g in range(sw_ref.shape[0])]
    vn_all = _head_rmsnorm(gv, gmat, sg_ref[...]).astype(BF16)
    for p in range(sgu_w // LANES):
        vn = vn_all[:, p * LANES:(p + 1) * LANES]
        for ci in range(tm // chunk):
            vblk = vn[ci * chunk:(ci + 1) * chunk, :]
            mixed = jnp.where(lane_c < HEAD_DIM, _dot(wt[2 * p], vblk), _dot(wt[2 * p + 1], vblk))
            mixed = mixed + sb_ref[:, p * LANES:(p + 1) * LANES]
            og_ref[0, ci * chunk:(ci + 1) * chunk, p * LANES:(p + 1) * LANES] = (
                gu[ci * chunk:(ci + 1) * chunk, p * LANES:(p + 1) * LANES] * mixed).astype(BF16)


def _in_proj(x, mod, g1, w, wf, bf, qx, qg, kg, sg, sw, sb, *, layer, tm, sb_w, fox_w, sgu_w):
    bsz, s, d = x.shape
    chunk = sw.shape[-1]
    wn = w.shape[2]
    fox_heads = fox_w // HEAD_DIM

    def of_layer(a):
        nd = a.ndim - 1
        return pl.BlockSpec((None,) + a.shape[1:], lambda b, i: (layer,) + (0,) * nd)

    assert fox_heads <= 8 and fox_w % MXU_DIM == 0 and sgu_w == MXU_DIM
    tok_idx = jnp.arange(tm)
    ltri = (tok_idx[:, None] >= tok_idx[None, :]).astype(BF16)
    const2 = lambda b, i: (0, 0)
    tok = lambda b, i: (b, i, 0)
    tok_t = lambda b, i: (b, 0, i)
    kern = functools.partial(_in_kernel, sb_w=sb_w, fox_w=fox_w, sgu_w=sgu_w, chunk=chunk)
    out_shape = [
        jax.ShapeDtypeStruct((bsz, s, sb_w), BF16),
        jax.ShapeDtypeStruct((bsz, s, sb_w), BF16),
        jax.ShapeDtypeStruct((bsz, sb_w, s), BF16),
        jax.ShapeDtypeStruct((bsz, s, fox_heads * LANES), BF16),
        jax.ShapeDtypeStruct((bsz, s, fox_heads * LANES), BF16),
        jax.ShapeDtypeStruct((bsz, fox_heads * LANES, s), BF16),
        jax.ShapeDtypeStruct((bsz, s, sgu_w), BF16),
        jax.ShapeDtypeStruct((bsz, fox_heads, s), F32),
    ]
    out_specs = [
        pl.BlockSpec((1, tm, sb_w), tok),
        pl.BlockSpec((1, tm, sb_w), tok),
        pl.BlockSpec((1, sb_w, tm), tok_t),
        pl.BlockSpec((1, tm, fox_heads * LANES), tok),
        pl.BlockSpec((1, tm, fox_heads * LANES), tok),
        pl.BlockSpec((1, fox_heads * LANES, tm), tok_t),
        pl.BlockSpec((1, tm, sgu_w), tok),
        pl.BlockSpec((1, fox_heads, tm), tok_t),
    ]
    return pl.pallas_call(
        kern,
        grid=(bsz, s // tm),
        in_specs=[
            pl.BlockSpec((1, tm, d), tok),
            pl.BlockSpec((None, 1, 6, d), lambda b, i: (layer, b, 0, 0)),
            of_layer(g1), of_layer(w), of_layer(wf), of_layer(bf), of_layer(qx), of_layer(qg),
            of_layer(kg), of_layer(sg), of_layer(sw), of_layer(sb),
            pl.BlockSpec((tm, tm), const2),
            pl.BlockSpec((LANES, 2 * fox_heads * LANES), const2),
            pl.BlockSpec((MXU_DIM, MXU_DIM), const2),
        ],
        out_specs=out_specs,
        out_shape=out_shape,
        scratch_shapes=[pltpu.VMEM((8, LANES), F32), pltpu.VMEM((tm, wn), F32)],
        compiler_params=pltpu.CompilerParams(
            dimension_semantics=("arbitrary", "arbitrary"), vmem_limit_bytes=VMEM_LIMIT),
        name="in_proj",
    )(x, mod, g1, w, wf, bf, qx, qg, kg, sg, sw, sb, ltri, _placement_matrix(fox_heads),
      _group_mean_matrix())


def _sb_kernel(q_ref, k_ref, vt_ref, after_ref, o_ref, acc_ref, c_ref, *, tq):
    tk = tq
    n_blk = q_ref.shape[1] // tq
    lane = lax.broadcasted_iota(jnp.int32, (1, LANES), 1)
    key = lax.broadcasted_iota(jnp.int32, (tk, tq), 0)
    qry = lax.broadcasted_iota(jnp.int32, (tk, tq), 1)
    causal = key < qry
    after = after_ref[...]

    def process(chains):
        units, starts, qh = [], {}, {}
        for slot, qi, blocks, first in chains:
            q = q_ref[0, pl.ds(_block_start(qi, tq), tq), :]
            qh[slot] = (jnp.where(lane < HEAD_DIM, q, 0).astype(BF16),
                        jnp.where(lane >= HEAD_DIM, q, 0).astype(BF16))
            for b, j in enumerate(blocks):
                starts[slot, b] = _block_start(j, tk)
                units += [(slot, b, h, first and b == 0) for h in range(2)]
        z = {(sl, b, h): _dot_nt(k_ref[0, pl.ds(starts[sl, b], tk), :], qh[sl][h])
             for sl, b, h, _ in units}
        l1mb, head = {}, {}
        for sl, b, h, diag in units:
            zz = z[sl, b, h]
            lg = -(jnp.maximum(zz, 0.0) + jnp.log(1.0 + jnp.exp(-jnp.abs(zz))))
            if diag:
                lg = jnp.where(causal, lg, 0.0)
            l1mb[sl, b, h] = lg.astype(BF16)
            head[sl, b, h] = (zz + lg, lg[0:1, :])
        between = {(sl, b, h): _dot(after, l1mb[sl, b, h]) for sl, b, h, _ in units}
        a = {}
        for slot, qi, blocks, first in chains:
            for h in range(2):
                c = None if first else c_ref[slot, h]
                for b in range(len(blocks)):
                    e = head[slot, b, h][0] + between[slot, b, h]
                    if c is not None:
                        e = e + c
                    w = jnp.exp(e)
                    if first and b == 0:
                        w = jnp.where(causal, w, 0.0)
                    a[slot, b, h] = w.astype(BF16)
                    block_sum = between[slot, b, h][0:1, :] + head[slot, b, h][1]
                    c = block_sum if c is None else c + block_sum
                c_ref[slot, h] = c
        for slot, qi, blocks, first in chains:
            for h in range(2):
                pv = None
                for b in range(len(blocks)):
                    part = _dot(vt_ref[0, :, pl.ds(starts[slot, b], tk)], a[slot, b, h])
                    pv = part if pv is None else pv + part
                if first:
                    acc_ref[slot, h] = pv
                else:
                    acc_ref[slot, h] += pv

    def carry_max(slot):
        cm = jnp.maximum(jnp.max(c_ref[slot, 0], axis=1, keepdims=True),
                         jnp.max(c_ref[slot, 1], axis=1, keepdims=True))
        return cm[0, 0]

    chan = lax.broadcasted_iota(jnp.int32, (LANES, tq), 0)

    def finish(slot, qi):
        def cond(carry):
            j, cmax = carry
            return jnp.logical_and(j >= 0, cmax > EXP_ZERO)

        def body(carry):
            j, _ = carry
            process([(slot, qi, [j], False)])
            return j - 1, carry_max(slot)

        lax.while_loop(cond, body, (jnp.asarray(qi - 2, jnp.int32), carry_max(slot)))
        o_ref[0, pl.ds(_block_start(qi, tq), tq), :] = (
            jnp.where(chan < HEAD_DIM, acc_ref[slot, 0], acc_ref[slot, 1]).T.astype(o_ref.dtype))

    process([(0, 0, [0], True)])
    finish(0, 0)

    def two_blocks(g, carry):
        qa, qb = 1 + 2 * g, 2 + 2 * g
        process([(0, qa, [qa, qa - 1], True), (1, qb, [qb, qb - 1], True)])
        finish(0, qa)
        finish(1, qb)
        return carry

    lax.fori_loop(0, (n_blk - 1) // 2, two_blocks, 0)
    if (n_blk - 1) % 2:
        last = n_blk - 1
        process([(0, last, [last, last - 1], True)])
        finish(0, last)


def _sb_attention(q, k, vt, *, tq):
    bsz, s, w = q.shape
    assert s >= 2 * tq
    idx = jnp.arange(tq)
    after = (idx[None, :] > idx[:, None]).astype(BF16)
    kern = functools.partial(_sb_kernel, tq=tq)
    return pl.pallas_call(
        kern,
        grid=(bsz, w // LANES),
        in_specs=[
            pl.BlockSpec((1, s, LANES), lambda b, p: (b, 0, p)),
            pl.BlockSpec((1, s, LANES), lambda b, p: (b, 0, p)),
            pl.BlockSpec((1, LANES, s), lambda b, p: (b, p, 0)),
            pl.BlockSpec((tq, tq), lambda b, p: (0, 0)),
        ],
        out_specs=pl.BlockSpec((1, s, LANES), lambda b, p: (b, 0, p)),
        out_shape=jax.ShapeDtypeStruct((bsz, s, w), BF16),
        scratch_shapes=[pltpu.VMEM((2, 2, LANES, tq), F32), pltpu.VMEM((2, 2, 1, tq), F32)],
        compiler_params=pltpu.CompilerParams(
            dimension_semantics=("arbitrary", "arbitrary"), vmem_limit_bytes=VMEM_LIMIT),
        name="sb_attn",
    )(q, k, vt, after)


def _fox_kernel(fend_ref, par_ref, q_ref, k_ref, vt_ref, o_ref, acc_ref, m_ref, p_ref, *, tq, n_heads):
    tk = tq
    n_blk = q_ref.shape[1] // tq
    head0 = (pl.program_id(0) * n_heads + 2 * pl.program_id(1))

    def first_dead_block(head, qi):
        base = head * n_blk
        f_q = fend_ref[base + jnp.maximum(qi - 1, 0)]

        def alive(j):
            jc = jnp.maximum(j, 0)
            return jnp.logical_and(j >= 0, f_q - fend_ref[base + jc] >= -par_ref[0])

        return lax.while_loop(alive, lambda j: j - 1, qi - 1)

    def plan(qi):
        j_dead = jnp.minimum(first_dead_block(head0, qi), first_dead_block(head0 + 1, qi))
        n_left = jnp.maximum(qi - 2 - j_dead, 0)
        odd = n_left % 2
        has_dead_below = j_dead >= 0
        n_pairs = n_left // 2 + jnp.where(has_dead_below, odd, 0)
        return n_pairs, jnp.logical_and(odd == 1, jnp.logical_not(has_dead_below))

    def q_tile(qi, h):
        return q_ref[0, pl.ds(pl.multiple_of(qi * tq, tq), tq), h * LANES:(h + 1) * LANES]

    def finalize(qi):
        outs = []
        for h in range(2):
            acc = acc_ref[h]
            outs.append(acc[0:HEAD_DIM, :] / acc[HEAD_DIM:HEAD_DIM + 1, :])
        o_ref[0, pl.ds(pl.multiple_of(qi * tq, tq), tq), :] = (
            jnp.concatenate(outs, axis=0).T.astype(o_ref.dtype))

    key = lax.broadcasted_iota(jnp.int32, (tk, tq), 0)
    qry = lax.broadcasted_iota(jnp.int32, (tk, tq), 1)
    causal = key <= qry

    def process(qi, blocks, first):
        starts = [pl.multiple_of(j * tk, tk) for j in blocks]
        units = [(b, h) for b in range(len(blocks)) for h in range(2)]
        s = {(b, h): _dot_nt(k_ref[0, pl.ds(starts[b], tk), h * LANES:(h + 1) * LANES],
                             q_tile(qi, h)) for b, h in units}
        p = {}
        alpha = {}
        for h in range(2):
            if first:
                s[0, h] = jnp.where(causal, s[0, h], NEG_BIG)
            m_new = None if first else m_ref[h]
            for b in range(len(blocks)):
                mb = jnp.max(s[b, h], axis=0, keepdims=True)
                m_new = mb if m_new is None else jnp.maximum(m_new, mb)
            if not first:
                alpha[h] = jnp.exp(m_ref[h] - m_new)
            for b in range(len(blocks)):
                p[b, h] = jnp.exp(s[b, h] - m_new).astype(BF16)
            m_ref[h] = m_new
        for h in range(2):
            pv = None
            for b in range(len(blocks)):
                part = _dot(vt_ref[0, h * LANES:(h + 1) * LANES, pl.ds(starts[b], tk)], p[b, h])
                pv = part if pv is None else pv + part
            acc_ref[h] = pv if first else alpha[h] * acc_ref[h] + pv

    def online_path():
        def q_block(qi, carry):
            n_pairs, lone_block0 = plan(qi)

            @pl.when(qi == 0)
            def _():
                process(qi, [qi], True)

            @pl.when(qi > 0)
            def _():
                process(qi, [qi, qi - 1], True)

            def body(i, c):
                j = qi - 2 - 2 * i
                process(qi, [j, j - 1], False)
                return c

            lax.fori_loop(0, n_pairs, body, 0)

            @pl.when(lone_block0)
            def _():
                process(qi, [0], False)

            finalize(qi)
            return carry

        lax.fori_loop(0, n_blk, q_block, 0)

    win = 2 * tk
    wrow = lax.broadcasted_iota(jnp.int32, (win, tq), 0)
    wcol = lax.broadcasted_iota(jnp.int32, (win, tq), 1)

    def window(qi, i):
        jl = qi - 1 - 2 * i
        j_lo = jnp.maximum(jl, 0)
        return j_lo, pl.multiple_of(j_lo * tk, tk), jnp.where(jl < 0, tk, win)

    def stage_a(qi, i, slot, first):
        j_lo, start, row_lim = window(qi, i)
        keep = wrow < row_lim
        if first:
            keep = jnp.logical_and(keep, wrow <= wcol + (qi - j_lo) * tk)
        for h in range(2):
            s = _dot_nt(k_ref[0, pl.ds(start, win), h * LANES:(h + 1) * LANES], q_tile(qi, h))
            p_ref[slot, h] = jnp.exp(jnp.where(keep, s, NEG_BIG)).astype(BF16)

    def stage_c(qi, i, slot):
        _, start, _ = window(qi, i)
        for h in range(2):
            acc_ref[h] += _dot(vt_ref[0, h * LANES:(h + 1) * LANES, pl.ds(start, win)], p_ref[slot, h])

    def bounded_path():
        acc_ref[...] = jnp.zeros_like(acc_ref)
        stage_a(0, 0, 0, True)

        def q_block(qi, t):
            n_pairs, lone_block0 = plan(qi)
            n_stages = 1 + n_pairs + jnp.where(lone_block0, 1, 0)

            def body(i, t):
                stage_c(qi, i - 1, t & 1)
                stage_a(qi, i, (t + 1) & 1, False)
                return t + 1

            t = lax.fori_loop(1, n_stages, body, t)
            stage_c(qi, n_stages - 1, t & 1)
            stage_a(jnp.minimum(qi + 1, n_blk - 1), 0, (t + 1) & 1, True)
            finalize(qi)
            acc_ref[...] = jnp.zeros_like(acc_ref)
            return t + 1

        lax.fori_loop(0, n_blk, q_block, jnp.int32(0))

    bounded = par_ref[1] > 0.5
    pl.when(bounded)(bounded_path)
    pl.when(jnp.logical_not(bounded))(online_path)


def _fox_attention(q, k, vt, f, qk_bound, *, tq):
    bsz, s, w = q.shape
    pair = 2 * LANES
    n_heads = w // LANES
    assert s >= 2 * tq
    f_end = f[:, :, tq - 1::tq].reshape(-1)
    bounded = qk_bound <= MAX_UNSTABILISED_LOGIT
    par = jnp.stack([jnp.where(bounded, 0.5 - EXP_ZERO, 2.0 * qk_bound - EXP_ZERO),
                     jnp.where(bounded, 1.0, 0.0)]).astype(F32)
    kern = functools.partial(_fox_kernel, tq=tq, n_heads=n_heads)
    grid_spec = pltpu.PrefetchScalarGridSpec(
        num_scalar_prefetch=2,
        grid=(bsz, w // pair),
        in_specs=[
            pl.BlockSpec((1, s, pair), lambda b, p, fe, th: (b, 0, p)),
            pl.BlockSpec((1, s, pair), lambda b, p, fe, th: (b, 0, p)),
            pl.BlockSpec((1, pair, s), lambda b, p, fe, th: (b, p, 0)),
        ],
        out_specs=pl.BlockSpec((1, s, LANES), lambda b, p, fe, th: (b, 0, p)),
        scratch_shapes=[pltpu.VMEM((2, LANES, tq), F32), pltpu.VMEM((2, 1, tq), F32),
                        pltpu.VMEM((2, 2, 2 * tq, tq), BF16)],
    )
    return pl.pallas_call(
        kern,
        grid_spec=grid_spec,
        out_shape=jax.ShapeDtypeStruct((bsz, s, w // 2), BF16),
        compiler_params=pltpu.CompilerParams(
            dimension_semantics=("arbitrary", "arbitrary"), vmem_limit_bytes=VMEM_LIMIT),
        name="fox_attn",
    )(f_end, par, q, k, vt)


def _out_kernel(x_ref, osb_ref, ofox_ref, osgu_ref, mod_ref, g2_ref, wo_ref, w1_ref, w2_ref,
                o_ref, *, ff_chunk):
    sb_w = osb_ref.shape[2]
    fox_w = ofox_ref.shape[2]
    x = x_ref[0]
    mix = (_dot(osb_ref[0], wo_ref[0:sb_w, :])
           + _dot(ofox_ref[0], wo_ref[sb_w:sb_w + fox_w, :])
           + _dot(osgu_ref[0], wo_ref[sb_w + fox_w:, :]))
    x1 = x + mod_ref[0, 2:3, :] * mix
    ms = jnp.mean(x1 * x1, axis=-1, keepdims=True)
    h = x1 * lax.rsqrt(ms + EPS) * g2_ref[...]
    hb = (h * (1.0 + mod_ref[0, 4:5, :]) + mod_ref[0, 3:4, :]).astype(BF16)
    d_ff = w1_ref.shape[1]
    acc = None
    for c in range(d_ff // ff_chunk):
        hid = jnp.maximum(_dot(hb, w1_ref[:, c * ff_chunk:(c + 1) * ff_chunk]), 0.0)
        part = _dot((hid * hid).astype(BF16), w2_ref[c * ff_chunk:(c + 1) * ff_chunk, :])
        acc = part if acc is None else acc + part
    o_ref[0] = x1 + mod_ref[0, 5:6, :] * acc


def _out_mlp(x, osb, ofox, osgu, mod, g2, wo, w1, w2, *, layer, tm):
    bsz, s, d = x.shape
    tok = lambda b, i: (b, i, 0)

    def of_layer(a, **kw):
        nd = a.ndim - 1
        return pl.BlockSpec((None,) + a.shape[1:], lambda b, i: (layer,) + (0,) * nd, **kw)

    single = pl.Buffered(1)
    kern = functools.partial(_out_kernel, ff_chunk=1024)
    return pl.pallas_call(
        kern,
        grid=(bsz, s // tm),
        in_specs=[
            pl.BlockSpec((1, tm, d), tok),
            pl.BlockSpec((1, tm, osb.shape[2]), tok),
            pl.BlockSpec((1, tm, ofox.shape[2]), tok),
            pl.BlockSpec((1, tm, osgu.shape[2]), tok),
            pl.BlockSpec((None, 1, 6, d), lambda b, i: (layer, b, 0, 0)),
            of_layer(g2),
            of_layer(wo, pipeline_mode=single),
            of_layer(w1, pipeline_mode=single),
            of_layer(w2, pipeline_mode=single),
        ],
        out_specs=pl.BlockSpec((1, tm, d), tok),
        out_shape=jax.ShapeDtypeStruct((bsz, s, d), F32),
        compiler_params=pltpu.CompilerParams(
            dimension_semantics=("arbitrary", "arbitrary"), vmem_limit_bytes=VMEM_LIMIT),
        name="out_mlp",
    )(x, osb, ofox, osgu, mod, g2, wo, w1, w2)


def kernel(x, c, ada_w, ada_b, norm1_g, norm2_g, w_in, b_forget, q_norm_g, k_norm_g, sgu_norm_g,
           sgu_w, sgu_b, w_out, mlp_w1, mlp_w2):
    depth, d, _ = ada_w.shape
    bsz, s, _ = x.shape
    fox_heads = b_forget.shape[1]
    fox_w = fox_heads * HEAD_DIM
    sgu_groups, chunk = sgu_b.shape[1], sgu_b.shape[2]
    sgu_wd = sgu_groups * sgu_norm_g.shape[2]
    sb_w = (w_in.shape[2] - 3 * fox_w - fox_heads - 2 * sgu_wd) // 3
    f_lo = 3 * sb_w + 3 * fox_w

    mod = _modulation(c, ada_w, ada_b).reshape(depth, bsz, 6, d)

    perm = jnp.argsort(b_forget, axis=1)

    def relabel_cols(cols):
        t = cols.reshape(depth, d, fox_heads, HEAD_DIM)
        return jnp.take_along_axis(t, perm[:, None, :, None], axis=2).reshape(depth, d, fox_w)

    o = 3 * sb_w
    fox_cols = [relabel_cols(w_in[:, :, o + i * fox_w:o + (i + 1) * fox_w]) for i in range(3)]
    w = jnp.concatenate([w_in[:, :, :o]] + fox_cols + [w_in[:, :, f_lo + fox_heads:]], axis=2).astype(BF16)
    wf = jnp.take_along_axis(w_in[:, :, f_lo:f_lo + fox_heads], perm[:, None, :], axis=2)
    wf = jnp.pad(wf, ((0, 0), (0, 0), (0, LANES - fox_heads)))
    wf_hi = wf.astype(BF16)
    wf = jnp.concatenate([wf_hi, (wf - wf_hi.astype(F32)).astype(BF16)], axis=2)
    bf = jnp.pad(jnp.take_along_axis(b_forget, perm, axis=1), ((0, 0), (0, LANES - fox_heads)))
    bf = bf.reshape(depth, 1, LANES)
    qg = jnp.tile(q_norm_g, (1, MXU_DIM // HEAD_DIM)).reshape(depth, 1, MXU_DIM)
    kg = jnp.tile(k_norm_g, (1, MXU_DIM // HEAD_DIM)).reshape(depth, 1, MXU_DIM)
    sg = sgu_norm_g.reshape(depth, 1, sgu_wd)
    sb = jnp.repeat(jnp.swapaxes(sgu_b, 1, 2), sgu_norm_g.shape[2], axis=2)
    qk_bound = ((1.03 * HEAD_DIM ** 0.5) * jnp.max(jnp.abs(q_norm_g), axis=1)
                * jnp.max(jnp.abs(k_norm_g), axis=1))
    qx = jnp.zeros((depth, 1, LANES), F32).at[:, 0, HEAD_DIM:HEAD_DIM + 3].set(-1.0)
    qx = qx.at[:, 0, HEAD_DIM + 6].set(-qk_bound)
    wo_fox = w_out[:, sb_w:sb_w + fox_w].reshape(depth, fox_heads, HEAD_DIM, d)
    wo_fox = jnp.take_along_axis(wo_fox, perm[:, :, None, None], axis=1).reshape(depth, fox_w, d)
    wo = jnp.concatenate([w_out[:, :sb_w], wo_fox, w_out[:, sb_w + fox_w:]], axis=1).astype(BF16)
    w1 = mlp_w1.astype(BF16)
    w2 = mlp_w2.astype(BF16)
    g1 = norm1_g.reshape(depth, 1, d)
    g2 = norm2_g.reshape(depth, 1, d)

    for l in range(depth):
        qa, ka, vat, qf, kf, vft, osgu, f = _in_proj(
            x, mod, g1, w, wf, bf, qx, qg, kg, sg, sgu_w, sb,
            layer=l, tm=512, sb_w=sb_w, fox_w=fox_w, sgu_w=sgu_wd)
        osb = _sb_attention(qa, ka, vat, tq=256)
        ofox = _fox_attention(qf, kf, vft, f, qk_bound[l], tq=256)
        x = _out_mlp(x, osb, ofox, osgu, mod, g2, wo, w1, w2, layer=l, tm=512)
    return x
```

```python
import functools
import math

import jax
import jax.numpy as jnp
from jax import lax
from jax.experimental import pallas as pl
from jax.experimental.pallas import tpu as pltpu

HEAD_DIM = 64
LANES = 128
MXU_DIM = 256
EPS = 1e-6
NEG_BIG = -1e30
EXP_ZERO = -104.0
LOG2_E = math.log2(math.e)
MAX_UNSTABILISED_LOGIT = 40.0
VMEM_LIMIT = 56 * 1024 * 1024

F32 = jnp.float32
BF16 = jnp.bfloat16


def _dot(a, b):
    return jnp.dot(a, b, preferred_element_type=F32)


def _dot_nt(a, b):
    return lax.dot_general(a, b, (((1,), (1,)), ((), ())), preferred_element_type=F32)


def _block_start(j, size):
    return j * size if isinstance(j, int) else pl.multiple_of(j * size, size)


def _split2(x):
    hi = x.astype(BF16)
    lo = (x - hi.astype(F32)).astype(BF16)
    return hi, lo


def _split3(x):
    hi = x.astype(BF16)
    r = x - hi.astype(F32)
    mid = r.astype(BF16)
    lo = (r - mid.astype(F32)).astype(BF16)
    return hi, mid, lo


def _group_mean_matrix():
    head = jnp.arange(MXU_DIM) // HEAD_DIM
    return jnp.where(head[:, None] == head[None, :], 1.0 / HEAD_DIM, 0.0).astype(BF16)


def _placement_matrix(n_heads):
    row = jnp.arange(LANES)[:, None]
    col = jnp.arange(2 * n_heads * LANES)[None, :]
    tile = col // LANES
    is_q = (tile >= n_heads).astype(jnp.int32)
    part = col % LANES - HEAD_DIM - 3 * is_q
    hit = (part >= 0) & (part < 3) & (row == part * 8 + tile - n_heads * is_q)
    return hit.astype(BF16)


def _head_rmsnorm(t, gmat, gain):
    ms = _dot((t * t).astype(BF16), gmat)
    return t * lax.rsqrt(ms + EPS) * gain


def _gelu_tanh(x):
    c = math.sqrt(2.0 / math.pi)
    return x * (0.5 * (1.0 + jnp.tanh(c * (x + 0.044715 * (x * x * x)))))


def _log_sigmoid(x):
    return jnp.minimum(x, 0.0) - jnp.log(1.0 + jnp.exp(-jnp.abs(x)))


def _mod_kernel(ct_ref, w_ref, b_ref, o_ref):
    ct = ct_ref[...]
    cond = ct * (1.0 / (1.0 + jnp.exp(-ct)))
    w = w_ref[0]
    rows = [jnp.sum(cond[:, b:b + 1] * w, axis=0, keepdims=True) for b in range(ct.shape[1])]
    o_ref[0] = jnp.concatenate(rows, axis=0) + b_ref[0]


def _modulation(c, ada_w, ada_b):
    depth, d, n = ada_w.shape
    bsz = c.shape[0]
    tn = 1536
    return pl.pallas_call(
        _mod_kernel,
        grid=(depth, n // tn),
        in_specs=[
            pl.BlockSpec((d, bsz), lambda l, j: (0, 0)),
            pl.BlockSpec((1, d, tn), lambda l, j: (l, 0, j)),
            pl.BlockSpec((1, 1, tn), lambda l, j: (l, 0, j)),
        ],
        out_specs=pl.BlockSpec((1, bsz, tn), lambda l, j: (l, 0, j)),
        out_shape=jax.ShapeDtypeStruct((depth, bsz, n), F32),
        compiler_params=pltpu.CompilerParams(
            dimension_semantics=("arbitrary", "arbitrary"), vmem_limit_bytes=VMEM_LIMIT),
        name="adaln_mod",
    )(c.T, ada_w, ada_b.reshape(depth, 1, n))


def _in_kernel(x_ref, mod_ref, g1_ref, w_ref, wf_ref, bf_ref, qx_ref, qg_ref, kg_ref,
               sg_ref, sw_ref, sb_ref, ltri_ref, sel_ref, gmat_ref,
               qa_ref, ka_ref, vat_ref, qf_ref, kf_ref, vft_ref, og_ref, f_ref,
               carry_ref, proj_ref, *, sb_w, fox_w, sgu_w, chunk):
    @pl.when(pl.program_id(1) == 0)
    def _():
        carry_ref[...] = jnp.zeros_like(carry_ref)

    tm = x_ref.shape[1]
    x = x_ref[0]
    ms = jnp.mean(x * x, axis=-1, keepdims=True)
    h = x * lax.rsqrt(ms + EPS) * g1_ref[...]
    h = h * (1.0 + mod_ref[0, 1:2, :]) + mod_ref[0, 0:1, :]
    hb = h.astype(BF16)
    scale = HEAD_DIM ** -0.5
    n_attn = 3 * sb_w + 3 * fox_w
    wn = w_ref.shape[1]
    n_heads = f_ref.shape[1]
    lane = lax.broadcasted_iota(jnp.int32, (1, LANES), 1)
    head_lane = lane < n_heads
    gmat = gmat_ref[...]

    def pack3(parts):
        a, b, c = (p.astype(F32) for p in parts)
        return (a + pltpu.roll(b, 8, 1) + pltpu.roll(c, 16, 1)).astype(BF16)

    proj_ref[:, 0:n_attn] = _dot(hb, w_ref[:, 0:n_attn])
    h_hi, h_lo = _split2(h)
    t2 = _dot(h_hi, wf_ref[...])
    fl = t2[:, :LANES] + t2[:, LANES:] + _dot(h_lo, wf_ref[:, :LANES])
    proj_ref[:, n_attn:wn] = _dot(hb, w_ref[:, n_attn:wn])
    logf = jnp.where(head_lane, _log_sigmoid(fl + bf_ref[...]), 0.0)
    cp = _dot(ltri_ref[...], pack3(_split3(logf)))

    o = 3 * sb_w
    q_tiles = [_head_rmsnorm(proj_ref[:, o + j * MXU_DIM:o + (j + 1) * MXU_DIM], gmat, qg_ref[...]) * scale
               for j in range(fox_w // MXU_DIM)]
    o += fox_w

    cum = cp + pltpu.roll(cp, LANES - 8, 1) + pltpu.roll(cp, LANES - 16, 1)
    cum = jnp.where(head_lane, cum, 0.0) + carry_ref[0:1, :]
    carry_ref[0:1, :] = cum[tm - 1:tm, :]
    f_ref[0] = cum.T[0:n_heads, :]
    extras = _dot(pack3(_split3(cum)), sel_ref[...])

    k_tiles = [_head_rmsnorm(proj_ref[:, o + j * MXU_DIM:o + (j + 1) * MXU_DIM], gmat, kg_ref[...])
               for j in range(fox_w // MXU_DIM)]
    o += fox_w

    qa_ref[0] = (proj_ref[:, 0:sb_w] * (scale * LOG2_E)).astype(BF16)
    ka_ref[0] = proj_ref[:, sb_w:2 * sb_w].astype(BF16)
    vat_ref[0] = proj_ref[:, 2 * sb_w:3 * sb_w].T.astype(BF16)

    is_head = lane < HEAD_DIM
    q_extra = qx_ref[...]
    k_extra = jnp.where((lane >= HEAD_DIM + 3) & (lane < HEAD_DIM + 7), 1.0, 0.0)
    heads_per_mxu = MXU_DIM // HEAD_DIM

    def head_tile(tn, s):
        half = tn[:, (s // 2) * LANES:(s // 2 + 1) * LANES]
        return half if s % 2 == 0 else pltpu.roll(half, HEAD_DIM, 1)

    for j in range(fox_w // MXU_DIM):
        for s in range(heads_per_mxu):
            hh = heads_per_mxu * j + s
            xq = extras[:, (n_heads + hh) * LANES:(n_heads + hh + 1) * LANES] + q_extra
            qf_ref[0, :, hh * LANES:(hh + 1) * LANES] = jnp.where(
                is_head, head_tile(q_tiles[j], s), xq).astype(BF16)
            xk = extras[:, hh * LANES:(hh + 1) * LANES] + k_extra
            kf_ref[0, :, hh * LANES:(hh + 1) * LANES] = jnp.where(
                is_head, head_tile(k_tiles[j], s), xk).astype(BF16)
    vt = proj_ref[:, o:o + fox_w].T.astype(BF16); o += fox_w
    ones = jnp.ones((HEAD_DIM, tm), BF16)
    for hh in range(fox_w // HEAD_DIM):
        vft_ref[0, hh * LANES:hh * LANES + HEAD_DIM, :] = vt[hh * HEAD_DIM:(hh + 1) * HEAD_DIM, :]
        vft_ref[0, hh * LANES + HEAD_DIM:(hh + 1) * LANES, :] = ones

    gu = _gelu_tanh(proj_ref[:, o:o + sgu_w]); o += sgu_w
    gv = _gelu_tanh(proj_ref[:, o:o + sgu_w]); o += sgu_w
    lane_c = lax.broadcasted_iota(jnp.int32, (chunk, LANES), 1)
    rr = lax.broadcasted_iota(jnp.int32, (chunk, chunk), 0)
    cs = lax.broadcasted_iota(jnp.int32, (chunk, chunk), 1)
    wt = [jnp.where(rr >= cs, sw_ref[g], 0.0).astype(BF16) for g in range(sw_ref.shape[0])]
    vn_all = _head_rmsnorm(gv, gmat, sg_ref[...]).astype(BF16)
    for p in range(sgu_w // LANES):
        vn = vn_all[:, p * LANES:(p + 1) * LANES]
        for ci in range(tm // chunk):
            vblk = vn[ci * chunk:(ci + 1) * chunk, :]
            mixed = jnp.where(lane_c < HEAD_DIM, _dot(wt[2 * p], vblk), _dot(wt[2 * p + 1], vblk))
            mixed = mixed + sb_ref[:, p * LANES:(p + 1) * LANES]
            og_ref[0, ci * chunk:(ci + 1) * chunk, p * LANES:(p + 1) * LANES] = (
                gu[ci * chunk:(ci + 1) * chunk, p * LANES:(p + 1) * LANES] * mixed).astype(BF16)


def _in_proj(x, mod, g1, w, wf, bf, qx, qg, kg, sg, sw, sb, *, layer, tm, sb_w, fox_w, sgu_w):
    bsz, s, d = x.shape
    chunk = sw.shape[-1]
    wn = w.shape[2]
    fox_heads = fox_w // HEAD_DIM

    def of_layer(a):
        nd = a.ndim - 1
        return pl.BlockSpec((None,) + a.shape[1:], lambda b, i: (layer,) + (0,) * nd)

    assert fox_heads <= 8 and fox_w % MXU_DIM == 0 and sgu_w == MXU_DIM
    tok_idx = jnp.arange(tm)
    ltri = (tok_idx[:, None] >= tok_idx[None, :]).astype(BF16)
    const2 = lambda b, i: (0, 0)
    tok = lambda b, i: (b, i, 0)
    tok_t = lambda b, i: (b, 0, i)
    kern = functools.partial(_in_kernel, sb_w=sb_w, fox_w=fox_w, sgu_w=sgu_w, chunk=chunk)
    out_shape = [
        jax.ShapeDtypeStruct((bsz, s, sb_w), BF16),
        jax.ShapeDtypeStruct((bsz, s, sb_w), BF16),
        jax.ShapeDtypeStruct((bsz, sb_w, s), BF16),
        jax.ShapeDtypeStruct((bsz, s, fox_heads * LANES), BF16),
        jax.ShapeDtypeStruct((bsz, s, fox_heads * LANES), BF16),
        jax.ShapeDtypeStruct((bsz, fox_heads * LANES, s), BF16),
        jax.ShapeDtypeStruct((bsz, s, sgu_w), BF16),
        jax.ShapeDtypeStruct((bsz, fox_heads, s), F32),
    ]
    out_specs = [
        pl.BlockSpec((1, tm, sb_w), tok),
        pl.BlockSpec((1, tm, sb_w), tok),
        pl.BlockSpec((1, sb_w, tm), tok_t),
        pl.BlockSpec((1, tm, fox_heads * LANES), tok),
        pl.BlockSpec((1, tm, fox_heads * LANES), tok),
        pl.BlockSpec((1, fox_heads * LANES, tm), tok_t),
        pl.BlockSpec((1, tm, sgu_w), tok),
        pl.BlockSpec((1, fox_heads, tm), tok_t),
    ]
    return pl.pallas_call(
        kern,
        grid=(bsz, s // tm),
        in_specs=[
            pl.BlockSpec((1, tm, d), tok),
            pl.BlockSpec((None, 1, 6, d), lambda b, i: (layer, b, 0, 0)),
            of_layer(g1), of_layer(w), of_layer(wf), of_layer(bf), of_layer(qx), of_layer(qg),
            of_layer(kg), of_layer(sg), of_layer(sw), of_layer(sb),
            pl.BlockSpec((tm, tm), const2),
            pl.BlockSpec((LANES, 2 * fox_heads * LANES), const2),
            pl.BlockSpec((MXU_DIM, MXU_DIM), const2),
        ],
        out_specs=out_specs,
        out_shape=out_shape,
        scratch_shapes=[pltpu.VMEM((8, LANES), F32), pltpu.VMEM((tm, wn), F32)],
        compiler_params=pltpu.CompilerParams(
            dimension_semantics=("arbitrary", "arbitrary"), vmem_limit_bytes=VMEM_LIMIT),
        name="in_proj",
    )(x, mod, g1, w, wf, bf, qx, qg, kg, sg, sw, sb, ltri, _placement_matrix(fox_heads),
      _group_mean_matrix())


def _sb_kernel(q_ref, k_ref, vt_ref, after_ref, o_ref, acc_ref, c_ref, *, tq):
    tk = tq
    n_blk = q_ref.shape[1] // tq
    lane = lax.broadcasted_iota(jnp.int32, (1, LANES), 1)
    key = lax.broadcasted_iota(jnp.int32, (tk, tq), 0)
    qry = lax.broadcasted_iota(jnp.int32, (tk, tq), 1)
    causal = key < qry
    after = after_ref[...]

    def process(chains):
        units, starts, qh = [], {}, {}
        for slot, qi, blocks, first in chains:
            q = q_ref[0, pl.ds(_block_start(qi, tq), tq), :]
            qh[slot] = (jnp.where(lane < HEAD_DIM, q, 0).astype(BF16),
                        jnp.where(lane >= HEAD_DIM, q, 0).astype(BF16))
            for b, j in enumerate(blocks):
                starts[slot, b] = _block_start(j, tk)
                units += [(slot, b, h, first and b == 0) for h in range(2)]
        z = {(sl, b, h): _dot_nt(k_ref[0, pl.ds(starts[sl, b], tk), :], qh[sl][h])
             for sl, b, h, _ in units}
        l1mb, head = {}, {}
        for sl, b, h, diag in units:
            zz = z[sl, b, h]
            nz = -zz
            lg = jnp.minimum(nz, 0.0) - jnp.log2(1.0 + jnp.exp2(jnp.minimum(zz, nz)))
            if diag:
                lg = jnp.where(causal, lg, 0.0)
            l1mb[sl, b, h] = lg.astype(BF16)
            head[sl, b, h] = (zz + lg, lg[0:1, :])
        between = {(sl, b, h): _dot(after, l1mb[sl, b, h]) for sl, b, h, _ in units}
        a = {}
        for slot, qi, blocks, first in chains:
            for h in range(2):
                c = None if first else c_ref[slot, h]
                for b in range(len(blocks)):
                    e = head[slot, b, h][0] + between[slot, b, h]
                    if c is not None:
                        e = e + c
                    w = jnp.exp2(e)
                    if first and b == 0:
                        w = jnp.where(causal, w, 0.0)
                    a[slot, b, h] = w.astype(BF16)
                    block_sum = between[slot, b, h][0:1, :] + head[slot, b, h][1]
                    c = block_sum if c is None else c + block_sum
                c_ref[slot, h] = c
        for slot, qi, blocks, first in chains:
            for h in range(2):
                pv = None
                for b in range(len(blocks)):
                    part = _dot(vt_ref[0, :, pl.ds(starts[slot, b], tk)], a[slot, b, h])
                    pv = part if pv is None else pv + part
                if first:
                    acc_ref[slot, h] = pv
                else:
                    acc_ref[slot, h] += pv

    def carry_max(slot):
        cm = jnp.maximum(jnp.max(c_ref[slot, 0], axis=1, keepdims=True),
                         jnp.max(c_ref[slot, 1], axis=1, keepdims=True))
        return cm[0, 0]

    chan = lax.broadcasted_iota(jnp.int32, (LANES, tq), 0)

    def finish(slot, qi):
        def cond(carry):
            j, cmax = carry
            return jnp.logical_and(j >= 0, cmax > EXP_ZERO * LOG2_E)

        def body(carry):
            j, _ = carry
            process([(slot, qi, [j], False)])
            return j - 1, carry_max(slot)

        lax.while_loop(cond, body, (jnp.asarray(qi - 2, jnp.int32), carry_max(slot)))
        o_ref[0, pl.ds(_block_start(qi, tq), tq), :] = (
            jnp.where(chan < HEAD_DIM, acc_ref[slot, 0], acc_ref[slot, 1]).T.astype(o_ref.dtype))

    process([(0, 0, [0], True)])
    finish(0, 0)

    def two_blocks(g, carry):
        qa, qb = 1 + 2 * g, 2 + 2 * g
        process([(0, qa, [qa, qa - 1], True), (1, qb, [qb, qb - 1], True)])
        finish(0, qa)
        finish(1, qb)
        return carry

    lax.fori_loop(0, (n_blk - 1) // 2, two_blocks, 0)
    if (n_blk - 1) % 2:
        last = n_blk - 1
        process([(0, last, [last, last - 1], True)])
        finish(0, last)


def _sb_attention(q, k, vt, *, tq):
    bsz, s, w = q.shape
    assert s >= 2 * tq
    idx = jnp.arange(tq)
    after = (idx[None, :] > idx[:, None]).astype(BF16)
    kern = functools.partial(_sb_kernel, tq=tq)
    return pl.pallas_call(
        kern,
        grid=(bsz, w // LANES),
        in_specs=[
            pl.BlockSpec((1, s, LANES), lambda b, p: (b, 0, p)),
            pl.BlockSpec((1, s, LANES), lambda b, p: (b, 0, p)),
            pl.BlockSpec((1, LANES, s), lambda b, p: (b, p, 0)),
            pl.BlockSpec((tq, tq), lambda b, p: (0, 0)),
        ],
        out_specs=pl.BlockSpec((1, s, LANES), lambda b, p: (b, 0, p)),
        out_shape=jax.ShapeDtypeStruct((bsz, s, w), BF16),
        scratch_shapes=[pltpu.VMEM((2, 2, LANES, tq), F32), pltpu.VMEM((2, 2, 1, tq), F32)],
        compiler_params=pltpu.CompilerParams(
            dimension_semantics=("arbitrary", "arbitrary"), vmem_limit_bytes=VMEM_LIMIT),
        name="sb_attn",
    )(q, k, vt, after)


def _fox_kernel(fend_ref, par_ref, q_ref, k_ref, vt_ref, o_ref, acc_ref, m_ref, p_ref, *, tq, n_heads):
    tk = tq
    n_blk = q_ref.shape[1] // tq
    head0 = (pl.program_id(0) * n_heads + 2 * pl.program_id(1))

    def first_dead_block(head, qi):
        base = head * n_blk
        f_q = fend_ref[base + jnp.maximum(qi - 1, 0)]

        def alive(j):
            jc = jnp.maximum(j, 0)
            return jnp.logical_and(j >= 0, f_q - fend_ref[base + jc] >= -par_ref[0])

        return lax.while_loop(alive, lambda j: j - 1, qi - 1)

    def plan(qi):
        j_dead = jnp.minimum(first_dead_block(head0, qi), first_dead_block(head0 + 1, qi))
        n_left = jnp.maximum(qi - 2 - j_dead, 0)
        odd = n_left % 2
        has_dead_below = j_dead >= 0
        n_pairs = n_left // 2 + jnp.where(has_dead_below, odd, 0)
        return n_pairs, jnp.logical_and(odd == 1, jnp.logical_not(has_dead_below))

    def q_tile(qi, h):
        return q_ref[0, pl.ds(pl.multiple_of(qi * tq, tq), tq), h * LANES:(h + 1) * LANES]

    def finalize(qi):
        outs = []
        for h in range(2):
            acc = acc_ref[h]
            outs.append(acc[0:HEAD_DIM, :] / acc[HEAD_DIM:HEAD_DIM + 1, :])
        o_ref[0, pl.ds(pl.multiple_of(qi * tq, tq), tq), :] = (
            jnp.concatenate(outs, axis=0).T.astype(o_ref.dtype))

    key = lax.broadcasted_iota(jnp.int32, (tk, tq), 0)
    qry = lax.broadcasted_iota(jnp.int32, (tk, tq), 1)
    causal = key <= qry

    def process(qi, blocks, first):
        starts = [pl.multiple_of(j * tk, tk) for j in blocks]
        units = [(b, h) for b in range(len(blocks)) for h in range(2)]
        s = {(b, h): _dot_nt(k_ref[0, pl.ds(starts[b], tk), h * LANES:(h + 1) * LANES],
                             q_tile(qi, h)) for b, h in units}
        p = {}
        alpha = {}
        for h in range(2):
            if first:
                s[0, h] = jnp.where(causal, s[0, h], NEG_BIG)
            m_new = None if first else m_ref[h]
            for b in range(len(blocks)):
                mb = jnp.max(s[b, h], axis=0, keepdims=True)
                m_new = mb if m_new is None else jnp.maximum(m_new, mb)
            if not first:
                alpha[h] = jnp.exp(m_ref[h] - m_new)
            for b in range(len(blocks)):
                p[b, h] = jnp.exp(s[b, h] - m_new).astype(BF16)
            m_ref[h] = m_new
        for h in range(2):
            pv = None
            for b in range(len(blocks)):
                part = _dot(vt_ref[0, h * LANES:(h + 1) * LANES, pl.ds(starts[b], tk)], p[b, h])
                pv = part if pv is None else pv + part
            acc_ref[h] = pv if first else alpha[h] * acc_ref[h] + pv

    def online_path():
        def q_block(qi, carry):
            n_pairs, lone_block0 = plan(qi)

            @pl.when(qi == 0)
            def _():
                process(qi, [qi], True)

            @pl.when(qi > 0)
            def _():
                process(qi, [qi, qi - 1], True)

            def body(i, c):
                j = qi - 2 - 2 * i
                process(qi, [j, j - 1], False)
                return c

            lax.fori_loop(0, n_pairs, body, 0)

            @pl.when(lone_block0)
            def _():
                process(qi, [0], False)

            finalize(qi)
            return carry

        lax.fori_loop(0, n_blk, q_block, 0)

    win = 2 * tk
    wrow = lax.broadcasted_iota(jnp.int32, (win, tq), 0)
    wcol = lax.broadcasted_iota(jnp.int32, (win, tq), 1)

    def window(qi, i):
        jl = qi - 1 - 2 * i
        j_lo = jnp.maximum(jl, 0)
        return j_lo, pl.multiple_of(j_lo * tk, tk), jnp.where(jl < 0, tk, win)

    def stage_a(qi, i, slot, first):
        j_lo, start, row_lim = window(qi, i)
        keep = wrow < row_lim
        if first:
            keep = jnp.logical_and(keep, wrow <= wcol + (qi - j_lo) * tk)
        for h in range(2):
            s = _dot_nt(k_ref[0, pl.ds(start, win), h * LANES:(h + 1) * LANES], q_tile(qi, h))
            p_ref[slot, h] = jnp.exp(jnp.where(keep, s, NEG_BIG)).astype(BF16)

    def stage_c(qi, i, slot):
        _, start, _ = window(qi, i)
        for h in range(2):
            acc_ref[h] += _dot(vt_ref[0, h * LANES:(h + 1) * LANES, pl.ds(start, win)], p_ref[slot, h])

    def bounded_path():
        acc_ref[...] = jnp.zeros_like(acc_ref)
        stage_a(0, 0, 0, True)

        def q_block(qi, t):
            n_pairs, lone_block0 = plan(qi)
            n_stages = 1 + n_pairs + jnp.where(lone_block0, 1, 0)

            def body(i, t):
                stage_c(qi, i - 1, t & 1)
                stage_a(qi, i, (t + 1) & 1, False)
                return t + 1

            t = lax.fori_loop(1, n_stages, body, t)
            stage_c(qi, n_stages - 1, t & 1)
            stage_a(jnp.minimum(qi + 1, n_blk - 1), 0, (t + 1) & 1, True)
            finalize(qi)
            acc_ref[...] = jnp.zeros_like(acc_ref)
            return t + 1

        lax.fori_loop(0, n_blk, q_block, jnp.int32(0))

    bounded = par_ref[1] > 0.5
    pl.when(bounded)(bounded_path)
    pl.when(jnp.logical_not(bounded))(online_path)


def _fox_attention(q, k, vt, f, qk_bound, *, tq):
    bsz, s, w = q.shape
    pair = 2 * LANES
    n_heads = w // LANES
    assert s >= 2 * tq
    f_end = f[:, :, tq - 1::tq].reshape(-1)
    bounded = qk_bound <= MAX_UNSTABILISED_LOGIT
    par = jnp.stack([jnp.where(bounded, 0.5 - EXP_ZERO, 2.0 * qk_bound - EXP_ZERO),
                     jnp.where(bounded, 1.0, 0.0)]).astype(F32)
    kern = functools.partial(_fox_kernel, tq=tq, n_heads=n_heads)
    grid_spec = pltpu.PrefetchScalarGridSpec(
        num_scalar_prefetch=2,
        grid=(bsz, w // pair),
        in_specs=[
            pl.BlockSpec((1, s, pair), lambda b, p, fe, th: (b, 0, p)),
            pl.BlockSpec((1, s, pair), lambda b, p, fe, th: (b, 0, p)),
            pl.BlockSpec((1, pair, s), lambda b, p, fe, th: (b, p, 0)),
        ],
        out_specs=pl.BlockSpec((1, s, LANES), lambda b, p, fe, th: (b, 0, p)),
        scratch_shapes=[pltpu.VMEM((2, LANES, tq), F32), pltpu.VMEM((2, 1, tq), F32),
                        pltpu.VMEM((2, 2, 2 * tq, tq), BF16)],
    )
    return pl.pallas_call(
        kern,
        grid_spec=grid_spec,
        out_shape=jax.ShapeDtypeStruct((bsz, s, w // 2), BF16),
        compiler_params=pltpu.CompilerParams(
            dimension_semantics=("arbitrary", "arbitrary"), vmem_limit_bytes=VMEM_LIMIT),
        name="fox_attn",
    )(f_end, par, q, k, vt)


def _out_kernel(x_ref, osb_ref, ofox_ref, osgu_ref, mod_ref, g2_ref, wo_ref, w1_ref, w2_ref,
                o_ref, *, ff_chunk):
    sb_w = osb_ref.shape[2]
    fox_w = ofox_ref.shape[2]
    x = x_ref[0]
    mix = (_dot(osb_ref[0], wo_ref[0:sb_w, :])
           + _dot(ofox_ref[0], wo_ref[sb_w:sb_w + fox_w, :])
           + _dot(osgu_ref[0], wo_ref[sb_w + fox_w:, :]))
    x1 = x + mod_ref[0, 2:3, :] * mix
    ms = jnp.mean(x1 * x1, axis=-1, keepdims=True)
    h = x1 * lax.rsqrt(ms + EPS) * g2_ref[...]
    hb = (h * (1.0 + mod_ref[0, 4:5, :]) + mod_ref[0, 3:4, :]).astype(BF16)
    d_ff = w1_ref.shape[1]
    acc = None
    for c in range(d_ff // ff_chunk):
        hid = jnp.maximum(_dot(hb, w1_ref[:, c * ff_chunk:(c + 1) * ff_chunk]), 0.0)
        part = _dot((hid * hid).astype(BF16), w2_ref[c * ff_chunk:(c + 1) * ff_chunk, :])
        acc = part if acc is None else acc + part
    o_ref[0] = x1 + mod_ref[0, 5:6, :] * acc


def _out_mlp(x, osb, ofox, osgu, mod, g2, wo, w1, w2, *, layer, tm):
    bsz, s, d = x.shape
    tok = lambda b, i: (b, i, 0)

    def of_layer(a, **kw):
        nd = a.ndim - 1
        return pl.BlockSpec((None,) + a.shape[1:], lambda b, i: (layer,) + (0,) * nd, **kw)

    single = pl.Buffered(1)
    kern = functools.partial(_out_kernel, ff_chunk=1024)
    return pl.pallas_call(
        kern,
        grid=(bsz, s // tm),
        in_specs=[
            pl.BlockSpec((1, tm, d), tok),
            pl.BlockSpec((1, tm, osb.shape[2]), tok),
            pl.BlockSpec((1, tm, ofox.shape[2]), tok),
            pl.BlockSpec((1, tm, osgu.shape[2]), tok),
            pl.BlockSpec((None, 1, 6, d), lambda b, i: (layer, b, 0, 0)),
            of_layer(g2),
            of_layer(wo, pipeline_mode=single),
            of_layer(w1, pipeline_mode=single),
            of_layer(w2, pipeline_mode=single),
        ],
        out_specs=pl.BlockSpec((1, tm, d), tok),
        out_shape=jax.ShapeDtypeStruct((bsz, s, d), F32),
        compiler_params=pltpu.CompilerParams(
            dimension_semantics=("arbitrary", "arbitrary"), vmem_limit_bytes=VMEM_LIMIT),
        name="out_mlp",
    )(x, osb, ofox, osgu, mod, g2, wo, w1, w2)


def kernel(x, c, ada_w, ada_b, norm1_g, norm2_g, w_in, b_forget, q_norm_g, k_norm_g, sgu_norm_g,
           sgu_w, sgu_b, w_out, mlp_w1, mlp_w2):
    depth, d, _ = ada_w.shape
    bsz, s, _ = x.shape
    fox_heads = b_forget.shape[1]
    fox_w = fox_heads * HEAD_DIM
    sgu_groups, chunk = sgu_b.shape[1], sgu_b.shape[2]
    sgu_wd = sgu_groups * sgu_norm_g.shape[2]
    sb_w = (w_in.shape[2] - 3 * fox_w - fox_heads - 2 * sgu_wd) // 3
    f_lo = 3 * sb_w + 3 * fox_w

    mod = _modulation(c, ada_w, ada_b).reshape(depth, bsz, 6, d)

    perm = jnp.argsort(b_forget, axis=1)

    def relabel_cols(cols):
        t = cols.reshape(depth, d, fox_heads, HEAD_DIM)
        return jnp.take_along_axis(t, perm[:, None, :, None], axis=2).reshape(depth, d, fox_w)

    o = 3 * sb_w
    w_bf = w_in.astype(BF16)
    fox_cols = [relabel_cols(w_bf[:, :, o + i * fox_w:o + (i + 1) * fox_w]) for i in range(3)]
    w = jnp.concatenate([w_bf[:, :, :o]] + fox_cols + [w_bf[:, :, f_lo + fox_heads:]], axis=2)
    wf = jnp.take_along_axis(w_in[:, :, f_lo:f_lo + fox_heads], perm[:, None, :], axis=2)
    wf = jnp.pad(wf, ((0, 0), (0, 0), (0, LANES - fox_heads)))
    wf_hi = wf.astype(BF16)
    wf = jnp.concatenate([wf_hi, (wf - wf_hi.astype(F32)).astype(BF16)], axis=2)
    bf = jnp.pad(jnp.take_along_axis(b_forget, perm, axis=1), ((0, 0), (0, LANES - fox_heads)))
    bf = bf.reshape(depth, 1, LANES)
    qg = jnp.tile(q_norm_g, (1, MXU_DIM // HEAD_DIM)).reshape(depth, 1, MXU_DIM)
    kg = jnp.tile(k_norm_g, (1, MXU_DIM // HEAD_DIM)).reshape(depth, 1, MXU_DIM)
    sg = sgu_norm_g.reshape(depth, 1, sgu_wd)
    sb = jnp.repeat(jnp.swapaxes(sgu_b, 1, 2), sgu_norm_g.shape[2], axis=2)
    qk_bound = ((1.03 * HEAD_DIM ** 0.5) * jnp.max(jnp.abs(q_norm_g), axis=1)
                * jnp.max(jnp.abs(k_norm_g), axis=1))
    qx = jnp.zeros((depth, 1, LANES), F32).at[:, 0, HEAD_DIM:HEAD_DIM + 3].set(-1.0)
    qx = qx.at[:, 0, HEAD_DIM + 6].set(-qk_bound)
    wo_fox = w_out[:, sb_w:sb_w + fox_w].reshape(depth, fox_heads, HEAD_DIM, d)
    wo_fox = jnp.take_along_axis(wo_fox, perm[:, :, None, None], axis=1).reshape(depth, fox_w, d)
    wo = jnp.concatenate([w_out[:, :sb_w], wo_fox, w_out[:, sb_w + fox_w:]], axis=1).astype(BF16)
    w1 = mlp_w1.astype(BF16)
    w2 = mlp_w2.astype(BF16)
    g1 = norm1_g.reshape(depth, 1, d)
    g2 = norm2_g.reshape(depth, 1, d)

    for l in range(depth):
        qa, ka, vat, qf, kf, vft, osgu, f = _in_proj(
            x, mod, g1, w, wf, bf, qx, qg, kg, sg, sgu_w, sb,
            layer=l, tm=512, sb_w=sb_w, fox_w=fox_w, sgu_w=sgu_wd)
        osb = _sb_attention(qa, ka, vat, tq=256)
        ofox = _fox_attention(qf, kf, vft, f, qk_bound[l], tq=256)
        x = _out_mlp(x, osb, ofox, osgu, mod, g2, wo, w1, w2, layer=l, tm=512)
    return x
```

```python
import functools
import math

import jax
import jax.numpy as jnp
from jax import lax
from jax.experimental import pallas as pl
from jax.experimental.pallas import tpu as pltpu

HEAD_DIM = 64
LANES = 128
MXU_DIM = 256
BF16_ROWS = 16
V_ROWS = HEAD_DIM + BF16_ROWS
EPS = 1e-6
NEG_BIG = -1e30
EXP_ZERO = -104.0
LOG2_E = math.log2(math.e)
MAX_UNSTABILISED_LOGIT = 40.0
VMEM_LIMIT = 56 * 1024 * 1024

F32 = jnp.float32
BF16 = jnp.bfloat16


def _dot(a, b):
    return jnp.dot(a, b, preferred_element_type=F32)


def _dot_nt(a, b):
    return lax.dot_general(a, b, (((1,), (1,)), ((), ())), preferred_element_type=F32)


def _block_start(j, size):
    return j * size if isinstance(j, int) else pl.multiple_of(j * size, size)


def _split2(x):
    hi = x.astype(BF16)
    lo = (x - hi.astype(F32)).astype(BF16)
    return hi, lo


def _split3(x):
    hi = x.astype(BF16)
    r = x - hi.astype(F32)
    mid = r.astype(BF16)
    lo = (r - mid.astype(F32)).astype(BF16)
    return hi, mid, lo


def _group_mean_matrix():
    head = jnp.arange(MXU_DIM) // HEAD_DIM
    return jnp.where(head[:, None] == head[None, :], 1.0 / HEAD_DIM, 0.0).astype(BF16)


def _placement_matrix(n_heads):
    row = jnp.arange(LANES)[:, None]
    col = jnp.arange(2 * n_heads * LANES)[None, :]
    tile = col // LANES
    is_q = (tile >= n_heads).astype(jnp.int32)
    part = col % LANES - HEAD_DIM - 3 * is_q
    hit = (part >= 0) & (part < 3) & (row == part * 8 + tile - n_heads * is_q)
    return hit.astype(BF16)


def _head_rmsnorm(t, gmat, gain):
    ms = _dot((t * t).astype(BF16), gmat)
    return t * lax.rsqrt(ms + EPS) * gain


def _gelu_tanh(x):
    c = math.sqrt(2.0 / math.pi)
    return x * (0.5 * (1.0 + jnp.tanh(c * (x + 0.044715 * (x * x * x)))))


def _log_sigmoid(x):
    return jnp.minimum(x, 0.0) - jnp.log(1.0 + jnp.exp(-jnp.abs(x)))


def _mod_kernel(ct_ref, w_ref, b_ref, o_ref):
    ct = ct_ref[...]
    cond = ct * (1.0 / (1.0 + jnp.exp(-ct)))
    w = w_ref[0]
    rows = [jnp.sum(cond[:, b:b + 1] * w, axis=0, keepdims=True) for b in range(ct.shape[1])]
    o_ref[0] = jnp.concatenate(rows, axis=0) + b_ref[0]


def _modulation(c, ada_w, ada_b):
    depth, d, n = ada_w.shape
    bsz = c.shape[0]
    tn = 1536
    return pl.pallas_call(
        _mod_kernel,
        grid=(depth, n // tn),
        in_specs=[
            pl.BlockSpec((d, bsz), lambda l, j: (0, 0)),
            pl.BlockSpec((1, d, tn), lambda l, j: (l, 0, j)),
            pl.BlockSpec((1, 1, tn), lambda l, j: (l, 0, j)),
        ],
        out_specs=pl.BlockSpec((1, bsz, tn), lambda l, j: (l, 0, j)),
        out_shape=jax.ShapeDtypeStruct((depth, bsz, n), F32),
        compiler_params=pltpu.CompilerParams(
            dimension_semantics=("arbitrary", "arbitrary"), vmem_limit_bytes=VMEM_LIMIT),
        name="adaln_mod",
    )(c.T, ada_w, ada_b.reshape(depth, 1, n))


def _in_kernel(x_ref, mod_ref, g1_ref, w_ref, wf_ref, bf_ref, qx_ref, qg_ref, kg_ref,
               sg_ref, sw_ref, sb_ref, ltri_ref, sel_ref, gmat_ref,
               qa_ref, ka_ref, vat_ref, qf_ref, kf_ref, vft_ref, og_ref, f_ref,
               carry_ref, proj_ref, *, sb_w, fox_w, sgu_w, chunk):
    @pl.when(pl.program_id(1) == 0)
    def _():
        carry_ref[...] = jnp.zeros_like(carry_ref)

    tm = x_ref.shape[1]
    x = x_ref[0]
    ms = jnp.mean(x * x, axis=-1, keepdims=True)
    h = x * lax.rsqrt(ms + EPS) * g1_ref[...]
    h = h * (1.0 + mod_ref[0, 1:2, :]) + mod_ref[0, 0:1, :]
    hb = h.astype(BF16)
    scale = HEAD_DIM ** -0.5
    n_attn = 3 * sb_w + 3 * fox_w
    wn = w_ref.shape[1]
    n_heads = f_ref.shape[1]
    lane = lax.broadcasted_iota(jnp.int32, (1, LANES), 1)
    head_lane = lane < n_heads
    gmat = gmat_ref[...]

    def pack3(parts):
        a, b, c = (p.astype(F32) for p in parts)
        return (a + pltpu.roll(b, 8, 1) + pltpu.roll(c, 16, 1)).astype(BF16)

    proj_ref[:, 0:n_attn] = _dot(hb, w_ref[:, 0:n_attn])
    h_hi, h_lo = _split2(h)
    t2 = _dot(h_hi, wf_ref[...])
    fl = t2[:, :LANES] + t2[:, LANES:] + _dot(h_lo, wf_ref[:, :LANES])
    proj_ref[:, n_attn:wn] = _dot(hb, w_ref[:, n_attn:wn])
    logf = jnp.where(head_lane, _log_sigmoid(fl + bf_ref[...]), 0.0)
    cp = _dot(ltri_ref[...], pack3(_split3(logf)))

    o = 3 * sb_w
    q_tiles = [_head_rmsnorm(proj_ref[:, o + j * MXU_DIM:o + (j + 1) * MXU_DIM], gmat, qg_ref[...]) * scale
               for j in range(fox_w // MXU_DIM)]
    o += fox_w

    cum = cp + pltpu.roll(cp, LANES - 8, 1) + pltpu.roll(cp, LANES - 16, 1)
    cum = jnp.where(head_lane, cum, 0.0) + carry_ref[0:1, :]
    carry_ref[0:1, :] = cum[tm - 1:tm, :]
    f_ref[0] = cum.T[0:n_heads, :]
    extras = _dot(pack3(_split3(cum)), sel_ref[...])

    k_tiles = [_head_rmsnorm(proj_ref[:, o + j * MXU_DIM:o + (j + 1) * MXU_DIM], gmat, kg_ref[...])
               for j in range(fox_w // MXU_DIM)]
    o += fox_w

    qa_ref[0] = (proj_ref[:, 0:sb_w] * (scale * LOG2_E)).astype(BF16)
    ka_ref[0] = proj_ref[:, sb_w:2 * sb_w].astype(BF16)
    vat_ref[0] = proj_ref[:, 2 * sb_w:3 * sb_w].T.astype(BF16)

    is_head = lane < HEAD_DIM
    q_extra = qx_ref[...]
    k_extra = jnp.where((lane >= HEAD_DIM + 3) & (lane < HEAD_DIM + 7), 1.0, 0.0)
    heads_per_mxu = MXU_DIM // HEAD_DIM

    def head_tile(tn, s):
        half = tn[:, (s // 2) * LANES:(s // 2 + 1) * LANES]
        return half if s % 2 == 0 else pltpu.roll(half, HEAD_DIM, 1)

    for j in range(fox_w // MXU_DIM):
        for s in range(heads_per_mxu):
            hh = heads_per_mxu * j + s
            xq = extras[:, (n_heads + hh) * LANES:(n_heads + hh + 1) * LANES] + q_extra
            qf_ref[0, :, hh * LANES:(hh + 1) * LANES] = jnp.where(
                is_head, head_tile(q_tiles[j], s), xq).astype(BF16)
            xk = extras[:, hh * LANES:(hh + 1) * LANES] + k_extra
            kf_ref[0, :, hh * LANES:(hh + 1) * LANES] = jnp.where(
                is_head, head_tile(k_tiles[j], s), xk).astype(BF16)
    vt = proj_ref[:, o:o + fox_w].T.astype(BF16); o += fox_w
    ones = jnp.ones((V_ROWS - HEAD_DIM, tm), BF16)
    for hh in range(fox_w // HEAD_DIM):
        vft_ref[0, hh * V_ROWS:hh * V_ROWS + HEAD_DIM, :] = vt[hh * HEAD_DIM:(hh + 1) * HEAD_DIM, :]
        vft_ref[0, hh * V_ROWS + HEAD_DIM:(hh + 1) * V_ROWS, :] = ones

    gu = _gelu_tanh(proj_ref[:, o:o + sgu_w]); o += sgu_w
    gv = _gelu_tanh(proj_ref[:, o:o + sgu_w]); o += sgu_w
    lane_c = lax.broadcasted_iota(jnp.int32, (chunk, LANES), 1)
    rr = lax.broadcasted_iota(jnp.int32, (chunk, chunk), 0)
    cs = lax.broadcasted_iota(jnp.int32, (chunk, chunk), 1)
    wt = [jnp.where(rr >= cs, sw_ref[g], 0.0).astype(BF16) for g in range(sw_ref.shape[0])]
    vn_all = _head_rmsnorm(gv, gmat, sg_ref[...]).astype(BF16)
    for p in range(sgu_w // LANES):
        vn = vn_all[:, p * LANES:(p + 1) * LANES]
        for ci in range(tm // chunk):
            vblk = vn[ci * chunk:(ci + 1) * chunk, :]
            mixed = jnp.where(lane_c < HEAD_DIM, _dot(wt[2 * p], vblk), _dot(wt[2 * p + 1], vblk))
            mixed = mixed + sb_ref[:, p * LANES:(p + 1) * LANES]
            og_ref[0, ci * chunk:(ci + 1) * chunk, p * LANES:(p + 1) * LANES] = (
                gu[ci * chunk:(ci + 1) * chunk, p * LANES:(p + 1) * LANES] * mixed).astype(BF16)


def _in_proj(x, mod, g1, w, wf, bf, qx, qg, kg, sg, sw, sb, *, layer, tm, sb_w, fox_w, sgu_w):
    bsz, s, d = x.shape
    chunk = sw.shape[-1]
    wn = w.shape[2]
    fox_heads = fox_w // HEAD_DIM

    def of_layer(a):
        nd = a.ndim - 1
        return pl.BlockSpec((None,) + a.shape[1:], lambda b, i: (layer,) + (0,) * nd)

    assert fox_heads <= 8 and fox_w % MXU_DIM == 0 and sgu_w == MXU_DIM
    tok_idx = jnp.arange(tm)
    ltri = (tok_idx[:, None] >= tok_idx[None, :]).astype(BF16)
    const2 = lambda b, i: (0, 0)
    tok = lambda b, i: (b, i, 0)
    tok_t = lambda b, i: (b, 0, i)
    kern = functools.partial(_in_kernel, sb_w=sb_w, fox_w=fox_w, sgu_w=sgu_w, chunk=chunk)
    out_shape = [
        jax.ShapeDtypeStruct((bsz, s, sb_w), BF16),
        jax.ShapeDtypeStruct((bsz, s, sb_w), BF16),
        jax.ShapeDtypeStruct((bsz, sb_w, s), BF16),
        jax.ShapeDtypeStruct((bsz, s, fox_heads * LANES), BF16),
        jax.ShapeDtypeStruct((bsz, s, fox_heads * LANES), BF16),
        jax.ShapeDtypeStruct((bsz, fox_heads * V_ROWS, s), BF16),
        jax.ShapeDtypeStruct((bsz, s, sgu_w), BF16),
        jax.ShapeDtypeStruct((bsz, fox_heads, s), F32),
    ]
    out_specs = [
        pl.BlockSpec((1, tm, sb_w), tok),
        pl.BlockSpec((1, tm, sb_w), tok),
        pl.BlockSpec((1, sb_w, tm), tok_t),
        pl.BlockSpec((1, tm, fox_heads * LANES), tok),
        pl.BlockSpec((1, tm, fox_heads * LANES), tok),
        pl.BlockSpec((1, fox_heads * V_ROWS, tm), tok_t),
        pl.BlockSpec((1, tm, sgu_w), tok),
        pl.BlockSpec((1, fox_heads, tm), tok_t),
    ]
    return pl.pallas_call(
        kern,
        grid=(bsz, s // tm),
        in_specs=[
            pl.BlockSpec((1, tm, d), tok),
            pl.BlockSpec((None, 1, 6, d), lambda b, i: (layer, b, 0, 0)),
            of_layer(g1), of_layer(w), of_layer(wf), of_layer(bf), of_layer(qx), of_layer(qg),
            of_layer(kg), of_layer(sg), of_layer(sw), of_layer(sb),
            pl.BlockSpec((tm, tm), const2),
            pl.BlockSpec((LANES, 2 * fox_heads * LANES), const2),
            pl.BlockSpec((MXU_DIM, MXU_DIM), const2),
        ],
        out_specs=out_specs,
        out_shape=out_shape,
        scratch_shapes=[pltpu.VMEM((8, LANES), F32), pltpu.VMEM((tm, wn), F32)],
        compiler_params=pltpu.CompilerParams(
            dimension_semantics=("arbitrary", "arbitrary"), vmem_limit_bytes=VMEM_LIMIT),
        name="in_proj",
    )(x, mod, g1, w, wf, bf, qx, qg, kg, sg, sw, sb, ltri, _placement_matrix(fox_heads),
      _group_mean_matrix())


def _sb_kernel(q_ref, k_ref, vt_ref, after_ref, o_ref, acc_ref, c_ref, *, tq):
    tk = tq
    n_blk = q_ref.shape[1] // tq
    lane = lax.broadcasted_iota(jnp.int32, (1, LANES), 1)
    key = lax.broadcasted_iota(jnp.int32, (tk, tq), 0)
    qry = lax.broadcasted_iota(jnp.int32, (tk, tq), 1)
    causal = key < qry
    after = after_ref[...]

    def process(chains):
        units, starts, qh = [], {}, {}
        for slot, qi, blocks, first in chains:
            q = q_ref[0, pl.ds(_block_start(qi, tq), tq), :]
            qh[slot] = (jnp.where(lane < HEAD_DIM, q, 0).astype(BF16),
                        jnp.where(lane >= HEAD_DIM, q, 0).astype(BF16))
            for b, j in enumerate(blocks):
                starts[slot, b] = _block_start(j, tk)
                units += [(slot, b, h, first and b == 0) for h in range(2)]
        z = {(sl, b, h): _dot_nt(k_ref[0, pl.ds(starts[sl, b], tk), :], qh[sl][h])
             for sl, b, h, _ in units}
        l1mb, head = {}, {}
        for sl, b, h, diag in units:
            zz = z[sl, b, h]
            nz = -zz
            lg = jnp.minimum(nz, 0.0) - jnp.log2(1.0 + jnp.exp2(jnp.minimum(zz, nz)))
            if diag:
                lg = jnp.where(causal, lg, 0.0)
            l1mb[sl, b, h] = lg.astype(BF16)
            head[sl, b, h] = (zz + lg, lg[0:1, :])
        between = {(sl, b, h): _dot(after, l1mb[sl, b, h]) for sl, b, h, _ in units}
        a = {}
        for slot, qi, blocks, first in chains:
            for h in range(2):
                c = None if first else c_ref[slot, h]
                for b in range(len(blocks)):
                    e = head[slot, b, h][0] + between[slot, b, h]
                    if c is not None:
                        e = e + c
                    w = jnp.exp2(e)
                    if first and b == 0:
                        w = jnp.where(causal, w, 0.0)
                    a[slot, b, h] = w.astype(BF16)
                    block_sum = between[slot, b, h][0:1, :] + head[slot, b, h][1]
                    c = block_sum if c is None else c + block_sum
                c_ref[slot, h] = c
        for slot, qi, blocks, first in chains:
            for h in range(2):
                pv = None
                for b in range(len(blocks)):
                    part = _dot(vt_ref[0, h * HEAD_DIM:(h + 1) * HEAD_DIM, pl.ds(starts[slot, b], tk)],
                                a[slot, b, h])
                    pv = part if pv is None else pv + part
                if first:
                    acc_ref[slot, h] = pv
                else:
                    acc_ref[slot, h] += pv

    def carry_max(slot):
        cm = jnp.maximum(jnp.max(c_ref[slot, 0], axis=1, keepdims=True),
                         jnp.max(c_ref[slot, 1], axis=1, keepdims=True))
        return cm[0, 0]


    def finish(slot, qi):
        def cond(carry):
            j, cmax = carry
            return jnp.logical_and(j >= 0, cmax > EXP_ZERO * LOG2_E)

        def body(carry):
            j, _ = carry
            process([(slot, qi, [j], False)])
            return j - 1, carry_max(slot)

        lax.while_loop(cond, body, (jnp.asarray(qi - 2, jnp.int32), carry_max(slot)))
        o_ref[0, pl.ds(_block_start(qi, tq), tq), :] = (
            jnp.concatenate([acc_ref[slot, 0], acc_ref[slot, 1]], axis=0).T.astype(o_ref.dtype))

    process([(0, 0, [0], True)])
    finish(0, 0)

    def two_blocks(g, carry):
        qa, qb = 1 + 2 * g, 2 + 2 * g
        process([(0, qa, [qa, qa - 1], True), (1, qb, [qb, qb - 1], True)])
        finish(0, qa)
        finish(1, qb)
        return carry

    lax.fori_loop(0, (n_blk - 1) // 2, two_blocks, 0)
    if (n_blk - 1) % 2:
        last = n_blk - 1
        process([(0, last, [last, last - 1], True)])
        finish(0, last)


def _sb_attention(q, k, vt, *, tq):
    bsz, s, w = q.shape
    assert s >= 2 * tq
    idx = jnp.arange(tq)
    after = (idx[None, :] > idx[:, None]).astype(BF16)
    kern = functools.partial(_sb_kernel, tq=tq)
    return pl.pallas_call(
        kern,
        grid=(bsz, w // LANES),
        in_specs=[
            pl.BlockSpec((1, s, LANES), lambda b, p: (b, 0, p)),
            pl.BlockSpec((1, s, LANES), lambda b, p: (b, 0, p)),
            pl.BlockSpec((1, LANES, s), lambda b, p: (b, p, 0)),
            pl.BlockSpec((tq, tq), lambda b, p: (0, 0)),
        ],
        out_specs=pl.BlockSpec((1, s, LANES), lambda b, p: (b, 0, p)),
        out_shape=jax.ShapeDtypeStruct((bsz, s, w), BF16),
        scratch_shapes=[pltpu.VMEM((2, 2, HEAD_DIM, tq), F32), pltpu.VMEM((2, 2, 1, tq), F32)],
        compiler_params=pltpu.CompilerParams(
            dimension_semantics=("arbitrary", "arbitrary"), vmem_limit_bytes=VMEM_LIMIT),
        name="sb_attn",
    )(q, k, vt, after)


def _fox_kernel(fend_ref, par_ref, q_ref, k_ref, vt_ref, o_ref, acc_ref, m_ref, p0_ref, p1_ref, *, tq,
                n_heads):
    tk = tq
    n_blk = q_ref.shape[1] // tq
    head0 = (pl.program_id(0) * n_heads + 2 * pl.program_id(1))

    def first_dead_block(head, qi):
        base = head * n_blk
        f_q = fend_ref[base + jnp.maximum(qi - 1, 0)]

        def alive(j):
            jc = jnp.maximum(j, 0)
            return jnp.logical_and(j >= 0, f_q - fend_ref[base + jc] >= -par_ref[0])

        return lax.while_loop(alive, lambda j: j - 1, qi - 1)

    def plan(qi):
        j_dead = jnp.minimum(first_dead_block(head0, qi), first_dead_block(head0 + 1, qi))
        n_left = jnp.maximum(qi - 2 - j_dead, 0)
        odd = n_left % 2
        has_dead_below = j_dead >= 0
        n_pairs = n_left // 2 + jnp.where(has_dead_below, odd, 0)
        return n_pairs, jnp.logical_and(odd == 1, jnp.logical_not(has_dead_below))

    def q_tile(qi, h):
        return q_ref[0, pl.ds(pl.multiple_of(qi * tq, tq), tq), h * LANES:(h + 1) * LANES]

    def finalize(qi):
        outs = []
        for h in range(2):
            acc = acc_ref[h]
            outs.append(acc[0:HEAD_DIM, :] / acc[HEAD_DIM:HEAD_DIM + 1, :])
        o_ref[0, pl.ds(pl.multiple_of(qi * tq, tq), tq), :] = (
            jnp.concatenate(outs, axis=0).T.astype(o_ref.dtype))

    key = lax.broadcasted_iota(jnp.int32, (tk, tq), 0)
    qry = lax.broadcasted_iota(jnp.int32, (tk, tq), 1)
    causal = key <= qry

    def process(qi, blocks, first):
        starts = [pl.multiple_of(j * tk, tk) for j in blocks]
        units = [(b, h) for b in range(len(blocks)) for h in range(2)]
        s = {(b, h): _dot_nt(k_ref[0, pl.ds(starts[b], tk), h * LANES:(h + 1) * LANES],
                             q_tile(qi, h)) for b, h in units}
        p = {}
        alpha = {}
        for h in range(2):
            if first:
                s[0, h] = jnp.where(causal, s[0, h], NEG_BIG)
            m_new = None if first else m_ref[h]
            for b in range(len(blocks)):
                mb = jnp.max(s[b, h], axis=0, keepdims=True)
                m_new = mb if m_new is None else jnp.maximum(m_new, mb)
            if not first:
                alpha[h] = jnp.exp(m_ref[h] - m_new)
            for b in range(len(blocks)):
                p[b, h] = jnp.exp(s[b, h] - m_new).astype(BF16)
            m_ref[h] = m_new
        for h in range(2):
            pv = None
            for b in range(len(blocks)):
                part = _dot(vt_ref[0, h * V_ROWS:(h + 1) * V_ROWS, pl.ds(starts[b], tk)], p[b, h])
                pv = part if pv is None else pv + part
            acc_ref[h] = pv if first else alpha[h] * acc_ref[h] + pv

    def online_path():
        def q_block(qi, carry):
            n_pairs, lone_block0 = plan(qi)

            @pl.when(qi == 0)
            def _():
                process(qi, [qi], True)

            @pl.when(qi > 0)
            def _():
                process(qi, [qi, qi - 1], True)

            def body(i, c):
                j = qi - 2 - 2 * i
                process(qi, [j, j - 1], False)
                return c

            lax.fori_loop(0, n_pairs, body, 0)

            @pl.when(lone_block0)
            def _():
                process(qi, [0], False)

            finalize(qi)
            return carry

        lax.fori_loop(0, n_blk, q_block, 0)

    win = 2 * tk
    wrow = lax.broadcasted_iota(jnp.int32, (win, tq), 0)
    wcol = lax.broadcasted_iota(jnp.int32, (win, tq), 1)

    def window(qi, i):
        jl = qi - 1 - 2 * i
        j_lo = jnp.maximum(jl, 0)
        return j_lo, pl.multiple_of(j_lo * tk, tk), jnp.where(jl < 0, tk, win)

    slots = (p0_ref, p1_ref)

    def stage_a(qi, i, first, dst_ref):
        j_lo, start, row_lim = window(qi, i)
        keep = wrow < row_lim
        if first:
            keep = jnp.logical_and(keep, wrow <= wcol + (qi - j_lo) * tk)
        for h in range(2):
            s = _dot_nt(k_ref[0, pl.ds(start, win), h * LANES:(h + 1) * LANES], q_tile(qi, h))
            dst_ref[h] = jnp.exp(jnp.where(keep, s, NEG_BIG)).astype(BF16)

    def stage_c(qi, i, src_ref):
        _, start, _ = window(qi, i)
        for h in range(2):
            acc_ref[h] += _dot(vt_ref[0, h * V_ROWS:(h + 1) * V_ROWS, pl.ds(start, win)], src_ref[h])

    def overlapped(t, a_args, c_args):
        for parity in range(2):
            @pl.when((t & 1) == parity)
            def _():
                stage_a(*a_args, slots[parity])
                stage_c(*c_args, slots[1 - parity])

    def bounded_path():
        acc_ref[...] = jnp.zeros_like(acc_ref)
        stage_a(0, 0, True, slots[0])

        def q_block(qi, t):
            n_pairs, lone_block0 = plan(qi)
            n_stages = 1 + n_pairs + jnp.where(lone_block0, 1, 0)

            def body(i, t):
                overlapped(t + 1, (qi, i, False), (qi, i - 1))
                return t + 1

            t = lax.fori_loop(1, n_stages, body, t)
            overlapped(t + 1, (jnp.minimum(qi + 1, n_blk - 1), 0, True), (qi, n_stages - 1))
            finalize(qi)
            acc_ref[...] = jnp.zeros_like(acc_ref)
            return t + 1

        lax.fori_loop(0, n_blk, q_block, jnp.int32(0))

    bounded = par_ref[1] > 0.5
    pl.when(bounded)(bounded_path)
    pl.when(jnp.logical_not(bounded))(online_path)


def _fox_attention(q, k, vt, f, qk_bound, *, tq):
    bsz, s, w = q.shape
    pair = 2 * LANES
    n_heads = w // LANES
    assert s >= 2 * tq
    f_end = f[:, :, tq - 1::tq].reshape(-1)
    bounded = qk_bound <= MAX_UNSTABILISED_LOGIT
    par = jnp.stack([jnp.where(bounded, 0.5 - EXP_ZERO, 2.0 * qk_bound - EXP_ZERO),
                     jnp.where(bounded, 1.0, 0.0)]).astype(F32)
    kern = functools.partial(_fox_kernel, tq=tq, n_heads=n_heads)
    grid_spec = pltpu.PrefetchScalarGridSpec(
        num_scalar_prefetch=2,
        grid=(bsz, w // pair),
        in_specs=[
            pl.BlockSpec((1, s, pair), lambda b, p, fe, th: (b, 0, p)),
            pl.BlockSpec((1, s, pair), lambda b, p, fe, th: (b, 0, p)),
            pl.BlockSpec((1, 2 * V_ROWS, s), lambda b, p, fe, th: (b, p, 0)),
        ],
        out_specs=pl.BlockSpec((1, s, LANES), lambda b, p, fe, th: (b, 0, p)),
        scratch_shapes=[pltpu.VMEM((2, V_ROWS, tq), F32), pltpu.VMEM((2, 1, tq), F32),
                        pltpu.VMEM((2, 2 * tq, tq), BF16), pltpu.VMEM((2, 2 * tq, tq), BF16)],
    )
    return pl.pallas_call(
        kern,
        grid_spec=grid_spec,
        out_shape=jax.ShapeDtypeStruct((bsz, s, w // 2), BF16),
        compiler_params=pltpu.CompilerParams(
            dimension_semantics=("arbitrary", "arbitrary"), vmem_limit_bytes=VMEM_LIMIT),
        name="fox_attn",
    )(f_end, par, q, k, vt)


def _out_kernel(x_ref, osb_ref, ofox_ref, osgu_ref, mod_ref, g2_ref, wo_ref, w1_ref, w2_ref,
                o_ref, *, ff_chunk):
    sb_w = osb_ref.shape[2]
    fox_w = ofox_ref.shape[2]
    x = x_ref[0]
    mix = (_dot(osb_ref[0], wo_ref[0:sb_w, :])
           + _dot(ofox_ref[0], wo_ref[sb_w:sb_w + fox_w, :])
           + _dot(osgu_ref[0], wo_ref[sb_w + fox_w:, :]))
    x1 = x + mod_ref[0, 2:3, :] * mix
    ms = jnp.mean(x1 * x1, axis=-1, keepdims=True)
    h = x1 * lax.rsqrt(ms + EPS) * g2_ref[...]
    hb = (h * (1.0 + mod_ref[0, 4:5, :]) + mod_ref[0, 3:4, :]).astype(BF16)
    d_ff = w1_ref.shape[1]
    acc = None
    for c in range(d_ff // ff_chunk):
        hid = jnp.maximum(_dot(hb, w1_ref[:, c * ff_chunk:(c + 1) * ff_chunk]), 0.0)
        part = _dot((hid * hid).astype(BF16), w2_ref[c * ff_chunk:(c + 1) * ff_chunk, :])
        acc = part if acc is None else acc + part
    o_ref[0] = x1 + mod_ref[0, 5:6, :] * acc


def _out_mlp(x, osb, ofox, osgu, mod, g2, wo, w1, w2, *, layer, tm):
    bsz, s, d = x.shape
    tok = lambda b, i: (b, i, 0)

    def of_layer(a, **kw):
        nd = a.ndim - 1
        return pl.BlockSpec((None,) + a.shape[1:], lambda b, i: (layer,) + (0,) * nd, **kw)

    single = pl.Buffered(1)
    kern = functools.partial(_out_kernel, ff_chunk=1024)
    return pl.pallas_call(
        kern,
        grid=(bsz, s // tm),
        in_specs=[
            pl.BlockSpec((1, tm, d), tok),
            pl.BlockSpec((1, tm, osb.shape[2]), tok),
            pl.BlockSpec((1, tm, ofox.shape[2]), tok),
            pl.BlockSpec((1, tm, osgu.shape[2]), tok),
            pl.BlockSpec((None, 1, 6, d), lambda b, i: (layer, b, 0, 0)),
            of_layer(g2),
            of_layer(wo, pipeline_mode=single),
            of_layer(w1, pipeline_mode=single),
            of_layer(w2, pipeline_mode=single),
        ],
        out_specs=pl.BlockSpec((1, tm, d), tok),
        out_shape=jax.ShapeDtypeStruct((bsz, s, d), F32),
        compiler_params=pltpu.CompilerParams(
            dimension_semantics=("arbitrary", "arbitrary"), vmem_limit_bytes=VMEM_LIMIT),
        name="out_mlp",
    )(x, osb, ofox, osgu, mod, g2, wo, w1, w2)


def kernel(x, c, ada_w, ada_b, norm1_g, norm2_g, w_in, b_forget, q_norm_g, k_norm_g, sgu_norm_g,
           sgu_w, sgu_b, w_out, mlp_w1, mlp_w2):
    depth, d, _ = ada_w.shape
    bsz, s, _ = x.shape
    fox_heads = b_forget.shape[1]
    fox_w = fox_heads * HEAD_DIM
    sgu_groups, chunk = sgu_b.shape[1], sgu_b.shape[2]
    sgu_wd = sgu_groups * sgu_norm_g.shape[2]
    sb_w = (w_in.shape[2] - 3 * fox_w - fox_heads - 2 * sgu_wd) // 3
    f_lo = 3 * sb_w + 3 * fox_w

    mod = _modulation(c, ada_w, ada_b).reshape(depth, bsz, 6, d)

    perm = jnp.argsort(b_forget, axis=1)

    def relabel_cols(cols):
        t = cols.reshape(depth, d, fox_heads, HEAD_DIM)
        return jnp.take_along_axis(t, perm[:, None, :, None], axis=2).reshape(depth, d, fox_w)

    o = 3 * sb_w
    w_bf = w_in.astype(BF16)
    fox_cols = [relabel_cols(w_bf[:, :, o + i * fox_w:o + (i + 1) * fox_w]) for i in range(3)]
    w = jnp.concatenate([w_bf[:, :, :o]] + fox_cols + [w_bf[:, :, f_lo + fox_heads:]], axis=2)
    wf = jnp.take_along_axis(w_in[:, :, f_lo:f_lo + fox_heads], perm[:, None, :], axis=2)
    wf = jnp.pad(wf, ((0, 0), (0, 0), (0, LANES - fox_heads)))
    wf_hi = wf.astype(BF16)
    wf = jnp.concatenate([wf_hi, (wf - wf_hi.astype(F32)).astype(BF16)], axis=2)
    bf = jnp.pad(jnp.take_along_axis(b_forget, perm, axis=1), ((0, 0), (0, LANES - fox_heads)))
    bf = bf.reshape(depth, 1, LANES)
    qg = jnp.tile(q_norm_g, (1, MXU_DIM // HEAD_DIM)).reshape(depth, 1, MXU_DIM)
    kg = jnp.tile(k_norm_g, (1, MXU_DIM // HEAD_DIM)).reshape(depth, 1, MXU_DIM)
    sg = sgu_norm_g.reshape(depth, 1, sgu_wd)
    sb = jnp.repeat(jnp.swapaxes(sgu_b, 1, 2), sgu_norm_g.shape[2], axis=2)
    qk_bound = ((1.03 * HEAD_DIM ** 0.5) * jnp.max(jnp.abs(q_norm_g), axis=1)
                * jnp.max(jnp.abs(k_norm_g), axis=1))
    qx = jnp.zeros((depth, 1, LANES), F32).at[:, 0, HEAD_DIM:HEAD_DIM + 3].set(-1.0)
    qx = qx.at[:, 0, HEAD_DIM + 6].set(-qk_bound)
    wo_fox = w_out[:, sb_w:sb_w + fox_w].reshape(depth, fox_heads, HEAD_DIM, d)
    wo_fox = jnp.take_along_axis(wo_fox, perm[:, :, None, None], axis=1).reshape(depth, fox_w, d)
    wo = jnp.concatenate([w_out[:, :sb_w], wo_fox, w_out[:, sb_w + fox_w:]], axis=1).astype(BF16)
    w1 = mlp_w1.astype(BF16)
    w2 = mlp_w2.astype(BF16)
    g1 = norm1_g.reshape(depth, 1, d)
    g2 = norm2_g.reshape(depth, 1, d)

    for l in range(depth):
        qa, ka, vat, qf, kf, vft, osgu, f = _in_proj(
            x, mod, g1, w, wf, bf, qx, qg, kg, sg, sgu_w, sb,
            layer=l, tm=512, sb_w=sb_w, fox_w=fox_w, sgu_w=sgu_wd)
        osb = _sb_attention(qa, ka, vat, tq=256)
        ofox = _fox_attention(qf, kf, vft, f, qk_bound[l], tq=256)
        x = _out_mlp(x, osb, ofox, osgu, mod, g2, wo, w1, w2, layer=l, tm=512)
    return x
```

```python
import functools
import math

import jax
import jax.numpy as jnp
from jax import lax
from jax.experimental import pallas as pl
from jax.experimental.pallas import tpu as pltpu

HEAD_DIM = 64
LANES = 128
MXU_DIM = 256
BF16_ROWS = 16
V_ROWS = HEAD_DIM + BF16_ROWS
EPS = 1e-6
NEG_BIG = -1e30
EXP_ZERO = -104.0
LOG2_E = math.log2(math.e)
MAX_UNSTABILISED_LOGIT = 40.0
VMEM_LIMIT = 56 * 1024 * 1024

F32 = jnp.float32
BF16 = jnp.bfloat16


def _dot(a, b):
    return jnp.dot(a, b, preferred_element_type=F32)


def _dot_nt(a, b):
    return lax.dot_general(a, b, (((1,), (1,)), ((), ())), preferred_element_type=F32)


def _block_start(j, size):
    return j * size if isinstance(j, int) else pl.multiple_of(j * size, size)


def _split2(x):
    hi = x.astype(BF16)
    lo = (x - hi.astype(F32)).astype(BF16)
    return hi, lo


def _split3(x):
    hi = x.astype(BF16)
    r = x - hi.astype(F32)
    mid = r.astype(BF16)
    lo = (r - mid.astype(F32)).astype(BF16)
    return hi, mid, lo


def _group_mean_matrix():
    head = jnp.arange(MXU_DIM) // HEAD_DIM
    return jnp.where(head[:, None] == head[None, :], 1.0 / HEAD_DIM, 0.0).astype(BF16)


def _placement_matrix(n_heads):
    row = jnp.arange(LANES)[:, None]
    col = jnp.arange(2 * n_heads * LANES)[None, :]
    tile = col // LANES
    is_q = (tile >= n_heads).astype(jnp.int32)
    part = col % LANES - HEAD_DIM - 3 * is_q
    hit = (part >= 0) & (part < 3) & (row == part * 8 + tile - n_heads * is_q)
    return hit.astype(BF16)


def _head_rmsnorm(t, gmat, gain):
    ms = _dot((t * t).astype(BF16), gmat)
    return t * lax.rsqrt(ms + EPS) * gain


def _gelu_tanh(x):
    c = math.sqrt(2.0 / math.pi)
    return x * (0.5 * (1.0 + jnp.tanh(c * (x + 0.044715 * (x * x * x)))))


def _log_sigmoid(x):
    return jnp.minimum(x, 0.0) - jnp.log(1.0 + jnp.exp(-jnp.abs(x)))


def _mod_kernel(ct_ref, w_ref, b_ref, o_ref):
    ct = ct_ref[...]
    cond = ct * (1.0 / (1.0 + jnp.exp(-ct)))
    w = w_ref[0]
    rows = [jnp.sum(cond[:, b:b + 1] * w, axis=0, keepdims=True) for b in range(ct.shape[1])]
    o_ref[0] = jnp.concatenate(rows, axis=0) + b_ref[0]


def _modulation(c, ada_w, ada_b):
    depth, d, n = ada_w.shape
    bsz = c.shape[0]
    tn = 1536
    return pl.pallas_call(
        _mod_kernel,
        grid=(depth, n // tn),
        in_specs=[
            pl.BlockSpec((d, bsz), lambda l, j: (0, 0)),
            pl.BlockSpec((1, d, tn), lambda l, j: (l, 0, j)),
            pl.BlockSpec((1, 1, tn), lambda l, j: (l, 0, j)),
        ],
        out_specs=pl.BlockSpec((1, bsz, tn), lambda l, j: (l, 0, j)),
        out_shape=jax.ShapeDtypeStruct((depth, bsz, n), F32),
        compiler_params=pltpu.CompilerParams(
            dimension_semantics=("arbitrary", "arbitrary"), vmem_limit_bytes=VMEM_LIMIT),
        name="adaln_mod",
    )(c.T, ada_w, ada_b.reshape(depth, 1, n))


def _in_kernel(x_ref, mod_ref, g1_ref, w_ref, wf_ref, bf_ref, qx_ref, qg_ref, kg_ref,
               sg_ref, sw_ref, sb_ref, ltri_ref, sel_ref, gmat_ref,
               qa_ref, ka_ref, vat_ref, qf_ref, kf_ref, vft_ref, og_ref, f_ref,
               carry_ref, proj_ref, *, sb_w, fox_w, sgu_w, chunk):
    @pl.when(pl.program_id(1) == 0)
    def _():
        carry_ref[...] = jnp.zeros_like(carry_ref)

    tm = x_ref.shape[1]
    x = x_ref[0]
    ms = jnp.mean(x * x, axis=-1, keepdims=True)
    h = x * lax.rsqrt(ms + EPS) * g1_ref[...]
    h = h * (1.0 + mod_ref[0, 1:2, :]) + mod_ref[0, 0:1, :]
    hb = h.astype(BF16)
    scale = HEAD_DIM ** -0.5
    n_attn = 3 * sb_w + 3 * fox_w
    wn = w_ref.shape[1]
    n_heads = f_ref.shape[1]
    lane = lax.broadcasted_iota(jnp.int32, (1, LANES), 1)
    head_lane = lane < n_heads
    gmat = gmat_ref[...]

    def pack3(parts):
        a, b, c = (p.astype(F32) for p in parts)
        return (a + pltpu.roll(b, 8, 1) + pltpu.roll(c, 16, 1)).astype(BF16)

    proj_ref[:, 0:n_attn] = _dot(hb, w_ref[:, 0:n_attn])
    h_hi, h_lo = _split2(h)
    t2 = _dot(h_hi, wf_ref[...])
    fl = t2[:, :LANES] + t2[:, LANES:] + _dot(h_lo, wf_ref[:, :LANES])
    proj_ref[:, n_attn:wn] = _dot(hb, w_ref[:, n_attn:wn])
    logf = jnp.where(head_lane, _log_sigmoid(fl + bf_ref[...]), 0.0)
    cp = _dot(ltri_ref[...], pack3(_split3(logf)))

    o = 3 * sb_w
    q_tiles = [_head_rmsnorm(proj_ref[:, o + j * MXU_DIM:o + (j + 1) * MXU_DIM], gmat, qg_ref[...]) * scale
               for j in range(fox_w // MXU_DIM)]
    o += fox_w

    cum = cp + pltpu.roll(cp, LANES - 8, 1) + pltpu.roll(cp, LANES - 16, 1)
    cum = jnp.where(head_lane, cum, 0.0) + carry_ref[0:1, :]
    carry_ref[0:1, :] = cum[tm - 1:tm, :]
    f_ref[0] = cum.T[0:n_heads, :]
    extras = _dot(pack3(_split3(cum)), sel_ref[...])

    k_tiles = [_head_rmsnorm(proj_ref[:, o + j * MXU_DIM:o + (j + 1) * MXU_DIM], gmat, kg_ref[...])
               for j in range(fox_w // MXU_DIM)]
    o += fox_w

    qa_ref[0] = (proj_ref[:, 0:sb_w] * (scale * LOG2_E)).astype(BF16)
    ka_ref[0] = proj_ref[:, sb_w:2 * sb_w].astype(BF16)
    vat_ref[0] = proj_ref[:, 2 * sb_w:3 * sb_w].T.astype(BF16)

    is_head = lane < HEAD_DIM
    q_extra = qx_ref[...]
    k_extra = jnp.where((lane >= HEAD_DIM + 3) & (lane < HEAD_DIM + 7), 1.0, 0.0)
    heads_per_mxu = MXU_DIM // HEAD_DIM

    def head_tile(tn, s):
        half = tn[:, (s // 2) * LANES:(s // 2 + 1) * LANES]
        return half if s % 2 == 0 else pltpu.roll(half, HEAD_DIM, 1)

    for j in range(fox_w // MXU_DIM):
        for s in range(heads_per_mxu):
            hh = heads_per_mxu * j + s
            xq = extras[:, (n_heads + hh) * LANES:(n_heads + hh + 1) * LANES] + q_extra
            qf_ref[0, :, hh * LANES:(hh + 1) * LANES] = jnp.where(
                is_head, head_tile(q_tiles[j], s), xq).astype(BF16)
            xk = extras[:, hh * LANES:(hh + 1) * LANES] + k_extra
            kf_ref[0, :, hh * LANES:(hh + 1) * LANES] = jnp.where(
                is_head, head_tile(k_tiles[j], s), xk).astype(BF16)
    vt = proj_ref[:, o:o + fox_w].T.astype(BF16); o += fox_w
    ones = jnp.ones((V_ROWS - HEAD_DIM, tm), BF16)
    for hh in range(fox_w // HEAD_DIM):
        vft_ref[0, hh * V_ROWS:hh * V_ROWS + HEAD_DIM, :] = vt[hh * HEAD_DIM:(hh + 1) * HEAD_DIM, :]
        vft_ref[0, hh * V_ROWS + HEAD_DIM:(hh + 1) * V_ROWS, :] = ones

    gu = _gelu_tanh(proj_ref[:, o:o + sgu_w]); o += sgu_w
    gv = _gelu_tanh(proj_ref[:, o:o + sgu_w]); o += sgu_w
    lane_c = lax.broadcasted_iota(jnp.int32, (chunk, LANES), 1)
    rr = lax.broadcasted_iota(jnp.int32, (chunk, chunk), 0)
    cs = lax.broadcasted_iota(jnp.int32, (chunk, chunk), 1)
    wt = [jnp.where(rr >= cs, sw_ref[g], 0.0).astype(BF16) for g in range(sw_ref.shape[0])]
    vn_all = _head_rmsnorm(gv, gmat, sg_ref[...]).astype(BF16)
    for p in range(sgu_w // LANES):
        vn = vn_all[:, p * LANES:(p + 1) * LANES]
        for ci in range(tm // chunk):
            vblk = vn[ci * chunk:(ci + 1) * chunk, :]
            mixed = jnp.where(lane_c < HEAD_DIM, _dot(wt[2 * p], vblk), _dot(wt[2 * p + 1], vblk))
            mixed = mixed + sb_ref[:, p * LANES:(p + 1) * LANES]
            og_ref[0, ci * chunk:(ci + 1) * chunk, p * LANES:(p + 1) * LANES] = (
                gu[ci * chunk:(ci + 1) * chunk, p * LANES:(p + 1) * LANES] * mixed).astype(BF16)


def _in_proj(x, mod, g1, w, wf, bf, qx, qg, kg, sg, sw, sb, *, layer, tm, sb_w, fox_w, sgu_w):
    bsz, s, d = x.shape
    chunk = sw.shape[-1]
    wn = w.shape[2]
    fox_heads = fox_w // HEAD_DIM

    def of_layer(a):
        nd = a.ndim - 1
        return pl.BlockSpec((None,) + a.shape[1:], lambda b, i: (layer,) + (0,) * nd)

    assert fox_heads <= 8 and fox_w % MXU_DIM == 0 and sgu_w == MXU_DIM
    tok_idx = jnp.arange(tm)
    ltri = (tok_idx[:, None] >= tok_idx[None, :]).astype(BF16)
    const2 = lambda b, i: (0, 0)
    tok = lambda b, i: (b, i, 0)
    tok_t = lambda b, i: (b, 0, i)
    kern = functools.partial(_in_kernel, sb_w=sb_w, fox_w=fox_w, sgu_w=sgu_w, chunk=chunk)
    out_shape = [
        jax.ShapeDtypeStruct((bsz, s, sb_w), BF16),
        jax.ShapeDtypeStruct((bsz, s, sb_w), BF16),
        jax.ShapeDtypeStruct((bsz, sb_w, s), BF16),
        jax.ShapeDtypeStruct((bsz, s, fox_heads * LANES), BF16),
        jax.ShapeDtypeStruct((bsz, s, fox_heads * LANES), BF16),
        jax.ShapeDtypeStruct((bsz, fox_heads * V_ROWS, s), BF16),
        jax.ShapeDtypeStruct((bsz, s, sgu_w), BF16),
        jax.ShapeDtypeStruct((bsz, fox_heads, s), F32),
    ]
    out_specs = [
        pl.BlockSpec((1, tm, sb_w), tok),
        pl.BlockSpec((1, tm, sb_w), tok),
        pl.BlockSpec((1, sb_w, tm), tok_t),
        pl.BlockSpec((1, tm, fox_heads * LANES), tok),
        pl.BlockSpec((1, tm, fox_heads * LANES), tok),
        pl.BlockSpec((1, fox_heads * V_ROWS, tm), tok_t),
        pl.BlockSpec((1, tm, sgu_w), tok),
        pl.BlockSpec((1, fox_heads, tm), tok_t),
    ]
    return pl.pallas_call(
        kern,
        grid=(bsz, s // tm),
        in_specs=[
            pl.BlockSpec((1, tm, d), tok),
            pl.BlockSpec((None, 1, 6, d), lambda b, i: (layer, b, 0, 0)),
            of_layer(g1), of_layer(w), of_layer(wf), of_layer(bf), of_layer(qx), of_layer(qg),
            of_layer(kg), of_layer(sg), of_layer(sw), of_layer(sb),
            pl.BlockSpec((tm, tm), const2),
            pl.BlockSpec((LANES, 2 * fox_heads * LANES), const2),
            pl.BlockSpec((MXU_DIM, MXU_DIM), const2),
        ],
        out_specs=out_specs,
        out_shape=out_shape,
        scratch_shapes=[pltpu.VMEM((8, LANES), F32), pltpu.VMEM((tm, wn), F32)],
        compiler_params=pltpu.CompilerParams(
            dimension_semantics=("arbitrary", "arbitrary"), vmem_limit_bytes=VMEM_LIMIT),
        name="in_proj",
    )(x, mod, g1, w, wf, bf, qx, qg, kg, sg, sw, sb, ltri, _placement_matrix(fox_heads),
      _group_mean_matrix())


def _sb_kernel(q_ref, k_ref, vt_ref, after_ref, o_ref, acc_ref, c_ref, *, tq):
    tk = tq
    n_blk = q_ref.shape[1] // tq
    lane = lax.broadcasted_iota(jnp.int32, (1, LANES), 1)
    key = lax.broadcasted_iota(jnp.int32, (tk, tq), 0)
    qry = lax.broadcasted_iota(jnp.int32, (tk, tq), 1)
    causal = key < qry
    after = after_ref[...]

    def process(chains):
        units, starts, qh = [], {}, {}
        for slot, qi, blocks, first in chains:
            q = q_ref[0, pl.ds(_block_start(qi, tq), tq), :]
            qh[slot] = (jnp.where(lane < HEAD_DIM, q, 0).astype(BF16),
                        jnp.where(lane >= HEAD_DIM, q, 0).astype(BF16))
            for b, j in enumerate(blocks):
                starts[slot, b] = _block_start(j, tk)
                units += [(slot, b, h, first and b == 0) for h in range(2)]
        z = {(sl, b, h): _dot_nt(k_ref[0, pl.ds(starts[sl, b], tk), :], qh[sl][h])
             for sl, b, h, _ in units}
        l1mb, head = {}, {}
        for sl, b, h, diag in units:
            zz = z[sl, b, h]
            nz = -zz
            lg = jnp.minimum(nz, 0.0) - jnp.log2(1.0 + jnp.exp2(jnp.minimum(zz, nz)))
            if diag:
                lg = jnp.where(causal, lg, 0.0)
            l1mb[sl, b, h] = lg.astype(BF16)
            head[sl, b, h] = (zz + lg, lg[0:1, :])
        between = {(sl, b, h): _dot(after, l1mb[sl, b, h]) for sl, b, h, _ in units}
        a = {}
        for slot, qi, blocks, first in chains:
            for h in range(2):
                c = None if first else c_ref[slot, h]
                for b in range(len(blocks)):
                    e = head[slot, b, h][0] + between[slot, b, h]
                    if c is not None:
                        e = e + c
                    w = jnp.exp2(e)
                    if first and b == 0:
                        w = jnp.where(causal, w, 0.0)
                    a[slot, b, h] = w.astype(BF16)
                    block_sum = between[slot, b, h][0:1, :] + head[slot, b, h][1]
                    c = block_sum if c is None else c + block_sum
                c_ref[slot, h] = c
        for slot, qi, blocks, first in chains:
            for h in range(2):
                pv = None
                for b in range(len(blocks)):
                    part = _dot(vt_ref[0, h * HEAD_DIM:(h + 1) * HEAD_DIM, pl.ds(starts[slot, b], tk)],
                                a[slot, b, h])
                    pv = part if pv is None else pv + part
                if first:
                    acc_ref[slot, h] = pv
                else:
                    acc_ref[slot, h] += pv

    def carry_max(slot):
        cm = jnp.maximum(jnp.max(c_ref[slot, 0], axis=1, keepdims=True),
                         jnp.max(c_ref[slot, 1], axis=1, keepdims=True))
        return cm[0, 0]


    def finish(slot, qi):
        def cond(carry):
            j, cmax = carry
            return jnp.logical_and(j >= 0, cmax > EXP_ZERO * LOG2_E)

        def body(carry):
            j, _ = carry
            process([(slot, qi, [j], False)])
            return j - 1, carry_max(slot)

        lax.while_loop(cond, body, (jnp.asarray(qi - 2, jnp.int32), carry_max(slot)))
        o_ref[0, pl.ds(_block_start(qi, tq), tq), :] = (
            jnp.concatenate([acc_ref[slot, 0], acc_ref[slot, 1]], axis=0).T.astype(o_ref.dtype))

    process([(0, 0, [0], True)])
    finish(0, 0)

    def two_blocks(g, carry):
        qa, qb = 1 + 2 * g, 2 + 2 * g
        process([(0, qa, [qa, qa - 1], True), (1, qb, [qb, qb - 1], True)])
        finish(0, qa)
        finish(1, qb)
        return carry

    lax.fori_loop(0, (n_blk - 1) // 2, two_blocks, 0)
    if (n_blk - 1) % 2:
        last = n_blk - 1
        process([(0, last, [last, last - 1], True)])
        finish(0, last)


def _sb_attention(q, k, vt, *, tq):
    bsz, s, w = q.shape
    assert s >= 2 * tq
    idx = jnp.arange(tq)
    after = (idx[None, :] > idx[:, None]).astype(BF16)
    kern = functools.partial(_sb_kernel, tq=tq)
    return pl.pallas_call(
        kern,
        grid=(bsz, w // LANES),
        in_specs=[
            pl.BlockSpec((1, s, LANES), lambda b, p: (b, 0, p)),
            pl.BlockSpec((1, s, LANES), lambda b, p: (b, 0, p)),
            pl.BlockSpec((1, LANES, s), lambda b, p: (b, p, 0)),
            pl.BlockSpec((tq, tq), lambda b, p: (0, 0)),
        ],
        out_specs=pl.BlockSpec((1, s, LANES), lambda b, p: (b, 0, p)),
        out_shape=jax.ShapeDtypeStruct((bsz, s, w), BF16),
        scratch_shapes=[pltpu.VMEM((2, 2, HEAD_DIM, tq), F32), pltpu.VMEM((2, 2, 1, tq), F32)],
        compiler_params=pltpu.CompilerParams(
            dimension_semantics=("arbitrary", "arbitrary"), vmem_limit_bytes=VMEM_LIMIT),
        name="sb_attn",
    )(q, k, vt, after)


def _fox_kernel(fend_ref, par_ref, q_ref, k_ref, vt_ref, o_ref, acc_ref, m_ref, p_ref, *, tq, n_heads):
    tk = tq
    n_blk = q_ref.shape[1] // tq
    head0 = (pl.program_id(0) * n_heads + 2 * pl.program_id(1))

    def first_dead_block(head, qi):
        base = head * n_blk
        f_q = fend_ref[base + jnp.maximum(qi - 1, 0)]

        def alive(j):
            jc = jnp.maximum(j, 0)
            return jnp.logical_and(j >= 0, f_q - fend_ref[base + jc] >= -par_ref[0])

        return lax.while_loop(alive, lambda j: j - 1, qi - 1)

    def plan(qi):
        j_dead = jnp.minimum(first_dead_block(head0, qi), first_dead_block(head0 + 1, qi))
        n_left = jnp.maximum(qi - 2 - j_dead, 0)
        odd = n_left % 2
        has_dead_below = j_dead >= 0
        n_pairs = n_left // 2 + jnp.where(has_dead_below, odd, 0)
        return n_pairs, jnp.logical_and(odd == 1, jnp.logical_not(has_dead_below))

    def q_tile(qi, h):
        return q_ref[0, pl.ds(pl.multiple_of(qi * tq, tq), tq), h * LANES:(h + 1) * LANES]

    def finalize(qi):
        outs = []
        for h in range(2):
            acc = acc_ref[h]
            outs.append(acc[0:HEAD_DIM, :] / acc[HEAD_DIM:HEAD_DIM + 1, :])
        o_ref[0, pl.ds(pl.multiple_of(qi * tq, tq), tq), :] = (
            jnp.concatenate(outs, axis=0).T.astype(o_ref.dtype))

    key = lax.broadcasted_iota(jnp.int32, (tk, tq), 0)
    qry = lax.broadcasted_iota(jnp.int32, (tk, tq), 1)
    causal = key <= qry

    def process(qi, blocks, first):
        starts = [pl.multiple_of(j * tk, tk) for j in blocks]
        units = [(b, h) for b in range(len(blocks)) for h in range(2)]
        s = {(b, h): _dot_nt(k_ref[0, pl.ds(starts[b], tk), h * LANES:(h + 1) * LANES],
                             q_tile(qi, h)) for b, h in units}
        p = {}
        alpha = {}
        for h in range(2):
            if first:
                s[0, h] = jnp.where(causal, s[0, h], NEG_BIG)
            m_new = None if first else m_ref[h]
            for b in range(len(blocks)):
                mb = jnp.max(s[b, h], axis=0, keepdims=True)
                m_new = mb if m_new is None else jnp.maximum(m_new, mb)
            if not first:
                alpha[h] = jnp.exp(m_ref[h] - m_new)
            for b in range(len(blocks)):
                p[b, h] = jnp.exp(s[b, h] - m_new).astype(BF16)
            m_ref[h] = m_new
        for h in range(2):
            pv = None
            for b in range(len(blocks)):
                part = _dot(vt_ref[0, h * V_ROWS:(h + 1) * V_ROWS, pl.ds(starts[b], tk)], p[b, h])
                pv = part if pv is None else pv + part
            acc_ref[h] = pv if first else alpha[h] * acc_ref[h] + pv

    def online_path():
        def q_block(qi, carry):
            n_pairs, lone_block0 = plan(qi)

            @pl.when(qi == 0)
            def _():
                process(qi, [qi], True)

            @pl.when(qi > 0)
            def _():
                process(qi, [qi, qi - 1], True)

            def body(i, c):
                j = qi - 2 - 2 * i
                process(qi, [j, j - 1], False)
                return c

            lax.fori_loop(0, n_pairs, body, 0)

            @pl.when(lone_block0)
            def _():
                process(qi, [0], False)

            finalize(qi)
            return carry

        lax.fori_loop(0, n_blk, q_block, 0)

    win = 2 * tk
    wrow = lax.broadcasted_iota(jnp.int32, (win, tq), 0)
    wcol = lax.broadcasted_iota(jnp.int32, (win, tq), 1)

    def window(qi, i):
        jl = qi - 1 - 2 * i
        j_lo = jnp.maximum(jl, 0)
        return j_lo, pl.multiple_of(j_lo * tk, tk), jnp.where(jl < 0, tk, win)

    def stage_a(qi, i, slot, first):
        j_lo, start, row_lim = window(qi, i)
        keep = wrow < row_lim
        if first:
            keep = jnp.logical_and(keep, wrow <= wcol + (qi - j_lo) * tk)
        for h in range(2):
            s = _dot_nt(k_ref[0, pl.ds(start, win), h * LANES:(h + 1) * LANES], q_tile(qi, h))
            p_ref[slot, h] = jnp.exp(jnp.where(keep, s, NEG_BIG)).astype(BF16)

    def stage_c(qi, i, slot):
        _, start, _ = window(qi, i)
        for h in range(2):
            acc_ref[h] += _dot(vt_ref[0, h * V_ROWS:(h + 1) * V_ROWS, pl.ds(start, win)], p_ref[slot, h])

    def bounded_path():
        acc_ref[...] = jnp.zeros_like(acc_ref)
        stage_a(0, 0, 0, True)

        def q_block(qi, t):
            n_pairs, lone_block0 = plan(qi)
            n_stages = 1 + n_pairs + jnp.where(lone_block0, 1, 0)

            def body(i, t):
                stage_c(qi, i - 1, t & 1)
                stage_a(qi, i, (t + 1) & 1, False)
                return t + 1

            t = lax.fori_loop(1, n_stages, body, t)
            stage_c(qi, n_stages - 1, t & 1)
            stage_a(jnp.minimum(qi + 1, n_blk - 1), 0, (t + 1) & 1, True)
            finalize(qi)
            acc_ref[...] = jnp.zeros_like(acc_ref)
            return t + 1

        lax.fori_loop(0, n_blk, q_block, jnp.int32(0))

    bounded = par_ref[1] > 0.5
    pl.when(bounded)(bounded_path)
    pl.when(jnp.logical_not(bounded))(online_path)


def _fox_attention(q, k, vt, f, qk_bound, *, tq):
    bsz, s, w = q.shape
    pair = 2 * LANES
    n_heads = w // LANES
    assert s >= 2 * tq
    f_end = f[:, :, tq - 1::tq].reshape(-1)
    bounded = qk_bound <= MAX_UNSTABILISED_LOGIT
    par = jnp.stack([jnp.where(bounded, 0.5 - EXP_ZERO, 2.0 * qk_bound - EXP_ZERO),
                     jnp.where(bounded, 1.0, 0.0)]).astype(F32)
    kern = functools.partial(_fox_kernel, tq=tq, n_heads=n_heads)
    grid_spec = pltpu.PrefetchScalarGridSpec(
        num_scalar_prefetch=2,
        grid=(bsz, w // pair),
        in_specs=[
            pl.BlockSpec((1, s, pair), lambda b, p, fe, th: (b, 0, p)),
            pl.BlockSpec((1, s, pair), lambda b, p, fe, th: (b, 0, p)),
            pl.BlockSpec((1, 2 * V_ROWS, s), lambda b, p, fe, th: (b, p, 0)),
        ],
        out_specs=pl.BlockSpec((1, s, LANES), lambda b, p, fe, th: (b, 0, p)),
        scratch_shapes=[pltpu.VMEM((2, V_ROWS, tq), F32), pltpu.VMEM((2, 1, tq), F32),
                        pltpu.VMEM((2, 2, 2 * tq, tq), BF16)],
    )
    return pl.pallas_call(
        kern,
        grid_spec=grid_spec,
        out_shape=jax.ShapeDtypeStruct((bsz, s, w // 2), BF16),
        compiler_params=pltpu.CompilerParams(
            dimension_semantics=("arbitrary", "arbitrary"), vmem_limit_bytes=VMEM_LIMIT),
        name="fox_attn",
    )(f_end, par, q, k, vt)


def _out_kernel(x_ref, osb_ref, ofox_ref, osgu_ref, mod_ref, g2_ref, wo_ref, w1_ref, w2_ref,
                o_ref, *, ff_chunk):
    sb_w = osb_ref.shape[2]
    fox_w = ofox_ref.shape[2]
    x = x_ref[0]
    mix = (_dot(osb_ref[0], wo_ref[0:sb_w, :])
           + _dot(ofox_ref[0], wo_ref[sb_w:sb_w + fox_w, :])
           + _dot(osgu_ref[0], wo_ref[sb_w + fox_w:, :]))
    x1 = x + mod_ref[0, 2:3, :] * mix
    ms = jnp.mean(x1 * x1, axis=-1, keepdims=True)
    h = x1 * lax.rsqrt(ms + EPS) * g2_ref[...]
    hb = (h * (1.0 + mod_ref[0, 4:5, :]) + mod_ref[0, 3:4, :]).astype(BF16)
    d_ff = w1_ref.shape[1]
    acc = None
    for c in range(d_ff // ff_chunk):
        hid = jnp.maximum(_dot(hb, w1_ref[:, c * ff_chunk:(c + 1) * ff_chunk]), 0.0)
        part = _dot((hid * hid).astype(BF16), w2_ref[c * ff_chunk:(c + 1) * ff_chunk, :])
        acc = part if acc is None else acc + part
    o_ref[0] = x1 + mod_ref[0, 5:6, :] * acc


def _out_mlp(x, osb, ofox, osgu, mod, g2, wo, w1, w2, *, layer, tm):
    bsz, s, d = x.shape
    tok = lambda b, i: (b, i, 0)

    def of_layer(a, **kw):
        nd = a.ndim - 1
        return pl.BlockSpec((None,) + a.shape[1:], lambda b, i: (layer,) + (0,) * nd, **kw)

    single = pl.Buffered(1)
    kern = functools.partial(_out_kernel, ff_chunk=1024)
    return pl.pallas_call(
        kern,
        grid=(bsz, s // tm),
        in_specs=[
            pl.BlockSpec((1, tm, d), tok),
            pl.BlockSpec((1, tm, osb.shape[2]), tok),
            pl.BlockSpec((1, tm, ofox.shape[2]), tok),
            pl.BlockSpec((1, tm, osgu.shape[2]), tok),
            pl.BlockSpec((None, 1, 6, d), lambda b, i: (layer, b, 0, 0)),
            of_layer(g2),
            of_layer(wo, pipeline_mode=single),
            of_layer(w1, pipeline_mode=single),
            of_layer(w2, pipeline_mode=single),
        ],
        out_specs=pl.BlockSpec((1, tm, d), tok),
        out_shape=jax.ShapeDtypeStruct((bsz, s, d), F32),
        compiler_params=pltpu.CompilerParams(
            dimension_semantics=("arbitrary", "arbitrary"), vmem_limit_bytes=VMEM_LIMIT),
        name="out_mlp",
    )(x, osb, ofox, osgu, mod, g2, wo, w1, w2)


def kernel(x, c, ada_w, ada_b, norm1_g, norm2_g, w_in, b_forget, q_norm_g, k_norm_g, sgu_norm_g,
           sgu_w, sgu_b, w_out, mlp_w1, mlp_w2):
    depth, d, _ = ada_w.shape
    bsz, s, _ = x.shape
    fox_heads = b_forget.shape[1]
    fox_w = fox_heads * HEAD_DIM
    sgu_groups, chunk = sgu_b.shape[1], sgu_b.shape[2]
    sgu_wd = sgu_groups * sgu_norm_g.shape[2]
    sb_w = (w_in.shape[2] - 3 * fox_w - fox_heads - 2 * sgu_wd) // 3
    f_lo = 3 * sb_w + 3 * fox_w

    mod = _modulation(c, ada_w, ada_b).reshape(depth, bsz, 6, d)

    perm = jnp.argsort(b_forget, axis=1)

    def relabel_cols(cols):
        t = cols.reshape(depth, d, fox_heads, HEAD_DIM)
        return jnp.take_along_axis(t, perm[:, None, :, None], axis=2).reshape(depth, d, fox_w)

    o = 3 * sb_w
    w_bf = w_in.astype(BF16)
    fox_cols = [relabel_cols(w_bf[:, :, o + i * fox_w:o + (i + 1) * fox_w]) for i in range(3)]
    w = jnp.concatenate([w_bf[:, :, :o]] + fox_cols + [w_bf[:, :, f_lo + fox_heads:]], axis=2)
    wf = jnp.take_along_axis(w_in[:, :, f_lo:f_lo + fox_heads], perm[:, None, :], axis=2)
    wf = jnp.pad(wf, ((0, 0), (0, 0), (0, LANES - fox_heads)))
    wf_hi = wf.astype(BF16)
    wf = jnp.concatenate([wf_hi, (wf - wf_hi.astype(F32)).astype(BF16)], axis=2)
    bf = jnp.pad(jnp.take_along_axis(b_forget, perm, axis=1), ((0, 0), (0, LANES - fox_heads)))
    bf = bf.reshape(depth, 1, LANES)
    qg = jnp.tile(q_norm_g, (1, MXU_DIM // HEAD_DIM)).reshape(depth, 1, MXU_DIM)
    kg = jnp.tile(k_norm_g, (1, MXU_DIM // HEAD_DIM)).reshape(depth, 1, MXU_DIM)
    sg = sgu_norm_g.reshape(depth, 1, sgu_wd)
    sb = jnp.repeat(jnp.swapaxes(sgu_b, 1, 2), sgu_norm_g.shape[2], axis=2)
    qk_bound = ((1.03 * HEAD_DIM ** 0.5) * jnp.max(jnp.abs(q_norm_g), axis=1)
                * jnp.max(jnp.abs(k_norm_g), axis=1))
    qx = jnp.zeros((depth, 1, LANES), F32).at[:, 0, HEAD_DIM:HEAD_DIM + 3].set(-1.0)
    qx = qx.at[:, 0, HEAD_DIM + 6].set(-qk_bound)
    wo_fox = w_out[:, sb_w:sb_w + fox_w].reshape(depth, fox_heads, HEAD_DIM, d)
    wo_fox = jnp.take_along_axis(wo_fox, perm[:, :, None, None], axis=1).reshape(depth, fox_w, d)
    wo = jnp.concatenate([w_out[:, :sb_w], wo_fox, w_out[:, sb_w + fox_w:]], axis=1).astype(BF16)
    w1 = mlp_w1.astype(BF16)
    w2 = mlp_w2.astype(BF16)
    g1 = norm1_g.reshape(depth, 1, d)
    g2 = norm2_g.reshape(depth, 1, d)

    for l in range(depth):
        qa, ka, vat, qf, kf, vft, osgu, f = _in_proj(
            x, mod, g1, w, wf, bf, qx, qg, kg, sg, sgu_w, sb,
            layer=l, tm=512, sb_w=sb_w, fox_w=fox_w, sgu_w=sgu_wd)
        osb = _sb_attention(qa, ka, vat, tq=256)
        ofox = _fox_attention(qf, kf, vft, f, qk_bound[l], tq=256)
        x = _out_mlp(x, osb, ofox, osgu, mod, g2, wo, w1, w2, layer=l, tm=512)
    return x
```

```python
import functools
import math

import jax
import jax.numpy as jnp
from jax import lax
from jax.experimental import pallas as pl
from jax.experimental.pallas import tpu as pltpu

HEAD_DIM = 64
LANES = 128
MXU_DIM = 256
V_ROWS = 2 * HEAD_DIM
EPS = 1e-6
NEG_BIG = -1e30
EXP_ZERO = -104.0
LOG2_E = math.log2(math.e)
MAX_UNSTABILISED_LOGIT = 40.0
VMEM_LIMIT = 56 * 1024 * 1024

F32 = jnp.float32
BF16 = jnp.bfloat16


def _dot(a, b):
    return jnp.dot(a, b, preferred_element_type=F32)


def _dot_nt(a, b):
    return lax.dot_general(a, b, (((1,), (1,)), ((), ())), preferred_element_type=F32)


def _block_start(j, size):
    return j * size if isinstance(j, int) else pl.multiple_of(j * size, size)


def _split3(x):
    hi = x.astype(BF16)
    r = x - hi.astype(F32)
    mid = r.astype(BF16)
    lo = (r - mid.astype(F32)).astype(BF16)
    return hi, mid, lo


def _group_mean_matrix():
    head = jnp.arange(MXU_DIM) // HEAD_DIM
    return jnp.where(head[:, None] == head[None, :], 1.0 / HEAD_DIM, 0.0).astype(BF16)


def _placement_matrix(n_heads):
    row = jnp.arange(LANES)[:, None]
    col = jnp.arange(2 * n_heads * LANES)[None, :]
    tile = col // LANES
    is_q = (tile >= n_heads).astype(jnp.int32)
    part = col % LANES - HEAD_DIM - 3 * is_q
    hit = (part >= 0) & (part < 3) & (row == part * 8 + tile - n_heads * is_q)
    return hit.astype(BF16)


def _head_rmsnorm(t, gmat, gain):
    ms = _dot((t * t).astype(BF16), gmat)
    return t * lax.rsqrt(ms + EPS) * gain


def _gelu_tanh(x):
    c = math.sqrt(2.0 / math.pi)
    return x * (0.5 * (1.0 + jnp.tanh(c * (x + 0.044715 * (x * x * x)))))


def _log_sigmoid(x):
    return jnp.minimum(x, 0.0) - jnp.log(1.0 + jnp.exp(-jnp.abs(x)))


def _mod_kernel(ct_ref, w_ref, b_ref, o_ref):
    ct = ct_ref[...]
    cond = ct * (1.0 / (1.0 + jnp.exp(-ct)))
    w = w_ref[0]
    rows = [jnp.sum(cond[:, b:b + 1] * w, axis=0, keepdims=True) for b in range(ct.shape[1])]
    o_ref[0] = jnp.concatenate(rows, axis=0) + b_ref[0]


def _modulation(c, ada_w, ada_b):
    depth, d, n = ada_w.shape
    bsz = c.shape[0]
    tn = 1536
    return pl.pallas_call(
        _mod_kernel,
        grid=(depth, n // tn),
        in_specs=[
            pl.BlockSpec((d, bsz), lambda l, j: (0, 0)),
            pl.BlockSpec((1, d, tn), lambda l, j: (l, 0, j)),
            pl.BlockSpec((1, 1, tn), lambda l, j: (l, 0, j)),
        ],
        out_specs=pl.BlockSpec((1, bsz, tn), lambda l, j: (l, 0, j)),
        out_shape=jax.ShapeDtypeStruct((depth, bsz, n), F32),
        compiler_params=pltpu.CompilerParams(
            dimension_semantics=("arbitrary", "arbitrary"), vmem_limit_bytes=VMEM_LIMIT),
        name="adaln_mod",
    )(c.T, ada_w, ada_b.reshape(depth, 1, n))


def _in_kernel(x_ref, mod_ref, g1_ref, w_ref, wf_ref, bf_ref, qx_ref, qg_ref, kg_ref,
               sg_ref, sw_ref, sb_ref, ltri_ref, sel_ref, gmat_ref,
               qa_ref, ka_ref, vat_ref, qf_ref, kf_ref, vft_ref, og_ref, f_ref,
               carry_ref, proj_ref, *, sb_w, fox_w, sgu_w, chunk):
    @pl.when(pl.program_id(1) == 0)
    def _():
        carry_ref[...] = jnp.zeros_like(carry_ref)

    tm = x_ref.shape[1]
    x = x_ref[0]
    ms = jnp.mean(x * x, axis=-1, keepdims=True)
    h = x * lax.rsqrt(ms + EPS) * g1_ref[...]
    h = h * (1.0 + mod_ref[0, 1:2, :]) + mod_ref[0, 0:1, :]
    hb = h.astype(BF16)
    scale = HEAD_DIM ** -0.5
    n_attn = 3 * sb_w + 3 * fox_w
    wn = w_ref.shape[1]
    n_heads = f_ref.shape[1]
    lane = lax.broadcasted_iota(jnp.int32, (1, LANES), 1)
    head_lane = lane < n_heads
    gmat = gmat_ref[...]

    def pack3(parts):
        a, b, c = (p.astype(F32) for p in parts)
        return (a + pltpu.roll(b, 8, 1) + pltpu.roll(c, 16, 1)).astype(BF16)

    proj_ref[:, 0:n_attn] = _dot(hb, w_ref[:, 0:n_attn])
    fl = _dot(hb, wf_ref[...])
    proj_ref[:, n_attn:wn] = _dot(hb, w_ref[:, n_attn:wn])
    logf = jnp.where(head_lane, _log_sigmoid(fl + bf_ref[...]), 0.0)
    cp = _dot(ltri_ref[...], pack3(_split3(logf)))

    o = 3 * sb_w
    q_tiles = [_head_rmsnorm(proj_ref[:, o + j * MXU_DIM:o + (j + 1) * MXU_DIM], gmat, qg_ref[...]) * scale
               for j in range(fox_w // MXU_DIM)]
    o += fox_w

    cum = cp + pltpu.roll(cp, LANES - 8, 1) + pltpu.roll(cp, LANES - 16, 1)
    cum = jnp.where(head_lane, cum, 0.0) + carry_ref[0:1, :]
    carry_ref[0:1, :] = cum[tm - 1:tm, :]
    f_ref[0] = cum.T[0:n_heads, :]
    extras = _dot(pack3(_split3(cum)), sel_ref[...])

    k_tiles = [_head_rmsnorm(proj_ref[:, o + j * MXU_DIM:o + (j + 1) * MXU_DIM], gmat, kg_ref[...])
               for j in range(fox_w // MXU_DIM)]
    o += fox_w

    qa_ref[0] = (proj_ref[:, 0:sb_w] * (scale * LOG2_E)).astype(BF16)
    ka_ref[0] = proj_ref[:, sb_w:2 * sb_w].astype(BF16)
    vat_ref[0] = proj_ref[:, 2 * sb_w:3 * sb_w].T.astype(BF16)

    is_head = lane < HEAD_DIM
    q_extra = qx_ref[...]
    k_extra = jnp.where((lane >= HEAD_DIM + 3) & (lane < HEAD_DIM + 7), 1.0, 0.0)
    heads_per_mxu = MXU_DIM // HEAD_DIM

    def head_tile(tn, s):
        half = tn[:, (s // 2) * LANES:(s // 2 + 1) * LANES]
        return half if s % 2 == 0 else pltpu.roll(half, HEAD_DIM, 1)

    for j in range(fox_w // MXU_DIM):
        for s in range(heads_per_mxu):
            hh = heads_per_mxu * j + s
            xq = extras[:, (n_heads + hh) * LANES:(n_heads + hh + 1) * LANES] + q_extra
            qf_ref[0, :, hh * LANES:(hh + 1) * LANES] = jnp.where(
                is_head, head_tile(q_tiles[j], s), xq).astype(BF16)
            xk = extras[:, hh * LANES:(hh + 1) * LANES] + k_extra
            kf_ref[0, :, hh * LANES:(hh + 1) * LANES] = jnp.where(
                is_head, head_tile(k_tiles[j], s), xk).astype(BF16)
    vt = proj_ref[:, o:o + fox_w].T.astype(BF16); o += fox_w
    ones = jnp.ones((V_ROWS - HEAD_DIM, tm), BF16)
    for hh in range(fox_w // HEAD_DIM):
        vft_ref[0, hh * V_ROWS:hh * V_ROWS + HEAD_DIM, :] = vt[hh * HEAD_DIM:(hh + 1) * HEAD_DIM, :]
        vft_ref[0, hh * V_ROWS + HEAD_DIM:(hh + 1) * V_ROWS, :] = ones

    gu = _gelu_tanh(proj_ref[:, o:o + sgu_w]); o += sgu_w
    gv = _gelu_tanh(proj_ref[:, o:o + sgu_w]); o += sgu_w
    lane_c = lax.broadcasted_iota(jnp.int32, (chunk, LANES), 1)
    rr = lax.broadcasted_iota(jnp.int32, (chunk, chunk), 0)
    cs = lax.broadcasted_iota(jnp.int32, (chunk, chunk), 1)
    wt = [jnp.where(rr >= cs, sw_ref[g], 0.0).astype(BF16) for g in range(sw_ref.shape[0])]
    vn_all = _head_rmsnorm(gv, gmat, sg_ref[...]).astype(BF16)
    for p in range(sgu_w // LANES):
        vn = vn_all[:, p * LANES:(p + 1) * LANES]
        for ci in range(tm // chunk):
            vblk = vn[ci * chunk:(ci + 1) * chunk, :]
            mixed = jnp.where(lane_c < HEAD_DIM, _dot(wt[2 * p], vblk), _dot(wt[2 * p + 1], vblk))
            mixed = mixed + sb_ref[:, p * LANES:(p + 1) * LANES]
            og_ref[0, ci * chunk:(ci + 1) * chunk, p * LANES:(p + 1) * LANES] = (
                gu[ci * chunk:(ci + 1) * chunk, p * LANES:(p + 1) * LANES] * mixed).astype(BF16)


def _in_proj(x, mod, g1, w, wf, bf, qx, qg, kg, sg, sw, sb, *, layer, tm, sb_w, fox_w, sgu_w):
    bsz, s, d = x.shape
    chunk = sw.shape[-1]
    wn = w.shape[2]
    fox_heads = fox_w // HEAD_DIM

    def of_layer(a):
        nd = a.ndim - 1
        return pl.BlockSpec((None,) + a.shape[1:], lambda b, i: (layer,) + (0,) * nd)

    assert fox_heads <= 8 and fox_w % MXU_DIM == 0 and sgu_w == MXU_DIM
    tok_idx = jnp.arange(tm)
    ltri = (tok_idx[:, None] >= tok_idx[None, :]).astype(BF16)
    const2 = lambda b, i: (0, 0)
    tok = lambda b, i: (b, i, 0)
    tok_t = lambda b, i: (b, 0, i)
    kern = functools.partial(_in_kernel, sb_w=sb_w, fox_w=fox_w, sgu_w=sgu_w, chunk=chunk)
    out_shape = [
        jax.ShapeDtypeStruct((bsz, s, sb_w), BF16),
        jax.ShapeDtypeStruct((bsz, s, sb_w), BF16),
        jax.ShapeDtypeStruct((bsz, sb_w, s), BF16),
        jax.ShapeDtypeStruct((bsz, s, fox_heads * LANES), BF16),
        jax.ShapeDtypeStruct((bsz, s, fox_heads * LANES), BF16),
        jax.ShapeDtypeStruct((bsz, fox_heads * V_ROWS, s), BF16),
        jax.ShapeDtypeStruct((bsz, s, sgu_w), BF16),
        jax.ShapeDtypeStruct((bsz, fox_heads, s), F32),
    ]
    out_specs = [
        pl.BlockSpec((1, tm, sb_w), tok),
        pl.BlockSpec((1, tm, sb_w), tok),
        pl.BlockSpec((1, sb_w, tm), tok_t),
        pl.BlockSpec((1, tm, fox_heads * LANES), tok),
        pl.BlockSpec((1, tm, fox_heads * LANES), tok),
        pl.BlockSpec((1, fox_heads * V_ROWS, tm), tok_t),
        pl.BlockSpec((1, tm, sgu_w), tok),
        pl.BlockSpec((1, fox_heads, tm), tok_t),
    ]
    return pl.pallas_call(
        kern,
        grid=(bsz, s // tm),
        in_specs=[
            pl.BlockSpec((1, tm, d), tok),
            pl.BlockSpec((None, 1, 6, d), lambda b, i: (layer, b, 0, 0)),
            of_layer(g1), of_layer(w), of_layer(wf), of_layer(bf), of_layer(qx), of_layer(qg),
            of_layer(kg), of_layer(sg), of_layer(sw), of_layer(sb),
            pl.BlockSpec((tm, tm), const2),
            pl.BlockSpec((LANES, 2 * fox_heads * LANES), const2),
            pl.BlockSpec((MXU_DIM, MXU_DIM), const2),
        ],
        out_specs=out_specs,
        out_shape=out_shape,
        scratch_shapes=[pltpu.VMEM((8, LANES), F32), pltpu.VMEM((tm, wn), F32)],
        compiler_params=pltpu.CompilerParams(
            dimension_semantics=("arbitrary", "arbitrary"), vmem_limit_bytes=VMEM_LIMIT),
        name="in_proj",
    )(x, mod, g1, w, wf, bf, qx, qg, kg, sg, sw, sb, ltri, _placement_matrix(fox_heads),
      _group_mean_matrix())


def _sb_kernel(q_ref, k_ref, vt_ref, after_ref, o_ref, acc_ref, c_ref, *, tq):
    tk = tq
    n_blk = q_ref.shape[1] // tq
    lane = lax.broadcasted_iota(jnp.int32, (1, LANES), 1)
    key = lax.broadcasted_iota(jnp.int32, (tk, tq), 0)
    qry = lax.broadcasted_iota(jnp.int32, (tk, tq), 1)
    causal = key < qry
    after = after_ref[...]

    def process(chains):
        units, starts, qh = [], {}, {}
        for slot, qi, blocks, first in chains:
            q = q_ref[0, pl.ds(_block_start(qi, tq), tq), :]
            qh[slot] = (jnp.where(lane < HEAD_DIM, q, 0).astype(BF16),
                        jnp.where(lane >= HEAD_DIM, q, 0).astype(BF16))
            for b, j in enumerate(blocks):
                starts[slot, b] = _block_start(j, tk)
                units += [(slot, b, h, first and b == 0) for h in range(2)]
        z = {(sl, b, h): _dot_nt(k_ref[0, pl.ds(starts[sl, b], tk), :], qh[sl][h])
             for sl, b, h, _ in units}
        l1mb, head = {}, {}
        for sl, b, h, diag in units:
            zz = z[sl, b, h]
            nz = -zz
            lg = jnp.minimum(nz, 0.0) - jnp.log2(1.0 + jnp.exp2(jnp.minimum(zz, nz)))
            if diag:
                lg = jnp.where(causal, lg, 0.0)
            l1mb[sl, b, h] = lg.astype(BF16)
            head[sl, b, h] = (zz + lg, lg[0:1, :])
        between = {(sl, b, h): _dot(after, l1mb[sl, b, h]) for sl, b, h, _ in units}
        a = {}
        for slot, qi, blocks, first in chains:
            for h in range(2):
                c = None if first else c_ref[slot, h]
                for b in range(len(blocks)):
                    e = head[slot, b, h][0] + between[slot, b, h]
                    if c is not None:
                        e = e + c
                    w = jnp.exp2(e)
                    if first and b == 0:
                        w = jnp.where(causal, w, 0.0)
                    a[slot, b, h] = w.astype(BF16)
                    block_sum = between[slot, b, h][0:1, :] + head[slot, b, h][1]
                    c = block_sum if c is None else c + block_sum
                c_ref[slot, h] = c
        for slot, qi, blocks, first in chains:
            for h in range(2):
                pv = None
                for b in range(len(blocks)):
                    part = _dot(vt_ref[0, h * HEAD_DIM:(h + 1) * HEAD_DIM, pl.ds(starts[slot, b], tk)],
                                a[slot, b, h])
                    pv = part if pv is None else pv + part
                if first:
                    acc_ref[slot, h] = pv
                else:
                    acc_ref[slot, h] += pv

    def carry_max(slots):
        rows = [c_ref[slot, h] for slot in slots for h in range(2)]
        cm = functools.reduce(jnp.maximum, rows)
        return jnp.max(cm, axis=1, keepdims=True)[0, 0]

    def finish(slot, qi, cmax):
        def cond(carry):
            j, cm = carry
            return jnp.logical_and(j >= 0, cm > EXP_ZERO * LOG2_E)

        def body(carry):
            j, _ = carry
            process([(slot, qi, [j], False)])
            return j - 1, carry_max([slot])

        lax.while_loop(cond, body, (jnp.asarray(qi - 2, jnp.int32), cmax))
        o_ref[0, pl.ds(_block_start(qi, tq), tq), :] = (
            jnp.concatenate([acc_ref[slot, 0], acc_ref[slot, 1]], axis=0).T.astype(o_ref.dtype))

    process([(0, 0, [0], True)])
    finish(0, 0, carry_max([0]))

    def two_blocks(g, carry):
        qa, qb = 1 + 2 * g, 2 + 2 * g
        process([(0, qa, [qa, qa - 1], True), (1, qb, [qb, qb - 1], True)])
        cmax = carry_max([0, 1])
        finish(0, qa, cmax)
        finish(1, qb, cmax)
        return carry

    lax.fori_loop(0, (n_blk - 1) // 2, two_blocks, 0)
    if (n_blk - 1) % 2:
        last = n_blk - 1
        process([(0, last, [last, last - 1], True)])
        finish(0, last, carry_max([0]))


def _sb_attention(q, k, vt, *, tq):
    bsz, s, w = q.shape
    assert s >= 2 * tq
    idx = jnp.arange(tq)
    after = (idx[None, :] > idx[:, None]).astype(BF16)
    kern = functools.partial(_sb_kernel, tq=tq)
    return pl.pallas_call(
        kern,
        grid=(bsz, w // LANES),
        in_specs=[
            pl.BlockSpec((1, s, LANES), lambda b, p: (b, 0, p)),
            pl.BlockSpec((1, s, LANES), lambda b, p: (b, 0, p)),
            pl.BlockSpec((1, LANES, s), lambda b, p: (b, p, 0)),
            pl.BlockSpec((tq, tq), lambda b, p: (0, 0)),
        ],
        out_specs=pl.BlockSpec((1, s, LANES), lambda b, p: (b, 0, p)),
        out_shape=jax.ShapeDtypeStruct((bsz, s, w), BF16),
        scratch_shapes=[pltpu.VMEM((2, 2, HEAD_DIM, tq), F32), pltpu.VMEM((2, 2, 1, tq), F32)],
        compiler_params=pltpu.CompilerParams(
            dimension_semantics=("arbitrary", "arbitrary"), vmem_limit_bytes=VMEM_LIMIT),
        name="sb_attn",
    )(q, k, vt, after)


def _fox_kernel(fend_ref, par_ref, q_ref, k_ref, vt_ref, o_ref, acc_ref, m_ref, p_ref, *, tq, n_heads):
    tk = tq
    n_blk = q_ref.shape[1] // tq
    head0 = (pl.program_id(0) * n_heads + 2 * pl.program_id(1))

    def last_dead_block(head, qi, j_prev):
        base = head * n_blk
        f_q = fend_ref[base + jnp.maximum(qi - 1, 0)]

        def next_is_dead(j):
            jn = jnp.minimum(j + 1, n_blk - 1)
            return jnp.logical_and(j + 1 < qi, f_q - fend_ref[base + jn] < -par_ref[0])

        return lax.while_loop(next_is_dead, lambda j: j + 1, j_prev)

    def plan(qi, dead):
        dead = (last_dead_block(head0, qi, dead[0]), last_dead_block(head0 + 1, qi, dead[1]))
        j_dead = jnp.minimum(dead[0], dead[1])
        n_left = jnp.maximum(qi - 2 - j_dead, 0)
        odd = n_left % 2
        has_dead_below = j_dead >= 0
        n_pairs = n_left // 2 + jnp.where(has_dead_below, odd, 0)
        return n_pairs, jnp.logical_and(odd == 1, jnp.logical_not(has_dead_below)), dead

    def q_tile(qi, h):
        return q_ref[0, pl.ds(pl.multiple_of(qi * tq, tq), tq), h * LANES:(h + 1) * LANES]

    def finalize(qi):
        outs = []
        for h in range(2):
            acc = acc_ref[h]
            outs.append(acc[0:HEAD_DIM, :] / acc[HEAD_DIM:HEAD_DIM + 1, :])
        o_ref[0, pl.ds(pl.multiple_of(qi * tq, tq), tq), :] = (
            jnp.concatenate(outs, axis=0).T.astype(o_ref.dtype))

    key = lax.broadcasted_iota(jnp.int32, (tk, tq), 0)
    qry = lax.broadcasted_iota(jnp.int32, (tk, tq), 1)
    causal = key <= qry

    def process(qi, blocks, first):
        starts = [pl.multiple_of(j * tk, tk) for j in blocks]
        units = [(b, h) for b in range(len(blocks)) for h in range(2)]
        s = {(b, h): _dot_nt(k_ref[0, pl.ds(starts[b], tk), h * LANES:(h + 1) * LANES],
                             q_tile(qi, h)) for b, h in units}
        p = {}
        alpha = {}
        for h in range(2):
            if first:
                s[0, h] = jnp.where(causal, s[0, h], NEG_BIG)
            m_new = None if first else m_ref[h]
            for b in range(len(blocks)):
                mb = jnp.max(s[b, h], axis=0, keepdims=True)
                m_new = mb if m_new is None else jnp.maximum(m_new, mb)
            if not first:
                alpha[h] = jnp.exp(m_ref[h] - m_new)
            for b in range(len(blocks)):
                p[b, h] = jnp.exp(s[b, h] - m_new).astype(BF16)
            m_ref[h] = m_new
        for h in range(2):
            pv = None
            for b in range(len(blocks)):
                part = _dot(vt_ref[0, h * V_ROWS:(h + 1) * V_ROWS, pl.ds(starts[b], tk)], p[b, h])
                pv = part if pv is None else pv + part
            acc_ref[h] = pv if first else alpha[h] * acc_ref[h] + pv

    def online_path():
        def q_block(qi, dead):
            n_pairs, lone_block0, dead = plan(qi, dead)

            @pl.when(qi == 0)
            def _():
                process(qi, [qi], True)

            @pl.when(qi > 0)
            def _():
                process(qi, [qi, qi - 1], True)

            def body(i, c):
                j = qi - 2 - 2 * i
                process(qi, [j, j - 1], False)
                return c

            lax.fori_loop(0, n_pairs, body, 0)

            @pl.when(lone_block0)
            def _():
                process(qi, [0], False)

            finalize(qi)
            return dead

        lax.fori_loop(0, n_blk, q_block, (jnp.int32(-1), jnp.int32(-1)))

    win = 2 * tk
    wrow = lax.broadcasted_iota(jnp.int32, (win, tq), 0)
    wcol = lax.broadcasted_iota(jnp.int32, (win, tq), 1)

    def window(qi, i):
        jl = qi - 1 - 2 * i
        j_lo = jnp.maximum(jl, 0)
        return j_lo, pl.multiple_of(j_lo * tk, tk), jnp.where(jl < 0, tk, win)

    def stage_a(qi, i, slot, first):
        j_lo, start, row_lim = window(qi, i)
        keep = wrow < row_lim
        if first:
            keep = jnp.logical_and(keep, wrow <= wcol + (qi - j_lo) * tk)
        for h in range(2):
            s = _dot_nt(k_ref[0, pl.ds(start, win), h * LANES:(h + 1) * LANES], q_tile(qi, h))
            p_ref[slot, h] = jnp.exp(jnp.where(keep, s, NEG_BIG)).astype(BF16)

    def stage_c(qi, i, slot):
        _, start, _ = window(qi, i)
        for h in range(2):
            acc_ref[h] += _dot(vt_ref[0, h * V_ROWS:(h + 1) * V_ROWS, pl.ds(start, win)], p_ref[slot, h])

    def bounded_path():
        acc_ref[...] = jnp.zeros_like(acc_ref)
        stage_a(0, 0, 0, True)

        def q_block(qi, carry):
            t, dead = carry[0], carry[1:]
            n_pairs, lone_block0, dead = plan(qi, dead)
            n_stages = 1 + n_pairs + jnp.where(lone_block0, 1, 0)

            def body(i, t):
                stage_c(qi, i - 1, t & 1)
                stage_a(qi, i, (t + 1) & 1, False)
                return t + 1

            t = lax.fori_loop(1, n_stages, body, t)
            stage_c(qi, n_stages - 1, t & 1)
            stage_a(jnp.minimum(qi + 1, n_blk - 1), 0, (t + 1) & 1, True)
            finalize(qi)
            acc_ref[...] = jnp.zeros_like(acc_ref)
            return (t + 1,) + dead

        lax.fori_loop(0, n_blk, q_block, (jnp.int32(0), jnp.int32(-1), jnp.int32(-1)))

    bounded = par_ref[1] > 0.5
    pl.when(bounded)(bounded_path)
    pl.when(jnp.logical_not(bounded))(online_path)


def _fox_attention(q, k, vt, f, qk_bound, *, tq):
    bsz, s, w = q.shape
    pair = 2 * LANES
    n_heads = w // LANES
    assert s >= 2 * tq
    f_end = f[:, :, tq - 1::tq].reshape(-1)
    bounded = qk_bound <= MAX_UNSTABILISED_LOGIT
    par = jnp.stack([jnp.where(bounded, 0.5 - EXP_ZERO, 2.0 * qk_bound - EXP_ZERO),
                     jnp.where(bounded, 1.0, 0.0)]).astype(F32)
    kern = functools.partial(_fox_kernel, tq=tq, n_heads=n_heads)
    grid_spec = pltpu.PrefetchScalarGridSpec(
        num_scalar_prefetch=2,
        grid=(bsz, w // pair),
        in_specs=[
            pl.BlockSpec((1, s, pair), lambda b, p, fe, th: (b, 0, p)),
            pl.BlockSpec((1, s, pair), lambda b, p, fe, th: (b, 0, p)),
            pl.BlockSpec((1, 2 * V_ROWS, s), lambda b, p, fe, th: (b, p, 0)),
        ],
        out_specs=pl.BlockSpec((1, s, LANES), lambda b, p, fe, th: (b, 0, p)),
        scratch_shapes=[pltpu.VMEM((2, V_ROWS, tq), F32), pltpu.VMEM((2, 1, tq), F32),
                        pltpu.VMEM((2, 2, 2 * tq, tq), BF16)],
    )
    return pl.pallas_call(
        kern,
        grid_spec=grid_spec,
        out_shape=jax.ShapeDtypeStruct((bsz, s, w // 2), BF16),
        compiler_params=pltpu.CompilerParams(
            dimension_semantics=("arbitrary", "arbitrary"), vmem_limit_bytes=VMEM_LIMIT),
        name="fox_attn",
    )(f_end, par, q, k, vt)


def _out_kernel(x_ref, osb_ref, ofox_ref, osgu_ref, mod_ref, g2_ref, wo_ref, w1_ref, w2_ref,
                o_ref, *, ff_chunk):
    sb_w = osb_ref.shape[2]
    fox_w = ofox_ref.shape[2]
    x = x_ref[0]
    mix = (_dot(osb_ref[0], wo_ref[0:sb_w, :])
           + _dot(ofox_ref[0], wo_ref[sb_w:sb_w + fox_w, :])
           + _dot(osgu_ref[0], wo_ref[sb_w + fox_w:, :]))
    x1 = x + mod_ref[0, 2:3, :] * mix
    ms = jnp.mean(x1 * x1, axis=-1, keepdims=True)
    h = x1 * lax.rsqrt(ms + EPS) * g2_ref[...]
    hb = (h * (1.0 + mod_ref[0, 4:5, :]) + mod_ref[0, 3:4, :]).astype(BF16)
    d_ff = w1_ref.shape[1]
    acc = None
    for c in range(d_ff // ff_chunk):
        hid = jnp.maximum(_dot(hb, w1_ref[:, c * ff_chunk:(c + 1) * ff_chunk]), 0.0)
        part = _dot((hid * hid).astype(BF16), w2_ref[c * ff_chunk:(c + 1) * ff_chunk, :])
        acc = part if acc is None else acc + part
    o_ref[0] = x1 + mod_ref[0, 5:6, :] * acc


def _out_mlp(x, osb, ofox, osgu, mod, g2, wo, w1, w2, *, layer, tm):
    bsz, s, d = x.shape
    tok = lambda b, i: (b, i, 0)

    def of_layer(a, **kw):
        nd = a.ndim - 1
        return pl.BlockSpec((None,) + a.shape[1:], lambda b, i: (layer,) + (0,) * nd, **kw)

    single = pl.Buffered(1)
    kern = functools.partial(_out_kernel, ff_chunk=1024)
    return pl.pallas_call(
        kern,
        grid=(bsz, s // tm),
        in_specs=[
            pl.BlockSpec((1, tm, d), tok),
            pl.BlockSpec((1, tm, osb.shape[2]), tok),
            pl.BlockSpec((1, tm, ofox.shape[2]), tok),
            pl.BlockSpec((1, tm, osgu.shape[2]), tok),
            pl.BlockSpec((None, 1, 6, d), lambda b, i: (layer, b, 0, 0)),
            of_layer(g2),
            of_layer(wo, pipeline_mode=single),
            of_layer(w1, pipeline_mode=single),
            of_layer(w2, pipeline_mode=single),
        ],
        out_specs=pl.BlockSpec((1, tm, d), tok),
        out_shape=jax.ShapeDtypeStruct((bsz, s, d), F32),
        compiler_params=pltpu.CompilerParams(
            dimension_semantics=("arbitrary", "arbitrary"), vmem_limit_bytes=VMEM_LIMIT),
        name="out_mlp",
    )(x, osb, ofox, osgu, mod, g2, wo, w1, w2)


def kernel(x, c, ada_w, ada_b, norm1_g, norm2_g, w_in, b_forget, q_norm_g, k_norm_g, sgu_norm_g,
           sgu_w, sgu_b, w_out, mlp_w1, mlp_w2):
    depth, d, _ = ada_w.shape
    bsz, s, _ = x.shape
    fox_heads = b_forget.shape[1]
    fox_w = fox_heads * HEAD_DIM
    sgu_groups, chunk = sgu_b.shape[1], sgu_b.shape[2]
    sgu_wd = sgu_groups * sgu_norm_g.shape[2]
    sb_w = (w_in.shape[2] - 3 * fox_w - fox_heads - 2 * sgu_wd) // 3
    f_lo = 3 * sb_w + 3 * fox_w

    mod = _modulation(c, ada_w, ada_b).reshape(depth, bsz, 6, d)

    perm = jnp.argsort(b_forget, axis=1)

    def relabel_cols(cols):
        t = cols.reshape(depth, d, fox_heads, HEAD_DIM)
        return jnp.take_along_axis(t, perm[:, None, :, None], axis=2).reshape(depth, d, fox_w)

    o = 3 * sb_w
    w_bf = w_in.astype(BF16)
    fox_cols = [relabel_cols(w_bf[:, :, o + i * fox_w:o + (i + 1) * fox_w]) for i in range(3)]
    w = jnp.concatenate([w_bf[:, :, :o]] + fox_cols + [w_bf[:, :, f_lo + fox_heads:]], axis=2)
    wf = jnp.take_along_axis(w_in[:, :, f_lo:f_lo + fox_heads], perm[:, None, :], axis=2)
    wf = jnp.pad(wf, ((0, 0), (0, 0), (0, LANES - fox_heads))).astype(BF16)
    bf = jnp.pad(jnp.take_along_axis(b_forget, perm, axis=1), ((0, 0), (0, LANES - fox_heads)))
    bf = bf.reshape(depth, 1, LANES)
    qg = jnp.tile(q_norm_g, (1, MXU_DIM // HEAD_DIM)).reshape(depth, 1, MXU_DIM)
    kg = jnp.tile(k_norm_g, (1, MXU_DIM // HEAD_DIM)).reshape(depth, 1, MXU_DIM)
    sg = sgu_norm_g.reshape(depth, 1, sgu_wd)
    sb = jnp.repeat(jnp.swapaxes(sgu_b, 1, 2), sgu_norm_g.shape[2], axis=2)
    qk_bound = ((1.03 * HEAD_DIM ** 0.5) * jnp.max(jnp.abs(q_norm_g), axis=1)
                * jnp.max(jnp.abs(k_norm_g), axis=1))
    qx = jnp.zeros((depth, 1, LANES), F32).at[:, 0, HEAD_DIM:HEAD_DIM + 3].set(-1.0)
    qx = qx.at[:, 0, HEAD_DIM + 6].set(-qk_bound)
    wo_fox = w_out[:, sb_w:sb_w + fox_w].reshape(depth, fox_heads, HEAD_DIM, d)
    wo_fox = jnp.take_along_axis(wo_fox, perm[:, :, None, None], axis=1).reshape(depth, fox_w, d)
    wo = jnp.concatenate([w_out[:, :sb_w], wo_fox, w_out[:, sb_w + fox_w:]], axis=1).astype(BF16)
    w1 = mlp_w1.astype(BF16)
    w2 = mlp_w2.astype(BF16)
    g1 = norm1_g.reshape(depth, 1, d)
    g2 = norm2_g.reshape(depth, 1, d)

    for l in range(depth):
        qa, ka, vat, qf, kf, vft, osgu, f = _in_proj(
            x, mod, g1, w, wf, bf, qx, qg, kg, sg, sgu_w, sb,
            layer=l, tm=512, sb_w=sb_w, fox_w=fox_w, sgu_w=sgu_wd)
        osb = _sb_attention(qa, ka, vat, tq=256)
        ofox = _fox_attention(qf, kf, vft, f, qk_bound[l], tq=256)
        x = _out_mlp(x, osb, ofox, osgu, mod, g2, wo, w1, w2, layer=l, tm=512)
    return x
```

```python
import functools
import math

import jax
import jax.numpy as jnp
from jax import lax
from jax.experimental import pallas as pl
from jax.experimental.pallas import tpu as pltpu

HEAD_DIM = 64
LANES = 128
MXU_DIM = 256
V_ROWS = 2 * HEAD_DIM
EPS = 1e-6
NEG_BIG = -1e30
EXP_ZERO = -104.0
LOG2_E = math.log2(math.e)
MAX_UNSTABILISED_LOGIT = 40.0
VMEM_LIMIT = 56 * 1024 * 1024

F32 = jnp.float32
BF16 = jnp.bfloat16


def _dot(a, b):
    return jnp.dot(a, b, preferred_element_type=F32)


def _dot_nt(a, b):
    return lax.dot_general(a, b, (((1,), (1,)), ((), ())), preferred_element_type=F32)


def _block_start(j, size):
    return j * size if isinstance(j, int) else pl.multiple_of(j * size, size)


def _split3(x):
    hi = x.astype(BF16)
    r = x - hi.astype(F32)
    mid = r.astype(BF16)
    lo = (r - mid.astype(F32)).astype(BF16)
    return hi, mid, lo


def _group_mean_matrix():
    head = jnp.arange(MXU_DIM) // HEAD_DIM
    return jnp.where(head[:, None] == head[None, :], 1.0 / HEAD_DIM, 0.0).astype(BF16)


def _placement_matrix(n_heads):
    row = jnp.arange(LANES)[:, None]
    col = jnp.arange(2 * n_heads * LANES)[None, :]
    tile = col // LANES
    is_q = (tile >= n_heads).astype(jnp.int32)
    part = col % LANES - HEAD_DIM - 3 * is_q
    hit = (part >= 0) & (part < 3) & (row == part * 8 + tile - n_heads * is_q)
    return hit.astype(BF16)


def _head_rmsnorm(t, gmat, gain):
    ms = _dot((t * t).astype(BF16), gmat)
    return t * lax.rsqrt(ms + EPS) * gain


def _gelu_tanh(x):
    c = math.sqrt(2.0 / math.pi)
    return x * (0.5 * (1.0 + jnp.tanh(c * (x + 0.044715 * (x * x * x)))))


def _log_sigmoid(x):
    return jnp.minimum(x, 0.0) - jnp.log(1.0 + jnp.exp(-jnp.abs(x)))


def _mod_kernel(ct_ref, w_ref, b_ref, o_ref):
    ct = ct_ref[...]
    cond = ct * (1.0 / (1.0 + jnp.exp(-ct)))
    w = w_ref[0]
    rows = [jnp.sum(cond[:, b:b + 1] * w, axis=0, keepdims=True) for b in range(ct.shape[1])]
    o_ref[0] = jnp.concatenate(rows, axis=0) + b_ref[0]


def _modulation(c, ada_w, ada_b):
    depth, d, n = ada_w.shape
    bsz = c.shape[0]
    tn = 1536
    return pl.pallas_call(
        _mod_kernel,
        grid=(depth, n // tn),
        in_specs=[
            pl.BlockSpec((d, bsz), lambda l, j: (0, 0)),
            pl.BlockSpec((1, d, tn), lambda l, j: (l, 0, j)),
            pl.BlockSpec((1, 1, tn), lambda l, j: (l, 0, j)),
        ],
        out_specs=pl.BlockSpec((1, bsz, tn), lambda l, j: (l, 0, j)),
        out_shape=jax.ShapeDtypeStruct((depth, bsz, n), F32),
        compiler_params=pltpu.CompilerParams(
            dimension_semantics=("arbitrary", "arbitrary"), vmem_limit_bytes=VMEM_LIMIT),
        name="adaln_mod",
    )(c.T, ada_w, ada_b.reshape(depth, 1, n))


def _in_kernel(x_ref, mod_ref, g1_ref, w_ref, wf_ref, bf_ref, qx_ref, qg_ref, kg_ref,
               sg_ref, sw_ref, sb_ref, ltri_ref, sel_ref, gmat_ref,
               qa_ref, ka_ref, vat_ref, qf_ref, kf_ref, vft_ref, og_ref, f_ref,
               carry_ref, proj_ref, *, sb_w, fox_w, sgu_w, chunk):
    @pl.when(pl.program_id(1) == 0)
    def _():
        carry_ref[...] = jnp.zeros_like(carry_ref)

    tm = x_ref.shape[1]
    x = x_ref[0]
    ms = jnp.mean(x * x, axis=-1, keepdims=True)
    h = x * lax.rsqrt(ms + EPS) * g1_ref[...]
    h = h * (1.0 + mod_ref[0, 1:2, :]) + mod_ref[0, 0:1, :]
    hb = h.astype(BF16)
    scale = HEAD_DIM ** -0.5
    n_attn = 3 * sb_w + 3 * fox_w
    wn = w_ref.shape[1]
    n_heads = f_ref.shape[1]
    lane = lax.broadcasted_iota(jnp.int32, (1, LANES), 1)
    head_lane = lane < n_heads
    gmat = gmat_ref[...]

    def pack3(parts):
        a, b, c = (p.astype(F32) for p in parts)
        return (a + pltpu.roll(b, 8, 1) + pltpu.roll(c, 16, 1)).astype(BF16)

    def project(lo, hi):
        proj_ref[:, lo:hi] = _dot(hb, w_ref[:, lo:hi])

    o_q, o_k, o_v = 3 * sb_w, 3 * sb_w + fox_w, 3 * sb_w + 2 * fox_w
    project(n_attn, wn)
    fl = _dot(hb, wf_ref[...])
    project(o_q, o_k)
    gu = _gelu_tanh(proj_ref[:, n_attn:n_attn + sgu_w])
    gv = _gelu_tanh(proj_ref[:, n_attn + sgu_w:wn])
    project(o_k, o_v)
    logf = jnp.where(head_lane, _log_sigmoid(fl + bf_ref[...]), 0.0)
    cp = _dot(ltri_ref[...], pack3(_split3(logf)))
    vn_all = _head_rmsnorm(gv, gmat, sg_ref[...]).astype(BF16)
    q_tiles = [_head_rmsnorm(proj_ref[:, o_q + j * MXU_DIM:o_q + (j + 1) * MXU_DIM], gmat, qg_ref[...]) * scale
               for j in range(fox_w // MXU_DIM)]
    project(o_v, n_attn)

    cum = cp + pltpu.roll(cp, LANES - 8, 1) + pltpu.roll(cp, LANES - 16, 1)
    cum = jnp.where(head_lane, cum, 0.0) + carry_ref[0:1, :]
    carry_ref[0:1, :] = cum[tm - 1:tm, :]
    f_ref[0] = cum.T[0:n_heads, :]
    extras = _dot(pack3(_split3(cum)), sel_ref[...])
    k_tiles = [_head_rmsnorm(proj_ref[:, o_k + j * MXU_DIM:o_k + (j + 1) * MXU_DIM], gmat, kg_ref[...])
               for j in range(fox_w // MXU_DIM)]

    lane_c = lax.broadcasted_iota(jnp.int32, (chunk, LANES), 1)
    rr = lax.broadcasted_iota(jnp.int32, (chunk, chunk), 0)
    cs = lax.broadcasted_iota(jnp.int32, (chunk, chunk), 1)
    wt = [jnp.where(rr >= cs, sw_ref[g], 0.0).astype(BF16) for g in range(sw_ref.shape[0])]
    for p in range(sgu_w // LANES):
        vn = vn_all[:, p * LANES:(p + 1) * LANES]
        for ci in range(tm // chunk):
            vblk = vn[ci * chunk:(ci + 1) * chunk, :]
            mixed = jnp.where(lane_c < HEAD_DIM, _dot(wt[2 * p], vblk), _dot(wt[2 * p + 1], vblk))
            mixed = mixed + sb_ref[:, p * LANES:(p + 1) * LANES]
            og_ref[0, ci * chunk:(ci + 1) * chunk, p * LANES:(p + 1) * LANES] = (
                gu[ci * chunk:(ci + 1) * chunk, p * LANES:(p + 1) * LANES] * mixed).astype(BF16)

    project(0, o_q)
    qa_ref[0] = (proj_ref[:, 0:sb_w] * (scale * LOG2_E)).astype(BF16)
    ka_ref[0] = proj_ref[:, sb_w:2 * sb_w].astype(BF16)
    vat_ref[0] = proj_ref[:, 2 * sb_w:3 * sb_w].T.astype(BF16)

    is_head = lane < HEAD_DIM
    q_extra = qx_ref[...]
    k_extra = jnp.where((lane >= HEAD_DIM + 3) & (lane < HEAD_DIM + 7), 1.0, 0.0)
    heads_per_mxu = MXU_DIM // HEAD_DIM

    def head_tile(tn, s):
        half = tn[:, (s // 2) * LANES:(s // 2 + 1) * LANES]
        return half if s % 2 == 0 else pltpu.roll(half, HEAD_DIM, 1)

    for j in range(fox_w // MXU_DIM):
        for s in range(heads_per_mxu):
            hh = heads_per_mxu * j + s
            xq = extras[:, (n_heads + hh) * LANES:(n_heads + hh + 1) * LANES] + q_extra
            qf_ref[0, :, hh * LANES:(hh + 1) * LANES] = jnp.where(
                is_head, head_tile(q_tiles[j], s), xq).astype(BF16)
            xk = extras[:, hh * LANES:(hh + 1) * LANES] + k_extra
            kf_ref[0, :, hh * LANES:(hh + 1) * LANES] = jnp.where(
                is_head, head_tile(k_tiles[j], s), xk).astype(BF16)
    vt = proj_ref[:, o_v:n_attn].T.astype(BF16)
    ones = jnp.ones((V_ROWS - HEAD_DIM, tm), BF16)
    for hh in range(fox_w // HEAD_DIM):
        vft_ref[0, hh * V_ROWS:hh * V_ROWS + HEAD_DIM, :] = vt[hh * HEAD_DIM:(hh + 1) * HEAD_DIM, :]
        vft_ref[0, hh * V_ROWS + HEAD_DIM:(hh + 1) * V_ROWS, :] = ones


def _in_proj(x, mod, g1, w, wf, bf, qx, qg, kg, sg, sw, sb, *, layer, tm, sb_w, fox_w, sgu_w):
    bsz, s, d = x.shape
    chunk = sw.shape[-1]
    wn = w.shape[2]
    fox_heads = fox_w // HEAD_DIM

    def of_layer(a):
        nd = a.ndim - 1
        return pl.BlockSpec((None,) + a.shape[1:], lambda b, i: (layer,) + (0,) * nd)

    assert fox_heads <= 8 and fox_w % MXU_DIM == 0 and sgu_w == MXU_DIM
    tok_idx = jnp.arange(tm)
    ltri = (tok_idx[:, None] >= tok_idx[None, :]).astype(BF16)
    const2 = lambda b, i: (0, 0)
    tok = lambda b, i: (b, i, 0)
    tok_t = lambda b, i: (b, 0, i)
    kern = functools.partial(_in_kernel, sb_w=sb_w, fox_w=fox_w, sgu_w=sgu_w, chunk=chunk)
    out_shape = [
        jax.ShapeDtypeStruct((bsz, s, sb_w), BF16),
        jax.ShapeDtypeStruct((bsz, s, sb_w), BF16),
        jax.ShapeDtypeStruct((bsz, sb_w, s), BF16),
        jax.ShapeDtypeStruct((bsz, s, fox_heads * LANES), BF16),
        jax.ShapeDtypeStruct((bsz, s, fox_heads * LANES), BF16),
        jax.ShapeDtypeStruct((bsz, fox_heads * V_ROWS, s), BF16),
        jax.ShapeDtypeStruct((bsz, s, sgu_w), BF16),
        jax.ShapeDtypeStruct((bsz, fox_heads, s), F32),
    ]
    out_specs = [
        pl.BlockSpec((1, tm, sb_w), tok),
        pl.BlockSpec((1, tm, sb_w), tok),
        pl.BlockSpec((1, sb_w, tm), tok_t),
        pl.BlockSpec((1, tm, fox_heads * LANES), tok),
        pl.BlockSpec((1, tm, fox_heads * LANES), tok),
        pl.BlockSpec((1, fox_heads * V_ROWS, tm), tok_t),
        pl.BlockSpec((1, tm, sgu_w), tok),
        pl.BlockSpec((1, fox_heads, tm), tok_t),
    ]
    return pl.pallas_call(
        kern,
        grid=(bsz, s // tm),
        in_specs=[
            pl.BlockSpec((1, tm, d), tok),
            pl.BlockSpec((None, 1, 6, d), lambda b, i: (layer, b, 0, 0)),
            of_layer(g1), of_layer(w), of_layer(wf), of_layer(bf), of_layer(qx), of_layer(qg),
            of_layer(kg), of_layer(sg), of_layer(sw), of_layer(sb),
            pl.BlockSpec((tm, tm), const2),
            pl.BlockSpec((LANES, 2 * fox_heads * LANES), const2),
            pl.BlockSpec((MXU_DIM, MXU_DIM), const2),
        ],
        out_specs=out_specs,
        out_shape=out_shape,
        scratch_shapes=[pltpu.VMEM((8, LANES), F32), pltpu.VMEM((tm, wn), F32)],
        compiler_params=pltpu.CompilerParams(
            dimension_semantics=("arbitrary", "arbitrary"), vmem_limit_bytes=VMEM_LIMIT),
        name="in_proj",
    )(x, mod, g1, w, wf, bf, qx, qg, kg, sg, sw, sb, ltri, _placement_matrix(fox_heads),
      _group_mean_matrix())


def _sb_kernel(q_ref, k_ref, vt_ref, after_ref, o_ref, acc_ref, c_ref, *, tq):
    tk = tq
    n_blk = q_ref.shape[1] // tq
    lane = lax.broadcasted_iota(jnp.int32, (1, LANES), 1)
    key = lax.broadcasted_iota(jnp.int32, (tk, tq), 0)
    qry = lax.broadcasted_iota(jnp.int32, (tk, tq), 1)
    causal = key < qry
    after = after_ref[...]

    def process(chains):
        units, starts, qh = [], {}, {}
        for slot, qi, blocks, first in chains:
            q = q_ref[0, pl.ds(_block_start(qi, tq), tq), :]
            qh[slot] = (jnp.where(lane < HEAD_DIM, q, 0).astype(BF16),
                        jnp.where(lane >= HEAD_DIM, q, 0).astype(BF16))
            for b, j in enumerate(blocks):
                starts[slot, b] = _block_start(j, tk)
                units += [(slot, b, h, first and b == 0) for h in range(2)]
        z = {(sl, b, h): _dot_nt(k_ref[0, pl.ds(starts[sl, b], tk), :], qh[sl][h])
             for sl, b, h, _ in units}
        l1mb, head = {}, {}
        for sl, b, h, diag in units:
            zz = z[sl, b, h]
            nz = -zz
            lg = jnp.minimum(nz, 0.0) - jnp.log2(1.0 + jnp.exp2(jnp.minimum(zz, nz)))
            if diag:
                lg = jnp.where(causal, lg, 0.0)
            l1mb[sl, b, h] = lg.astype(BF16)
            head[sl, b, h] = (zz + lg, lg[0:1, :])
        between = {(sl, b, h): _dot(after, l1mb[sl, b, h]) for sl, b, h, _ in units}
        a = {}
        for slot, qi, blocks, first in chains:
            for h in range(2):
                c = None if first else c_ref[slot, h]
                for b in range(len(blocks)):
                    e = head[slot, b, h][0] + between[slot, b, h]
                    if c is not None:
                        e = e + c
                    w = jnp.exp2(e)
                    if first and b == 0:
                        w = jnp.where(causal, w, 0.0)
                    a[slot, b, h] = w.astype(BF16)
                    block_sum = between[slot, b, h][0:1, :] + head[slot, b, h][1]
                    c = block_sum if c is None else c + block_sum
                c_ref[slot, h] = c
        for slot, qi, blocks, first in chains:
            for h in range(2):
                pv = None
                for b in range(len(blocks)):
                    part = _dot(vt_ref[0, h * HEAD_DIM:(h + 1) * HEAD_DIM, pl.ds(starts[slot, b], tk)],
                                a[slot, b, h])
                    pv = part if pv is None else pv + part
                if first:
                    acc_ref[slot, h] = pv
                else:
                    acc_ref[slot, h] += pv

    def carry_max(slot):
        cm = jnp.maximum(c_ref[slot, 0], c_ref[slot, 1])
        return jnp.max(cm, axis=1, keepdims=True)[0, 0]

    def finish(slot, qi):
        def cond(carry):
            j, cmax = carry
            return jnp.logical_and(j >= 0, cmax > EXP_ZERO * LOG2_E)

        def body(carry):
            j, _ = carry
            process([(slot, qi, [j], False)])
            return j - 1, carry_max(slot)

        lax.while_loop(cond, body, (jnp.asarray(qi - 2, jnp.int32), carry_max(slot)))
        o_ref[0, pl.ds(_block_start(qi, tq), tq), :] = (
            jnp.concatenate([acc_ref[slot, 0], acc_ref[slot, 1]], axis=0).T.astype(o_ref.dtype))

    process([(0, 0, [0], True)])
    finish(0, 0)

    def two_blocks(g, carry):
        qa, qb = 1 + 2 * g, 2 + 2 * g
        process([(0, qa, [qa, qa - 1], True), (1, qb, [qb, qb - 1], True)])
        finish(0, qa)
        finish(1, qb)
        return carry

    lax.fori_loop(0, (n_blk - 1) // 2, two_blocks, 0)
    if (n_blk - 1) % 2:
        last = n_blk - 1
        process([(0, last, [last, last - 1], True)])
        finish(0, last)


def _sb_attention(q, k, vt, *, tq):
    bsz, s, w = q.shape
    assert s >= 2 * tq
    idx = jnp.arange(tq)
    after = (idx[None, :] > idx[:, None]).astype(BF16)
    kern = functools.partial(_sb_kernel, tq=tq)
    return pl.pallas_call(
        kern,
        grid=(bsz, w // LANES),
        in_specs=[
            pl.BlockSpec((1, s, LANES), lambda b, p: (b, 0, p)),
            pl.BlockSpec((1, s, LANES), lambda b, p: (b, 0, p)),
            pl.BlockSpec((1, LANES, s), lambda b, p: (b, p, 0)),
            pl.BlockSpec((tq, tq), lambda b, p: (0, 0)),
        ],
        out_specs=pl.BlockSpec((1, s, LANES), lambda b, p: (b, 0, p)),
        out_shape=jax.ShapeDtypeStruct((bsz, s, w), BF16),
        scratch_shapes=[pltpu.VMEM((2, 2, HEAD_DIM, tq), F32), pltpu.VMEM((2, 2, 1, tq), F32)],
        compiler_params=pltpu.CompilerParams(
            dimension_semantics=("arbitrary", "arbitrary"), vmem_limit_bytes=VMEM_LIMIT),
        name="sb_attn",
    )(q, k, vt, after)


def _fox_kernel(fend_ref, par_ref, q_ref, k_ref, vt_ref, o_ref, acc_ref, m_ref, p_ref, *, tq, n_heads):
    tk = tq
    n_blk = q_ref.shape[1] // tq
    head0 = (pl.program_id(0) * n_heads + 2 * pl.program_id(1))

    def last_dead_block(head, qi, j_prev):
        base = head * n_blk
        f_q = fend_ref[base + jnp.maximum(qi - 1, 0)]

        def next_is_dead(j):
            jn = jnp.minimum(j + 1, n_blk - 1)
            return jnp.logical_and(j + 1 < qi, f_q - fend_ref[base + jn] < -par_ref[0])

        return lax.while_loop(next_is_dead, lambda j: j + 1, j_prev)

    def plan(qi, dead):
        dead = (last_dead_block(head0, qi, dead[0]), last_dead_block(head0 + 1, qi, dead[1]))
        j_dead = jnp.minimum(dead[0], dead[1])
        n_left = jnp.maximum(qi - 2 - j_dead, 0)
        odd = n_left % 2
        has_dead_below = j_dead >= 0
        n_pairs = n_left // 2 + jnp.where(has_dead_below, odd, 0)
        return n_pairs, jnp.logical_and(odd == 1, jnp.logical_not(has_dead_below)), dead

    def q_tile(qi, h):
        return q_ref[0, pl.ds(pl.multiple_of(qi * tq, tq), tq), h * LANES:(h + 1) * LANES]

    def finalize(qi):
        outs = []
        for h in range(2):
            acc = acc_ref[h]
            outs.append(acc[0:HEAD_DIM, :] / acc[HEAD_DIM:HEAD_DIM + 1, :])
        o_ref[0, pl.ds(pl.multiple_of(qi * tq, tq), tq), :] = (
            jnp.concatenate(outs, axis=0).T.astype(o_ref.dtype))

    key = lax.broadcasted_iota(jnp.int32, (tk, tq), 0)
    qry = lax.broadcasted_iota(jnp.int32, (tk, tq), 1)
    causal = key <= qry

    def process(qi, blocks, first):
        starts = [pl.multiple_of(j * tk, tk) for j in blocks]
        units = [(b, h) for b in range(len(blocks)) for h in range(2)]
        s = {(b, h): _dot_nt(k_ref[0, pl.ds(starts[b], tk), h * LANES:(h + 1) * LANES],
                             q_tile(qi, h)) for b, h in units}
        p = {}
        alpha = {}
        for h in range(2):
            if first:
                s[0, h] = jnp.where(causal, s[0, h], NEG_BIG)
            m_new = None if first else m_ref[h]
            for b in range(len(blocks)):
                mb = jnp.max(s[b, h], axis=0, keepdims=True)
                m_new = mb if m_new is None else jnp.maximum(m_new, mb)
            if not first:
                alpha[h] = jnp.exp(m_ref[h] - m_new)
            for b in range(len(blocks)):
                p[b, h] = jnp.exp(s[b, h] - m_new).astype(BF16)
            m_ref[h] = m_new
        for h in range(2):
            pv = None
            for b in range(len(blocks)):
                part = _dot(vt_ref[0, h * V_ROWS:(h + 1) * V_ROWS, pl.ds(starts[b], tk)], p[b, h])
                pv = part if pv is None else pv + part
            acc_ref[h] = pv if first else alpha[h] * acc_ref[h] + pv

    def online_path():
        def q_block(qi, dead):
            n_pairs, lone_block0, dead = plan(qi, dead)

            @pl.when(qi == 0)
            def _():
                process(qi, [qi], True)

            @pl.when(qi > 0)
            def _():
                process(qi, [qi, qi - 1], True)

            def body(i, c):
                j = qi - 2 - 2 * i
                process(qi, [j, j - 1], False)
                return c

            lax.fori_loop(0, n_pairs, body, 0)

            @pl.when(lone_block0)
            def _():
                process(qi, [0], False)

            finalize(qi)
            return dead

        lax.fori_loop(0, n_blk, q_block, (jnp.int32(-1), jnp.int32(-1)))

    win = 2 * tk
    wrow = lax.broadcasted_iota(jnp.int32, (win, tq), 0)
    wcol = lax.broadcasted_iota(jnp.int32, (win, tq), 1)

    def window(qi, i):
        jl = qi - 1 - 2 * i
        j_lo = jnp.maximum(jl, 0)
        return j_lo, pl.multiple_of(j_lo * tk, tk), jnp.where(jl < 0, tk, win)

    def stage_a(qi, i, slot, first):
        j_lo, start, row_lim = window(qi, i)
        keep = wrow < row_lim
        if first:
            keep = jnp.logical_and(keep, wrow <= wcol + (qi - j_lo) * tk)
        for h in range(2):
            s = _dot_nt(k_ref[0, pl.ds(start, win), h * LANES:(h + 1) * LANES], q_tile(qi, h))
            p_ref[slot, h] = jnp.exp(jnp.where(keep, s, NEG_BIG)).astype(BF16)

    def stage_c(qi, i, slot):
        _, start, _ = window(qi, i)
        for h in range(2):
            acc_ref[h] += _dot(vt_ref[0, h * V_ROWS:(h + 1) * V_ROWS, pl.ds(start, win)], p_ref[slot, h])

    def bounded_path():
        acc_ref[...] = jnp.zeros_like(acc_ref)
        stage_a(0, 0, 0, True)

        def q_block(qi, carry):
            t, dead = carry[0], carry[1:]
            n_pairs, lone_block0, dead = plan(qi, dead)
            n_stages = 1 + n_pairs + jnp.where(lone_block0, 1, 0)

            def body(i, t):
                stage_c(qi, i - 1, t & 1)
                stage_a(qi, i, (t + 1) & 1, False)
                return t + 1

            t = lax.fori_loop(1, n_stages, body, t)
            stage_c(qi, n_stages - 1, t & 1)
            stage_a(jnp.minimum(qi + 1, n_blk - 1), 0, (t + 1) & 1, True)
            finalize(qi)
            acc_ref[...] = jnp.zeros_like(acc_ref)
            return (t + 1,) + dead

        lax.fori_loop(0, n_blk, q_block, (jnp.int32(0), jnp.int32(-1), jnp.int32(-1)))

    bounded = par_ref[1] > 0.5
    pl.when(bounded)(bounded_path)
    pl.when(jnp.logical_not(bounded))(online_path)


def _fox_attention(q, k, vt, f, qk_bound, *, tq):
    bsz, s, w = q.shape
    pair = 2 * LANES
    n_heads = w // LANES
    assert s >= 2 * tq
    f_end = f[:, :, tq - 1::tq].reshape(-1)
    bounded = qk_bound <= MAX_UNSTABILISED_LOGIT
    par = jnp.stack([jnp.where(bounded, 0.5 - EXP_ZERO, 2.0 * qk_bound - EXP_ZERO),
                     jnp.where(bounded, 1.0, 0.0)]).astype(F32)
    kern = functools.partial(_fox_kernel, tq=tq, n_heads=n_heads)
    grid_spec = pltpu.PrefetchScalarGridSpec(
        num_scalar_prefetch=2,
        grid=(bsz, w // pair),
        in_specs=[
            pl.BlockSpec((1, s, pair), lambda b, p, fe, th: (b, 0, p)),
            pl.BlockSpec((1, s, pair), lambda b, p, fe, th: (b, 0, p)),
            pl.BlockSpec((1, 2 * V_ROWS, s), lambda b, p, fe, th: (b, p, 0)),
        ],
        out_specs=pl.BlockSpec((1, s, LANES), lambda b, p, fe, th: (b, 0, p)),
        scratch_shapes=[pltpu.VMEM((2, V_ROWS, tq), F32), pltpu.VMEM((2, 1, tq), F32),
                        pltpu.VMEM((2, 2, 2 * tq, tq), BF16)],
    )
    return pl.pallas_call(
        kern,
        grid_spec=grid_spec,
        out_shape=jax.ShapeDtypeStruct((bsz, s, w // 2), BF16),
        compiler_params=pltpu.CompilerParams(
            dimension_semantics=("arbitrary", "arbitrary"), vmem_limit_bytes=VMEM_LIMIT),
        name="fox_attn",
    )(f_end, par, q, k, vt)


def _out_kernel(x_ref, osb_ref, ofox_ref, osgu_ref, mod_ref, g2_ref, wo_ref, w1_ref, w2_ref,
                o_ref, *, ff_chunk):
    sb_w = osb_ref.shape[2]
    fox_w = ofox_ref.shape[2]
    x = x_ref[0]
    mix = (_dot(osb_ref[0], wo_ref[0:sb_w, :])
           + _dot(ofox_ref[0], wo_ref[sb_w:sb_w + fox_w, :])
           + _dot(osgu_ref[0], wo_ref[sb_w + fox_w:, :]))
    x1 = x + mod_ref[0, 2:3, :] * mix
    ms = jnp.mean(x1 * x1, axis=-1, keepdims=True)
    h = x1 * lax.rsqrt(ms + EPS) * g2_ref[...]
    hb = (h * (1.0 + mod_ref[0, 4:5, :]) + mod_ref[0, 3:4, :]).astype(BF16)
    d_ff = w1_ref.shape[1]
    acc = None
    for c in range(d_ff // ff_chunk):
        hid = jnp.maximum(_dot(hb, w1_ref[:, c * ff_chunk:(c + 1) * ff_chunk]), 0.0)
        part = _dot((hid * hid).astype(BF16), w2_ref[c * ff_chunk:(c + 1) * ff_chunk, :])
        acc = part if acc is None else acc + part
    o_ref[0] = x1 + mod_ref[0, 5:6, :] * acc


def _out_mlp(x, osb, ofox, osgu, mod, g2, wo, w1, w2, *, layer, tm):
    bsz, s, d = x.shape
    tok = lambda b, i: (b, i, 0)

    def of_layer(a, **kw):
        nd = a.ndim - 1
        return pl.BlockSpec((None,) + a.shape[1:], lambda b, i: (layer,) + (0,) * nd, **kw)

    single = pl.Buffered(1)
    kern = functools.partial(_out_kernel, ff_chunk=1024)
    return pl.pallas_call(
        kern,
        grid=(bsz, s // tm),
        in_specs=[
            pl.BlockSpec((1, tm, d), tok),
            pl.BlockSpec((1, tm, osb.shape[2]), tok),
            pl.BlockSpec((1, tm, ofox.shape[2]), tok),
            pl.BlockSpec((1, tm, osgu.shape[2]), tok),
            pl.BlockSpec((None, 1, 6, d), lambda b, i: (layer, b, 0, 0)),
            of_layer(g2),
            of_layer(wo, pipeline_mode=single),
            of_layer(w1, pipeline_mode=single),
            of_layer(w2, pipeline_mode=single),
        ],
        out_specs=pl.BlockSpec((1, tm, d), tok),
        out_shape=jax.ShapeDtypeStruct((bsz, s, d), F32),
        compiler_params=pltpu.CompilerParams(
            dimension_semantics=("arbitrary", "arbitrary"), vmem_limit_bytes=VMEM_LIMIT),
        name="out_mlp",
    )(x, osb, ofox, osgu, mod, g2, wo, w1, w2)


def kernel(x, c, ada_w, ada_b, norm1_g, norm2_g, w_in, b_forget, q_norm_g, k_norm_g, sgu_norm_g,
           sgu_w, sgu_b, w_out, mlp_w1, mlp_w2):
    depth, d, _ = ada_w.shape
    bsz, s, _ = x.shape
    fox_heads = b_forget.shape[1]
    fox_w = fox_heads * HEAD_DIM
    sgu_groups, chunk = sgu_b.shape[1], sgu_b.shape[2]
    sgu_wd = sgu_groups * sgu_norm_g.shape[2]
    sb_w = (w_in.shape[2] - 3 * fox_w - fox_heads - 2 * sgu_wd) // 3
    f_lo = 3 * sb_w + 3 * fox_w

    mod = _modulation(c, ada_w, ada_b).reshape(depth, bsz, 6, d)

    perm = jnp.argsort(b_forget, axis=1)

    def relabel_cols(cols):
        t = cols.reshape(depth, d, fox_heads, HEAD_DIM)
        return jnp.take_along_axis(t, perm[:, None, :, None], axis=2).reshape(depth, d, fox_w)

    o = 3 * sb_w
    w_bf = w_in.astype(BF16)
    fox_cols = [relabel_cols(w_bf[:, :, o + i * fox_w:o + (i + 1) * fox_w]) for i in range(3)]
    w = jnp.concatenate([w_bf[:, :, :o]] + fox_cols + [w_bf[:, :, f_lo + fox_heads:]], axis=2)
    wf = jnp.take_along_axis(w_in[:, :, f_lo:f_lo + fox_heads], perm[:, None, :], axis=2)
    wf = jnp.pad(wf, ((0, 0), (0, 0), (0, LANES - fox_heads))).astype(BF16)
    bf = jnp.pad(jnp.take_along_axis(b_forget, perm, axis=1), ((0, 0), (0, LANES - fox_heads)))
    bf = bf.reshape(depth, 1, LANES)
    qg = jnp.tile(q_norm_g, (1, MXU_DIM // HEAD_DIM)).reshape(depth, 1, MXU_DIM)
    kg = jnp.tile(k_norm_g, (1, MXU_DIM // HEAD_DIM)).reshape(depth, 1, MXU_DIM)
    sg = sgu_norm_g.reshape(depth, 1, sgu_wd)
    sb = jnp.repeat(jnp.swapaxes(sgu_b, 1, 2), sgu_norm_g.shape[2], axis=2)
    qk_bound = ((1.03 * HEAD_DIM ** 0.5) * jnp.max(jnp.abs(q_norm_g), axis=1)
                * jnp.max(jnp.abs(k_norm_g), axis=1))
    qx = jnp.zeros((depth, 1, LANES), F32).at[:, 0, HEAD_DIM:HEAD_DIM + 3].set(-1.0)
    qx = qx.at[:, 0, HEAD_DIM + 6].set(-qk_bound)
    wo_fox = w_out[:, sb_w:sb_w + fox_w].reshape(depth, fox_heads, HEAD_DIM, d)
    wo_fox = jnp.take_along_axis(wo_fox, perm[:, :, None, None], axis=1).reshape(depth, fox_w, d)
    wo = jnp.concatenate([w_out[:, :sb_w], wo_fox, w_out[:, sb_w + fox_w:]], axis=1).astype(BF16)
    w1 = mlp_w1.astype(BF16)
    w2 = mlp_w2.astype(BF16)
    g1 = norm1_g.reshape(depth, 1, d)
    g2 = norm2_g.reshape(depth, 1, d)

    for l in range(depth):
        qa, ka, vat, qf, kf, vft, osgu, f = _in_proj(
            x, mod, g1, w, wf, bf, qx, qg, kg, sg, sgu_w, sb,
            layer=l, tm=512, sb_w=sb_w, fox_w=fox_w, sgu_w=sgu_wd)
        osb = _sb_attention(qa, ka, vat, tq=256)
        ofox = _fox_attention(qf, kf, vft, f, qk_bound[l], tq=256)
        x = _out_mlp(x, osb, ofox, osgu, mod, g2, wo, w1, w2, layer=l, tm=512)
    return x
```

```python
import functools
import math

import jax
import jax.numpy as jnp
from jax import lax
from jax.experimental import pallas as pl
from jax.experimental.pallas import tpu as pltpu

HEAD_DIM = 64
LANES = 128
MXU_DIM = 256
V_ROWS = 2 * HEAD_DIM
PART_STRIDE = 8
BF16_MARGIN = 1.03

TOKEN_TILE = 512
ATTN_BLOCK = MXU_DIM
MOD_COLS = 1536
FF_CHUNK = 1024
EPS = 1e-6
NEG_BIG = -1e30
EXP_ZERO = -104.0
LOG2_E = math.log2(math.e)
MAX_UNSTABILISED_LOGIT = 40.0
VMEM_LIMIT = 56 * 1024 * 1024

F32 = jnp.float32
BF16 = jnp.bfloat16


def _dot(a, b):
    return jnp.dot(a, b, preferred_element_type=F32)


def _dot_nt(a, b):
    return lax.dot_general(a, b, (((1,), (1,)), ((), ())), preferred_element_type=F32)


def _block_start(j, size):
    return j * size if isinstance(j, int) else pl.multiple_of(j * size, size)


def _split3(x):
    hi = x.astype(BF16)
    r = x - hi.astype(F32)
    mid = r.astype(BF16)
    lo = (r - mid.astype(F32)).astype(BF16)
    return hi, mid, lo


def _group_mean_matrix():
    head = jnp.arange(MXU_DIM) // HEAD_DIM
    return jnp.where(head[:, None] == head[None, :], 1.0 / HEAD_DIM, 0.0).astype(BF16)


def _placement_matrix(n_heads):
    row = jnp.arange(LANES)[:, None]
    col = jnp.arange(2 * n_heads * LANES)[None, :]
    tile = col // LANES
    is_q = (tile >= n_heads).astype(jnp.int32)
    part = col % LANES - HEAD_DIM - 3 * is_q
    hit = (part >= 0) & (part < 3) & (row == part * PART_STRIDE + tile - n_heads * is_q)
    return hit.astype(BF16)


def _head_rmsnorm(t, gmat, gain):
    ms = _dot((t * t).astype(BF16), gmat)
    return t * lax.rsqrt(ms + EPS) * gain


def _gelu_tanh(x):
    c = math.sqrt(2.0 / math.pi)
    return x * (0.5 * (1.0 + jnp.tanh(c * (x + 0.044715 * (x * x * x)))))


def _log_sigmoid(x):
    return jnp.minimum(x, 0.0) - jnp.log(1.0 + jnp.exp(-jnp.abs(x)))


def _mod_kernel(ct_ref, w_ref, b_ref, o_ref):
    ct = ct_ref[...]
    cond = ct * (1.0 / (1.0 + jnp.exp(-ct)))
    w = w_ref[0]
    rows = [jnp.sum(cond[:, b:b + 1] * w, axis=0, keepdims=True) for b in range(ct.shape[1])]
    o_ref[0] = jnp.concatenate(rows, axis=0) + b_ref[0]


def _modulation(c, ada_w, ada_b):
    depth, d, n = ada_w.shape
    bsz = c.shape[0]
    tn = MOD_COLS
    return pl.pallas_call(
        _mod_kernel,
        grid=(depth, n // tn),
        in_specs=[
            pl.BlockSpec((d, bsz), lambda l, j: (0, 0)),
            pl.BlockSpec((1, d, tn), lambda l, j: (l, 0, j)),
            pl.BlockSpec((1, 1, tn), lambda l, j: (l, 0, j)),
        ],
        out_specs=pl.BlockSpec((1, bsz, tn), lambda l, j: (l, 0, j)),
        out_shape=jax.ShapeDtypeStruct((depth, bsz, n), F32),
        compiler_params=pltpu.CompilerParams(
            dimension_semantics=("arbitrary", "arbitrary"), vmem_limit_bytes=VMEM_LIMIT),
        name="adaln_mod",
    )(c.T, ada_w, ada_b.reshape(depth, 1, n))


def _in_kernel(x_ref, mod_ref, g1_ref, w_ref, wf_ref, bf_ref, qx_ref, qg_ref, kg_ref,
               sg_ref, sw_ref, sb_ref, ltri_ref, sel_ref, gmat_ref,
               qa_ref, ka_ref, vat_ref, qf_ref, kf_ref, vft_ref, og_ref, f_ref,
               carry_ref, proj_ref, *, sb_w, fox_w, sgu_w, chunk):
    @pl.when(pl.program_id(1) == 0)
    def _():
        carry_ref[...] = jnp.zeros_like(carry_ref)

    tm = x_ref.shape[1]
    x = x_ref[0]
    ms = jnp.mean(x * x, axis=-1, keepdims=True)
    h = x * lax.rsqrt(ms + EPS) * g1_ref[...]
    h = h * (1.0 + mod_ref[0, 1:2, :]) + mod_ref[0, 0:1, :]
    hb = h.astype(BF16)
    scale = HEAD_DIM ** -0.5
    n_attn = 3 * sb_w + 3 * fox_w
    wn = w_ref.shape[1]
    n_heads = f_ref.shape[1]
    lane = lax.broadcasted_iota(jnp.int32, (1, LANES), 1)
    head_lane = lane < n_heads
    gmat = gmat_ref[...]

    def pack3(parts):
        a, b, c = (p.astype(F32) for p in parts)
        return (a + pltpu.roll(b, PART_STRIDE, 1) + pltpu.roll(c, 2 * PART_STRIDE, 1)).astype(BF16)

    def project(lo, hi):
        proj_ref[:, lo:hi] = _dot(hb, w_ref[:, lo:hi])

    o_q, o_k, o_v = 3 * sb_w, 3 * sb_w + fox_w, 3 * sb_w + 2 * fox_w
    project(n_attn, wn)
    fl = _dot(hb, wf_ref[...])
    project(o_q, o_k)
    gu = _gelu_tanh(proj_ref[:, n_attn:n_attn + sgu_w])
    gv = _gelu_tanh(proj_ref[:, n_attn + sgu_w:wn])
    project(o_k, o_v)
    logf = jnp.where(head_lane, _log_sigmoid(fl + bf_ref[...]), 0.0)
    cp = _dot(ltri_ref[...], pack3(_split3(logf)))
    vn_all = _head_rmsnorm(gv, gmat, sg_ref[...]).astype(BF16)
    q_tiles = [_head_rmsnorm(proj_ref[:, o_q + j * MXU_DIM:o_q + (j + 1) * MXU_DIM], gmat, qg_ref[...]) * scale
               for j in range(fox_w // MXU_DIM)]
    project(o_v, n_attn)

    cum = cp + pltpu.roll(cp, LANES - PART_STRIDE, 1) + pltpu.roll(cp, LANES - 2 * PART_STRIDE, 1)
    cum = jnp.where(head_lane, cum, 0.0) + carry_ref[0:1, :]
    carry_ref[0:1, :] = cum[tm - 1:tm, :]
    f_ref[0] = cum.T[0:n_heads, :]
    extras = _dot(pack3(_split3(cum)), sel_ref[...])
    k_tiles = [_head_rmsnorm(proj_ref[:, o_k + j * MXU_DIM:o_k + (j + 1) * MXU_DIM], gmat, kg_ref[...])
               for j in range(fox_w // MXU_DIM)]

    lane_c = lax.broadcasted_iota(jnp.int32, (chunk, LANES), 1)
    rr = lax.broadcasted_iota(jnp.int32, (chunk, chunk), 0)
    cs = lax.broadcasted_iota(jnp.int32, (chunk, chunk), 1)
    wt = [jnp.where(rr >= cs, sw_ref[g], 0.0).astype(BF16) for g in range(sw_ref.shape[0])]
    for p in range(sgu_w // LANES):
        vn = vn_all[:, p * LANES:(p + 1) * LANES]
        for ci in range(tm // chunk):
            vblk = vn[ci * chunk:(ci + 1) * chunk, :]
            mixed = jnp.where(lane_c < HEAD_DIM, _dot(wt[2 * p], vblk), _dot(wt[2 * p + 1], vblk))
            mixed = mixed + sb_ref[:, p * LANES:(p + 1) * LANES]
            og_ref[0, ci * chunk:(ci + 1) * chunk, p * LANES:(p + 1) * LANES] = (
                gu[ci * chunk:(ci + 1) * chunk, p * LANES:(p + 1) * LANES] * mixed).astype(BF16)

    project(0, o_q)
    qa_ref[0] = (proj_ref[:, 0:sb_w] * (scale * LOG2_E)).astype(BF16)
    ka_ref[0] = proj_ref[:, sb_w:2 * sb_w].astype(BF16)
    vat_ref[0] = proj_ref[:, 2 * sb_w:3 * sb_w].T.astype(BF16)

    is_head = lane < HEAD_DIM
    q_extra = qx_ref[...]
    k_extra = jnp.where((lane >= HEAD_DIM + 3) & (lane < HEAD_DIM + 7), 1.0, 0.0)
    heads_per_mxu = MXU_DIM // HEAD_DIM

    def head_tile(tn, s):
        half = tn[:, (s // 2) * LANES:(s // 2 + 1) * LANES]
        return half if s % 2 == 0 else pltpu.roll(half, HEAD_DIM, 1)

    for j in range(fox_w // MXU_DIM):
        for s in range(heads_per_mxu):
            hh = heads_per_mxu * j + s
            xq = extras[:, (n_heads + hh) * LANES:(n_heads + hh + 1) * LANES] + q_extra
            qf_ref[0, :, hh * LANES:(hh + 1) * LANES] = jnp.where(
                is_head, head_tile(q_tiles[j], s), xq).astype(BF16)
            xk = extras[:, hh * LANES:(hh + 1) * LANES] + k_extra
            kf_ref[0, :, hh * LANES:(hh + 1) * LANES] = jnp.where(
                is_head, head_tile(k_tiles[j], s), xk).astype(BF16)
    vt = proj_ref[:, o_v:n_attn].T.astype(BF16)
    ones = jnp.ones((V_ROWS - HEAD_DIM, tm), BF16)
    for hh in range(fox_w // HEAD_DIM):
        vft_ref[0, hh * V_ROWS:hh * V_ROWS + HEAD_DIM, :] = vt[hh * HEAD_DIM:(hh + 1) * HEAD_DIM, :]
        vft_ref[0, hh * V_ROWS + HEAD_DIM:(hh + 1) * V_ROWS, :] = ones


def _in_proj(x, mod, g1, w, wf, bf, qx, qg, kg, sg, sw, sb, *, layer, tm, sb_w, fox_w, sgu_w):
    bsz, s, d = x.shape
    chunk = sw.shape[-1]
    wn = w.shape[2]
    fox_heads = fox_w // HEAD_DIM

    def of_layer(a):
        nd = a.ndim - 1
        return pl.BlockSpec((None,) + a.shape[1:], lambda b, i: (layer,) + (0,) * nd)

    assert fox_heads <= PART_STRIDE and fox_w % MXU_DIM == 0 and sgu_w == MXU_DIM
    assert s % tm == 0 and tm % chunk == 0
    tok_idx = jnp.arange(tm)
    ltri = (tok_idx[:, None] >= tok_idx[None, :]).astype(BF16)
    const2 = lambda b, i: (0, 0)
    tok = lambda b, i: (b, i, 0)
    tok_t = lambda b, i: (b, 0, i)
    kern = functools.partial(_in_kernel, sb_w=sb_w, fox_w=fox_w, sgu_w=sgu_w, chunk=chunk)
    out_shape = [
        jax.ShapeDtypeStruct((bsz, s, sb_w), BF16),
        jax.ShapeDtypeStruct((bsz, s, sb_w), BF16),
        jax.ShapeDtypeStruct((bsz, sb_w, s), BF16),
        jax.ShapeDtypeStruct((bsz, s, fox_heads * LANES), BF16),
        jax.ShapeDtypeStruct((bsz, s, fox_heads * LANES), BF16),
        jax.ShapeDtypeStruct((bsz, fox_heads * V_ROWS, s), BF16),
        jax.ShapeDtypeStruct((bsz, s, sgu_w), BF16),
        jax.ShapeDtypeStruct((bsz, fox_heads, s), F32),
    ]
    out_specs = [
        pl.BlockSpec((1, tm, sb_w), tok),
        pl.BlockSpec((1, tm, sb_w), tok),
        pl.BlockSpec((1, sb_w, tm), tok_t),
        pl.BlockSpec((1, tm, fox_heads * LANES), tok),
        pl.BlockSpec((1, tm, fox_heads * LANES), tok),
        pl.BlockSpec((1, fox_heads * V_ROWS, tm), tok_t),
        pl.BlockSpec((1, tm, sgu_w), tok),
        pl.BlockSpec((1, fox_heads, tm), tok_t),
    ]
    return pl.pallas_call(
        kern,
        grid=(bsz, s // tm),
        in_specs=[
            pl.BlockSpec((1, tm, d), tok),
            pl.BlockSpec((None, 1, 6, d), lambda b, i: (layer, b, 0, 0)),
            of_layer(g1), of_layer(w), of_layer(wf), of_layer(bf), of_layer(qx), of_layer(qg),
            of_layer(kg), of_layer(sg), of_layer(sw), of_layer(sb),
            pl.BlockSpec((tm, tm), const2),
            pl.BlockSpec((LANES, 2 * fox_heads * LANES), const2),
            pl.BlockSpec((MXU_DIM, MXU_DIM), const2),
        ],
        out_specs=out_specs,
        out_shape=out_shape,
        scratch_shapes=[pltpu.VMEM((8, LANES), F32), pltpu.VMEM((tm, wn), F32)],
        compiler_params=pltpu.CompilerParams(
            dimension_semantics=("arbitrary", "arbitrary"), vmem_limit_bytes=VMEM_LIMIT),
        name="in_proj",
    )(x, mod, g1, w, wf, bf, qx, qg, kg, sg, sw, sb, ltri, _placement_matrix(fox_heads),
      _group_mean_matrix())


def _sb_kernel(q_ref, k_ref, vt_ref, after_ref, o_ref, acc_ref, c_ref, *, tq):
    tk = tq
    n_blk = q_ref.shape[1] // tq
    lane = lax.broadcasted_iota(jnp.int32, (1, LANES), 1)
    key = lax.broadcasted_iota(jnp.int32, (tk, tq), 0)
    qry = lax.broadcasted_iota(jnp.int32, (tk, tq), 1)
    causal = key < qry
    after = after_ref[...]

    def process(chains):
        units, starts, qh = [], {}, {}
        for slot, qi, blocks, first in chains:
            q = q_ref[0, pl.ds(_block_start(qi, tq), tq), :]
            qh[slot] = (jnp.where(lane < HEAD_DIM, q, 0).astype(BF16),
                        jnp.where(lane >= HEAD_DIM, q, 0).astype(BF16))
            for b, j in enumerate(blocks):
                starts[slot, b] = _block_start(j, tk)
                units += [(slot, b, h, first and b == 0) for h in range(2)]
        z = {(sl, b, h): _dot_nt(k_ref[0, pl.ds(starts[sl, b], tk), :], qh[sl][h])
             for sl, b, h, _ in units}
        l1mb, head = {}, {}
        for sl, b, h, diag in units:
            zz = z[sl, b, h]
            nz = -zz
            lg = jnp.minimum(nz, 0.0) - jnp.log2(1.0 + jnp.exp2(jnp.minimum(zz, nz)))
            if diag:
                lg = jnp.where(causal, lg, 0.0)
            l1mb[sl, b, h] = lg.astype(BF16)
            head[sl, b, h] = (zz + lg, lg[0:1, :])
        between = {(sl, b, h): _dot(after, l1mb[sl, b, h]) for sl, b, h, _ in units}
        a = {}
        for slot, qi, blocks, first in chains:
            for h in range(2):
                c = None if first else c_ref[slot, h]
                for b in range(len(blocks)):
                    e = head[slot, b, h][0] + between[slot, b, h]
                    if c is not None:
                        e = e + c
                    w = jnp.exp2(e)
                    if first and b == 0:
                        w = jnp.where(causal, w, 0.0)
                    a[slot, b, h] = w.astype(BF16)
                    block_sum = between[slot, b, h][0:1, :] + head[slot, b, h][1]
                    c = block_sum if c is None else c + block_sum
                c_ref[slot, h] = c
        for slot, qi, blocks, first in chains:
            for h in range(2):
                pv = None
                for b in range(len(blocks)):
                    part = _dot(vt_ref[0, h * HEAD_DIM:(h + 1) * HEAD_DIM, pl.ds(starts[slot, b], tk)],
                                a[slot, b, h])
                    pv = part if pv is None else pv + part
                if first:
                    acc_ref[slot, h] = pv
                else:
                    acc_ref[slot, h] += pv

    def carry_max(slot):
        cm = jnp.maximum(c_ref[slot, 0], c_ref[slot, 1])
        return jnp.max(cm, axis=1, keepdims=True)[0, 0]

    def finish(slot, qi):
        def cond(carry):
            j, cmax = carry
            return jnp.logical_and(j >= 0, cmax > EXP_ZERO * LOG2_E)

        def body(carry):
            j, _ = carry
            process([(slot, qi, [j], False)])
            return j - 1, carry_max(slot)

        lax.while_loop(cond, body, (jnp.asarray(qi - 2, jnp.int32), carry_max(slot)))
        o_ref[0, pl.ds(_block_start(qi, tq), tq), :] = (
            jnp.concatenate([acc_ref[slot, 0], acc_ref[slot, 1]], axis=0).T.astype(o_ref.dtype))

    process([(0, 0, [0], True)])
    finish(0, 0)

    def two_blocks(g, carry):
        qa, qb = 1 + 2 * g, 2 + 2 * g
        process([(0, qa, [qa, qa - 1], True), (1, qb, [qb, qb - 1], True)])
        finish(0, qa)
        finish(1, qb)
        return carry

    lax.fori_loop(0, (n_blk - 1) // 2, two_blocks, 0)
    if (n_blk - 1) % 2:
        last = n_blk - 1
        process([(0, last, [last, last - 1], True)])
        finish(0, last)


def _sb_attention(q, k, vt, *, tq):
    bsz, s, w = q.shape
    assert s >= 2 * tq
    idx = jnp.arange(tq)
    after = (idx[None, :] > idx[:, None]).astype(BF16)
    kern = functools.partial(_sb_kernel, tq=tq)
    return pl.pallas_call(
        kern,
        grid=(bsz, w // LANES),
        in_specs=[
            pl.BlockSpec((1, s, LANES), lambda b, p: (b, 0, p)),
            pl.BlockSpec((1, s, LANES), lambda b, p: (b, 0, p)),
            pl.BlockSpec((1, LANES, s), lambda b, p: (b, p, 0)),
            pl.BlockSpec((tq, tq), lambda b, p: (0, 0)),
        ],
        out_specs=pl.BlockSpec((1, s, LANES), lambda b, p: (b, 0, p)),
        out_shape=jax.ShapeDtypeStruct((bsz, s, w), BF16),
        scratch_shapes=[pltpu.VMEM((2, 2, HEAD_DIM, tq), F32), pltpu.VMEM((2, 2, 1, tq), F32)],
        compiler_params=pltpu.CompilerParams(
            dimension_semantics=("arbitrary", "arbitrary"), vmem_limit_bytes=VMEM_LIMIT),
        name="sb_attn",
    )(q, k, vt, after)


def _fox_kernel(fend_ref, par_ref, q_ref, k_ref, vt_ref, o_ref, acc_ref, m_ref, p_ref, *, tq, n_heads):
    tk = tq
    n_blk = q_ref.shape[1] // tq
    head0 = (pl.program_id(0) * n_heads + 2 * pl.program_id(1))

    def last_dead_block(head, qi, j_prev):
        base = head * n_blk
        f_q = fend_ref[base + jnp.maximum(qi - 1, 0)]

        def next_is_dead(j):
            jn = jnp.minimum(j + 1, n_blk - 1)
            return jnp.logical_and(j + 1 < qi, f_q - fend_ref[base + jn] < -par_ref[0])

        return lax.while_loop(next_is_dead, lambda j: j + 1, j_prev)

    def plan(qi, dead):
        dead = (last_dead_block(head0, qi, dead[0]), last_dead_block(head0 + 1, qi, dead[1]))
        j_dead = jnp.minimum(dead[0], dead[1])
        n_left = jnp.maximum(qi - 2 - j_dead, 0)
        odd = n_left % 2
        has_dead_below = j_dead >= 0
        n_pairs = n_left // 2 + jnp.where(has_dead_below, odd, 0)
        return n_pairs, jnp.logical_and(odd == 1, jnp.logical_not(has_dead_below)), dead

    def q_tile(qi, h):
        return q_ref[0, pl.ds(pl.multiple_of(qi * tq, tq), tq), h * LANES:(h + 1) * LANES]

    def finalize(qi):
        outs = []
        for h in range(2):
            acc = acc_ref[h]
            outs.append(acc[0:HEAD_DIM, :] / acc[HEAD_DIM:HEAD_DIM + 1, :])
        o_ref[0, pl.ds(pl.multiple_of(qi * tq, tq), tq), :] = (
            jnp.concatenate(outs, axis=0).T.astype(o_ref.dtype))

    key = lax.broadcasted_iota(jnp.int32, (tk, tq), 0)
    qry = lax.broadcasted_iota(jnp.int32, (tk, tq), 1)
    causal = key <= qry

    def process(qi, blocks, first):
        starts = [pl.multiple_of(j * tk, tk) for j in blocks]
        units = [(b, h) for b in range(len(blocks)) for h in range(2)]
        s = {(b, h): _dot_nt(k_ref[0, pl.ds(starts[b], tk), h * LANES:(h + 1) * LANES],
                             q_tile(qi, h)) for b, h in units}
        p = {}
        alpha = {}
        for h in range(2):
            if first:
                s[0, h] = jnp.where(causal, s[0, h], NEG_BIG)
            m_new = None if first else m_ref[h]
            for b in range(len(blocks)):
                mb = jnp.max(s[b, h], axis=0, keepdims=True)
                m_new = mb if m_new is None else jnp.maximum(m_new, mb)
            if not first:
                alpha[h] = jnp.exp(m_ref[h] - m_new)
            for b in range(len(blocks)):
                p[b, h] = jnp.exp(s[b, h] - m_new).astype(BF16)
            m_ref[h] = m_new
        for h in range(2):
            pv = None
            for b in range(len(blocks)):
                part = _dot(vt_ref[0, h * V_ROWS:(h + 1) * V_ROWS, pl.ds(starts[b], tk)], p[b, h])
                pv = part if pv is None else pv + part
            acc_ref[h] = pv if first else alpha[h] * acc_ref[h] + pv

    def online_path():
        def q_block(qi, dead):
            n_pairs, lone_block0, dead = plan(qi, dead)

            @pl.when(qi == 0)
            def _():
                process(qi, [qi], True)

            @pl.when(qi > 0)
            def _():
                process(qi, [qi, qi - 1], True)

            def body(i, c):
                j = qi - 2 - 2 * i
                process(qi, [j, j - 1], False)
                return c

            lax.fori_loop(0, n_pairs, body, 0)

            @pl.when(lone_block0)
            def _():
                process(qi, [0], False)

            finalize(qi)
            return dead

        lax.fori_loop(0, n_blk, q_block, (jnp.int32(-1), jnp.int32(-1)))

    win = 2 * tk
    wrow = lax.broadcasted_iota(jnp.int32, (win, tq), 0)
    wcol = lax.broadcasted_iota(jnp.int32, (win, tq), 1)

    def window(qi, i):
        jl = qi - 1 - 2 * i
        j_lo = jnp.maximum(jl, 0)
        return j_lo, pl.multiple_of(j_lo * tk, tk), jnp.where(jl < 0, tk, win)

    def stage_a(qi, i, slot, first):
        j_lo, start, row_lim = window(qi, i)
        keep = wrow < row_lim
        if first:
            keep = jnp.logical_and(keep, wrow <= wcol + (qi - j_lo) * tk)
        for h in range(2):
            s = _dot_nt(k_ref[0, pl.ds(start, win), h * LANES:(h + 1) * LANES], q_tile(qi, h))
            p_ref[slot, h] = jnp.exp(jnp.where(keep, s, NEG_BIG)).astype(BF16)

    def stage_c(qi, i, slot):
        _, start, _ = window(qi, i)
        for h in range(2):
            acc_ref[h] += _dot(vt_ref[0, h * V_ROWS:(h + 1) * V_ROWS, pl.ds(start, win)], p_ref[slot, h])

    def bounded_path():
        acc_ref[...] = jnp.zeros_like(acc_ref)
        stage_a(0, 0, 0, True)

        def q_block(qi, carry):
            t, dead = carry[0], carry[1:]
            n_pairs, lone_block0, dead = plan(qi, dead)
            n_stages = 1 + n_pairs + jnp.where(lone_block0, 1, 0)

            def body(i, t):
                stage_c(qi, i - 1, t & 1)
                stage_a(qi, i, (t + 1) & 1, False)
                return t + 1

            t = lax.fori_loop(1, n_stages, body, t)
            stage_c(qi, n_stages - 1, t & 1)
            stage_a(jnp.minimum(qi + 1, n_blk - 1), 0, (t + 1) & 1, True)
            finalize(qi)
            acc_ref[...] = jnp.zeros_like(acc_ref)
            return (t + 1,) + dead

        lax.fori_loop(0, n_blk, q_block, (jnp.int32(0), jnp.int32(-1), jnp.int32(-1)))

    bounded = par_ref[1] > 0.5
    pl.when(bounded)(bounded_path)
    pl.when(jnp.logical_not(bounded))(online_path)


def _fox_attention(q, k, vt, f, qk_bound, *, tq):
    bsz, s, w = q.shape
    pair = 2 * LANES
    n_heads = w // LANES
    assert s >= 2 * tq
    f_end = f[:, :, tq - 1::tq].reshape(-1)
    bounded = qk_bound <= MAX_UNSTABILISED_LOGIT
    stabiliser_slack = 0.5
    par = jnp.stack([jnp.where(bounded, stabiliser_slack - EXP_ZERO, 2.0 * qk_bound - EXP_ZERO),
                     jnp.where(bounded, 1.0, 0.0)]).astype(F32)
    kern = functools.partial(_fox_kernel, tq=tq, n_heads=n_heads)
    grid_spec = pltpu.PrefetchScalarGridSpec(
        num_scalar_prefetch=2,
        grid=(bsz, w // pair),
        in_specs=[
            pl.BlockSpec((1, s, pair), lambda b, p, fe, th: (b, 0, p)),
            pl.BlockSpec((1, s, pair), lambda b, p, fe, th: (b, 0, p)),
            pl.BlockSpec((1, 2 * V_ROWS, s), lambda b, p, fe, th: (b, p, 0)),
        ],
        out_specs=pl.BlockSpec((1, s, LANES), lambda b, p, fe, th: (b, 0, p)),
        scratch_shapes=[pltpu.VMEM((2, V_ROWS, tq), F32), pltpu.VMEM((2, 1, tq), F32),
                        pltpu.VMEM((2, 2, 2 * tq, tq), BF16)],
    )
    return pl.pallas_call(
        kern,
        grid_spec=grid_spec,
        out_shape=jax.ShapeDtypeStruct((bsz, s, w // 2), BF16),
        compiler_params=pltpu.CompilerParams(
            dimension_semantics=("arbitrary", "arbitrary"), vmem_limit_bytes=VMEM_LIMIT),
        name="fox_attn",
    )(f_end, par, q, k, vt)


def _out_kernel(x_ref, osb_ref, ofox_ref, osgu_ref, mod_ref, g2_ref, wo_ref, w1_ref, w2_ref,
                o_ref, *, ff_chunk):
    sb_w = osb_ref.shape[2]
    fox_w = ofox_ref.shape[2]
    x = x_ref[0]
    mix = (_dot(osb_ref[0], wo_ref[0:sb_w, :])
           + _dot(ofox_ref[0], wo_ref[sb_w:sb_w + fox_w, :])
           + _dot(osgu_ref[0], wo_ref[sb_w + fox_w:, :]))
    x1 = x + mod_ref[0, 2:3, :] * mix
    ms = jnp.mean(x1 * x1, axis=-1, keepdims=True)
    h = x1 * lax.rsqrt(ms + EPS) * g2_ref[...]
    hb = (h * (1.0 + mod_ref[0, 4:5, :]) + mod_ref[0, 3:4, :]).astype(BF16)
    d_ff = w1_ref.shape[1]
    acc = None
    for c in range(d_ff // ff_chunk):
        hid = jnp.maximum(_dot(hb, w1_ref[:, c * ff_chunk:(c + 1) * ff_chunk]), 0.0)
        part = _dot((hid * hid).astype(BF16), w2_ref[c * ff_chunk:(c + 1) * ff_chunk, :])
        acc = part if acc is None else acc + part
    o_ref[0] = x1 + mod_ref[0, 5:6, :] * acc


def _out_mlp(x, osb, ofox, osgu, mod, g2, wo, w1, w2, *, layer, tm):
    bsz, s, d = x.shape
    tok = lambda b, i: (b, i, 0)

    def of_layer(a, **kw):
        nd = a.ndim - 1
        return pl.BlockSpec((None,) + a.shape[1:], lambda b, i: (layer,) + (0,) * nd, **kw)

    single = pl.Buffered(1)
    kern = functools.partial(_out_kernel, ff_chunk=FF_CHUNK)
    return pl.pallas_call(
        kern,
        grid=(bsz, s // tm),
        in_specs=[
            pl.BlockSpec((1, tm, d), tok),
            pl.BlockSpec((1, tm, osb.shape[2]), tok),
            pl.BlockSpec((1, tm, ofox.shape[2]), tok),
            pl.BlockSpec((1, tm, osgu.shape[2]), tok),
            pl.BlockSpec((None, 1, 6, d), lambda b, i: (layer, b, 0, 0)),
            of_layer(g2),
            of_layer(wo, pipeline_mode=single),
            of_layer(w1, pipeline_mode=single),
            of_layer(w2, pipeline_mode=single),
        ],
        out_specs=pl.BlockSpec((1, tm, d), tok),
        out_shape=jax.ShapeDtypeStruct((bsz, s, d), F32),
        compiler_params=pltpu.CompilerParams(
            dimension_semantics=("arbitrary", "arbitrary"), vmem_limit_bytes=VMEM_LIMIT),
        name="out_mlp",
    )(x, osb, ofox, osgu, mod, g2, wo, w1, w2)


def kernel(x, c, ada_w, ada_b, norm1_g, norm2_g, w_in, b_forget, q_norm_g, k_norm_g, sgu_norm_g,
           sgu_w, sgu_b, w_out, mlp_w1, mlp_w2):
    depth, d, _ = ada_w.shape
    bsz, s, _ = x.shape
    fox_heads = b_forget.shape[1]
    fox_w = fox_heads * HEAD_DIM
    sgu_groups, chunk = sgu_b.shape[1], sgu_b.shape[2]
    sgu_wd = sgu_groups * sgu_norm_g.shape[2]
    sb_w = (w_in.shape[2] - 3 * fox_w - fox_heads - 2 * sgu_wd) // 3
    f_lo = 3 * sb_w + 3 * fox_w
    assert s % TOKEN_TILE == 0 and s % (2 * ATTN_BLOCK) == 0 and mlp_w1.shape[2] % FF_CHUNK == 0
    assert sb_w % LANES == 0 and ada_w.shape[2] % MOD_COLS == 0

    mod = _modulation(c, ada_w, ada_b).reshape(depth, bsz, 6, d)

    perm = jnp.argsort(b_forget, axis=1)

    def relabel_cols(cols):
        t = cols.reshape(depth, d, fox_heads, HEAD_DIM)
        return jnp.take_along_axis(t, perm[:, None, :, None], axis=2).reshape(depth, d, fox_w)

    o = 3 * sb_w
    w_bf = w_in.astype(BF16)
    fox_cols = [relabel_cols(w_bf[:, :, o + i * fox_w:o + (i + 1) * fox_w]) for i in range(3)]
    w = jnp.concatenate([w_bf[:, :, :o]] + fox_cols + [w_bf[:, :, f_lo + fox_heads:]], axis=2)
    wf = jnp.take_along_axis(w_in[:, :, f_lo:f_lo + fox_heads], perm[:, None, :], axis=2)
    wf = jnp.pad(wf, ((0, 0), (0, 0), (0, LANES - fox_heads))).astype(BF16)
    bf = jnp.pad(jnp.take_along_axis(b_forget, perm, axis=1), ((0, 0), (0, LANES - fox_heads)))
    bf = bf.reshape(depth, 1, LANES)
    qg = jnp.tile(q_norm_g, (1, MXU_DIM // HEAD_DIM)).reshape(depth, 1, MXU_DIM)
    kg = jnp.tile(k_norm_g, (1, MXU_DIM // HEAD_DIM)).reshape(depth, 1, MXU_DIM)
    sg = sgu_norm_g.reshape(depth, 1, sgu_wd)
    sb = jnp.repeat(jnp.swapaxes(sgu_b, 1, 2), sgu_norm_g.shape[2], axis=2)
    qk_bound = ((BF16_MARGIN * HEAD_DIM ** 0.5) * jnp.max(jnp.abs(q_norm_g), axis=1)
                * jnp.max(jnp.abs(k_norm_g), axis=1))
    qx = jnp.zeros((depth, 1, LANES), F32).at[:, 0, HEAD_DIM:HEAD_DIM + 3].set(-1.0)
    qx = qx.at[:, 0, HEAD_DIM + 6].set(-qk_bound)
    wo_fox = w_out[:, sb_w:sb_w + fox_w].reshape(depth, fox_heads, HEAD_DIM, d)
    wo_fox = jnp.take_along_axis(wo_fox, perm[:, :, None, None], axis=1).reshape(depth, fox_w, d)
    wo = jnp.concatenate([w_out[:, :sb_w], wo_fox, w_out[:, sb_w + fox_w:]], axis=1).astype(BF16)
    w1 = mlp_w1.astype(BF16)
    w2 = mlp_w2.astype(BF16)
    g1 = norm1_g.reshape(depth, 1, d)
    g2 = norm2_g.reshape(depth, 1, d)

    for l in range(depth):
        qa, ka, vat, qf, kf, vft, osgu, f = _in_proj(
            x, mod, g1, w, wf, bf, qx, qg, kg, sg, sgu_w, sb,
            layer=l, tm=TOKEN_TILE, sb_w=sb_w, fox_w=fox_w, sgu_w=sgu_wd)
        osb = _sb_attention(qa, ka, vat, tq=ATTN_BLOCK)
        ofox = _fox_attention(qf, kf, vft, f, qk_bound[l], tq=ATTN_BLOCK)
        x = _out_mlp(x, osb, ofox, osgu, mod, g2, wo, w1, w2, layer=l, tm=TOKEN_TILE)
    return x
```

```python
import functools
import math

import jax
import jax.numpy as jnp
from jax import lax
from jax.experimental import pallas as pl
from jax.experimental.pallas import tpu as pltpu

HEAD_DIM = 64
LANES = 128
MXU_DIM = 256
V_ROWS = 2 * HEAD_DIM
PART_STRIDE = 8
BF16_MARGIN = 1.03

TOKEN_TILE = 512
ATTN_BLOCK = MXU_DIM
MOD_COLS = 1536
FF_CHUNK = 1024
EPS = 1e-6
NEG_BIG = -1e30
EXP_ZERO = -104.0
LOG2_E = math.log2(math.e)
MAX_UNSTABILISED_LOGIT = 40.0
VMEM_LIMIT = 56 * 1024 * 1024

F32 = jnp.float32
BF16 = jnp.bfloat16


def _dot(a, b):
    return jnp.dot(a, b, preferred_element_type=F32)


def _dot_nt(a, b):
    return lax.dot_general(a, b, (((1,), (1,)), ((), ())), preferred_element_type=F32)


def _block_start(j, size):
    return j * size if isinstance(j, int) else pl.multiple_of(j * size, size)


def _split3(x):
    hi = x.astype(BF16)
    r = x - hi.astype(F32)
    mid = r.astype(BF16)
    lo = (r - mid.astype(F32)).astype(BF16)
    return hi, mid, lo


def _group_mean_matrix():
    head = jnp.arange(MXU_DIM) // HEAD_DIM
    return jnp.where(head[:, None] == head[None, :], 1.0 / HEAD_DIM, 0.0).astype(BF16)


def _placement_matrix(n_heads):
    row = jnp.arange(LANES)[:, None]
    col = jnp.arange(2 * n_heads * LANES)[None, :]
    tile = col // LANES
    is_q = (tile >= n_heads).astype(jnp.int32)
    part = col % LANES - HEAD_DIM - 3 * is_q
    hit = (part >= 0) & (part < 3) & (row == part * PART_STRIDE + tile - n_heads * is_q)
    return hit.astype(BF16)


def _head_rmsnorm(t, gmat, gain):
    ms = _dot((t * t).astype(BF16), gmat)
    return t * lax.rsqrt(ms + EPS) * gain


def _gelu_tanh(x):
    c = math.sqrt(2.0 / math.pi)
    return x * (0.5 * (1.0 + jnp.tanh(c * (x + 0.044715 * (x * x * x)))))


def _log_sigmoid(x):
    return jnp.minimum(x, 0.0) - jnp.log(1.0 + jnp.exp(-jnp.abs(x)))


def _mod_kernel(ct_ref, w_ref, b_ref, o_ref):
    ct = ct_ref[...]
    cond = ct * (1.0 / (1.0 + jnp.exp(-ct)))
    w = w_ref[0]
    rows = [jnp.sum(cond[:, b:b + 1] * w, axis=0, keepdims=True) for b in range(ct.shape[1])]
    o_ref[0] = jnp.concatenate(rows, axis=0) + b_ref[0]


def _modulation(c, ada_w, ada_b):
    depth, d, n = ada_w.shape
    bsz = c.shape[0]
    tn = MOD_COLS
    return pl.pallas_call(
        _mod_kernel,
        grid=(depth, n // tn),
        in_specs=[
            pl.BlockSpec((d, bsz), lambda l, j: (0, 0)),
            pl.BlockSpec((1, d, tn), lambda l, j: (l, 0, j)),
            pl.BlockSpec((1, 1, tn), lambda l, j: (l, 0, j)),
        ],
        out_specs=pl.BlockSpec((1, bsz, tn), lambda l, j: (l, 0, j)),
        out_shape=jax.ShapeDtypeStruct((depth, bsz, n), F32),
        compiler_params=pltpu.CompilerParams(
            dimension_semantics=("arbitrary", "arbitrary"), vmem_limit_bytes=VMEM_LIMIT),
        name="adaln_mod",
    )(c.T, ada_w, ada_b.reshape(depth, 1, n))


def _in_kernel(x_ref, mod_ref, g1_ref, w_ref, wf_ref, bf_ref, qx_ref, qg_ref, kg_ref,
               sg_ref, sw_ref, sb_ref, ltri_ref, sel_ref, gmat_ref,
               qa_ref, ka_ref, vat_ref, qf_ref, kf_ref, vft_ref, og_ref, f_ref,
               carry_ref, proj_ref, *, sb_w, fox_w, sgu_w, chunk):
    @pl.when(pl.program_id(1) == 0)
    def _():
        carry_ref[...] = jnp.zeros_like(carry_ref)

    tm = x_ref.shape[1]
    x = x_ref[0]
    ms = jnp.mean(x * x, axis=-1, keepdims=True)
    h = x * lax.rsqrt(ms + EPS) * g1_ref[...]
    h = h * (1.0 + mod_ref[0, 1:2, :]) + mod_ref[0, 0:1, :]
    hb = h.astype(BF16)
    scale = HEAD_DIM ** -0.5
    n_attn = 3 * sb_w + 3 * fox_w
    wn = w_ref.shape[1]
    n_heads = f_ref.shape[1]
    lane = lax.broadcasted_iota(jnp.int32, (1, LANES), 1)
    head_lane = lane < n_heads
    gmat = gmat_ref[...]

    def pack3(parts):
        a, b, c = (p.astype(F32) for p in parts)
        return (a + pltpu.roll(b, PART_STRIDE, 1) + pltpu.roll(c, 2 * PART_STRIDE, 1)).astype(BF16)

    def project(lo, hi):
        proj_ref[:, lo:hi] = _dot(hb, w_ref[:, lo:hi])

    o_q, o_k, o_v = 3 * sb_w, 3 * sb_w + fox_w, 3 * sb_w + 2 * fox_w
    project(n_attn, wn)
    fl = _dot(hb, wf_ref[...])
    project(o_q, o_k)
    gu = _gelu_tanh(proj_ref[:, n_attn:n_attn + sgu_w])
    gv = _gelu_tanh(proj_ref[:, n_attn + sgu_w:wn])
    project(o_k, o_v)
    logf = jnp.where(head_lane, _log_sigmoid(fl + bf_ref[...]), 0.0)
    cp = _dot(ltri_ref[...], pack3(_split3(logf)))
    vn_all = _head_rmsnorm(gv, gmat, sg_ref[...]).astype(BF16)
    q_tiles = [_head_rmsnorm(proj_ref[:, o_q + j * MXU_DIM:o_q + (j + 1) * MXU_DIM], gmat, qg_ref[...]) * scale
               for j in range(fox_w // MXU_DIM)]
    project(o_v, n_attn)

    cum = cp + pltpu.roll(cp, LANES - PART_STRIDE, 1) + pltpu.roll(cp, LANES - 2 * PART_STRIDE, 1)
    cum = jnp.where(head_lane, cum, 0.0) + carry_ref[0:1, :]
    carry_ref[0:1, :] = cum[tm - 1:tm, :]
    f_ref[0] = cum.T[0:n_heads, :]
    extras = _dot(pack3(_split3(cum)), sel_ref[...])
    k_tiles = [_head_rmsnorm(proj_ref[:, o_k + j * MXU_DIM:o_k + (j + 1) * MXU_DIM], gmat, kg_ref[...])
               for j in range(fox_w // MXU_DIM)]

    lane_c = lax.broadcasted_iota(jnp.int32, (chunk, LANES), 1)
    rr = lax.broadcasted_iota(jnp.int32, (chunk, chunk), 0)
    cs = lax.broadcasted_iota(jnp.int32, (chunk, chunk), 1)
    wt = [jnp.where(rr >= cs, sw_ref[g], 0.0).astype(BF16) for g in range(sw_ref.shape[0])]
    for p in range(sgu_w // LANES):
        vn = vn_all[:, p * LANES:(p + 1) * LANES]
        for ci in range(tm // chunk):
            vblk = vn[ci * chunk:(ci + 1) * chunk, :]
            mixed = jnp.where(lane_c < HEAD_DIM, _dot(wt[2 * p], vblk), _dot(wt[2 * p + 1], vblk))
            mixed = mixed + sb_ref[:, p * LANES:(p + 1) * LANES]
            og_ref[0, ci * chunk:(ci + 1) * chunk, p * LANES:(p + 1) * LANES] = (
                gu[ci * chunk:(ci + 1) * chunk, p * LANES:(p + 1) * LANES] * mixed).astype(BF16)

    project(0, o_q)
    qa_ref[0] = (proj_ref[:, 0:sb_w] * (scale * LOG2_E)).astype(BF16)
    ka_ref[0] = proj_ref[:, sb_w:2 * sb_w].astype(BF16)
    vat_ref[0] = proj_ref[:, 2 * sb_w:3 * sb_w].T.astype(BF16)

    is_head = lane < HEAD_DIM
    q_extra = qx_ref[...]
    k_extra = jnp.where((lane >= HEAD_DIM + 3) & (lane < HEAD_DIM + 7), 1.0, 0.0)
    heads_per_mxu = MXU_DIM // HEAD_DIM

    def head_tile(tn, s):
        half = tn[:, (s // 2) * LANES:(s // 2 + 1) * LANES]
        return half if s % 2 == 0 else pltpu.roll(half, HEAD_DIM, 1)

    for j in range(fox_w // MXU_DIM):
        for s in range(heads_per_mxu):
            hh = heads_per_mxu * j + s
            xq = extras[:, (n_heads + hh) * LANES:(n_heads + hh + 1) * LANES] + q_extra
            qf_ref[0, :, hh * LANES:(hh + 1) * LANES] = jnp.where(
                is_head, head_tile(q_tiles[j], s), xq).astype(BF16)
            xk = extras[:, hh * LANES:(hh + 1) * LANES] + k_extra
            kf_ref[0, :, hh * LANES:(hh + 1) * LANES] = jnp.where(
                is_head, head_tile(k_tiles[j], s), xk).astype(BF16)
    vt = proj_ref[:, o_v:n_attn].T.astype(BF16)
    ones = jnp.ones((V_ROWS - HEAD_DIM, tm), BF16)
    for hh in range(fox_w // HEAD_DIM):
        vft_ref[0, hh * V_ROWS:hh * V_ROWS + HEAD_DIM, :] = vt[hh * HEAD_DIM:(hh + 1) * HEAD_DIM, :]
        vft_ref[0, hh * V_ROWS + HEAD_DIM:(hh + 1) * V_ROWS, :] = ones


def _in_proj(x, mod, g1, w, wf, bf, qx, qg, kg, sg, sw, sb, *, layer, tm, sb_w, fox_w, sgu_w):
    bsz, s, d = x.shape
    chunk = sw.shape[-1]
    wn = w.shape[2]
    fox_heads = fox_w // HEAD_DIM

    def of_layer(a):
        nd = a.ndim - 1
        return pl.BlockSpec((None,) + a.shape[1:], lambda b, i: (layer,) + (0,) * nd)

    assert fox_heads <= PART_STRIDE and fox_w % MXU_DIM == 0 and sgu_w == MXU_DIM
    assert s % tm == 0 and tm % chunk == 0
    tok_idx = jnp.arange(tm)
    ltri = (tok_idx[:, None] >= tok_idx[None, :]).astype(BF16)
    const2 = lambda b, i: (0, 0)
    tok = lambda b, i: (b, i, 0)
    tok_t = lambda b, i: (b, 0, i)
    kern = functools.partial(_in_kernel, sb_w=sb_w, fox_w=fox_w, sgu_w=sgu_w, chunk=chunk)
    out_shape = [
        jax.ShapeDtypeStruct((bsz, s, sb_w), BF16),
        jax.ShapeDtypeStruct((bsz, s, sb_w), BF16),
        jax.ShapeDtypeStruct((bsz, sb_w, s), BF16),
        jax.ShapeDtypeStruct((bsz, s, fox_heads * LANES), BF16),
        jax.ShapeDtypeStruct((bsz, s, fox_heads * LANES), BF16),
        jax.ShapeDtypeStruct((bsz, fox_heads * V_ROWS, s), BF16),
        jax.ShapeDtypeStruct((bsz, s, sgu_w), BF16),
        jax.ShapeDtypeStruct((bsz, fox_heads, s), F32),
    ]
    out_specs = [
        pl.BlockSpec((1, tm, sb_w), tok),
        pl.BlockSpec((1, tm, sb_w), tok),
        pl.BlockSpec((1, sb_w, tm), tok_t),
        pl.BlockSpec((1, tm, fox_heads * LANES), tok),
        pl.BlockSpec((1, tm, fox_heads * LANES), tok),
        pl.BlockSpec((1, fox_heads * V_ROWS, tm), tok_t),
        pl.BlockSpec((1, tm, sgu_w), tok),
        pl.BlockSpec((1, fox_heads, tm), tok_t),
    ]
    return pl.pallas_call(
        kern,
        grid=(bsz, s // tm),
        in_specs=[
            pl.BlockSpec((1, tm, d), tok),
            pl.BlockSpec((None, 1, 6, d), lambda b, i: (layer, b, 0, 0)),
            of_layer(g1), of_layer(w), of_layer(wf), of_layer(bf), of_layer(qx), of_layer(qg),
            of_layer(kg), of_layer(sg), of_layer(sw), of_layer(sb),
            pl.BlockSpec((tm, tm), const2),
            pl.BlockSpec((LANES, 2 * fox_heads * LANES), const2),
            pl.BlockSpec((MXU_DIM, MXU_DIM), const2),
        ],
        out_specs=out_specs,
        out_shape=out_shape,
        scratch_shapes=[pltpu.VMEM((8, LANES), F32), pltpu.VMEM((tm, wn), F32)],
        compiler_params=pltpu.CompilerParams(
            dimension_semantics=("arbitrary", "arbitrary"), vmem_limit_bytes=VMEM_LIMIT),
        name="in_proj",
    )(x, mod, g1, w, wf, bf, qx, qg, kg, sg, sw, sb, ltri, _placement_matrix(fox_heads),
      _group_mean_matrix())


def _sb_kernel(q_ref, k_ref, vt_ref, after_ref, o_ref, acc_ref, c_ref, *, tq):
    tk = tq
    n_blk = q_ref.shape[1] // tq
    lane = lax.broadcasted_iota(jnp.int32, (1, LANES), 1)
    key = lax.broadcasted_iota(jnp.int32, (tk, tq), 0)
    qry = lax.broadcasted_iota(jnp.int32, (tk, tq), 1)
    causal = key < qry
    after = after_ref[...]

    def process(chains):
        units, starts, qh = [], {}, {}
        for slot, qi, blocks, first in chains:
            q = q_ref[0, pl.ds(_block_start(qi, tq), tq), :]
            qh[slot] = (jnp.where(lane < HEAD_DIM, q, 0).astype(BF16),
                        jnp.where(lane >= HEAD_DIM, q, 0).astype(BF16))
            for b, j in enumerate(blocks):
                starts[slot, b] = _block_start(j, tk)
                units += [(slot, b, h, first and b == 0) for h in range(2)]
        z = {(sl, b, h): _dot_nt(k_ref[0, pl.ds(starts[sl, b], tk), :], qh[sl][h])
             for sl, b, h, _ in units}
        l1mb, head = {}, {}
        for sl, b, h, diag in units:
            zz = z[sl, b, h]
            nz = -zz
            lg = jnp.minimum(nz, 0.0) - jnp.log2(1.0 + jnp.exp2(jnp.minimum(zz, nz)))
            if diag:
                lg = jnp.where(causal, lg, 0.0)
            l1mb[sl, b, h] = lg.astype(BF16)
            head[sl, b, h] = (zz + lg, lg[0:1, :])
        between = {(sl, b, h): _dot(after, l1mb[sl, b, h]) for sl, b, h, _ in units}
        a = {}
        for slot, qi, blocks, first in chains:
            for h in range(2):
                c = None if first else c_ref[slot, h]
                for b in range(len(blocks)):
                    e = head[slot, b, h][0] + between[slot, b, h]
                    if c is not None:
                        e = e + c
                    w = jnp.exp2(e)
                    if first and b == 0:
                        w = jnp.where(causal, w, 0.0)
                    a[slot, b, h] = w.astype(BF16)
                    block_sum = between[slot, b, h][0:1, :] + head[slot, b, h][1]
                    c = block_sum if c is None else c + block_sum
                c_ref[slot, h] = c
        for slot, qi, blocks, first in chains:
            for h in range(2):
                pv = None
                for b in range(len(blocks)):
                    part = _dot(vt_ref[0, h * HEAD_DIM:(h + 1) * HEAD_DIM, pl.ds(starts[slot, b], tk)],
                                a[slot, b, h])
                    pv = part if pv is None else pv + part
                if first:
                    acc_ref[slot, h] = pv
                else:
                    acc_ref[slot, h] += pv

    def carry_max(slot):
        cm = jnp.maximum(c_ref[slot, 0], c_ref[slot, 1])
        return jnp.max(cm, axis=1, keepdims=True)[0, 0]

    def finish(slot, qi):
        def cond(carry):
            j, cmax = carry
            return jnp.logical_and(j >= 0, cmax > EXP_ZERO * LOG2_E)

        def body(carry):
            j, _ = carry
            process([(slot, qi, [j], False)])
            return j - 1, carry_max(slot)

        lax.while_loop(cond, body, (jnp.asarray(qi - 2, jnp.int32), carry_max(slot)))
        o_ref[0, pl.ds(_block_start(qi, tq), tq), :] = (
            jnp.concatenate([acc_ref[slot, 0], acc_ref[slot, 1]], axis=0).T.astype(o_ref.dtype))

    process([(0, 0, [0], True)])
    finish(0, 0)

    def two_blocks(g, carry):
        qa, qb = 1 + 2 * g, 2 + 2 * g
        process([(0, qa, [qa, qa - 1], True), (1, qb, [qb, qb - 1], True)])
        finish(0, qa)
        finish(1, qb)
        return carry

    lax.fori_loop(0, (n_blk - 1) // 2, two_blocks, 0)
    if (n_blk - 1) % 2:
        last = n_blk - 1
        process([(0, last, [last, last - 1], True)])
        finish(0, last)


def _sb_attention(q, k, vt, *, tq):
    bsz, s, w = q.shape
    assert s >= 2 * tq
    idx = jnp.arange(tq)
    after = (idx[None, :] > idx[:, None]).astype(BF16)
    kern = functools.partial(_sb_kernel, tq=tq)
    return pl.pallas_call(
        kern,
        grid=(bsz, w // LANES),
        in_specs=[
            pl.BlockSpec((1, s, LANES), lambda b, p: (b, 0, p)),
            pl.BlockSpec((1, s, LANES), lambda b, p: (b, 0, p)),
            pl.BlockSpec((1, LANES, s), lambda b, p: (b, p, 0)),
            pl.BlockSpec((tq, tq), lambda b, p: (0, 0)),
        ],
        out_specs=pl.BlockSpec((1, s, LANES), lambda b, p: (b, 0, p)),
        out_shape=jax.ShapeDtypeStruct((bsz, s, w), BF16),
        scratch_shapes=[pltpu.VMEM((2, 2, HEAD_DIM, tq), F32), pltpu.VMEM((2, 2, 1, tq), F32)],
        compiler_params=pltpu.CompilerParams(
            dimension_semantics=("arbitrary", "arbitrary"), vmem_limit_bytes=VMEM_LIMIT),
        name="sb_attn",
    )(q, k, vt, after)


def _fox_kernel(fend_ref, par_ref, q_ref, k_ref, vt_ref, o_ref, acc_ref, m_ref, p_ref, *, tq, n_heads):
    tk = tq
    n_blk = q_ref.shape[1] // tq
    head0 = (pl.program_id(0) * n_heads + 2 * pl.program_id(1))

    def last_dead_block(head, qi, j_prev):
        base = head * n_blk
        f_q = fend_ref[base + jnp.maximum(qi - 1, 0)]

        def next_is_dead(j):
            jn = jnp.minimum(j + 1, n_blk - 1)
            return jnp.logical_and(j + 1 < qi, f_q - fend_ref[base + jn] < -par_ref[0])

        return lax.while_loop(next_is_dead, lambda j: j + 1, j_prev)

    def plan(qi, dead):
        dead = (last_dead_block(head0, qi, dead[0]), last_dead_block(head0 + 1, qi, dead[1]))
        j_dead = jnp.minimum(dead[0], dead[1])
        n_left = jnp.maximum(qi - 2 - j_dead, 0)
        odd = n_left % 2
        has_dead_below = j_dead >= 0
        n_pairs = n_left // 2 + jnp.where(has_dead_below, odd, 0)
        return n_pairs, jnp.logical_and(odd == 1, jnp.logical_not(has_dead_below)), dead

    def q_tile(qi, h):
        return q_ref[0, pl.ds(pl.multiple_of(qi * tq, tq), tq), h * LANES:(h + 1) * LANES]

    def finalize(qi):
        outs = []
        for h in range(2):
            acc = acc_ref[h]
            outs.append(acc[0:HEAD_DIM, :] / acc[HEAD_DIM:HEAD_DIM + 1, :])
        o_ref[0, pl.ds(pl.multiple_of(qi * tq, tq), tq), :] = (
            jnp.concatenate(outs, axis=0).T.astype(o_ref.dtype))

    key = lax.broadcasted_iota(jnp.int32, (tk, tq), 0)
    qry = lax.broadcasted_iota(jnp.int32, (tk, tq), 1)
    causal = key <= qry

    def process(qi, blocks, first):
        starts = [pl.multiple_of(j * tk, tk) for j in blocks]
        units = [(b, h) for b in range(len(blocks)) for h in range(2)]
        s = {(b, h): _dot_nt(k_ref[0, pl.ds(starts[b], tk), h * LANES:(h + 1) * LANES],
                             q_tile(qi, h)) for b, h in units}
        p = {}
        alpha = {}
        for h in range(2):
            if first:
                s[0, h] = jnp.where(causal, s[0, h], NEG_BIG)
            m_new = None if first else m_ref[h]
            for b in range(len(blocks)):
                mb = jnp.max(s[b, h], axis=0, keepdims=True)
                m_new = mb if m_new is None else jnp.maximum(m_new, mb)
            if not first:
                alpha[h] = jnp.exp(m_ref[h] - m_new)
            for b in range(len(blocks)):
                p[b, h] = jnp.exp(s[b, h] - m_new).astype(BF16)
            m_ref[h] = m_new
        for h in range(2):
            pv = None
            for b in range(len(blocks)):
                part = _dot(vt_ref[0, h * V_ROWS:(h + 1) * V_ROWS, pl.ds(starts[b], tk)], p[b, h])
                pv = part if pv is None else pv + part
            acc_ref[h] = pv if first else alpha[h] * acc_ref[h] + pv

    def online_path():
        def q_block(qi, dead):
            n_pairs, lone_block0, dead = plan(qi, dead)

            @pl.when(qi == 0)
            def _():
                process(qi, [qi], True)

            @pl.when(qi > 0)
            def _():
                process(qi, [qi, qi - 1], True)

            def body(i, c):
                j = qi - 2 - 2 * i
                process(qi, [j, j - 1], False)
                return c

            lax.fori_loop(0, n_pairs, body, 0)

            @pl.when(lone_block0)
            def _():
                process(qi, [0], False)

            finalize(qi)
            return dead

        lax.fori_loop(0, n_blk, q_block, (jnp.int32(-1), jnp.int32(-1)))

    win = 2 * tk
    wrow = lax.broadcasted_iota(jnp.int32, (win, tq), 0)
    wcol = lax.broadcasted_iota(jnp.int32, (win, tq), 1)

    def window(qi, i):
        jl = qi - 1 - 2 * i
        j_lo = jnp.maximum(jl, 0)
        return j_lo, pl.multiple_of(j_lo * tk, tk), jnp.where(jl < 0, tk, win)

    def stage_a(qi, i, slot, first):
        j_lo, start, row_lim = window(qi, i)
        keep = wrow < row_lim
        if first:
            keep = jnp.logical_and(keep, wrow <= wcol + (qi - j_lo) * tk)
        for h in range(2):
            s = _dot_nt(k_ref[0, pl.ds(start, win), h * LANES:(h + 1) * LANES], q_tile(qi, h))
            p_ref[slot, h] = jnp.exp(jnp.where(keep, s, NEG_BIG)).astype(BF16)

    def stage_c(qi, i, slot):
        _, start, _ = window(qi, i)
        for h in range(2):
            acc_ref[h] += _dot(vt_ref[0, h * V_ROWS:(h + 1) * V_ROWS, pl.ds(start, win)], p_ref[slot, h])

    def bounded_path():
        acc_ref[...] = jnp.zeros_like(acc_ref)
        stage_a(0, 0, 0, True)

        def q_block(qi, carry):
            t, dead = carry[0], carry[1:]
            n_pairs, lone_block0, dead = plan(qi, dead)
            n_stages = 1 + n_pairs + jnp.where(lone_block0, 1, 0)

            def body(i, t):
                stage_c(qi, i - 1, t & 1)
                stage_a(qi, i, (t + 1) & 1, False)
                return t + 1

            t = lax.fori_loop(1, n_stages, body, t)
            stage_c(qi, n_stages - 1, t & 1)
            stage_a(jnp.minimum(qi + 1, n_blk - 1), 0, (t + 1) & 1, True)
            finalize(qi)
            acc_ref[...] = jnp.zeros_like(acc_ref)
            return (t + 1,) + dead

        lax.fori_loop(0, n_blk, q_block, (jnp.int32(0), jnp.int32(-1), jnp.int32(-1)))

    bounded = par_ref[1] > 0.5
    pl.when(bounded)(bounded_path)
    pl.when(jnp.logical_not(bounded))(online_path)


def _fox_attention(q, k, vt, f, qk_bound, *, tq):
    bsz, s, w = q.shape
    pair = 2 * LANES
    n_heads = w // LANES
    assert s >= 2 * tq
    f_end = f[:, :, tq - 1::tq].reshape(-1)
    bounded = qk_bound <= MAX_UNSTABILISED_LOGIT
    stabiliser_slack = 0.5
    par = jnp.stack([jnp.where(bounded, stabiliser_slack - EXP_ZERO, 2.0 * qk_bound - EXP_ZERO),
                     jnp.where(bounded, 1.0, 0.0)]).astype(F32)
    kern = functools.partial(_fox_kernel, tq=tq, n_heads=n_heads)
    grid_spec = pltpu.PrefetchScalarGridSpec(
        num_scalar_prefetch=2,
        grid=(bsz, w // pair),
        in_specs=[
            pl.BlockSpec((1, s, pair), lambda b, p, fe, th: (b, 0, p)),
            pl.BlockSpec((1, s, pair), lambda b, p, fe, th: (b, 0, p)),
            pl.BlockSpec((1, 2 * V_ROWS, s), lambda b, p, fe, th: (b, p, 0)),
        ],
        out_specs=pl.BlockSpec((1, s, LANES), lambda b, p, fe, th: (b, 0, p)),
        scratch_shapes=[pltpu.VMEM((2, V_ROWS, tq), F32), pltpu.VMEM((2, 1, tq), F32),
                        pltpu.VMEM((2, 2, 2 * tq, tq), BF16)],
    )
    return pl.pallas_call(
        kern,
        grid_spec=grid_spec,
        out_shape=jax.ShapeDtypeStruct((bsz, s, w // 2), BF16),
        compiler_params=pltpu.CompilerParams(
            dimension_semantics=("arbitrary", "arbitrary"), vmem_limit_bytes=VMEM_LIMIT),
        name="fox_attn",
    )(f_end, par, q, k, vt)


def _out_kernel(x_ref, osb_ref, ofox_ref, osgu_ref, mod_ref, g2_ref, wo_ref, w1_ref, w2_ref,
                o_ref, *, ff_chunk):
    sb_w = osb_ref.shape[2]
    fox_w = ofox_ref.shape[2]
    x = x_ref[0]
    mix = (_dot(osb_ref[0], wo_ref[0:sb_w, :])
           + _dot(ofox_ref[0], wo_ref[sb_w:sb_w + fox_w, :])
           + _dot(osgu_ref[0], wo_ref[sb_w + fox_w:, :]))
    x1 = x + mod_ref[0, 2:3, :] * mix
    ms = jnp.mean(x1 * x1, axis=-1, keepdims=True)
    h = x1 * lax.rsqrt(ms + EPS) * g2_ref[...]
    hb = (h * (1.0 + mod_ref[0, 4:5, :]) + mod_ref[0, 3:4, :]).astype(BF16)
    d_ff = w1_ref.shape[1]
    acc = None
    for c in range(d_ff // ff_chunk):
        hid = jnp.maximum(_dot(hb, w1_ref[:, c * ff_chunk:(c + 1) * ff_chunk]), 0.0)
        part = _dot((hid * hid).astype(BF16), w2_ref[c * ff_chunk:(c + 1) * ff_chunk, :])
        acc = part if acc is None else acc + part
    o_ref[0] = x1 + mod_ref[0, 5:6, :] * acc


def _out_mlp(x, osb, ofox, osgu, mod, g2, wo, w1, w2, *, layer, tm):
    bsz, s, d = x.shape
    tok = lambda b, i: (b, i, 0)

    def of_layer(a, **kw):
        nd = a.ndim - 1
        return pl.BlockSpec((None,) + a.shape[1:], lambda b, i: (layer,) + (0,) * nd, **kw)

    single = pl.Buffered(1)
    kern = functools.partial(_out_kernel, ff_chunk=FF_CHUNK)
    return pl.pallas_call(
        kern,
        grid=(bsz, s // tm),
        in_specs=[
            pl.BlockSpec((1, tm, d), tok),
            pl.BlockSpec((1, tm, osb.shape[2]), tok),
            pl.BlockSpec((1, tm, ofox.shape[2]), tok),
            pl.BlockSpec((1, tm, osgu.shape[2]), tok),
            pl.BlockSpec((None, 1, 6, d), lambda b, i: (layer, b, 0, 0)),
            of_layer(g2),
            of_layer(wo, pipeline_mode=single),
            of_layer(w1, pipeline_mode=single),
            of_layer(w2, pipeline_mode=single),
        ],
        out_specs=pl.BlockSpec((1, tm, d), tok),
        out_shape=jax.ShapeDtypeStruct((bsz, s, d), F32),
        compiler_params=pltpu.CompilerParams(
            dimension_semantics=("arbitrary", "arbitrary"), vmem_limit_bytes=VMEM_LIMIT),
        name="out_mlp",
    )(x, osb, ofox, osgu, mod, g2, wo, w1, w2)


def kernel(x, c, ada_w, ada_b, norm1_g, norm2_g, w_in, b_forget, q_norm_g, k_norm_g, sgu_norm_g,
           sgu_w, sgu_b, w_out, mlp_w1, mlp_w2):
    depth, d, _ = ada_w.shape
    bsz, s, _ = x.shape
    fox_heads = b_forget.shape[1]
    fox_w = fox_heads * HEAD_DIM
    sgu_groups, chunk = sgu_b.shape[1], sgu_b.shape[2]
    sgu_wd = sgu_groups * sgu_norm_g.shape[2]
    sb_w = (w_in.shape[2] - 3 * fox_w - fox_heads - 2 * sgu_wd) // 3
    f_lo = 3 * sb_w + 3 * fox_w
    assert s % TOKEN_TILE == 0 and s % (2 * ATTN_BLOCK) == 0 and mlp_w1.shape[2] % FF_CHUNK == 0
    assert sb_w % LANES == 0 and ada_w.shape[2] % MOD_COLS == 0

    mod = _modulation(c, ada_w, ada_b).reshape(depth, bsz, 6, d)

    perm = jnp.argsort(b_forget, axis=1)

    def relabel_cols(cols):
        t = cols.reshape(depth, d, fox_heads, HEAD_DIM)
        return jnp.take_along_axis(t, perm[:, None, :, None], axis=2).reshape(depth, d, fox_w)

    o = 3 * sb_w
    w_bf = w_in.astype(BF16)
    fox_cols = [relabel_cols(w_bf[:, :, o + i * fox_w:o + (i + 1) * fox_w]) for i in range(3)]
    w = jnp.concatenate([w_bf[:, :, :o]] + fox_cols + [w_bf[:, :, f_lo + fox_heads:]], axis=2)
    wf = jnp.take_along_axis(w_in[:, :, f_lo:f_lo + fox_heads], perm[:, None, :], axis=2)
    wf = jnp.pad(wf, ((0, 0), (0, 0), (0, LANES - fox_heads))).astype(BF16)
    bf = jnp.pad(jnp.take_along_axis(b_forget, perm, axis=1), ((0, 0), (0, LANES - fox_heads)))
    bf = bf.reshape(depth, 1, LANES)
    qg = jnp.tile(q_norm_g, (1, MXU_DIM // HEAD_DIM)).reshape(depth, 1, MXU_DIM)
    kg = jnp.tile(k_norm_g, (1, MXU_DIM // HEAD_DIM)).reshape(depth, 1, MXU_DIM)
    sg = sgu_norm_g.reshape(depth, 1, sgu_wd)
    sb = jnp.repeat(jnp.swapaxes(sgu_b, 1, 2), sgu_norm_g.shape[2], axis=2)
    qk_bound = ((BF16_MARGIN * HEAD_DIM ** 0.5) * jnp.max(jnp.abs(q_norm_g), axis=1)
                * jnp.max(jnp.abs(k_norm_g), axis=1))
    qx = jnp.zeros((depth, 1, LANES), F32).at[:, 0, HEAD_DIM:HEAD_DIM + 3].set(-1.0)
    qx = qx.at[:, 0, HEAD_DIM + 6].set(-qk_bound)
    wo_fox = w_out[:, sb_w:sb_w + fox_w].reshape(depth, fox_heads, HEAD_DIM, d)
    wo_fox = jnp.take_along_axis(wo_fox, perm[:, :, None, None], axis=1).reshape(depth, fox_w, d)
    wo = jnp.concatenate([w_out[:, :sb_w], wo_fox, w_out[:, sb_w + fox_w:]], axis=1).astype(BF16)
    w1 = mlp_w1.astype(BF16)
    w2 = mlp_w2.astype(BF16)
    g1 = norm1_g.reshape(depth, 1, d)
    g2 = norm2_g.reshape(depth, 1, d)

    for l in range(depth):
        qa, ka, vat, qf, kf, vft, osgu, f = _in_proj(
            x, mod, g1, w, wf, bf, qx, qg, kg, sg, sgu_w, sb,
            layer=l, tm=TOKEN_TILE, sb_w=sb_w, fox_w=fox_w, sgu_w=sgu_wd)
        osb = _sb_attention(qa, ka, vat, tq=ATTN_BLOCK)
        ofox = _fox_attention(qf, kf, vft, f, qk_bound[l], tq=ATTN_BLOCK)
        x = _out_mlp(x, osb, ofox, osgu, mod, g2, wo, w1, w2, layer=l, tm=2 * TOKEN_TILE)
    return x
```

```python
import functools
import math

import jax
import jax.numpy as jnp
from jax import lax
from jax.experimental import pallas as pl
from jax.experimental.pallas import tpu as pltpu

HEAD_DIM = 64
LANES = 128
MXU_DIM = 256
V_ROWS = 2 * HEAD_DIM
PART_STRIDE = 8
BF16_MARGIN = 1.03

TOKEN_TILE = 512
ATTN_BLOCK = MXU_DIM
MOD_COLS = 1536
FF_CHUNK = 1024
EPS = 1e-6
NEG_BIG = -1e30
EXP_ZERO = -104.0
LOG2_E = math.log2(math.e)
MAX_UNSTABILISED_LOGIT = 40.0
VMEM_LIMIT = 56 * 1024 * 1024

F32 = jnp.float32
BF16 = jnp.bfloat16


def _dot(a, b):
    return jnp.dot(a, b, preferred_element_type=F32)


def _dot_nt(a, b):
    return lax.dot_general(a, b, (((1,), (1,)), ((), ())), preferred_element_type=F32)


def _block_start(j, size):
    return j * size if isinstance(j, int) else pl.multiple_of(j * size, size)


def _split3(x):
    hi = x.astype(BF16)
    r = x - hi.astype(F32)
    mid = r.astype(BF16)
    lo = (r - mid.astype(F32)).astype(BF16)
    return hi, mid, lo


def _group_mean_matrix():
    head = jnp.arange(MXU_DIM) // HEAD_DIM
    return jnp.where(head[:, None] == head[None, :], 1.0 / HEAD_DIM, 0.0).astype(BF16)


def _placement_matrix(n_heads):
    row = jnp.arange(LANES)[:, None]
    col = jnp.arange(2 * n_heads * LANES)[None, :]
    tile = col // LANES
    is_q = (tile >= n_heads).astype(jnp.int32)
    part = col % LANES - HEAD_DIM - 3 * is_q
    hit = (part >= 0) & (part < 3) & (row == part * PART_STRIDE + tile - n_heads * is_q)
    return hit.astype(BF16)


def _head_rmsnorm(t, gmat, gain):
    ms = _dot((t * t).astype(BF16), gmat)
    return t * lax.rsqrt(ms + EPS) * gain


def _gelu_tanh(x):
    c = math.sqrt(2.0 / math.pi)
    return x * (0.5 * (1.0 + jnp.tanh(c * (x + 0.044715 * (x * x * x)))))


def _log_sigmoid(x):
    return jnp.minimum(x, 0.0) - jnp.log(1.0 + jnp.exp(-jnp.abs(x)))


def _mod_kernel(ct_ref, w_ref, b_ref, o_ref):
    ct = ct_ref[...]
    cond = ct * (1.0 / (1.0 + jnp.exp(-ct)))
    w = w_ref[0]
    rows = [jnp.sum(cond[:, b:b + 1] * w, axis=0, keepdims=True) for b in range(ct.shape[1])]
    o_ref[0] = jnp.concatenate(rows, axis=0) + b_ref[0]


def _modulation(c, ada_w, ada_b):
    depth, d, n = ada_w.shape
    bsz = c.shape[0]
    tn = MOD_COLS
    return pl.pallas_call(
        _mod_kernel,
        grid=(depth, n // tn),
        in_specs=[
            pl.BlockSpec((d, bsz), lambda l, j: (0, 0)),
            pl.BlockSpec((1, d, tn), lambda l, j: (l, 0, j)),
            pl.BlockSpec((1, 1, tn), lambda l, j: (l, 0, j)),
        ],
        out_specs=pl.BlockSpec((1, bsz, tn), lambda l, j: (l, 0, j)),
        out_shape=jax.ShapeDtypeStruct((depth, bsz, n), F32),
        compiler_params=pltpu.CompilerParams(
            dimension_semantics=("arbitrary", "arbitrary"), vmem_limit_bytes=VMEM_LIMIT),
        name="adaln_mod",
    )(c.T, ada_w, ada_b.reshape(depth, 1, n))


def _in_kernel(x_ref, mod_ref, g1_ref, w_ref, wf_ref, bf_ref, qx_ref, qg_ref, kg_ref,
               sg_ref, sw_ref, sb_ref, ltri_ref, sel_ref, gmat_ref,
               qa_ref, ka_ref, vat_ref, qf_ref, kf_ref, vft_ref, og_ref, f_ref,
               carry_ref, proj_ref, *, sb_w, fox_w, sgu_w, chunk):
    @pl.when(pl.program_id(1) == 0)
    def _():
        carry_ref[...] = jnp.zeros_like(carry_ref)

    tm = x_ref.shape[1]
    x = x_ref[0]
    ms = jnp.mean(x * x, axis=-1, keepdims=True)
    h = x * lax.rsqrt(ms + EPS) * g1_ref[...]
    h = h * (1.0 + mod_ref[0, 1:2, :]) + mod_ref[0, 0:1, :]
    hb = h.astype(BF16)
    scale = HEAD_DIM ** -0.5
    n_attn = 3 * sb_w + 3 * fox_w
    wn = w_ref.shape[1]
    n_heads = f_ref.shape[1]
    lane = lax.broadcasted_iota(jnp.int32, (1, LANES), 1)
    head_lane = lane < n_heads
    gmat = gmat_ref[...]

    def pack3(parts):
        a, b, c = (p.astype(F32) for p in parts)
        return (a + pltpu.roll(b, PART_STRIDE, 1) + pltpu.roll(c, 2 * PART_STRIDE, 1)).astype(BF16)

    def project(lo, hi):
        proj_ref[:, lo:hi] = _dot(hb, w_ref[:, lo:hi])

    o_q, o_k, o_v = 3 * sb_w, 3 * sb_w + fox_w, 3 * sb_w + 2 * fox_w
    project(n_attn, wn)
    fl = _dot(hb, wf_ref[...])
    project(o_q, o_k)
    gu = _gelu_tanh(proj_ref[:, n_attn:n_attn + sgu_w])
    gv = _gelu_tanh(proj_ref[:, n_attn + sgu_w:wn])
    project(o_k, o_v)
    logf = jnp.where(head_lane, _log_sigmoid(fl + bf_ref[...]), 0.0)
    cp = _dot(ltri_ref[...], pack3(_split3(logf)))
    vn_all = _head_rmsnorm(gv, gmat, sg_ref[...]).astype(BF16)
    q_tiles = [_head_rmsnorm(proj_ref[:, o_q + j * MXU_DIM:o_q + (j + 1) * MXU_DIM], gmat, qg_ref[...]) * scale
               for j in range(fox_w // MXU_DIM)]
    project(o_v, n_attn)

    cum = cp + pltpu.roll(cp, LANES - PART_STRIDE, 1) + pltpu.roll(cp, LANES - 2 * PART_STRIDE, 1)
    cum = jnp.where(head_lane, cum, 0.0) + carry_ref[0:1, :]
    carry_ref[0:1, :] = cum[tm - 1:tm, :]
    f_ref[0] = cum.T[0:n_heads, :]
    extras = _dot(pack3(_split3(cum)), sel_ref[...])
    k_tiles = [_head_rmsnorm(proj_ref[:, o_k + j * MXU_DIM:o_k + (j + 1) * MXU_DIM], gmat, kg_ref[...])
               for j in range(fox_w // MXU_DIM)]

    lane_c = lax.broadcasted_iota(jnp.int32, (chunk, LANES), 1)
    rr = lax.broadcasted_iota(jnp.int32, (chunk, chunk), 0)
    cs = lax.broadcasted_iota(jnp.int32, (chunk, chunk), 1)
    wt = [jnp.where(rr >= cs, sw_ref[g], 0.0).astype(BF16) for g in range(sw_ref.shape[0])]
    for p in range(sgu_w // LANES):
        vn = vn_all[:, p * LANES:(p + 1) * LANES]
        for ci in range(tm // chunk):
            vblk = vn[ci * chunk:(ci + 1) * chunk, :]
            mixed = jnp.where(lane_c < HEAD_DIM, _dot(wt[2 * p], vblk), _dot(wt[2 * p + 1], vblk))
            mixed = mixed + sb_ref[:, p * LANES:(p + 1) * LANES]
            og_ref[0, ci * chunk:(ci + 1) * chunk, p * LANES:(p + 1) * LANES] = (
                gu[ci * chunk:(ci + 1) * chunk, p * LANES:(p + 1) * LANES] * mixed).astype(BF16)

    project(0, o_q)
    qa_ref[0] = (proj_ref[:, 0:sb_w] * (scale * LOG2_E)).astype(BF16)
    ka_ref[0] = proj_ref[:, sb_w:2 * sb_w].astype(BF16)
    vat_ref[0] = proj_ref[:, 2 * sb_w:3 * sb_w].T.astype(BF16)

    is_head = lane < HEAD_DIM
    q_extra = qx_ref[...]
    k_extra = jnp.where((lane >= HEAD_DIM + 3) & (lane < HEAD_DIM + 7), 1.0, 0.0)
    heads_per_mxu = MXU_DIM // HEAD_DIM

    def head_tile(tn, s):
        half = tn[:, (s // 2) * LANES:(s // 2 + 1) * LANES]
        return half if s % 2 == 0 else pltpu.roll(half, HEAD_DIM, 1)

    for j in range(fox_w // MXU_DIM):
        for s in range(heads_per_mxu):
            hh = heads_per_mxu * j + s
            xq = extras[:, (n_heads + hh) * LANES:(n_heads + hh + 1) * LANES] + q_extra
            qf_ref[0, :, hh * LANES:(hh + 1) * LANES] = jnp.where(
                is_head, head_tile(q_tiles[j], s), xq).astype(BF16)
            xk = extras[:, hh * LANES:(hh + 1) * LANES] + k_extra
            kf_ref[0, :, hh * LANES:(hh + 1) * LANES] = jnp.where(
                is_head, head_tile(k_tiles[j], s), xk).astype(BF16)
    vt = proj_ref[:, o_v:n_attn].T.astype(BF16)
    ones = jnp.ones((V_ROWS - HEAD_DIM, tm), BF16)
    for hh in range(fox_w // HEAD_DIM):
        vft_ref[0, hh * V_ROWS:hh * V_ROWS + HEAD_DIM, :] = vt[hh * HEAD_DIM:(hh + 1) * HEAD_DIM, :]
        vft_ref[0, hh * V_ROWS + HEAD_DIM:(hh + 1) * V_ROWS, :] = ones


def _in_proj(x, mod, g1, w, wf, bf, qx, qg, kg, sg, sw, sb, *, layer, tm, sb_w, fox_w, sgu_w):
    bsz, s, d = x.shape
    chunk = sw.shape[-1]
    wn = w.shape[2]
    fox_heads = fox_w // HEAD_DIM

    def of_layer(a):
        nd = a.ndim - 1
        return pl.BlockSpec((None,) + a.shape[1:], lambda b, i: (layer,) + (0,) * nd)

    assert fox_heads <= PART_STRIDE and fox_w % MXU_DIM == 0 and sgu_w == MXU_DIM
    assert s % tm == 0 and tm % chunk == 0
    tok_idx = jnp.arange(tm)
    ltri = (tok_idx[:, None] >= tok_idx[None, :]).astype(BF16)
    const2 = lambda b, i: (0, 0)
    tok = lambda b, i: (b, i, 0)
    tok_t = lambda b, i: (b, 0, i)
    kern = functools.partial(_in_kernel, sb_w=sb_w, fox_w=fox_w, sgu_w=sgu_w, chunk=chunk)
    out_shape = [
        jax.ShapeDtypeStruct((bsz, s, sb_w), BF16),
        jax.ShapeDtypeStruct((bsz, s, sb_w), BF16),
        jax.ShapeDtypeStruct((bsz, sb_w, s), BF16),
        jax.ShapeDtypeStruct((bsz, s, fox_heads * LANES), BF16),
        jax.ShapeDtypeStruct((bsz, s, fox_heads * LANES), BF16),
        jax.ShapeDtypeStruct((bsz, fox_heads * V_ROWS, s), BF16),
        jax.ShapeDtypeStruct((bsz, s, sgu_w), BF16),
        jax.ShapeDtypeStruct((bsz, fox_heads, s), F32),
    ]
    out_specs = [
        pl.BlockSpec((1, tm, sb_w), tok),
        pl.BlockSpec((1, tm, sb_w), tok),
        pl.BlockSpec((1, sb_w, tm), tok_t),
        pl.BlockSpec((1, tm, fox_heads * LANES), tok),
        pl.BlockSpec((1, tm, fox_heads * LANES), tok),
        pl.BlockSpec((1, fox_heads * V_ROWS, tm), tok_t),
        pl.BlockSpec((1, tm, sgu_w), tok),
        pl.BlockSpec((1, fox_heads, tm), tok_t),
    ]
    return pl.pallas_call(
        kern,
        grid=(bsz, s // tm),
        in_specs=[
            pl.BlockSpec((1, tm, d), tok),
            pl.BlockSpec((None, 1, 6, d), lambda b, i: (layer, b, 0, 0)),
            of_layer(g1), of_layer(w), of_layer(wf), of_layer(bf), of_layer(qx), of_layer(qg),
            of_layer(kg), of_layer(sg), of_layer(sw), of_layer(sb),
            pl.BlockSpec((tm, tm), const2),
            pl.BlockSpec((LANES, 2 * fox_heads * LANES), const2),
            pl.BlockSpec((MXU_DIM, MXU_DIM), const2),
        ],
        out_specs=out_specs,
        out_shape=out_shape,
        scratch_shapes=[pltpu.VMEM((8, LANES), F32), pltpu.VMEM((tm, wn), F32)],
        compiler_params=pltpu.CompilerParams(
            dimension_semantics=("arbitrary", "arbitrary"), vmem_limit_bytes=VMEM_LIMIT),
        name="in_proj",
    )(x, mod, g1, w, wf, bf, qx, qg, kg, sg, sw, sb, ltri, _placement_matrix(fox_heads),
      _group_mean_matrix())


def _sb_kernel(q_ref, k_ref, vt_ref, after_ref, o_ref, acc_ref, c_ref, *, tq):
    tk = tq
    n_blk = q_ref.shape[1] // tq
    lane = lax.broadcasted_iota(jnp.int32, (1, LANES), 1)
    key = lax.broadcasted_iota(jnp.int32, (tk, tq), 0)
    qry = lax.broadcasted_iota(jnp.int32, (tk, tq), 1)
    causal = key < qry
    def process(chains, size=tk):
        after = after_ref[0:size, 0:size]
        units, starts, qh = [], {}, {}
        for slot, qi, blocks, first in chains:
            q = q_ref[0, pl.ds(_block_start(qi, tq), tq), :]
            qh[slot] = (jnp.where(lane < HEAD_DIM, q, 0).astype(BF16),
                        jnp.where(lane >= HEAD_DIM, q, 0).astype(BF16))
            for b, j in enumerate(blocks):
                starts[slot, b] = _block_start(j, size)
                units += [(slot, b, h, first and b == 0) for h in range(2)]
        z = {(sl, b, h): _dot_nt(k_ref[0, pl.ds(starts[sl, b], size), :], qh[sl][h])
             for sl, b, h, _ in units}
        l1mb, head = {}, {}
        for sl, b, h, diag in units:
            zz = z[sl, b, h]
            nz = -zz
            lg = jnp.minimum(nz, 0.0) - jnp.log2(1.0 + jnp.exp2(jnp.minimum(zz, nz)))
            if diag:
                lg = jnp.where(causal, lg, 0.0)
            l1mb[sl, b, h] = lg.astype(BF16)
            head[sl, b, h] = (zz + lg, lg[0:1, :])
        between = {(sl, b, h): _dot(after, l1mb[sl, b, h]) for sl, b, h, _ in units}
        a = {}
        for slot, qi, blocks, first in chains:
            for h in range(2):
                c = None if first else c_ref[slot, h]
                for b in range(len(blocks)):
                    e = head[slot, b, h][0] + between[slot, b, h]
                    if c is not None:
                        e = e + c
                    w = jnp.exp2(e)
                    if first and b == 0:
                        w = jnp.where(causal, w, 0.0)
                    a[slot, b, h] = w.astype(BF16)
                    block_sum = between[slot, b, h][0:1, :] + head[slot, b, h][1]
                    c = block_sum if c is None else c + block_sum
                c_ref[slot, h] = c
        for slot, qi, blocks, first in chains:
            for h in range(2):
                pv = None
                for b in range(len(blocks)):
                    part = _dot(vt_ref[0, h * HEAD_DIM:(h + 1) * HEAD_DIM, pl.ds(starts[slot, b], size)],
                                a[slot, b, h])
                    pv = part if pv is None else pv + part
                if first:
                    acc_ref[slot, h] = pv
                else:
                    acc_ref[slot, h] += pv

    def carry_max(slot):
        cm = jnp.maximum(c_ref[slot, 0], c_ref[slot, 1])
        return jnp.max(cm, axis=1, keepdims=True)[0, 0]

    def finish(slot, qi):
        half = tk // 2

        def cond(carry):
            j, cmax = carry
            return jnp.logical_and(j >= 0, cmax > EXP_ZERO * LOG2_E)

        def body(carry):
            j, _ = carry
            process([(slot, qi, [j], False)], size=half)
            return j - 1, carry_max(slot)

        lax.while_loop(cond, body, (jnp.asarray(2 * (qi - 1) - 1, jnp.int32), carry_max(slot)))
        o_ref[0, pl.ds(_block_start(qi, tq), tq), :] = (
            jnp.concatenate([acc_ref[slot, 0], acc_ref[slot, 1]], axis=0).T.astype(o_ref.dtype))

    process([(0, 0, [0], True)])
    finish(0, 0)

    def two_blocks(g, carry):
        qa, qb = 1 + 2 * g, 2 + 2 * g
        process([(0, qa, [qa, qa - 1], True), (1, qb, [qb, qb - 1], True)])
        finish(0, qa)
        finish(1, qb)
        return carry

    lax.fori_loop(0, (n_blk - 1) // 2, two_blocks, 0)
    if (n_blk - 1) % 2:
        last = n_blk - 1
        process([(0, last, [last, last - 1], True)])
        finish(0, last)


def _sb_attention(q, k, vt, *, tq):
    bsz, s, w = q.shape
    assert s >= 2 * tq
    idx = jnp.arange(tq)
    after = (idx[None, :] > idx[:, None]).astype(BF16)
    kern = functools.partial(_sb_kernel, tq=tq)
    return pl.pallas_call(
        kern,
        grid=(bsz, w // LANES),
        in_specs=[
            pl.BlockSpec((1, s, LANES), lambda b, p: (b, 0, p)),
            pl.BlockSpec((1, s, LANES), lambda b, p: (b, 0, p)),
            pl.BlockSpec((1, LANES, s), lambda b, p: (b, p, 0)),
            pl.BlockSpec((tq, tq), lambda b, p: (0, 0)),
        ],
        out_specs=pl.BlockSpec((1, s, LANES), lambda b, p: (b, 0, p)),
        out_shape=jax.ShapeDtypeStruct((bsz, s, w), BF16),
        scratch_shapes=[pltpu.VMEM((2, 2, HEAD_DIM, tq), F32), pltpu.VMEM((2, 2, 1, tq), F32)],
        compiler_params=pltpu.CompilerParams(
            dimension_semantics=("arbitrary", "arbitrary"), vmem_limit_bytes=VMEM_LIMIT),
        name="sb_attn",
    )(q, k, vt, after)


def _fox_kernel(fend_ref, par_ref, q_ref, k_ref, vt_ref, o_ref, acc_ref, m_ref, p_ref, *, tq, n_heads):
    tk = tq
    n_blk = q_ref.shape[1] // tq
    head0 = (pl.program_id(0) * n_heads + 2 * pl.program_id(1))

    def last_dead_block(head, qi, j_prev):
        base = head * n_blk
        f_q = fend_ref[base + jnp.maximum(qi - 1, 0)]

        def next_is_dead(j):
            jn = jnp.minimum(j + 1, n_blk - 1)
            return jnp.logical_and(j + 1 < qi, f_q - fend_ref[base + jn] < -par_ref[0])

        return lax.while_loop(next_is_dead, lambda j: j + 1, j_prev)

    def plan(qi, dead):
        dead = (last_dead_block(head0, qi, dead[0]), last_dead_block(head0 + 1, qi, dead[1]))
        j_dead = jnp.minimum(dead[0], dead[1])
        n_left = jnp.maximum(qi - 2 - j_dead, 0)
        odd = n_left % 2
        has_dead_below = j_dead >= 0
        n_pairs = n_left // 2 + jnp.where(has_dead_below, odd, 0)
        return n_pairs, jnp.logical_and(odd == 1, jnp.logical_not(has_dead_below)), dead

    def q_tile(qi, h):
        return q_ref[0, pl.ds(pl.multiple_of(qi * tq, tq), tq), h * LANES:(h + 1) * LANES]

    def finalize(qi):
        outs = []
        for h in range(2):
            acc = acc_ref[h]
            outs.append(acc[0:HEAD_DIM, :] / acc[HEAD_DIM:HEAD_DIM + 1, :])
        o_ref[0, pl.ds(pl.multiple_of(qi * tq, tq), tq), :] = (
            jnp.concatenate(outs, axis=0).T.astype(o_ref.dtype))

    key = lax.broadcasted_iota(jnp.int32, (tk, tq), 0)
    qry = lax.broadcasted_iota(jnp.int32, (tk, tq), 1)
    causal = key <= qry

    def process(qi, blocks, first):
        starts = [pl.multiple_of(j * tk, tk) for j in blocks]
        units = [(b, h) for b in range(len(blocks)) for h in range(2)]
        s = {(b, h): _dot_nt(k_ref[0, pl.ds(starts[b], tk), h * LANES:(h + 1) * LANES],
                             q_tile(qi, h)) for b, h in units}
        p = {}
        alpha = {}
        for h in range(2):
            if first:
                s[0, h] = jnp.where(causal, s[0, h], NEG_BIG)
            m_new = None if first else m_ref[h]
            for b in range(len(blocks)):
                mb = jnp.max(s[b, h], axis=0, keepdims=True)
                m_new = mb if m_new is None else jnp.maximum(m_new, mb)
            if not first:
                alpha[h] = jnp.exp(m_ref[h] - m_new)
            for b in range(len(blocks)):
                p[b, h] = jnp.exp(s[b, h] - m_new).astype(BF16)
            m_ref[h] = m_new
        for h in range(2):
            pv = None
            for b in range(len(blocks)):
                part = _dot(vt_ref[0, h * V_ROWS:(h + 1) * V_ROWS, pl.ds(starts[b], tk)], p[b, h])
                pv = part if pv is None else pv + part
            acc_ref[h] = pv if first else alpha[h] * acc_ref[h] + pv

    def online_path():
        def q_block(qi, dead):
            n_pairs, lone_block0, dead = plan(qi, dead)

            @pl.when(qi == 0)
            def _():
                process(qi, [qi], True)

            @pl.when(qi > 0)
            def _():
                process(qi, [qi, qi - 1], True)

            def body(i, c):
                j = qi - 2 - 2 * i
                process(qi, [j, j - 1], False)
                return c

            lax.fori_loop(0, n_pairs, body, 0)

            @pl.when(lone_block0)
            def _():
                process(qi, [0], False)

            finalize(qi)
            return dead

        lax.fori_loop(0, n_blk, q_block, (jnp.int32(-1), jnp.int32(-1)))

    win = 2 * tk
    wrow = lax.broadcasted_iota(jnp.int32, (win, tq), 0)
    wcol = lax.broadcasted_iota(jnp.int32, (win, tq), 1)

    def window(qi, i):
        jl = qi - 1 - 2 * i
        j_lo = jnp.maximum(jl, 0)
        return j_lo, pl.multiple_of(j_lo * tk, tk), jnp.where(jl < 0, tk, win)

    def stage_a(qi, i, slot, first):
        j_lo, start, row_lim = window(qi, i)
        keep = wrow < row_lim
        if first:
            keep = jnp.logical_and(keep, wrow <= wcol + (qi - j_lo) * tk)
        for h in range(2):
            s = _dot_nt(k_ref[0, pl.ds(start, win), h * LANES:(h + 1) * LANES], q_tile(qi, h))
            p_ref[slot, h] = jnp.exp(jnp.where(keep, s, NEG_BIG)).astype(BF16)

    def stage_c(qi, i, slot):
        _, start, _ = window(qi, i)
        for h in range(2):
            acc_ref[h] += _dot(vt_ref[0, h * V_ROWS:(h + 1) * V_ROWS, pl.ds(start, win)], p_ref[slot, h])

    def bounded_path():
        acc_ref[...] = jnp.zeros_like(acc_ref)
        stage_a(0, 0, 0, True)

        def q_block(qi, carry):
            t, dead = carry[0], carry[1:]
            n_pairs, lone_block0, dead = plan(qi, dead)
            n_stages = 1 + n_pairs + jnp.where(lone_block0, 1, 0)

            def body(i, t):
                stage_c(qi, i - 1, t & 1)
                stage_a(qi, i, (t + 1) & 1, False)
                return t + 1

            t = lax.fori_loop(1, n_stages, body, t)
            stage_c(qi, n_stages - 1, t & 1)
            stage_a(jnp.minimum(qi + 1, n_blk - 1), 0, (t + 1) & 1, True)
            finalize(qi)
            acc_ref[...] = jnp.zeros_like(acc_ref)
            return (t + 1,) + dead

        lax.fori_loop(0, n_blk, q_block, (jnp.int32(0), jnp.int32(-1), jnp.int32(-1)))

    bounded = par_ref[1] > 0.5
    pl.when(bounded)(bounded_path)
    pl.when(jnp.logical_not(bounded))(online_path)


def _fox_attention(q, k, vt, f, qk_bound, *, tq):
    bsz, s, w = q.shape
    pair = 2 * LANES
    n_heads = w // LANES
    assert s >= 2 * tq
    f_end = f[:, :, tq - 1::tq].reshape(-1)
    bounded = qk_bound <= MAX_UNSTABILISED_LOGIT
    stabiliser_slack = 0.5
    par = jnp.stack([jnp.where(bounded, stabiliser_slack - EXP_ZERO, 2.0 * qk_bound - EXP_ZERO),
                     jnp.where(bounded, 1.0, 0.0)]).astype(F32)
    kern = functools.partial(_fox_kernel, tq=tq, n_heads=n_heads)
    grid_spec = pltpu.PrefetchScalarGridSpec(
        num_scalar_prefetch=2,
        grid=(bsz, w // pair),
        in_specs=[
            pl.BlockSpec((1, s, pair), lambda b, p, fe, th: (b, 0, p)),
            pl.BlockSpec((1, s, pair), lambda b, p, fe, th: (b, 0, p)),
            pl.BlockSpec((1, 2 * V_ROWS, s), lambda b, p, fe, th: (b, p, 0)),
        ],
        out_specs=pl.BlockSpec((1, s, LANES), lambda b, p, fe, th: (b, 0, p)),
        scratch_shapes=[pltpu.VMEM((2, V_ROWS, tq), F32), pltpu.VMEM((2, 1, tq), F32),
                        pltpu.VMEM((2, 2, 2 * tq, tq), BF16)],
    )
    return pl.pallas_call(
        kern,
        grid_spec=grid_spec,
        out_shape=jax.ShapeDtypeStruct((bsz, s, w // 2), BF16),
        compiler_params=pltpu.CompilerParams(
            dimension_semantics=("arbitrary", "arbitrary"), vmem_limit_bytes=VMEM_LIMIT),
        name="fox_attn",
    )(f_end, par, q, k, vt)


def _out_kernel(x_ref, osb_ref, ofox_ref, osgu_ref, mod_ref, g2_ref, wo_ref, w1_ref, w2_ref,
                o_ref, *, ff_chunk):
    sb_w = osb_ref.shape[2]
    fox_w = ofox_ref.shape[2]
    x = x_ref[0]
    mix = (_dot(osb_ref[0], wo_ref[0:sb_w, :])
           + _dot(ofox_ref[0], wo_ref[sb_w:sb_w + fox_w, :])
           + _dot(osgu_ref[0], wo_ref[sb_w + fox_w:, :]))
    x1 = x + mod_ref[0, 2:3, :] * mix
    ms = jnp.mean(x1 * x1, axis=-1, keepdims=True)
    h = x1 * lax.rsqrt(ms + EPS) * g2_ref[...]
    hb = (h * (1.0 + mod_ref[0, 4:5, :]) + mod_ref[0, 3:4, :]).astype(BF16)
    d_ff = w1_ref.shape[1]
    acc = None
    for c in range(d_ff // ff_chunk):
        hid = jnp.maximum(_dot(hb, w1_ref[:, c * ff_chunk:(c + 1) * ff_chunk]), 0.0)
        part = _dot((hid * hid).astype(BF16), w2_ref[c * ff_chunk:(c + 1) * ff_chunk, :])
        acc = part if acc is None else acc + part
    o_ref[0] = x1 + mod_ref[0, 5:6, :] * acc


def _out_mlp(x, osb, ofox, osgu, mod, g2, wo, w1, w2, *, layer, tm):
    bsz, s, d = x.shape
    tok = lambda b, i: (b, i, 0)

    def of_layer(a, **kw):
        nd = a.ndim - 1
        return pl.BlockSpec((None,) + a.shape[1:], lambda b, i: (layer,) + (0,) * nd, **kw)

    single = pl.Buffered(1)
    kern = functools.partial(_out_kernel, ff_chunk=FF_CHUNK)
    return pl.pallas_call(
        kern,
        grid=(bsz, s // tm),
        in_specs=[
            pl.BlockSpec((1, tm, d), tok),
            pl.BlockSpec((1, tm, osb.shape[2]), tok),
            pl.BlockSpec((1, tm, ofox.shape[2]), tok),
            pl.BlockSpec((1, tm, osgu.shape[2]), tok),
            pl.BlockSpec((None, 1, 6, d), lambda b, i: (layer, b, 0, 0)),
            of_layer(g2),
            of_layer(wo, pipeline_mode=single),
            of_layer(w1, pipeline_mode=single),
            of_layer(w2, pipeline_mode=single),
        ],
        out_specs=pl.BlockSpec((1, tm, d), tok),
        out_shape=jax.ShapeDtypeStruct((bsz, s, d), F32),
        compiler_params=pltpu.CompilerParams(
            dimension_semantics=("arbitrary", "arbitrary"), vmem_limit_bytes=VMEM_LIMIT),
        name="out_mlp",
    )(x, osb, ofox, osgu, mod, g2, wo, w1, w2)


def kernel(x, c, ada_w, ada_b, norm1_g, norm2_g, w_in, b_forget, q_norm_g, k_norm_g, sgu_norm_g,
           sgu_w, sgu_b, w_out, mlp_w1, mlp_w2):
    depth, d, _ = ada_w.shape
    bsz, s, _ = x.shape
    fox_heads = b_forget.shape[1]
    fox_w = fox_heads * HEAD_DIM
    sgu_groups, chunk = sgu_b.shape[1], sgu_b.shape[2]
    sgu_wd = sgu_groups * sgu_norm_g.shape[2]
    sb_w = (w_in.shape[2] - 3 * fox_w - fox_heads - 2 * sgu_wd) // 3
    f_lo = 3 * sb_w + 3 * fox_w
    assert s % TOKEN_TILE == 0 and s % (2 * ATTN_BLOCK) == 0 and mlp_w1.shape[2] % FF_CHUNK == 0
    assert sb_w % LANES == 0 and ada_w.shape[2] % MOD_COLS == 0

    mod = _modulation(c, ada_w, ada_b).reshape(depth, bsz, 6, d)

    perm = jnp.argsort(b_forget, axis=1)

    def relabel_cols(cols):
        t = cols.reshape(depth, d, fox_heads, HEAD_DIM)
        return jnp.take_along_axis(t, perm[:, None, :, None], axis=2).reshape(depth, d, fox_w)

    o = 3 * sb_w
    w_bf = w_in.astype(BF16)
    fox_cols = [relabel_cols(w_bf[:, :, o + i * fox_w:o + (i + 1) * fox_w]) for i in range(3)]
    w = jnp.concatenate([w_bf[:, :, :o]] + fox_cols + [w_bf[:, :, f_lo + fox_heads:]], axis=2)
    wf = jnp.take_along_axis(w_in[:, :, f_lo:f_lo + fox_heads], perm[:, None, :], axis=2)
    wf = jnp.pad(wf, ((0, 0), (0, 0), (0, LANES - fox_heads))).astype(BF16)
    bf = jnp.pad(jnp.take_along_axis(b_forget, perm, axis=1), ((0, 0), (0, LANES - fox_heads)))
    bf = bf.reshape(depth, 1, LANES)
    qg = jnp.tile(q_norm_g, (1, MXU_DIM // HEAD_DIM)).reshape(depth, 1, MXU_DIM)
    kg = jnp.tile(k_norm_g, (1, MXU_DIM // HEAD_DIM)).reshape(depth, 1, MXU_DIM)
    sg = sgu_norm_g.reshape(depth, 1, sgu_wd)
    sb = jnp.repeat(jnp.swapaxes(sgu_b, 1, 2), sgu_norm_g.shape[2], axis=2)
    qk_bound = ((BF16_MARGIN * HEAD_DIM ** 0.5) * jnp.max(jnp.abs(q_norm_g), axis=1)
                * jnp.max(jnp.abs(k_norm_g), axis=1))
    qx = jnp.zeros((depth, 1, LANES), F32).at[:, 0, HEAD_DIM:HEAD_DIM + 3].set(-1.0)
    qx = qx.at[:, 0, HEAD_DIM + 6].set(-qk_bound)
    wo_fox = w_out[:, sb_w:sb_w + fox_w].reshape(depth, fox_heads, HEAD_DIM, d)
    wo_fox = jnp.take_along_axis(wo_fox, perm[:, :, None, None], axis=1).reshape(depth, fox_w, d)
    wo = jnp.concatenate([w_out[:, :sb_w], wo_fox, w_out[:, sb_w + fox_w:]], axis=1).astype(BF16)
    w1 = mlp_w1.astype(BF16)
    w2 = mlp_w2.astype(BF16)
    g1 = norm1_g.reshape(depth, 1, d)
    g2 = norm2_g.reshape(depth, 1, d)

    for l in range(depth):
        qa, ka, vat, qf, kf, vft, osgu, f = _in_proj(
            x, mod, g1, w, wf, bf, qx, qg, kg, sg, sgu_w, sb,
            layer=l, tm=TOKEN_TILE, sb_w=sb_w, fox_w=fox_w, sgu_w=sgu_wd)
        osb = _sb_attention(qa, ka, vat, tq=ATTN_BLOCK)
        ofox = _fox_attention(qf, kf, vft, f, qk_bound[l], tq=ATTN_BLOCK)
        x = _out_mlp(x, osb, ofox, osgu, mod, g2, wo, w1, w2, layer=l, tm=TOKEN_TILE)
    return x
```

```python
import functools
import math

import jax
import jax.numpy as jnp
from jax import lax
from jax.experimental import pallas as pl
from jax.experimental.pallas import tpu as pltpu

HEAD_DIM = 64
LANES = 128
MXU_DIM = 256
V_ROWS = 2 * HEAD_DIM
PART_STRIDE = 8
BF16_MARGIN = 1.03

TOKEN_TILE = 512
ATTN_BLOCK = MXU_DIM
MOD_COLS = 1536
FF_CHUNK = 1024
EPS = 1e-6
NEG_BIG = -1e30
EXP_ZERO = -104.0
LOG2_E = math.log2(math.e)
MAX_UNSTABILISED_LOGIT = 40.0
VMEM_LIMIT = 56 * 1024 * 1024

F32 = jnp.float32
BF16 = jnp.bfloat16


def _dot(a, b):
    return jnp.dot(a, b, preferred_element_type=F32)


def _dot_nt(a, b):
    return lax.dot_general(a, b, (((1,), (1,)), ((), ())), preferred_element_type=F32)


def _block_start(j, size):
    return j * size if isinstance(j, int) else pl.multiple_of(j * size, size)


def _split3(x):
    hi = x.astype(BF16)
    r = x - hi.astype(F32)
    mid = r.astype(BF16)
    lo = (r - mid.astype(F32)).astype(BF16)
    return hi, mid, lo


def _group_mean_matrix():
    head = jnp.arange(MXU_DIM) // HEAD_DIM
    return jnp.where(head[:, None] == head[None, :], 1.0 / HEAD_DIM, 0.0).astype(BF16)


def _placement_matrix(n_heads):
    row = jnp.arange(LANES)[:, None]
    col = jnp.arange(2 * n_heads * LANES)[None, :]
    tile = col // LANES
    is_q = (tile >= n_heads).astype(jnp.int32)
    part = col % LANES - HEAD_DIM - 3 * is_q
    hit = (part >= 0) & (part < 3) & (row == part * PART_STRIDE + tile - n_heads * is_q)
    return hit.astype(BF16)


def _head_rmsnorm(t, gmat, gain):
    ms = _dot((t * t).astype(BF16), gmat)
    return t * lax.rsqrt(ms + EPS) * gain


def _gelu_tanh(x):
    c = math.sqrt(2.0 / math.pi)
    return x * (0.5 * (1.0 + jnp.tanh(c * (x + 0.044715 * (x * x * x)))))


def _log_sigmoid(x):
    return jnp.minimum(x, 0.0) - jnp.log(1.0 + jnp.exp(-jnp.abs(x)))


def _mod_kernel(ct_ref, w_ref, b_ref, o_ref):
    ct = ct_ref[...]
    cond = ct * (1.0 / (1.0 + jnp.exp(-ct)))
    w = w_ref[0]
    rows = [jnp.sum(cond[:, b:b + 1] * w, axis=0, keepdims=True) for b in range(ct.shape[1])]
    o_ref[0] = jnp.concatenate(rows, axis=0) + b_ref[0]


def _modulation(c, ada_w, ada_b):
    depth, d, n = ada_w.shape
    bsz = c.shape[0]
    tn = MOD_COLS
    return pl.pallas_call(
        _mod_kernel,
        grid=(depth, n // tn),
        in_specs=[
            pl.BlockSpec((d, bsz), lambda l, j: (0, 0)),
            pl.BlockSpec((1, d, tn), lambda l, j: (l, 0, j)),
            pl.BlockSpec((1, 1, tn), lambda l, j: (l, 0, j)),
        ],
        out_specs=pl.BlockSpec((1, bsz, tn), lambda l, j: (l, 0, j)),
        out_shape=jax.ShapeDtypeStruct((depth, bsz, n), F32),
        compiler_params=pltpu.CompilerParams(
            dimension_semantics=("arbitrary", "arbitrary"), vmem_limit_bytes=VMEM_LIMIT),
        name="adaln_mod",
    )(c.T, ada_w, ada_b.reshape(depth, 1, n))


def _in_kernel(x_ref, mod_ref, g1_ref, w_ref, wf_ref, bf_ref, qx_ref, qg_ref, kg_ref,
               sg_ref, sw_ref, sb_ref, ltri_ref, sel_ref, gmat_ref,
               qa_ref, ka_ref, vat_ref, qf_ref, kf_ref, vft_ref, og_ref, f_ref,
               carry_ref, proj_ref, *, sb_w, fox_w, sgu_w, chunk):
    @pl.when(pl.program_id(1) == 0)
    def _():
        carry_ref[...] = jnp.zeros_like(carry_ref)

    tm = x_ref.shape[1]
    x = x_ref[0]
    ms = jnp.mean(x * x, axis=-1, keepdims=True)
    h = x * lax.rsqrt(ms + EPS) * g1_ref[...]
    h = h * (1.0 + mod_ref[0, 1:2, :]) + mod_ref[0, 0:1, :]
    hb = h.astype(BF16)
    scale = HEAD_DIM ** -0.5
    n_attn = 3 * sb_w + 3 * fox_w
    wn = w_ref.shape[1]
    n_heads = f_ref.shape[1]
    lane = lax.broadcasted_iota(jnp.int32, (1, LANES), 1)
    head_lane = lane < n_heads
    gmat = gmat_ref[...]

    def pack3(parts):
        a, b, c = (p.astype(F32) for p in parts)
        return (a + pltpu.roll(b, PART_STRIDE, 1) + pltpu.roll(c, 2 * PART_STRIDE, 1)).astype(BF16)

    def project(lo, hi):
        proj_ref[:, lo:hi] = _dot(hb, w_ref[:, lo:hi])

    o_q, o_k, o_v = 3 * sb_w, 3 * sb_w + fox_w, 3 * sb_w + 2 * fox_w
    project(n_attn, wn)
    fl = _dot(hb, wf_ref[...])
    project(o_q, o_k)
    gu = _gelu_tanh(proj_ref[:, n_attn:n_attn + sgu_w])
    gv = _gelu_tanh(proj_ref[:, n_attn + sgu_w:wn])
    project(o_k, o_v)
    logf = jnp.where(head_lane, _log_sigmoid(fl + bf_ref[...]), 0.0)
    cp = _dot(ltri_ref[...], pack3(_split3(logf)))
    vn_all = _head_rmsnorm(gv, gmat, sg_ref[...]).astype(BF16)
    q_tiles = [_head_rmsnorm(proj_ref[:, o_q + j * MXU_DIM:o_q + (j + 1) * MXU_DIM], gmat, qg_ref[...]) * scale
               for j in range(fox_w // MXU_DIM)]
    project(o_v, n_attn)

    cum = cp + pltpu.roll(cp, LANES - PART_STRIDE, 1) + pltpu.roll(cp, LANES - 2 * PART_STRIDE, 1)
    cum = jnp.where(head_lane, cum, 0.0) + carry_ref[0:1, :]
    carry_ref[0:1, :] = cum[tm - 1:tm, :]
    f_ref[0] = cum.T[0:n_heads, :]
    extras = _dot(pack3(_split3(cum)), sel_ref[...])
    k_tiles = [_head_rmsnorm(proj_ref[:, o_k + j * MXU_DIM:o_k + (j + 1) * MXU_DIM], gmat, kg_ref[...])
               for j in range(fox_w // MXU_DIM)]

    lane_c = lax.broadcasted_iota(jnp.int32, (chunk, LANES), 1)
    rr = lax.broadcasted_iota(jnp.int32, (chunk, chunk), 0)
    cs = lax.broadcasted_iota(jnp.int32, (chunk, chunk), 1)
    wt = [jnp.where(rr >= cs, sw_ref[g], 0.0).astype(BF16) for g in range(sw_ref.shape[0])]
    for p in range(sgu_w // LANES):
        vn = vn_all[:, p * LANES:(p + 1) * LANES]
        for ci in range(tm // chunk):
            vblk = vn[ci * chunk:(ci + 1) * chunk, :]
            mixed = jnp.where(lane_c < HEAD_DIM, _dot(wt[2 * p], vblk), _dot(wt[2 * p + 1], vblk))
            mixed = mixed + sb_ref[:, p * LANES:(p + 1) * LANES]
            og_ref[0, ci * chunk:(ci + 1) * chunk, p * LANES:(p + 1) * LANES] = (
                gu[ci * chunk:(ci + 1) * chunk, p * LANES:(p + 1) * LANES] * mixed).astype(BF16)

    project(0, o_q)
    qa_ref[0] = (proj_ref[:, 0:sb_w] * (scale * LOG2_E)).astype(BF16)
    ka_ref[0] = proj_ref[:, sb_w:2 * sb_w].astype(BF16)
    vat_ref[0] = proj_ref[:, 2 * sb_w:3 * sb_w].T.astype(BF16)

    is_head = lane < HEAD_DIM
    q_extra = qx_ref[...]
    k_extra = jnp.where((lane >= HEAD_DIM + 3) & (lane < HEAD_DIM + 7), 1.0, 0.0)
    heads_per_mxu = MXU_DIM // HEAD_DIM

    def head_tile(tn, s):
        half = tn[:, (s // 2) * LANES:(s // 2 + 1) * LANES]
        return half if s % 2 == 0 else pltpu.roll(half, HEAD_DIM, 1)

    for j in range(fox_w // MXU_DIM):
        for s in range(heads_per_mxu):
            hh = heads_per_mxu * j + s
            xq = extras[:, (n_heads + hh) * LANES:(n_heads + hh + 1) * LANES] + q_extra
            qf_ref[0, :, hh * LANES:(hh + 1) * LANES] = jnp.where(
                is_head, head_tile(q_tiles[j], s), xq).astype(BF16)
            xk = extras[:, hh * LANES:(hh + 1) * LANES] + k_extra
            kf_ref[0, :, hh * LANES:(hh + 1) * LANES] = jnp.where(
                is_head, head_tile(k_tiles[j], s), xk).astype(BF16)
    vt = proj_ref[:, o_v:n_attn].T.astype(BF16)
    ones = jnp.ones((V_ROWS - HEAD_DIM, tm), BF16)
    for hh in range(fox_w // HEAD_DIM):
        vft_ref[0, hh * V_ROWS:hh * V_ROWS + HEAD_DIM, :] = vt[hh * HEAD_DIM:(hh + 1) * HEAD_DIM, :]
        vft_ref[0, hh * V_ROWS + HEAD_DIM:(hh + 1) * V_ROWS, :] = ones


def _in_proj(x, mod, g1, w, wf, bf, qx, qg, kg, sg, sw, sb, *, layer, tm, sb_w, fox_w, sgu_w):
    bsz, s, d = x.shape
    chunk = sw.shape[-1]
    wn = w.shape[2]
    fox_heads = fox_w // HEAD_DIM

    def of_layer(a):
        nd = a.ndim - 1
        return pl.BlockSpec((None,) + a.shape[1:], lambda b, i: (layer,) + (0,) * nd)

    assert fox_heads <= PART_STRIDE and fox_w % MXU_DIM == 0 and sgu_w == MXU_DIM
    assert s % tm == 0 and tm % chunk == 0
    tok_idx = jnp.arange(tm)
    ltri = (tok_idx[:, None] >= tok_idx[None, :]).astype(BF16)
    const2 = lambda b, i: (0, 0)
    tok = lambda b, i: (b, i, 0)
    tok_t = lambda b, i: (b, 0, i)
    kern = functools.partial(_in_kernel, sb_w=sb_w, fox_w=fox_w, sgu_w=sgu_w, chunk=chunk)
    out_shape = [
        jax.ShapeDtypeStruct((bsz, s, sb_w), BF16),
        jax.ShapeDtypeStruct((bsz, s, sb_w), BF16),
        jax.ShapeDtypeStruct((bsz, sb_w, s), BF16),
        jax.ShapeDtypeStruct((bsz, s, fox_heads * LANES), BF16),
        jax.ShapeDtypeStruct((bsz, s, fox_heads * LANES), BF16),
        jax.ShapeDtypeStruct((bsz, fox_heads * V_ROWS, s), BF16),
        jax.ShapeDtypeStruct((bsz, s, sgu_w), BF16),
        jax.ShapeDtypeStruct((bsz, fox_heads, s), F32),
    ]
    out_specs = [
        pl.BlockSpec((1, tm, sb_w), tok),
        pl.BlockSpec((1, tm, sb_w), tok),
        pl.BlockSpec((1, sb_w, tm), tok_t),
        pl.BlockSpec((1, tm, fox_heads * LANES), tok),
        pl.BlockSpec((1, tm, fox_heads * LANES), tok),
        pl.BlockSpec((1, fox_heads * V_ROWS, tm), tok_t),
        pl.BlockSpec((1, tm, sgu_w), tok),
        pl.BlockSpec((1, fox_heads, tm), tok_t),
    ]
    return pl.pallas_call(
        kern,
        grid=(bsz, s // tm),
        in_specs=[
            pl.BlockSpec((1, tm, d), tok),
            pl.BlockSpec((None, 1, 6, d), lambda b, i: (layer, b, 0, 0)),
            of_layer(g1), of_layer(w), of_layer(wf), of_layer(bf), of_layer(qx), of_layer(qg),
            of_layer(kg), of_layer(sg), of_layer(sw), of_layer(sb),
            pl.BlockSpec((tm, tm), const2),
            pl.BlockSpec((LANES, 2 * fox_heads * LANES), const2),
            pl.BlockSpec((MXU_DIM, MXU_DIM), const2),
        ],
        out_specs=out_specs,
        out_shape=out_shape,
        scratch_shapes=[pltpu.VMEM((8, LANES), F32), pltpu.VMEM((tm, wn), F32)],
        compiler_params=pltpu.CompilerParams(
            dimension_semantics=("arbitrary", "arbitrary"), vmem_limit_bytes=VMEM_LIMIT),
        name="in_proj",
    )(x, mod, g1, w, wf, bf, qx, qg, kg, sg, sw, sb, ltri, _placement_matrix(fox_heads),
      _group_mean_matrix())


def _sb_kernel(q_ref, k_ref, vt_ref, after_ref, o_ref, acc_ref, c_ref, *, tq):
    tk = tq
    n_blk = q_ref.shape[1] // tq
    lane = lax.broadcasted_iota(jnp.int32, (1, LANES), 1)
    key = lax.broadcasted_iota(jnp.int32, (tk, tq), 0)
    qry = lax.broadcasted_iota(jnp.int32, (tk, tq), 1)
    causal = key < qry
    def process(chains, size=tk):
        after = after_ref[0:size, 0:size]
        units, starts, qh = [], {}, {}
        for slot, qi, blocks, first in chains:
            q = q_ref[0, pl.ds(_block_start(qi, tq), tq), :]
            qh[slot] = (jnp.where(lane < HEAD_DIM, q, 0).astype(BF16),
                        jnp.where(lane >= HEAD_DIM, q, 0).astype(BF16))
            for b, j in enumerate(blocks):
                starts[slot, b] = _block_start(j, size)
                units += [(slot, b, h, first and b == 0) for h in range(2)]
        z = {(sl, b, h): _dot_nt(k_ref[0, pl.ds(starts[sl, b], size), :], qh[sl][h])
             for sl, b, h, _ in units}
        l1mb, head = {}, {}
        for sl, b, h, diag in units:
            zz = z[sl, b, h]
            nz = -zz
            lg = jnp.minimum(nz, 0.0) - jnp.log2(1.0 + jnp.exp2(jnp.minimum(zz, nz)))
            if diag:
                lg = jnp.where(causal, lg, 0.0)
            l1mb[sl, b, h] = lg.astype(BF16)
            head[sl, b, h] = (zz + lg, lg[0:1, :])
        between = {(sl, b, h): _dot(after, l1mb[sl, b, h]) for sl, b, h, _ in units}
        a = {}
        for slot, qi, blocks, first in chains:
            for h in range(2):
                c = None if first else c_ref[slot, h]
                for b in range(len(blocks)):
                    e = head[slot, b, h][0] + between[slot, b, h]
                    if c is not None:
                        e = e + c
                    w = jnp.exp2(e)
                    if first and b == 0:
                        w = jnp.where(causal, w, 0.0)
                    a[slot, b, h] = w.astype(BF16)
                    block_sum = between[slot, b, h][0:1, :] + head[slot, b, h][1]
                    c = block_sum if c is None else c + block_sum
                c_ref[slot, h] = c
        for slot, qi, blocks, first in chains:
            for h in range(2):
                pv = None
                for b in range(len(blocks)):
                    part = _dot(vt_ref[0, h * HEAD_DIM:(h + 1) * HEAD_DIM, pl.ds(starts[slot, b], size)],
                                a[slot, b, h])
                    pv = part if pv is None else pv + part
                if first:
                    acc_ref[slot, h] = pv
                else:
                    acc_ref[slot, h] += pv

    def carry_max(slot):
        cm = jnp.maximum(c_ref[slot, 0], c_ref[slot, 1])
        return jnp.max(cm, axis=1, keepdims=True)[0, 0]

    def finish(slot, qi):
        half = tk // 2

        def cond(carry):
            j, cmax = carry
            return jnp.logical_and(j >= 0, cmax > EXP_ZERO * LOG2_E)

        def body(carry):
            j, _ = carry
            process([(slot, qi, [j], False)], size=half)
            return j - 1, carry_max(slot)

        lax.while_loop(cond, body, (jnp.asarray(2 * (qi - 1) - 1, jnp.int32), carry_max(slot)))
        o_ref[0, pl.ds(_block_start(qi, tq), tq), :] = (
            jnp.concatenate([acc_ref[slot, 0], acc_ref[slot, 1]], axis=0).T.astype(o_ref.dtype))

    process([(0, 0, [0], True)])
    finish(0, 0)

    def two_blocks(g, carry):
        qa, qb = 1 + 2 * g, 2 + 2 * g
        process([(0, qa, [qa, qa - 1], True), (1, qb, [qb, qb - 1], True)])
        finish(0, qa)
        finish(1, qb)
        return carry

    lax.fori_loop(0, (n_blk - 1) // 2, two_blocks, 0)
    if (n_blk - 1) % 2:
        last = n_blk - 1
        process([(0, last, [last, last - 1], True)])
        finish(0, last)


def _sb_attention(q, k, vt, *, tq):
    bsz, s, w = q.shape
    assert s >= 2 * tq
    idx = jnp.arange(tq)
    after = (idx[None, :] > idx[:, None]).astype(BF16)
    kern = functools.partial(_sb_kernel, tq=tq)
    return pl.pallas_call(
        kern,
        grid=(bsz, w // LANES),
        in_specs=[
            pl.BlockSpec((1, s, LANES), lambda b, p: (b, 0, p)),
            pl.BlockSpec((1, s, LANES), lambda b, p: (b, 0, p)),
            pl.BlockSpec((1, LANES, s), lambda b, p: (b, p, 0)),
            pl.BlockSpec((tq, tq), lambda b, p: (0, 0)),
        ],
        out_specs=pl.BlockSpec((1, s, LANES), lambda b, p: (b, 0, p)),
        out_shape=jax.ShapeDtypeStruct((bsz, s, w), BF16),
        scratch_shapes=[pltpu.VMEM((2, 2, HEAD_DIM, tq), F32), pltpu.VMEM((2, 2, 1, tq), F32)],
        compiler_params=pltpu.CompilerParams(
            dimension_semantics=("arbitrary", "arbitrary"), vmem_limit_bytes=VMEM_LIMIT),
        name="sb_attn",
    )(q, k, vt, after)


def _fox_kernel(fend_ref, par_ref, order_ref, q0_ref, q1_ref, k0_ref, k1_ref, v0_ref, v1_ref, o_ref,
                acc_ref, m_ref, p_ref, *, tq, n_heads):
    tk = tq
    qs, ks, vs = (q0_ref, q1_ref), (k0_ref, k1_ref), (v0_ref, v1_ref)
    n_blk = q0_ref.shape[1] // tq
    pair = pl.program_id(1)
    heads = tuple(pl.program_id(0) * n_heads + order_ref[2 * pair + h] for h in range(2))

    def last_dead_block(head, qi, j_prev):
        base = head * n_blk
        f_q = fend_ref[base + jnp.maximum(qi - 1, 0)]

        def next_is_dead(j):
            jn = jnp.minimum(j + 1, n_blk - 1)
            return jnp.logical_and(j + 1 < qi, f_q - fend_ref[base + jn] < -par_ref[0])

        return lax.while_loop(next_is_dead, lambda j: j + 1, j_prev)

    def plan(qi, dead):
        dead = (last_dead_block(heads[0], qi, dead[0]), last_dead_block(heads[1], qi, dead[1]))
        j_dead = jnp.minimum(dead[0], dead[1])
        n_left = jnp.maximum(qi - 2 - j_dead, 0)
        odd = n_left % 2
        has_dead_below = j_dead >= 0
        n_pairs = n_left // 2 + jnp.where(has_dead_below, odd, 0)
        return n_pairs, jnp.logical_and(odd == 1, jnp.logical_not(has_dead_below)), dead

    def q_tile(qi, h):
        return qs[h][0, pl.ds(pl.multiple_of(qi * tq, tq), tq), :]

    def finalize(qi):
        outs = []
        for h in range(2):
            acc = acc_ref[h]
            outs.append(acc[0:HEAD_DIM, :] / acc[HEAD_DIM:HEAD_DIM + 1, :])
        o_ref[0, pl.ds(pl.multiple_of(qi * tq, tq), tq), :] = (
            jnp.concatenate(outs, axis=0).T.astype(o_ref.dtype))

    key = lax.broadcasted_iota(jnp.int32, (tk, tq), 0)
    qry = lax.broadcasted_iota(jnp.int32, (tk, tq), 1)
    causal = key <= qry

    def process(qi, blocks, first):
        starts = [pl.multiple_of(j * tk, tk) for j in blocks]
        units = [(b, h) for b in range(len(blocks)) for h in range(2)]
        s = {(b, h): _dot_nt(ks[h][0, pl.ds(starts[b], tk), :],
                             q_tile(qi, h)) for b, h in units}
        p = {}
        alpha = {}
        for h in range(2):
            if first:
                s[0, h] = jnp.where(causal, s[0, h], NEG_BIG)
            m_new = None if first else m_ref[h]
            for b in range(len(blocks)):
                mb = jnp.max(s[b, h], axis=0, keepdims=True)
                m_new = mb if m_new is None else jnp.maximum(m_new, mb)
            if not first:
                alpha[h] = jnp.exp(m_ref[h] - m_new)
            for b in range(len(blocks)):
                p[b, h] = jnp.exp(s[b, h] - m_new).astype(BF16)
            m_ref[h] = m_new
        for h in range(2):
            pv = None
            for b in range(len(blocks)):
                part = _dot(vs[h][0, :, pl.ds(starts[b], tk)], p[b, h])
                pv = part if pv is None else pv + part
            acc_ref[h] = pv if first else alpha[h] * acc_ref[h] + pv

    def online_path():
        def q_block(qi, dead):
            n_pairs, lone_block0, dead = plan(qi, dead)

            @pl.when(qi == 0)
            def _():
                process(qi, [qi], True)

            @pl.when(qi > 0)
            def _():
                process(qi, [qi, qi - 1], True)

            def body(i, c):
                j = qi - 2 - 2 * i
                process(qi, [j, j - 1], False)
                return c

            lax.fori_loop(0, n_pairs, body, 0)

            @pl.when(lone_block0)
            def _():
                process(qi, [0], False)

            finalize(qi)
            return dead

        lax.fori_loop(0, n_blk, q_block, (jnp.int32(-1), jnp.int32(-1)))

    win = 2 * tk
    wrow = lax.broadcasted_iota(jnp.int32, (win, tq), 0)
    wcol = lax.broadcasted_iota(jnp.int32, (win, tq), 1)

    def window(qi, i):
        jl = qi - 1 - 2 * i
        j_lo = jnp.maximum(jl, 0)
        return j_lo, pl.multiple_of(j_lo * tk, tk), jnp.where(jl < 0, tk, win)

    def stage_a(qi, i, slot, first):
        j_lo, start, row_lim = window(qi, i)
        keep = wrow < row_lim
        if first:
            keep = jnp.logical_and(keep, wrow <= wcol + (qi - j_lo) * tk)
        for h in range(2):
            s = _dot_nt(ks[h][0, pl.ds(start, win), :], q_tile(qi, h))
            p_ref[slot, h] = jnp.exp(jnp.where(keep, s, NEG_BIG)).astype(BF16)

    def stage_c(qi, i, slot):
        _, start, _ = window(qi, i)
        for h in range(2):
            acc_ref[h] += _dot(vs[h][0, :, pl.ds(start, win)], p_ref[slot, h])

    def bounded_path():
        acc_ref[...] = jnp.zeros_like(acc_ref)
        stage_a(0, 0, 0, True)

        def q_block(qi, carry):
            t, dead = carry[0], carry[1:]
            n_pairs, lone_block0, dead = plan(qi, dead)
            n_stages = 1 + n_pairs + jnp.where(lone_block0, 1, 0)

            def body(i, t):
                stage_c(qi, i - 1, t & 1)
                stage_a(qi, i, (t + 1) & 1, False)
                return t + 1

            t = lax.fori_loop(1, n_stages, body, t)
            stage_c(qi, n_stages - 1, t & 1)
            stage_a(jnp.minimum(qi + 1, n_blk - 1), 0, (t + 1) & 1, True)
            finalize(qi)
            acc_ref[...] = jnp.zeros_like(acc_ref)
            return (t + 1,) + dead

        lax.fori_loop(0, n_blk, q_block, (jnp.int32(0), jnp.int32(-1), jnp.int32(-1)))

    bounded = par_ref[1] > 0.5
    pl.when(bounded)(bounded_path)
    pl.when(jnp.logical_not(bounded))(online_path)


def _fox_attention(q, k, vt, f, qk_bound, order, *, tq):
    bsz, s, w = q.shape
    n_heads = w // LANES
    assert s >= 2 * tq
    f_end = f[:, :, tq - 1::tq].reshape(-1)
    bounded = qk_bound <= MAX_UNSTABILISED_LOGIT
    stabiliser_slack = 0.5
    par = jnp.stack([jnp.where(bounded, stabiliser_slack - EXP_ZERO, 2.0 * qk_bound - EXP_ZERO),
                     jnp.where(bounded, 1.0, 0.0)]).astype(F32)
    kern = functools.partial(_fox_kernel, tq=tq, n_heads=n_heads)
    tok_tile = lambda h: pl.BlockSpec((1, s, LANES), lambda b, p, fe, th, od: (b, 0, od[2 * p + h]))
    val_tile = lambda h: pl.BlockSpec((1, V_ROWS, s), lambda b, p, fe, th, od: (b, od[2 * p + h], 0))
    grid_spec = pltpu.PrefetchScalarGridSpec(
        num_scalar_prefetch=3,
        grid=(bsz, n_heads // 2),
        in_specs=[tok_tile(0), tok_tile(1), tok_tile(0), tok_tile(1), val_tile(0), val_tile(1)],
        out_specs=pl.BlockSpec((1, s, LANES), lambda b, p, fe, th, od: (b, 0, p)),
        scratch_shapes=[pltpu.VMEM((2, V_ROWS, tq), F32), pltpu.VMEM((2, 1, tq), F32),
                        pltpu.VMEM((2, 2, 2 * tq, tq), BF16)],
    )
    return pl.pallas_call(
        kern,
        grid_spec=grid_spec,
        out_shape=jax.ShapeDtypeStruct((bsz, s, w // 2), BF16),
        compiler_params=pltpu.CompilerParams(
            dimension_semantics=("arbitrary", "arbitrary"), vmem_limit_bytes=VMEM_LIMIT),
        name="fox_attn",
    )(f_end, par, order.astype(jnp.int32), q, q, k, k, vt, vt)


def _out_kernel(x_ref, osb_ref, ofox_ref, osgu_ref, mod_ref, g2_ref, wo_ref, w1_ref, w2_ref,
                o_ref, *, ff_chunk):
    sb_w = osb_ref.shape[2]
    fox_w = ofox_ref.shape[2]
    x = x_ref[0]
    mix = (_dot(osb_ref[0], wo_ref[0:sb_w, :])
           + _dot(ofox_ref[0], wo_ref[sb_w:sb_w + fox_w, :])
           + _dot(osgu_ref[0], wo_ref[sb_w + fox_w:, :]))
    x1 = x + mod_ref[0, 2:3, :] * mix
    ms = jnp.mean(x1 * x1, axis=-1, keepdims=True)
    h = x1 * lax.rsqrt(ms + EPS) * g2_ref[...]
    hb = (h * (1.0 + mod_ref[0, 4:5, :]) + mod_ref[0, 3:4, :]).astype(BF16)
    d_ff = w1_ref.shape[1]
    acc = None
    for c in range(d_ff // ff_chunk):
        hid = jnp.maximum(_dot(hb, w1_ref[:, c * ff_chunk:(c + 1) * ff_chunk]), 0.0)
        part = _dot((hid * hid).astype(BF16), w2_ref[c * ff_chunk:(c + 1) * ff_chunk, :])
        acc = part if acc is None else acc + part
    o_ref[0] = x1 + mod_ref[0, 5:6, :] * acc


def _out_mlp(x, osb, ofox, osgu, mod, g2, wo, w1, w2, *, layer, tm):
    bsz, s, d = x.shape
    tok = lambda b, i: (b, i, 0)

    def of_layer(a, **kw):
        nd = a.ndim - 1
        return pl.BlockSpec((None,) + a.shape[1:], lambda b, i: (layer,) + (0,) * nd, **kw)

    single = pl.Buffered(1)
    kern = functools.partial(_out_kernel, ff_chunk=FF_CHUNK)
    return pl.pallas_call(
        kern,
        grid=(bsz, s // tm),
        in_specs=[
            pl.BlockSpec((1, tm, d), tok),
            pl.BlockSpec((1, tm, osb.shape[2]), tok),
            pl.BlockSpec((1, tm, ofox.shape[2]), tok),
            pl.BlockSpec((1, tm, osgu.shape[2]), tok),
            pl.BlockSpec((None, 1, 6, d), lambda b, i: (layer, b, 0, 0)),
            of_layer(g2),
            of_layer(wo, pipeline_mode=single),
            of_layer(w1, pipeline_mode=single),
            of_layer(w2, pipeline_mode=single),
        ],
        out_specs=pl.BlockSpec((1, tm, d), tok),
        out_shape=jax.ShapeDtypeStruct((bsz, s, d), F32),
        compiler_params=pltpu.CompilerParams(
            dimension_semantics=("arbitrary", "arbitrary"), vmem_limit_bytes=VMEM_LIMIT),
        name="out_mlp",
    )(x, osb, ofox, osgu, mod, g2, wo, w1, w2)


def kernel(x, c, ada_w, ada_b, norm1_g, norm2_g, w_in, b_forget, q_norm_g, k_norm_g, sgu_norm_g,
           sgu_w, sgu_b, w_out, mlp_w1, mlp_w2):
    depth, d, _ = ada_w.shape
    bsz, s, _ = x.shape
    fox_heads = b_forget.shape[1]
    fox_w = fox_heads * HEAD_DIM
    sgu_groups, chunk = sgu_b.shape[1], sgu_b.shape[2]
    sgu_wd = sgu_groups * sgu_norm_g.shape[2]
    sb_w = (w_in.shape[2] - 3 * fox_w - fox_heads - 2 * sgu_wd) // 3
    f_lo = 3 * sb_w + 3 * fox_w
    assert s % TOKEN_TILE == 0 and s % (2 * ATTN_BLOCK) == 0 and mlp_w1.shape[2] % FF_CHUNK == 0
    assert sb_w % LANES == 0 and ada_w.shape[2] % MOD_COLS == 0

    mod = _modulation(c, ada_w, ada_b).reshape(depth, bsz, 6, d)

    perm = jnp.argsort(b_forget, axis=1)
    w_bf = w_in.astype(BF16)
    w = jnp.concatenate([w_bf[:, :, :f_lo], w_bf[:, :, f_lo + fox_heads:]], axis=2)
    wf = jnp.pad(w_bf[:, :, f_lo:f_lo + fox_heads], ((0, 0), (0, 0), (0, LANES - fox_heads)))
    bf = jnp.pad(b_forget, ((0, 0), (0, LANES - fox_heads))).reshape(depth, 1, LANES)
    qg = jnp.tile(q_norm_g, (1, MXU_DIM // HEAD_DIM)).reshape(depth, 1, MXU_DIM)
    kg = jnp.tile(k_norm_g, (1, MXU_DIM // HEAD_DIM)).reshape(depth, 1, MXU_DIM)
    sg = sgu_norm_g.reshape(depth, 1, sgu_wd)
    sb = jnp.repeat(jnp.swapaxes(sgu_b, 1, 2), sgu_norm_g.shape[2], axis=2)
    qk_bound = ((BF16_MARGIN * HEAD_DIM ** 0.5) * jnp.max(jnp.abs(q_norm_g), axis=1)
                * jnp.max(jnp.abs(k_norm_g), axis=1))
    qx = jnp.zeros((depth, 1, LANES), F32).at[:, 0, HEAD_DIM:HEAD_DIM + 3].set(-1.0)
    qx = qx.at[:, 0, HEAD_DIM + 6].set(-qk_bound)
    wo_fox = w_out[:, sb_w:sb_w + fox_w].reshape(depth, fox_heads, HEAD_DIM, d)
    wo_fox = jnp.take_along_axis(wo_fox, perm[:, :, None, None], axis=1).reshape(depth, fox_w, d)
    wo = jnp.concatenate([w_out[:, :sb_w], wo_fox, w_out[:, sb_w + fox_w:]], axis=1).astype(BF16)
    w1 = mlp_w1.astype(BF16)
    w2 = mlp_w2.astype(BF16)
    g1 = norm1_g.reshape(depth, 1, d)
    g2 = norm2_g.reshape(depth, 1, d)

    for l in range(depth):
        qa, ka, vat, qf, kf, vft, osgu, f = _in_proj(
            x, mod, g1, w, wf, bf, qx, qg, kg, sg, sgu_w, sb,
            layer=l, tm=TOKEN_TILE, sb_w=sb_w, fox_w=fox_w, sgu_w=sgu_wd)
        osb = _sb_attention(qa, ka, vat, tq=ATTN_BLOCK)
        ofox = _fox_attention(qf, kf, vft, f, qk_bound[l], perm[l], tq=ATTN_BLOCK)
        x = _out_mlp(x, osb, ofox, osgu, mod, g2, wo, w1, w2, layer=l, tm=TOKEN_TILE)
    return x
```

```python
import functools
import math

import jax
import jax.numpy as jnp
from jax import lax
from jax.experimental import pallas as pl
from jax.experimental.pallas import tpu as pltpu

HEAD_DIM = 64
LANES = 128
MXU_DIM = 256
V_ROWS = 2 * HEAD_DIM
PART_STRIDE = 8
BF16_MARGIN = 1.03

TOKEN_TILE = 512
ATTN_BLOCK = MXU_DIM
MOD_COLS = 1536
FF_CHUNK = 1024
EPS = 1e-6
NEG_BIG = -1e30
EXP_ZERO = -104.0
LOG2_E = math.log2(math.e)
MAX_UNSTABILISED_LOGIT = 40.0
VMEM_LIMIT = 56 * 1024 * 1024

F32 = jnp.float32
BF16 = jnp.bfloat16


def _dot(a, b):
    return jnp.dot(a, b, preferred_element_type=F32)


def _dot_nt(a, b):
    return lax.dot_general(a, b, (((1,), (1,)), ((), ())), preferred_element_type=F32)


def _block_start(j, size):
    return j * size if isinstance(j, int) else pl.multiple_of(j * size, size)


def _split3(x):
    hi = x.astype(BF16)
    r = x - hi.astype(F32)
    mid = r.astype(BF16)
    lo = (r - mid.astype(F32)).astype(BF16)
    return hi, mid, lo


def _group_mean_matrix():
    head = jnp.arange(MXU_DIM) // HEAD_DIM
    return jnp.where(head[:, None] == head[None, :], 1.0 / HEAD_DIM, 0.0).astype(BF16)


def _placement_matrix(n_heads):
    row = jnp.arange(LANES)[:, None]
    col = jnp.arange(2 * n_heads * LANES)[None, :]
    tile = col // LANES
    is_q = (tile >= n_heads).astype(jnp.int32)
    part = col % LANES - HEAD_DIM - 3 * is_q
    hit = (part >= 0) & (part < 3) & (row == part * PART_STRIDE + tile - n_heads * is_q)
    return hit.astype(BF16)


def _head_rmsnorm(t, gmat, gain):
    ms = _dot((t * t).astype(BF16), gmat)
    return t * lax.rsqrt(ms + EPS) * gain


def _gelu_tanh(x):
    c = math.sqrt(2.0 / math.pi)
    return x * (0.5 * (1.0 + jnp.tanh(c * (x + 0.044715 * (x * x * x)))))


def _log_sigmoid(x):
    return jnp.minimum(x, 0.0) - jnp.log(1.0 + jnp.exp(-jnp.abs(x)))


def _mod_kernel(ct_ref, w_ref, b_ref, o_ref):
    ct = ct_ref[...]
    cond = ct * (1.0 / (1.0 + jnp.exp(-ct)))
    w = w_ref[0]
    rows = [jnp.sum(cond[:, b:b + 1] * w, axis=0, keepdims=True) for b in range(ct.shape[1])]
    o_ref[0] = jnp.concatenate(rows, axis=0) + b_ref[0]


def _modulation(c, ada_w, ada_b):
    depth, d, n = ada_w.shape
    bsz = c.shape[0]
    tn = MOD_COLS
    return pl.pallas_call(
        _mod_kernel,
        grid=(depth, n // tn),
        in_specs=[
            pl.BlockSpec((d, bsz), lambda l, j: (0, 0)),
            pl.BlockSpec((1, d, tn), lambda l, j: (l, 0, j)),
            pl.BlockSpec((1, 1, tn), lambda l, j: (l, 0, j)),
        ],
        out_specs=pl.BlockSpec((1, bsz, tn), lambda l, j: (l, 0, j)),
        out_shape=jax.ShapeDtypeStruct((depth, bsz, n), F32),
        compiler_params=pltpu.CompilerParams(
            dimension_semantics=("arbitrary", "arbitrary"), vmem_limit_bytes=VMEM_LIMIT),
        name="adaln_mod",
    )(c.T, ada_w, ada_b.reshape(depth, 1, n))


def _in_kernel(x_ref, mod_ref, g1_ref, w_ref, wsgu_ref, wf_ref, bf_ref, qx_ref, qg_ref, kg_ref,
               sg_ref, sw_ref, sb_ref, ltri_ref, sel_ref, gmat_ref,
               qa_ref, ka_ref, vat_ref, qf_ref, kf_ref, vft_ref, og_ref, f_ref,
               carry_ref, proj_ref, *, sb_w, fox_w, sgu_w, chunk):
    @pl.when(pl.program_id(1) == 0)
    def _():
        carry_ref[...] = jnp.zeros_like(carry_ref)

    tm = x_ref.shape[1]
    x = x_ref[0]
    ms = jnp.mean(x * x, axis=-1, keepdims=True)
    h = x * lax.rsqrt(ms + EPS) * g1_ref[...]
    h = h * (1.0 + mod_ref[0, 1:2, :]) + mod_ref[0, 0:1, :]
    hb = h.astype(BF16)
    scale = HEAD_DIM ** -0.5
    n_attn = w_ref.shape[1]
    wn = n_attn + wsgu_ref.shape[1]
    n_heads = f_ref.shape[1]
    lane = lax.broadcasted_iota(jnp.int32, (1, LANES), 1)
    head_lane = lane < n_heads
    gmat = gmat_ref[...]

    def pack3(parts):
        a, b, c = (p.astype(F32) for p in parts)
        return (a + pltpu.roll(b, PART_STRIDE, 1) + pltpu.roll(c, 2 * PART_STRIDE, 1)).astype(BF16)

    def project(lo, hi):
        proj_ref[:, lo:hi] = _dot(hb, w_ref[:, lo:hi])

    o_q, o_k, o_v = 3 * sb_w, 3 * sb_w + fox_w, 3 * sb_w + 2 * fox_w
    proj_ref[:, n_attn:wn] = _dot(hb, wsgu_ref[...])
    fl = _dot(hb, wf_ref[...].astype(BF16))
    project(o_q, o_k)
    gu = _gelu_tanh(proj_ref[:, n_attn:n_attn + sgu_w])
    gv = _gelu_tanh(proj_ref[:, n_attn + sgu_w:wn])
    project(o_k, o_v)
    logf = jnp.where(head_lane, _log_sigmoid(fl + bf_ref[...]), 0.0)
    cp = _dot(ltri_ref[...], pack3(_split3(logf)))
    vn_all = _head_rmsnorm(gv, gmat, sg_ref[...]).astype(BF16)
    q_tiles = [_head_rmsnorm(proj_ref[:, o_q + j * MXU_DIM:o_q + (j + 1) * MXU_DIM], gmat, qg_ref[...]) * scale
               for j in range(fox_w // MXU_DIM)]
    project(o_v, n_attn)

    cum = cp + pltpu.roll(cp, LANES - PART_STRIDE, 1) + pltpu.roll(cp, LANES - 2 * PART_STRIDE, 1)
    cum = jnp.where(head_lane, cum, 0.0) + carry_ref[0:1, :]
    carry_ref[0:1, :] = cum[tm - 1:tm, :]
    f_ref[0] = cum.T[0:n_heads, :]
    extras = _dot(pack3(_split3(cum)), sel_ref[...])
    k_tiles = [_head_rmsnorm(proj_ref[:, o_k + j * MXU_DIM:o_k + (j + 1) * MXU_DIM], gmat, kg_ref[...])
               for j in range(fox_w // MXU_DIM)]

    lane_c = lax.broadcasted_iota(jnp.int32, (chunk, LANES), 1)
    rr = lax.broadcasted_iota(jnp.int32, (chunk, chunk), 0)
    cs = lax.broadcasted_iota(jnp.int32, (chunk, chunk), 1)
    wt = [jnp.where(rr >= cs, sw_ref[g], 0.0).astype(BF16) for g in range(sw_ref.shape[0])]
    for p in range(sgu_w // LANES):
        vn = vn_all[:, p * LANES:(p + 1) * LANES]
        for ci in range(tm // chunk):
            vblk = vn[ci * chunk:(ci + 1) * chunk, :]
            mixed = jnp.where(lane_c < HEAD_DIM, _dot(wt[2 * p], vblk), _dot(wt[2 * p + 1], vblk))
            mixed = mixed + sb_ref[:, p * LANES:(p + 1) * LANES]
            og_ref[0, ci * chunk:(ci + 1) * chunk, p * LANES:(p + 1) * LANES] = (
                gu[ci * chunk:(ci + 1) * chunk, p * LANES:(p + 1) * LANES] * mixed).astype(BF16)

    project(0, o_q)
    qa_ref[0] = (proj_ref[:, 0:sb_w] * (scale * LOG2_E)).astype(BF16)
    ka_ref[0] = proj_ref[:, sb_w:2 * sb_w].astype(BF16)
    vat_ref[0] = proj_ref[:, 2 * sb_w:3 * sb_w].T.astype(BF16)

    is_head = lane < HEAD_DIM
    q_extra = qx_ref[...]
    k_extra = jnp.where((lane >= HEAD_DIM + 3) & (lane < HEAD_DIM + 7), 1.0, 0.0)
    heads_per_mxu = MXU_DIM // HEAD_DIM

    def head_tile(tn, s):
        half = tn[:, (s // 2) * LANES:(s // 2 + 1) * LANES]
        return half if s % 2 == 0 else pltpu.roll(half, HEAD_DIM, 1)

    for j in range(fox_w // MXU_DIM):
        for s in range(heads_per_mxu):
            hh = heads_per_mxu * j + s
            xq = extras[:, (n_heads + hh) * LANES:(n_heads + hh + 1) * LANES] + q_extra
            qf_ref[0, :, hh * LANES:(hh + 1) * LANES] = jnp.where(
                is_head, head_tile(q_tiles[j], s), xq).astype(BF16)
            xk = extras[:, hh * LANES:(hh + 1) * LANES] + k_extra
            kf_ref[0, :, hh * LANES:(hh + 1) * LANES] = jnp.where(
                is_head, head_tile(k_tiles[j], s), xk).astype(BF16)
    vt = proj_ref[:, o_v:n_attn].T.astype(BF16)
    ones = jnp.ones((V_ROWS - HEAD_DIM, tm), BF16)
    for hh in range(fox_w // HEAD_DIM):
        vft_ref[0, hh * V_ROWS:hh * V_ROWS + HEAD_DIM, :] = vt[hh * HEAD_DIM:(hh + 1) * HEAD_DIM, :]
        vft_ref[0, hh * V_ROWS + HEAD_DIM:(hh + 1) * V_ROWS, :] = ones


def _in_proj(x, mod, g1, w, wsgu, w_in, bf, qx, qg, kg, sg, sw, sb, *, layer, tm, sb_w, fox_w, sgu_w):
    bsz, s, d = x.shape
    chunk = sw.shape[-1]
    n_attn = w.shape[2]
    wn = n_attn + wsgu.shape[2]
    fox_heads = fox_w // HEAD_DIM
    assert n_attn % LANES == 0 and n_attn + LANES <= w_in.shape[2]

    def of_layer(a):
        nd = a.ndim - 1
        return pl.BlockSpec((None,) + a.shape[1:], lambda b, i: (layer,) + (0,) * nd)

    assert fox_heads <= PART_STRIDE and fox_w % MXU_DIM == 0 and sgu_w == MXU_DIM
    assert s % tm == 0 and tm % chunk == 0
    tok_idx = jnp.arange(tm)
    ltri = (tok_idx[:, None] >= tok_idx[None, :]).astype(BF16)
    const2 = lambda b, i: (0, 0)
    tok = lambda b, i: (b, i, 0)
    tok_t = lambda b, i: (b, 0, i)
    kern = functools.partial(_in_kernel, sb_w=sb_w, fox_w=fox_w, sgu_w=sgu_w, chunk=chunk)
    out_shape = [
        jax.ShapeDtypeStruct((bsz, s, sb_w), BF16),
        jax.ShapeDtypeStruct((bsz, s, sb_w), BF16),
        jax.ShapeDtypeStruct((bsz, sb_w, s), BF16),
        jax.ShapeDtypeStruct((bsz, s, fox_heads * LANES), BF16),
        jax.ShapeDtypeStruct((bsz, s, fox_heads * LANES), BF16),
        jax.ShapeDtypeStruct((bsz, fox_heads * V_ROWS, s), BF16),
        jax.ShapeDtypeStruct((bsz, s, sgu_w), BF16),
        jax.ShapeDtypeStruct((bsz, fox_heads, s), F32),
    ]
    out_specs = [
        pl.BlockSpec((1, tm, sb_w), tok),
        pl.BlockSpec((1, tm, sb_w), tok),
        pl.BlockSpec((1, sb_w, tm), tok_t),
        pl.BlockSpec((1, tm, fox_heads * LANES), tok),
        pl.BlockSpec((1, tm, fox_heads * LANES), tok),
        pl.BlockSpec((1, fox_heads * V_ROWS, tm), tok_t),
        pl.BlockSpec((1, tm, sgu_w), tok),
        pl.BlockSpec((1, fox_heads, tm), tok_t),
    ]
    return pl.pallas_call(
        kern,
        grid=(bsz, s // tm),
        in_specs=[
            pl.BlockSpec((1, tm, d), tok),
            pl.BlockSpec((None, 1, 6, d), lambda b, i: (layer, b, 0, 0)),
            of_layer(g1), of_layer(w), of_layer(wsgu),
            pl.BlockSpec((None, d, LANES), lambda b, i: (layer, 0, n_attn // LANES)),
            of_layer(bf), of_layer(qx), of_layer(qg),
            of_layer(kg), of_layer(sg), of_layer(sw), of_layer(sb),
            pl.BlockSpec((tm, tm), const2),
            pl.BlockSpec((LANES, 2 * fox_heads * LANES), const2),
            pl.BlockSpec((MXU_DIM, MXU_DIM), const2),
        ],
        out_specs=out_specs,
        out_shape=out_shape,
        scratch_shapes=[pltpu.VMEM((8, LANES), F32), pltpu.VMEM((tm, wn), F32)],
        compiler_params=pltpu.CompilerParams(
            dimension_semantics=("arbitrary", "arbitrary"), vmem_limit_bytes=VMEM_LIMIT),
        name="in_proj",
    )(x, mod, g1, w, wsgu, w_in, bf, qx, qg, kg, sg, sw, sb, ltri, _placement_matrix(fox_heads),
      _group_mean_matrix())


def _sb_kernel(q_ref, k_ref, vt_ref, after_ref, o_ref, acc_ref, c_ref, *, tq):
    tk = tq
    n_blk = q_ref.shape[1] // tq
    lane = lax.broadcasted_iota(jnp.int32, (1, LANES), 1)
    key = lax.broadcasted_iota(jnp.int32, (tk, tq), 0)
    qry = lax.broadcasted_iota(jnp.int32, (tk, tq), 1)
    causal = key < qry
    def process(chains, size=tk):
        after = after_ref[0:size, 0:size]
        units, starts, qh = [], {}, {}
        for slot, qi, blocks, first in chains:
            q = q_ref[0, pl.ds(_block_start(qi, tq), tq), :]
            qh[slot] = (jnp.where(lane < HEAD_DIM, q, 0).astype(BF16),
                        jnp.where(lane >= HEAD_DIM, q, 0).astype(BF16))
            for b, j in enumerate(blocks):
                starts[slot, b] = _block_start(j, size)
                units += [(slot, b, h, first and b == 0) for h in range(2)]
        z = {(sl, b, h): _dot_nt(k_ref[0, pl.ds(starts[sl, b], size), :], qh[sl][h])
             for sl, b, h, _ in units}
        l1mb, head = {}, {}
        for sl, b, h, diag in units:
            zz = z[sl, b, h]
            nz = -zz
            lg = jnp.minimum(nz, 0.0) - jnp.log2(1.0 + jnp.exp2(jnp.minimum(zz, nz)))
            if diag:
                lg = jnp.where(causal, lg, 0.0)
            l1mb[sl, b, h] = lg.astype(BF16)
            head[sl, b, h] = (zz + lg, lg[0:1, :])
        between = {(sl, b, h): _dot(after, l1mb[sl, b, h]) for sl, b, h, _ in units}
        a = {}
        for slot, qi, blocks, first in chains:
            for h in range(2):
                c = None if first else c_ref[slot, h]
                for b in range(len(blocks)):
                    e = head[slot, b, h][0] + between[slot, b, h]
                    if c is not None:
                        e = e + c
                    w = jnp.exp2(e)
                    if first and b == 0:
                        w = jnp.where(causal, w, 0.0)
                    a[slot, b, h] = w.astype(BF16)
                    block_sum = between[slot, b, h][0:1, :] + head[slot, b, h][1]
                    c = block_sum if c is None else c + block_sum
                c_ref[slot, h] = c
        for slot, qi, blocks, first in chains:
            for h in range(2):
                pv = None
                for b in range(len(blocks)):
                    part = _dot(vt_ref[0, h * HEAD_DIM:(h + 1) * HEAD_DIM, pl.ds(starts[slot, b], size)],
                                a[slot, b, h])
                    pv = part if pv is None else pv + part
                if first:
                    acc_ref[slot, h] = pv
                else:
                    acc_ref[slot, h] += pv

    def carry_max(slot):
        cm = jnp.maximum(c_ref[slot, 0], c_ref[slot, 1])
        return jnp.max(cm, axis=1, keepdims=True)[0, 0]

    def finish(slot, qi):
        half = tk // 2

        def cond(carry):
            j, cmax = carry
            return jnp.logical_and(j >= 0, cmax > EXP_ZERO * LOG2_E)

        def body(carry):
            j, _ = carry
            process([(slot, qi, [j], False)], size=half)
            return j - 1, carry_max(slot)

        lax.while_loop(cond, body, (jnp.asarray(2 * (qi - 1) - 1, jnp.int32), carry_max(slot)))
        o_ref[0, pl.ds(_block_start(qi, tq), tq), :] = (
            jnp.concatenate([acc_ref[slot, 0], acc_ref[slot, 1]], axis=0).T.astype(o_ref.dtype))

    process([(0, 0, [0], True)])
    finish(0, 0)

    def two_blocks(g, carry):
        qa, qb = 1 + 2 * g, 2 + 2 * g
        process([(0, qa, [qa, qa - 1], True), (1, qb, [qb, qb - 1], True)])
        finish(0, qa)
        finish(1, qb)
        return carry

    lax.fori_loop(0, (n_blk - 1) // 2, two_blocks, 0)
    if (n_blk - 1) % 2:
        last = n_blk - 1
        process([(0, last, [last, last - 1], True)])
        finish(0, last)


def _sb_attention(q, k, vt, *, tq):
    bsz, s, w = q.shape
    assert s >= 2 * tq
    idx = jnp.arange(tq)
    after = (idx[None, :] > idx[:, None]).astype(BF16)
    kern = functools.partial(_sb_kernel, tq=tq)
    return pl.pallas_call(
        kern,
        grid=(bsz, w // LANES),
        in_specs=[
            pl.BlockSpec((1, s, LANES), lambda b, p: (b, 0, p)),
            pl.BlockSpec((1, s, LANES), lambda b, p: (b, 0, p)),
            pl.BlockSpec((1, LANES, s), lambda b, p: (b, p, 0)),
            pl.BlockSpec((tq, tq), lambda b, p: (0, 0)),
        ],
        out_specs=pl.BlockSpec((1, s, LANES), lambda b, p: (b, 0, p)),
        out_shape=jax.ShapeDtypeStruct((bsz, s, w), BF16),
        scratch_shapes=[pltpu.VMEM((2, 2, HEAD_DIM, tq), F32), pltpu.VMEM((2, 2, 1, tq), F32)],
        compiler_params=pltpu.CompilerParams(
            dimension_semantics=("arbitrary", "arbitrary"), vmem_limit_bytes=VMEM_LIMIT),
        name="sb_attn",
    )(q, k, vt, after)


def _fox_kernel(fend_ref, par_ref, order_ref, q0_ref, q1_ref, k0_ref, k1_ref, v0_ref, v1_ref, o_ref,
                acc_ref, m_ref, p_ref, *, tq, n_heads):
    tk = tq
    qs, ks, vs = (q0_ref, q1_ref), (k0_ref, k1_ref), (v0_ref, v1_ref)
    n_blk = q0_ref.shape[1] // tq
    pair = pl.program_id(1)
    heads = tuple(pl.program_id(0) * n_heads + order_ref[2 * pair + h] for h in range(2))

    def last_dead_block(head, qi, j_prev):
        base = head * n_blk
        f_q = fend_ref[base + jnp.maximum(qi - 1, 0)]

        def next_is_dead(j):
            jn = jnp.minimum(j + 1, n_blk - 1)
            return jnp.logical_and(j + 1 < qi, f_q - fend_ref[base + jn] < -par_ref[0])

        return lax.while_loop(next_is_dead, lambda j: j + 1, j_prev)

    def plan(qi, dead):
        dead = (last_dead_block(heads[0], qi, dead[0]), last_dead_block(heads[1], qi, dead[1]))
        j_dead = jnp.minimum(dead[0], dead[1])
        n_left = jnp.maximum(qi - 2 - j_dead, 0)
        odd = n_left % 2
        has_dead_below = j_dead >= 0
        n_pairs = n_left // 2 + jnp.where(has_dead_below, odd, 0)
        return n_pairs, jnp.logical_and(odd == 1, jnp.logical_not(has_dead_below)), dead

    def q_tile(qi, h):
        return qs[h][0, pl.ds(pl.multiple_of(qi * tq, tq), tq), :]

    def finalize(qi):
        outs = []
        for h in range(2):
            acc = acc_ref[h]
            outs.append(acc[0:HEAD_DIM, :] / acc[HEAD_DIM:HEAD_DIM + 1, :])
        o_ref[0, pl.ds(pl.multiple_of(qi * tq, tq), tq), :] = (
            jnp.concatenate(outs, axis=0).T.astype(o_ref.dtype))

    key = lax.broadcasted_iota(jnp.int32, (tk, tq), 0)
    qry = lax.broadcasted_iota(jnp.int32, (tk, tq), 1)
    causal = key <= qry

    def process(qi, blocks, first):
        starts = [pl.multiple_of(j * tk, tk) for j in blocks]
        units = [(b, h) for b in range(len(blocks)) for h in range(2)]
        s = {(b, h): _dot_nt(ks[h][0, pl.ds(starts[b], tk), :],
                             q_tile(qi, h)) for b, h in units}
        p = {}
        alpha = {}
        for h in range(2):
            if first:
                s[0, h] = jnp.where(causal, s[0, h], NEG_BIG)
            m_new = None if first else m_ref[h]
            for b in range(len(blocks)):
                mb = jnp.max(s[b, h], axis=0, keepdims=True)
                m_new = mb if m_new is None else jnp.maximum(m_new, mb)
            if not first:
                alpha[h] = jnp.exp(m_ref[h] - m_new)
            for b in range(len(blocks)):
                p[b, h] = jnp.exp(s[b, h] - m_new).astype(BF16)
            m_ref[h] = m_new
        for h in range(2):
            pv = None
            for b in range(len(blocks)):
                part = _dot(vs[h][0, :, pl.ds(starts[b], tk)], p[b, h])
                pv = part if pv is None else pv + part
            acc_ref[h] = pv if first else alpha[h] * acc_ref[h] + pv

    def online_path():
        def q_block(qi, dead):
            n_pairs, lone_block0, dead = plan(qi, dead)

            @pl.when(qi == 0)
            def _():
                process(qi, [qi], True)

            @pl.when(qi > 0)
            def _():
                process(qi, [qi, qi - 1], True)

            def body(i, c):
                j = qi - 2 - 2 * i
                process(qi, [j, j - 1], False)
                return c

            lax.fori_loop(0, n_pairs, body, 0)

            @pl.when(lone_block0)
            def _():
                process(qi, [0], False)

            finalize(qi)
            return dead

        lax.fori_loop(0, n_blk, q_block, (jnp.int32(-1), jnp.int32(-1)))

    win = 2 * tk
    wrow = lax.broadcasted_iota(jnp.int32, (win, tq), 0)
    wcol = lax.broadcasted_iota(jnp.int32, (win, tq), 1)

    def window(qi, i):
        jl = qi - 1 - 2 * i
        j_lo = jnp.maximum(jl, 0)
        return j_lo, pl.multiple_of(j_lo * tk, tk), jnp.where(jl < 0, tk, win)

    def stage_a(qi, i, slot, first):
        j_lo, start, row_lim = window(qi, i)
        keep = wrow < row_lim
        if first:
            keep = jnp.logical_and(keep, wrow <= wcol + (qi - j_lo) * tk)
        for h in range(2):
            s = _dot_nt(ks[h][0, pl.ds(start, win), :], q_tile(qi, h))
            p_ref[slot, h] = jnp.exp(jnp.where(keep, s, NEG_BIG)).astype(BF16)

    def stage_c(qi, i, slot):
        _, start, _ = window(qi, i)
        for h in range(2):
            acc_ref[h] += _dot(vs[h][0, :, pl.ds(start, win)], p_ref[slot, h])

    def bounded_path():
        acc_ref[...] = jnp.zeros_like(acc_ref)
        stage_a(0, 0, 0, True)

        def q_block(qi, carry):
            t, dead = carry[0], carry[1:]
            n_pairs, lone_block0, dead = plan(qi, dead)
            n_stages = 1 + n_pairs + jnp.where(lone_block0, 1, 0)

            def body(i, t):
                stage_c(qi, i - 1, t & 1)
                stage_a(qi, i, (t + 1) & 1, False)
                return t + 1

            t = lax.fori_loop(1, n_stages, body, t)
            stage_c(qi, n_stages - 1, t & 1)
            stage_a(jnp.minimum(qi + 1, n_blk - 1), 0, (t + 1) & 1, True)
            finalize(qi)
            acc_ref[...] = jnp.zeros_like(acc_ref)
            return (t + 1,) + dead

        lax.fori_loop(0, n_blk, q_block, (jnp.int32(0), jnp.int32(-1), jnp.int32(-1)))

    bounded = par_ref[1] > 0.5
    pl.when(bounded)(bounded_path)
    pl.when(jnp.logical_not(bounded))(online_path)


def _fox_attention(q, k, vt, f, qk_bound, order, *, tq):
    bsz, s, w = q.shape
    n_heads = w // LANES
    assert s >= 2 * tq
    f_end = f[:, :, tq - 1::tq].reshape(-1)
    bounded = qk_bound <= MAX_UNSTABILISED_LOGIT
    stabiliser_slack = 0.5
    par = jnp.stack([jnp.where(bounded, stabiliser_slack - EXP_ZERO, 2.0 * qk_bound - EXP_ZERO),
                     jnp.where(bounded, 1.0, 0.0)]).astype(F32)
    kern = functools.partial(_fox_kernel, tq=tq, n_heads=n_heads)
    tok_tile = lambda h: pl.BlockSpec((1, s, LANES), lambda b, p, fe, th, od: (b, 0, od[2 * p + h]))
    val_tile = lambda h: pl.BlockSpec((1, V_ROWS, s), lambda b, p, fe, th, od: (b, od[2 * p + h], 0))
    grid_spec = pltpu.PrefetchScalarGridSpec(
        num_scalar_prefetch=3,
        grid=(bsz, n_heads // 2),
        in_specs=[tok_tile(0), tok_tile(1), tok_tile(0), tok_tile(1), val_tile(0), val_tile(1)],
        out_specs=pl.BlockSpec((1, s, LANES), lambda b, p, fe, th, od: (b, 0, p)),
        scratch_shapes=[pltpu.VMEM((2, V_ROWS, tq), F32), pltpu.VMEM((2, 1, tq), F32),
                        pltpu.VMEM((2, 2, 2 * tq, tq), BF16)],
    )
    return pl.pallas_call(
        kern,
        grid_spec=grid_spec,
        out_shape=jax.ShapeDtypeStruct((bsz, s, w // 2), BF16),
        compiler_params=pltpu.CompilerParams(
            dimension_semantics=("arbitrary", "arbitrary"), vmem_limit_bytes=VMEM_LIMIT),
        name="fox_attn",
    )(f_end, par, order.astype(jnp.int32), q, q, k, k, vt, vt)


def _out_kernel(x_ref, osb_ref, ofox_ref, osgu_ref, mod_ref, g2_ref, wo_ref, w1_ref, w2_ref,
                o_ref, *, ff_chunk):
    sb_w = osb_ref.shape[2]
    fox_w = ofox_ref.shape[2]
    x = x_ref[0]
    mix = (_dot(osb_ref[0], wo_ref[0:sb_w, :])
           + _dot(ofox_ref[0], wo_ref[sb_w:sb_w + fox_w, :])
           + _dot(osgu_ref[0], wo_ref[sb_w + fox_w:, :]))
    x1 = x + mod_ref[0, 2:3, :] * mix
    ms = jnp.mean(x1 * x1, axis=-1, keepdims=True)
    h = x1 * lax.rsqrt(ms + EPS) * g2_ref[...]
    hb = (h * (1.0 + mod_ref[0, 4:5, :]) + mod_ref[0, 3:4, :]).astype(BF16)
    d_ff = w1_ref.shape[1]
    acc = None
    for c in range(d_ff // ff_chunk):
        hid = jnp.maximum(_dot(hb, w1_ref[:, c * ff_chunk:(c + 1) * ff_chunk]), 0.0)
        part = _dot((hid * hid).astype(BF16), w2_ref[c * ff_chunk:(c + 1) * ff_chunk, :])
        acc = part if acc is None else acc + part
    o_ref[0] = x1 + mod_ref[0, 5:6, :] * acc


def _out_mlp(x, osb, ofox, osgu, mod, g2, wo, w1, w2, *, layer, tm):
    bsz, s, d = x.shape
    tok = lambda b, i: (b, i, 0)

    def of_layer(a, **kw):
        nd = a.ndim - 1
        return pl.BlockSpec((None,) + a.shape[1:], lambda b, i: (layer,) + (0,) * nd, **kw)

    single = pl.Buffered(1)
    kern = functools.partial(_out_kernel, ff_chunk=FF_CHUNK)
    return pl.pallas_call(
        kern,
        grid=(bsz, s // tm),
        in_specs=[
            pl.BlockSpec((1, tm, d), tok),
            pl.BlockSpec((1, tm, osb.shape[2]), tok),
            pl.BlockSpec((1, tm, ofox.shape[2]), tok),
            pl.BlockSpec((1, tm, osgu.shape[2]), tok),
            pl.BlockSpec((None, 1, 6, d), lambda b, i: (layer, b, 0, 0)),
            of_layer(g2),
            of_layer(wo, pipeline_mode=single),
            of_layer(w1, pipeline_mode=single),
            of_layer(w2, pipeline_mode=single),
        ],
        out_specs=pl.BlockSpec((1, tm, d), tok),
        out_shape=jax.ShapeDtypeStruct((bsz, s, d), F32),
        compiler_params=pltpu.CompilerParams(
            dimension_semantics=("arbitrary", "arbitrary"), vmem_limit_bytes=VMEM_LIMIT),
        name="out_mlp",
    )(x, osb, ofox, osgu, mod, g2, wo, w1, w2)


def kernel(x, c, ada_w, ada_b, norm1_g, norm2_g, w_in, b_forget, q_norm_g, k_norm_g, sgu_norm_g,
           sgu_w, sgu_b, w_out, mlp_w1, mlp_w2):
    depth, d, _ = ada_w.shape
    bsz, s, _ = x.shape
    fox_heads = b_forget.shape[1]
    fox_w = fox_heads * HEAD_DIM
    sgu_groups, chunk = sgu_b.shape[1], sgu_b.shape[2]
    sgu_wd = sgu_groups * sgu_norm_g.shape[2]
    sb_w = (w_in.shape[2] - 3 * fox_w - fox_heads - 2 * sgu_wd) // 3
    f_lo = 3 * sb_w + 3 * fox_w
    assert s % TOKEN_TILE == 0 and s % (2 * ATTN_BLOCK) == 0 and mlp_w1.shape[2] % FF_CHUNK == 0
    assert sb_w % LANES == 0 and ada_w.shape[2] % MOD_COLS == 0

    mod = _modulation(c, ada_w, ada_b).reshape(depth, bsz, 6, d)

    perm = jnp.argsort(b_forget, axis=1)
    w = w_in[:, :, :f_lo].astype(BF16)
    wsgu = w_in[:, :, f_lo + fox_heads:].astype(BF16)
    bf = jnp.pad(b_forget, ((0, 0), (0, LANES - fox_heads))).reshape(depth, 1, LANES)
    qg = jnp.tile(q_norm_g, (1, MXU_DIM // HEAD_DIM)).reshape(depth, 1, MXU_DIM)
    kg = jnp.tile(k_norm_g, (1, MXU_DIM // HEAD_DIM)).reshape(depth, 1, MXU_DIM)
    sg = sgu_norm_g.reshape(depth, 1, sgu_wd)
    sb = jnp.repeat(jnp.swapaxes(sgu_b, 1, 2), sgu_norm_g.shape[2], axis=2)
    qk_bound = ((BF16_MARGIN * HEAD_DIM ** 0.5) * jnp.max(jnp.abs(q_norm_g), axis=1)
                * jnp.max(jnp.abs(k_norm_g), axis=1))
    qx = jnp.zeros((depth, 1, LANES), F32).at[:, 0, HEAD_DIM:HEAD_DIM + 3].set(-1.0)
    qx = qx.at[:, 0, HEAD_DIM + 6].set(-qk_bound)
    wo_fox = w_out[:, sb_w:sb_w + fox_w].reshape(depth, fox_heads, HEAD_DIM, d)
    wo_fox = jnp.take_along_axis(wo_fox, perm[:, :, None, None], axis=1).reshape(depth, fox_w, d)
    wo = jnp.concatenate([w_out[:, :sb_w], wo_fox, w_out[:, sb_w + fox_w:]], axis=1).astype(BF16)
    w1 = mlp_w1.astype(BF16)
    w2 = mlp_w2.astype(BF16)
    g1 = norm1_g.reshape(depth, 1, d)
    g2 = norm2_g.reshape(depth, 1, d)

    for l in range(depth):
        qa, ka, vat, qf, kf, vft, osgu, f = _in_proj(
            x, mod, g1, w, wsgu, w_in, bf, qx, qg, kg, sg, sgu_w, sb,
            layer=l, tm=TOKEN_TILE, sb_w=sb_w, fox_w=fox_w, sgu_w=sgu_wd)
        osb = _sb_attention(qa, ka, vat, tq=ATTN_BLOCK)
        ofox = _fox_attention(qf, kf, vft, f, qk_bound[l], perm[l], tq=ATTN_BLOCK)
        x = _out_mlp(x, osb, ofox, osgu, mod, g2, wo, w1, w2, layer=l, tm=TOKEN_TILE)
    return x
```

```python
import functools
import math

import jax
import jax.numpy as jnp
from jax import lax
from jax.experimental import pallas as pl
from jax.experimental.pallas import tpu as pltpu

HEAD_DIM = 64
LANES = 128
MXU_DIM = 256
V_ROWS = 2 * HEAD_DIM
PART_STRIDE = 8
BF16_MARGIN = 1.03

TOKEN_TILE = 512
ATTN_BLOCK = MXU_DIM
MOD_COLS = 1536
FF_CHUNK = 1024
EPS = 1e-6
NEG_BIG = -1e30
EXP_ZERO = -104.0
LOG2_E = math.log2(math.e)
MAX_UNSTABILISED_LOGIT = 40.0
VMEM_LIMIT = 56 * 1024 * 1024

F32 = jnp.float32
BF16 = jnp.bfloat16


def _dot(a, b):
    return jnp.dot(a, b, preferred_element_type=F32)


def _dot_nt(a, b):
    return lax.dot_general(a, b, (((1,), (1,)), ((), ())), preferred_element_type=F32)


def _block_start(j, size):
    return j * size if isinstance(j, int) else pl.multiple_of(j * size, size)


def _split3(x):
    hi = x.astype(BF16)
    r = x - hi.astype(F32)
    mid = r.astype(BF16)
    lo = (r - mid.astype(F32)).astype(BF16)
    return hi, mid, lo


def _group_mean_matrix():
    head = jnp.arange(MXU_DIM) // HEAD_DIM
    return jnp.where(head[:, None] == head[None, :], 1.0 / HEAD_DIM, 0.0).astype(BF16)


def _placement_matrix(n_heads):
    row = jnp.arange(LANES)[:, None]
    col = jnp.arange(2 * n_heads * LANES)[None, :]
    tile = col // LANES
    is_q = (tile >= n_heads).astype(jnp.int32)
    part = col % LANES - HEAD_DIM - 3 * is_q
    hit = (part >= 0) & (part < 3) & (row == part * PART_STRIDE + tile - n_heads * is_q)
    return hit.astype(BF16)


def _head_rmsnorm(t, gmat, gain):
    ms = _dot((t * t).astype(BF16), gmat)
    return t * lax.rsqrt(ms + EPS) * gain


def _gelu_tanh(x):
    c = math.sqrt(2.0 / math.pi)
    return x * (0.5 * (1.0 + jnp.tanh(c * (x + 0.044715 * (x * x * x)))))


def _log_sigmoid(x):
    return jnp.minimum(x, 0.0) - jnp.log(1.0 + jnp.exp(-jnp.abs(x)))


def _mod_kernel(ct_ref, w_ref, b_ref, o_ref):
    ct = ct_ref[...]
    cond = ct * (1.0 / (1.0 + jnp.exp(-ct)))
    w = w_ref[0]
    rows = [jnp.sum(cond[:, b:b + 1] * w, axis=0, keepdims=True) for b in range(ct.shape[1])]
    o_ref[0] = jnp.concatenate(rows, axis=0) + b_ref[0]


def _modulation(c, ada_w, ada_b):
    depth, d, n = ada_w.shape
    bsz = c.shape[0]
    tn = MOD_COLS
    return pl.pallas_call(
        _mod_kernel,
        grid=(depth, n // tn),
        in_specs=[
            pl.BlockSpec((d, bsz), lambda l, j: (0, 0)),
            pl.BlockSpec((1, d, tn), lambda l, j: (l, 0, j)),
            pl.BlockSpec((1, 1, tn), lambda l, j: (l, 0, j)),
        ],
        out_specs=pl.BlockSpec((1, bsz, tn), lambda l, j: (l, 0, j)),
        out_shape=jax.ShapeDtypeStruct((depth, bsz, n), F32),
        compiler_params=pltpu.CompilerParams(
            dimension_semantics=("arbitrary", "arbitrary"), vmem_limit_bytes=VMEM_LIMIT),
        name="adaln_mod",
    )(c.T, ada_w, ada_b.reshape(depth, 1, n))


def _in_kernel(x_ref, mod_ref, g1_ref, w_ref, wsgu_ref, wf_ref, bf_ref, qx_ref, qg_ref, kg_ref,
               sg_ref, sw_ref, sb_ref, ltri_ref, sel_ref, gmat_ref,
               qa_ref, ka_ref, vat_ref, qf_ref, kf_ref, vft_ref, og_ref, f_ref,
               carry_ref, proj_ref, *, sb_w, fox_w, sgu_w, chunk):
    @pl.when(pl.program_id(1) == 0)
    def _():
        carry_ref[...] = jnp.zeros_like(carry_ref)

    tm = x_ref.shape[1]
    x = x_ref[0]
    ms = jnp.mean(x * x, axis=-1, keepdims=True)
    h = x * lax.rsqrt(ms + EPS) * g1_ref[...]
    h = h * (1.0 + mod_ref[0, 1:2, :]) + mod_ref[0, 0:1, :]
    hb = h.astype(BF16)
    scale = HEAD_DIM ** -0.5
    n_attn = w_ref.shape[1]
    wn = n_attn + wsgu_ref.shape[1]
    n_heads = f_ref.shape[1]
    lane = lax.broadcasted_iota(jnp.int32, (1, LANES), 1)
    head_lane = lane < n_heads
    gmat = gmat_ref[...]

    def pack3(parts):
        a, b, c = (p.astype(F32) for p in parts)
        return (a + pltpu.roll(b, PART_STRIDE, 1) + pltpu.roll(c, 2 * PART_STRIDE, 1)).astype(BF16)

    def project(lo, hi):
        proj_ref[:, lo:hi] = _dot(hb, w_ref[:, lo:hi])

    o_q, o_k, o_v = 3 * sb_w, 3 * sb_w + fox_w, 3 * sb_w + 2 * fox_w
    proj_ref[:, n_attn:wn] = _dot(hb, wsgu_ref[...])
    fl = _dot(hb, wf_ref[...])
    project(o_q, o_k)
    gu = _gelu_tanh(proj_ref[:, n_attn:n_attn + sgu_w])
    gv = _gelu_tanh(proj_ref[:, n_attn + sgu_w:wn])
    project(o_k, o_v)
    logf = jnp.where(head_lane, _log_sigmoid(fl + bf_ref[...]), 0.0)
    cp = _dot(ltri_ref[...], pack3(_split3(logf)))
    vn_all = _head_rmsnorm(gv, gmat, sg_ref[...]).astype(BF16)
    q_tiles = [_head_rmsnorm(proj_ref[:, o_q + j * MXU_DIM:o_q + (j + 1) * MXU_DIM], gmat, qg_ref[...]) * scale
               for j in range(fox_w // MXU_DIM)]
    project(o_v, n_attn)

    cum = cp + pltpu.roll(cp, LANES - PART_STRIDE, 1) + pltpu.roll(cp, LANES - 2 * PART_STRIDE, 1)
    cum = jnp.where(head_lane, cum, 0.0) + carry_ref[0:1, :]
    carry_ref[0:1, :] = cum[tm - 1:tm, :]
    f_ref[0] = cum.T[0:n_heads, :]
    extras = _dot(pack3(_split3(cum)), sel_ref[...])
    k_tiles = [_head_rmsnorm(proj_ref[:, o_k + j * MXU_DIM:o_k + (j + 1) * MXU_DIM], gmat, kg_ref[...])
               for j in range(fox_w // MXU_DIM)]

    lane_c = lax.broadcasted_iota(jnp.int32, (chunk, LANES), 1)
    rr = lax.broadcasted_iota(jnp.int32, (chunk, chunk), 0)
    cs = lax.broadcasted_iota(jnp.int32, (chunk, chunk), 1)
    wt = [jnp.where(rr >= cs, sw_ref[g], 0.0).astype(BF16) for g in range(sw_ref.shape[0])]
    for p in range(sgu_w // LANES):
        vn = vn_all[:, p * LANES:(p + 1) * LANES]
        for ci in range(tm // chunk):
            vblk = vn[ci * chunk:(ci + 1) * chunk, :]
            mixed = jnp.where(lane_c < HEAD_DIM, _dot(wt[2 * p], vblk), _dot(wt[2 * p + 1], vblk))
            mixed = mixed + sb_ref[:, p * LANES:(p + 1) * LANES]
            og_ref[0, ci * chunk:(ci + 1) * chunk, p * LANES:(p + 1) * LANES] = (
                gu[ci * chunk:(ci + 1) * chunk, p * LANES:(p + 1) * LANES] * mixed).astype(BF16)

    project(0, o_q)
    qa_ref[0] = (proj_ref[:, 0:sb_w] * (scale * LOG2_E)).astype(BF16)
    ka_ref[0] = proj_ref[:, sb_w:2 * sb_w].astype(BF16)
    vat_ref[0] = proj_ref[:, 2 * sb_w:3 * sb_w].T.astype(BF16)

    is_head = lane < HEAD_DIM
    q_extra = qx_ref[...]
    k_extra = jnp.where((lane >= HEAD_DIM + 3) & (lane < HEAD_DIM + 7), 1.0, 0.0)
    heads_per_mxu = MXU_DIM // HEAD_DIM

    def head_tile(tn, s):
        half = tn[:, (s // 2) * LANES:(s // 2 + 1) * LANES]
        return half if s % 2 == 0 else pltpu.roll(half, HEAD_DIM, 1)

    for j in range(fox_w // MXU_DIM):
        for s in range(heads_per_mxu):
            hh = heads_per_mxu * j + s
            xq = extras[:, (n_heads + hh) * LANES:(n_heads + hh + 1) * LANES] + q_extra
            qf_ref[0, :, hh * LANES:(hh + 1) * LANES] = jnp.where(
                is_head, head_tile(q_tiles[j], s), xq).astype(BF16)
            xk = extras[:, hh * LANES:(hh + 1) * LANES] + k_extra
            kf_ref[0, :, hh * LANES:(hh + 1) * LANES] = jnp.where(
                is_head, head_tile(k_tiles[j], s), xk).astype(BF16)
    vt = proj_ref[:, o_v:n_attn].T.astype(BF16)
    ones = jnp.ones((V_ROWS - HEAD_DIM, tm), BF16)
    for hh in range(fox_w // HEAD_DIM):
        vft_ref[0, hh * V_ROWS:hh * V_ROWS + HEAD_DIM, :] = vt[hh * HEAD_DIM:(hh + 1) * HEAD_DIM, :]
        vft_ref[0, hh * V_ROWS + HEAD_DIM:(hh + 1) * V_ROWS, :] = ones


def _in_proj(x, mod, g1, w, wsgu, wf, bf, qx, qg, kg, sg, sw, sb, *, layer, tm, sb_w, fox_w, sgu_w):
    bsz, s, d = x.shape
    chunk = sw.shape[-1]
    wn = w.shape[2] + wsgu.shape[2]
    fox_heads = fox_w // HEAD_DIM

    def of_layer(a):
        nd = a.ndim - 1
        return pl.BlockSpec((None,) + a.shape[1:], lambda b, i: (layer,) + (0,) * nd)

    assert fox_heads <= PART_STRIDE and fox_w % MXU_DIM == 0 and sgu_w == MXU_DIM
    assert s % tm == 0 and tm % chunk == 0
    tok_idx = jnp.arange(tm)
    ltri = (tok_idx[:, None] >= tok_idx[None, :]).astype(BF16)
    const2 = lambda b, i: (0, 0)
    tok = lambda b, i: (b, i, 0)
    tok_t = lambda b, i: (b, 0, i)
    kern = functools.partial(_in_kernel, sb_w=sb_w, fox_w=fox_w, sgu_w=sgu_w, chunk=chunk)
    out_shape = [
        jax.ShapeDtypeStruct((bsz, s, sb_w), BF16),
        jax.ShapeDtypeStruct((bsz, s, sb_w), BF16),
        jax.ShapeDtypeStruct((bsz, sb_w, s), BF16),
        jax.ShapeDtypeStruct((bsz, s, fox_heads * LANES), BF16),
        jax.ShapeDtypeStruct((bsz, s, fox_heads * LANES), BF16),
        jax.ShapeDtypeStruct((bsz, fox_heads * V_ROWS, s), BF16),
        jax.ShapeDtypeStruct((bsz, s, sgu_w), BF16),
        jax.ShapeDtypeStruct((bsz, fox_heads, s), F32),
    ]
    out_specs = [
        pl.BlockSpec((1, tm, sb_w), tok),
        pl.BlockSpec((1, tm, sb_w), tok),
        pl.BlockSpec((1, sb_w, tm), tok_t),
        pl.BlockSpec((1, tm, fox_heads * LANES), tok),
        pl.BlockSpec((1, tm, fox_heads * LANES), tok),
        pl.BlockSpec((1, fox_heads * V_ROWS, tm), tok_t),
        pl.BlockSpec((1, tm, sgu_w), tok),
        pl.BlockSpec((1, fox_heads, tm), tok_t),
    ]
    return pl.pallas_call(
        kern,
        grid=(bsz, s // tm),
        in_specs=[
            pl.BlockSpec((1, tm, d), tok),
            pl.BlockSpec((None, 1, 6, d), lambda b, i: (layer, b, 0, 0)),
            of_layer(g1), of_layer(w), of_layer(wsgu), of_layer(wf), of_layer(bf), of_layer(qx), of_layer(qg),
            of_layer(kg), of_layer(sg), of_layer(sw), of_layer(sb),
            pl.BlockSpec((tm, tm), const2),
            pl.BlockSpec((LANES, 2 * fox_heads * LANES), const2),
            pl.BlockSpec((MXU_DIM, MXU_DIM), const2),
        ],
        out_specs=out_specs,
        out_shape=out_shape,
        scratch_shapes=[pltpu.VMEM((8, LANES), F32), pltpu.VMEM((tm, wn), F32)],
        compiler_params=pltpu.CompilerParams(
            dimension_semantics=("arbitrary", "arbitrary"), vmem_limit_bytes=VMEM_LIMIT),
        name="in_proj",
    )(x, mod, g1, w, wsgu, wf, bf, qx, qg, kg, sg, sw, sb, ltri, _placement_matrix(fox_heads),
      _group_mean_matrix())


def _sb_kernel(q_ref, k_ref, vt_ref, after_ref, o_ref, acc_ref, c_ref, *, tq):
    tk = tq
    n_blk = q_ref.shape[1] // tq
    lane = lax.broadcasted_iota(jnp.int32, (1, LANES), 1)
    key = lax.broadcasted_iota(jnp.int32, (tk, tq), 0)
    qry = lax.broadcasted_iota(jnp.int32, (tk, tq), 1)
    causal = key < qry
    def process(chains, size=tk):
        after = after_ref[0:size, 0:size]
        units, starts, qh = [], {}, {}
        for slot, qi, blocks, first in chains:
            q = q_ref[0, pl.ds(_block_start(qi, tq), tq), :]
            qh[slot] = (jnp.where(lane < HEAD_DIM, q, 0).astype(BF16),
                        jnp.where(lane >= HEAD_DIM, q, 0).astype(BF16))
            for b, j in enumerate(blocks):
                starts[slot, b] = _block_start(j, size)
                units += [(slot, b, h, first and b == 0) for h in range(2)]
        z = {(sl, b, h): _dot_nt(k_ref[0, pl.ds(starts[sl, b], size), :], qh[sl][h])
             for sl, b, h, _ in units}
        l1mb, head = {}, {}
        for sl, b, h, diag in units:
            zz = z[sl, b, h]
            nz = -zz
            lg = jnp.minimum(nz, 0.0) - jnp.log2(1.0 + jnp.exp2(jnp.minimum(zz, nz)))
            if diag:
                lg = jnp.where(causal, lg, 0.0)
            l1mb[sl, b, h] = lg.astype(BF16)
            head[sl, b, h] = (zz + lg, lg[0:1, :])
        between = {(sl, b, h): _dot(after, l1mb[sl, b, h]) for sl, b, h, _ in units}
        a = {}
        for slot, qi, blocks, first in chains:
            for h in range(2):
                c = None if first else c_ref[slot, h]
                for b in range(len(blocks)):
                    e = head[slot, b, h][0] + between[slot, b, h]
                    if c is not None:
                        e = e + c
                    w = jnp.exp2(e)
                    if first and b == 0:
                        w = jnp.where(causal, w, 0.0)
                    a[slot, b, h] = w.astype(BF16)
                    block_sum = between[slot, b, h][0:1, :] + head[slot, b, h][1]
                    c = block_sum if c is None else c + block_sum
                c_ref[slot, h] = c
        for slot, qi, blocks, first in chains:
            for h in range(2):
                pv = None
                for b in range(len(blocks)):
                    part = _dot(vt_ref[0, h * HEAD_DIM:(h + 1) * HEAD_DIM, pl.ds(starts[slot, b], size)],
                                a[slot, b, h])
                    pv = part if pv is None else pv + part
                if first:
                    acc_ref[slot, h] = pv
                else:
                    acc_ref[slot, h] += pv

    def carry_max(slot):
        cm = jnp.maximum(c_ref[slot, 0], c_ref[slot, 1])
        return jnp.max(cm, axis=1, keepdims=True)[0, 0]

    def finish(slot, qi):
        half = tk // 2

        def cond(carry):
            j, cmax = carry
            return jnp.logical_and(j >= 0, cmax > EXP_ZERO * LOG2_E)

        def body(carry):
            j, _ = carry
            process([(slot, qi, [j], False)], size=half)
            return j - 1, carry_max(slot)

        lax.while_loop(cond, body, (jnp.asarray(2 * (qi - 1) - 1, jnp.int32), carry_max(slot)))
        o_ref[0, pl.ds(_block_start(qi, tq), tq), :] = (
            jnp.concatenate([acc_ref[slot, 0], acc_ref[slot, 1]], axis=0).T.astype(o_ref.dtype))

    process([(0, 0, [0], True)])
    finish(0, 0)

    def two_blocks(g, carry):
        qa, qb = 1 + 2 * g, 2 + 2 * g
        process([(0, qa, [qa, qa - 1], True), (1, qb, [qb, qb - 1], True)])
        finish(0, qa)
        finish(1, qb)
        return carry

    lax.fori_loop(0, (n_blk - 1) // 2, two_blocks, 0)
    if (n_blk - 1) % 2:
        last = n_blk - 1
        process([(0, last, [last, last - 1], True)])
        finish(0, last)


def _sb_attention(q, k, vt, *, tq):
    bsz, s, w = q.shape
    assert s >= 2 * tq
    idx = jnp.arange(tq)
    after = (idx[None, :] > idx[:, None]).astype(BF16)
    kern = functools.partial(_sb_kernel, tq=tq)
    return pl.pallas_call(
        kern,
        grid=(bsz, w // LANES),
        in_specs=[
            pl.BlockSpec((1, s, LANES), lambda b, p: (b, 0, p)),
            pl.BlockSpec((1, s, LANES), lambda b, p: (b, 0, p)),
            pl.BlockSpec((1, LANES, s), lambda b, p: (b, p, 0)),
            pl.BlockSpec((tq, tq), lambda b, p: (0, 0)),
        ],
        out_specs=pl.BlockSpec((1, s, LANES), lambda b, p: (b, 0, p)),
        out_shape=jax.ShapeDtypeStruct((bsz, s, w), BF16),
        scratch_shapes=[pltpu.VMEM((2, 2, HEAD_DIM, tq), F32), pltpu.VMEM((2, 2, 1, tq), F32)],
        compiler_params=pltpu.CompilerParams(
            dimension_semantics=("arbitrary", "arbitrary"), vmem_limit_bytes=VMEM_LIMIT),
        name="sb_attn",
    )(q, k, vt, after)


def _fox_kernel(fend_ref, par_ref, order_ref, q0_ref, q1_ref, k0_ref, k1_ref, v0_ref, v1_ref, o_ref,
                acc_ref, m_ref, p_ref, *, tq, n_heads):
    tk = tq
    qs, ks, vs = (q0_ref, q1_ref), (k0_ref, k1_ref), (v0_ref, v1_ref)
    n_blk = q0_ref.shape[1] // tq
    pair = pl.program_id(1)
    heads = tuple(pl.program_id(0) * n_heads + order_ref[2 * pair + h] for h in range(2))

    def last_dead_block(head, qi, j_prev):
        base = head * n_blk
        f_q = fend_ref[base + jnp.maximum(qi - 1, 0)]

        def next_is_dead(j):
            jn = jnp.minimum(j + 1, n_blk - 1)
            return jnp.logical_and(j + 1 < qi, f_q - fend_ref[base + jn] < -par_ref[0])

        return lax.while_loop(next_is_dead, lambda j: j + 1, j_prev)

    def plan(qi, dead):
        dead = (last_dead_block(heads[0], qi, dead[0]), last_dead_block(heads[1], qi, dead[1]))
        j_dead = jnp.minimum(dead[0], dead[1])
        n_left = jnp.maximum(qi - 2 - j_dead, 0)
        odd = n_left % 2
        has_dead_below = j_dead >= 0
        n_pairs = n_left // 2 + jnp.where(has_dead_below, odd, 0)
        return n_pairs, jnp.logical_and(odd == 1, jnp.logical_not(has_dead_below)), dead

    def q_tile(qi, h):
        return qs[h][0, pl.ds(pl.multiple_of(qi * tq, tq), tq), :]

    def finalize(qi):
        outs = []
        for h in range(2):
            acc = acc_ref[h]
            outs.append(acc[0:HEAD_DIM, :] / acc[HEAD_DIM:HEAD_DIM + 1, :])
        o_ref[0, pl.ds(pl.multiple_of(qi * tq, tq), tq), :] = (
            jnp.concatenate(outs, axis=0).T.astype(o_ref.dtype))

    key = lax.broadcasted_iota(jnp.int32, (tk, tq), 0)
    qry = lax.broadcasted_iota(jnp.int32, (tk, tq), 1)
    causal = key <= qry

    def process(qi, blocks, first):
        starts = [pl.multiple_of(j * tk, tk) for j in blocks]
        units = [(b, h) for b in range(len(blocks)) for h in range(2)]
        s = {(b, h): _dot_nt(ks[h][0, pl.ds(starts[b], tk), :],
                             q_tile(qi, h)) for b, h in units}
        p = {}
        alpha = {}
        for h in range(2):
            if first:
                s[0, h] = jnp.where(causal, s[0, h], NEG_BIG)
            m_new = None if first else m_ref[h]
            for b in range(len(blocks)):
                mb = jnp.max(s[b, h], axis=0, keepdims=True)
                m_new = mb if m_new is None else jnp.maximum(m_new, mb)
            if not first:
                alpha[h] = jnp.exp(m_ref[h] - m_new)
            for b in range(len(blocks)):
                p[b, h] = jnp.exp(s[b, h] - m_new).astype(BF16)
            m_ref[h] = m_new
        for h in range(2):
            pv = None
            for b in range(len(blocks)):
                part = _dot(vs[h][0, :, pl.ds(starts[b], tk)], p[b, h])
                pv = part if pv is None else pv + part
            acc_ref[h] = pv if first else alpha[h] * acc_ref[h] + pv

    def online_path():
        def q_block(qi, dead):
            n_pairs, lone_block0, dead = plan(qi, dead)

            @pl.when(qi == 0)
            def _():
                process(qi, [qi], True)

            @pl.when(qi > 0)
            def _():
                process(qi, [qi, qi - 1], True)

            def body(i, c):
                j = qi - 2 - 2 * i
                process(qi, [j, j - 1], False)
                return c

            lax.fori_loop(0, n_pairs, body, 0)

            @pl.when(lone_block0)
            def _():
                process(qi, [0], False)

            finalize(qi)
            return dead

        lax.fori_loop(0, n_blk, q_block, (jnp.int32(-1), jnp.int32(-1)))

    win = 2 * tk
    wrow = lax.broadcasted_iota(jnp.int32, (win, tq), 0)
    wcol = lax.broadcasted_iota(jnp.int32, (win, tq), 1)

    def window(qi, i):
        jl = qi - 1 - 2 * i
        j_lo = jnp.maximum(jl, 0)
        return j_lo, pl.multiple_of(j_lo * tk, tk), jnp.where(jl < 0, tk, win)

    def stage_a(qi, i, slot, first):
        j_lo, start, row_lim = window(qi, i)
        keep = wrow < row_lim
        if first:
            keep = jnp.logical_and(keep, wrow <= wcol + (qi - j_lo) * tk)
        for h in range(2):
            s = _dot_nt(ks[h][0, pl.ds(start, win), :], q_tile(qi, h))
            p_ref[slot, h] = jnp.exp(jnp.where(keep, s, NEG_BIG)).astype(BF16)

    def stage_c(qi, i, slot):
        _, start, _ = window(qi, i)
        for h in range(2):
            acc_ref[h] += _dot(vs[h][0, :, pl.ds(start, win)], p_ref[slot, h])

    def bounded_path():
        acc_ref[...] = jnp.zeros_like(acc_ref)
        stage_a(0, 0, 0, True)

        def q_block(qi, carry):
            t, dead = carry[0], carry[1:]
            n_pairs, lone_block0, dead = plan(qi, dead)
            n_stages = 1 + n_pairs + jnp.where(lone_block0, 1, 0)

            def body(i, t):
                stage_c(qi, i - 1, t & 1)
                stage_a(qi, i, (t + 1) & 1, False)
                return t + 1

            t = lax.fori_loop(1, n_stages, body, t)
            stage_c(qi, n_stages - 1, t & 1)
            stage_a(jnp.minimum(qi + 1, n_blk - 1), 0, (t + 1) & 1, True)
            finalize(qi)
            acc_ref[...] = jnp.zeros_like(acc_ref)
            return (t + 1,) + dead

        lax.fori_loop(0, n_blk, q_block, (jnp.int32(0), jnp.int32(-1), jnp.int32(-1)))

    bounded = par_ref[1] > 0.5
    pl.when(bounded)(bounded_path)
    pl.when(jnp.logical_not(bounded))(online_path)


def _fox_attention(q, k, vt, f, qk_bound, order, *, tq):
    bsz, s, w = q.shape
    n_heads = w // LANES
    assert s >= 2 * tq
    f_end = f[:, :, tq - 1::tq].reshape(-1)
    bounded = qk_bound <= MAX_UNSTABILISED_LOGIT
    stabiliser_slack = 0.5
    par = jnp.stack([jnp.where(bounded, stabiliser_slack - EXP_ZERO, 2.0 * qk_bound - EXP_ZERO),
                     jnp.where(bounded, 1.0, 0.0)]).astype(F32)
    kern = functools.partial(_fox_kernel, tq=tq, n_heads=n_heads)
    tok_tile = lambda h: pl.BlockSpec((1, s, LANES), lambda b, p, fe, th, od: (b, 0, od[2 * p + h]))
    val_tile = lambda h: pl.BlockSpec((1, V_ROWS, s), lambda b, p, fe, th, od: (b, od[2 * p + h], 0))
    grid_spec = pltpu.PrefetchScalarGridSpec(
        num_scalar_prefetch=3,
        grid=(bsz, n_heads // 2),
        in_specs=[tok_tile(0), tok_tile(1), tok_tile(0), tok_tile(1), val_tile(0), val_tile(1)],
        out_specs=pl.BlockSpec((1, s, LANES), lambda b, p, fe, th, od: (b, 0, p)),
        scratch_shapes=[pltpu.VMEM((2, V_ROWS, tq), F32), pltpu.VMEM((2, 1, tq), F32),
                        pltpu.VMEM((2, 2, 2 * tq, tq), BF16)],
    )
    return pl.pallas_call(
        kern,
        grid_spec=grid_spec,
        out_shape=jax.ShapeDtypeStruct((bsz, s, w // 2), BF16),
        compiler_params=pltpu.CompilerParams(
            dimension_semantics=("arbitrary", "arbitrary"), vmem_limit_bytes=VMEM_LIMIT),
        name="fox_attn",
    )(f_end, par, order.astype(jnp.int32), q, q, k, k, vt, vt)


def _out_kernel(x_ref, osb_ref, ofox_ref, osgu_ref, mod_ref, g2_ref, wo_ref, w1_ref, w2_ref,
                o_ref, *, ff_chunk):
    sb_w = osb_ref.shape[2]
    fox_w = ofox_ref.shape[2]
    x = x_ref[0]
    mix = (_dot(osb_ref[0], wo_ref[0:sb_w, :])
           + _dot(ofox_ref[0], wo_ref[sb_w:sb_w + fox_w, :])
           + _dot(osgu_ref[0], wo_ref[sb_w + fox_w:, :]))
    x1 = x + mod_ref[0, 2:3, :] * mix
    ms = jnp.mean(x1 * x1, axis=-1, keepdims=True)
    h = x1 * lax.rsqrt(ms + EPS) * g2_ref[...]
    hb = (h * (1.0 + mod_ref[0, 4:5, :]) + mod_ref[0, 3:4, :]).astype(BF16)
    d_ff = w1_ref.shape[1]
    acc = None
    for c in range(d_ff // ff_chunk):
        hid = jnp.maximum(_dot(hb, w1_ref[:, c * ff_chunk:(c + 1) * ff_chunk]), 0.0)
        part = _dot((hid * hid).astype(BF16), w2_ref[c * ff_chunk:(c + 1) * ff_chunk, :])
        acc = part if acc is None else acc + part
    o_ref[0] = x1 + mod_ref[0, 5:6, :] * acc


def _out_mlp(x, osb, ofox, osgu, mod, g2, wo, w1, w2, *, layer, tm):
    bsz, s, d = x.shape
    tok = lambda b, i: (b, i, 0)

    def of_layer(a, **kw):
        nd = a.ndim - 1
        return pl.BlockSpec((None,) + a.shape[1:], lambda b, i: (layer,) + (0,) * nd, **kw)

    single = pl.Buffered(1)
    kern = functools.partial(_out_kernel, ff_chunk=FF_CHUNK)
    return pl.pallas_call(
        kern,
        grid=(bsz, s // tm),
        in_specs=[
            pl.BlockSpec((1, tm, d), tok),
            pl.BlockSpec((1, tm, osb.shape[2]), tok),
            pl.BlockSpec((1, tm, ofox.shape[2]), tok),
            pl.BlockSpec((1, tm, osgu.shape[2]), tok),
            pl.BlockSpec((None, 1, 6, d), lambda b, i: (layer, b, 0, 0)),
            of_layer(g2),
            of_layer(wo, pipeline_mode=single),
            of_layer(w1, pipeline_mode=single),
            of_layer(w2, pipeline_mode=single),
        ],
        out_specs=pl.BlockSpec((1, tm, d), tok),
        out_shape=jax.ShapeDtypeStruct((bsz, s, d), F32),
        compiler_params=pltpu.CompilerParams(
            dimension_semantics=("arbitrary", "arbitrary"), vmem_limit_bytes=VMEM_LIMIT),
        name="out_mlp",
    )(x, osb, ofox, osgu, mod, g2, wo, w1, w2)


def kernel(x, c, ada_w, ada_b, norm1_g, norm2_g, w_in, b_forget, q_norm_g, k_norm_g, sgu_norm_g,
           sgu_w, sgu_b, w_out, mlp_w1, mlp_w2):
    depth, d, _ = ada_w.shape
    bsz, s, _ = x.shape
    fox_heads = b_forget.shape[1]
    fox_w = fox_heads * HEAD_DIM
    sgu_groups, chunk = sgu_b.shape[1], sgu_b.shape[2]
    sgu_wd = sgu_groups * sgu_norm_g.shape[2]
    sb_w = (w_in.shape[2] - 3 * fox_w - fox_heads - 2 * sgu_wd) // 3
    f_lo = 3 * sb_w + 3 * fox_w
    assert s % TOKEN_TILE == 0 and s % (2 * ATTN_BLOCK) == 0 and mlp_w1.shape[2] % FF_CHUNK == 0
    assert sb_w % LANES == 0 and ada_w.shape[2] % MOD_COLS == 0

    mod = _modulation(c, ada_w, ada_b).reshape(depth, bsz, 6, d)

    perm = jnp.argsort(b_forget, axis=1)
    assert f_lo % LANES == 0 and f_lo + LANES <= w_in.shape[2]
    w = w_in[:, :, :f_lo].astype(BF16)
    wsgu = w_in[:, :, f_lo + fox_heads:].astype(BF16)
    wf = w_in[:, :, f_lo:f_lo + LANES].astype(BF16)
    bf = jnp.pad(b_forget, ((0, 0), (0, LANES - fox_heads))).reshape(depth, 1, LANES)
    qg = jnp.tile(q_norm_g, (1, MXU_DIM // HEAD_DIM)).reshape(depth, 1, MXU_DIM)
    kg = jnp.tile(k_norm_g, (1, MXU_DIM // HEAD_DIM)).reshape(depth, 1, MXU_DIM)
    sg = sgu_norm_g.reshape(depth, 1, sgu_wd)
    sb = jnp.repeat(jnp.swapaxes(sgu_b, 1, 2), sgu_norm_g.shape[2], axis=2)
    qk_bound = ((BF16_MARGIN * HEAD_DIM ** 0.5) * jnp.max(jnp.abs(q_norm_g), axis=1)
                * jnp.max(jnp.abs(k_norm_g), axis=1))
    qx = jnp.zeros((depth, 1, LANES), F32).at[:, 0, HEAD_DIM:HEAD_DIM + 3].set(-1.0)
    qx = qx.at[:, 0, HEAD_DIM + 6].set(-qk_bound)
    wo_fox = w_out[:, sb_w:sb_w + fox_w].reshape(depth, fox_heads, HEAD_DIM, d)
    wo_fox = jnp.take_along_axis(wo_fox, perm[:, :, None, None], axis=1).reshape(depth, fox_w, d)
    wo = jnp.concatenate([w_out[:, :sb_w], wo_fox, w_out[:, sb_w + fox_w:]], axis=1).astype(BF16)
    w1 = mlp_w1.astype(BF16)
    w2 = mlp_w2.astype(BF16)
    g1 = norm1_g.reshape(depth, 1, d)
    g2 = norm2_g.reshape(depth, 1, d)

    for l in range(depth):
        qa, ka, vat, qf, kf, vft, osgu, f = _in_proj(
            x, mod, g1, w, wsgu, wf, bf, qx, qg, kg, sg, sgu_w, sb,
            layer=l, tm=TOKEN_TILE, sb_w=sb_w, fox_w=fox_w, sgu_w=sgu_wd)
        osb = _sb_attention(qa, ka, vat, tq=ATTN_BLOCK)
        ofox = _fox_attention(qf, kf, vft, f, qk_bound[l], perm[l], tq=ATTN_BLOCK)
        x = _out_mlp(x, osb, ofox, osgu, mod, g2, wo, w1, w2, layer=l, tm=TOKEN_TILE)
    return x
```

```python
import functools
import math

import jax
import jax.numpy as jnp
from jax import lax
from jax.experimental import pallas as pl
from jax.experimental.pallas import tpu as pltpu

HEAD_DIM = 64
LANES = 128
MXU_DIM = 256
V_ROWS = 2 * HEAD_DIM
PART_STRIDE = 8
BF16_MARGIN = 1.03

TOKEN_TILE = 512
ATTN_BLOCK = MXU_DIM
MOD_COLS = 1536
FF_CHUNK = 1024
EPS = 1e-6
NEG_BIG = -1e30
EXP_ZERO = -104.0
LOG2_E = math.log2(math.e)
MAX_UNSTABILISED_LOGIT = 40.0
VMEM_LIMIT = 56 * 1024 * 1024

F32 = jnp.float32
BF16 = jnp.bfloat16


def _dot(a, b):
    return jnp.dot(a, b, preferred_element_type=F32)


def _dot_nt(a, b):
    return lax.dot_general(a, b, (((1,), (1,)), ((), ())), preferred_element_type=F32)


def _block_start(j, size):
    return j * size if isinstance(j, int) else pl.multiple_of(j * size, size)


def _split3(x):
    hi = x.astype(BF16)
    r = x - hi.astype(F32)
    mid = r.astype(BF16)
    lo = (r - mid.astype(F32)).astype(BF16)
    return hi, mid, lo


def _group_mean_matrix():
    head = jnp.arange(MXU_DIM) // HEAD_DIM
    return jnp.where(head[:, None] == head[None, :], 1.0 / HEAD_DIM, 0.0).astype(BF16)


def _placement_matrix(n_heads):
    row = jnp.arange(LANES)[:, None]
    col = jnp.arange(2 * n_heads * LANES)[None, :]
    tile = col // LANES
    is_q = (tile >= n_heads).astype(jnp.int32)
    part = col % LANES - HEAD_DIM - 3 * is_q
    hit = (part >= 0) & (part < 3) & (row == part * PART_STRIDE + tile - n_heads * is_q)
    return hit.astype(BF16)


def _head_rmsnorm(t, gmat, gain):
    ms = _dot((t * t).astype(BF16), gmat)
    return t * lax.rsqrt(ms + EPS) * gain


def _gelu_tanh(x):
    c = math.sqrt(2.0 / math.pi)
    return x * (0.5 * (1.0 + jnp.tanh(c * (x + 0.044715 * (x * x * x)))))


def _log_sigmoid(x):
    return jnp.minimum(x, 0.0) - jnp.log(1.0 + jnp.exp(-jnp.abs(x)))


def _mod_kernel(ct_ref, w_ref, b_ref, o_ref):
    ct = ct_ref[...]
    cond = ct * (1.0 / (1.0 + jnp.exp(-ct)))
    w = w_ref[0]
    rows = [jnp.sum(cond[:, b:b + 1] * w, axis=0, keepdims=True) for b in range(ct.shape[1])]
    o_ref[0] = jnp.concatenate(rows, axis=0) + b_ref[0]


def _modulation(c, ada_w, ada_b):
    depth, d, n = ada_w.shape
    bsz = c.shape[0]
    tn = MOD_COLS
    return pl.pallas_call(
        _mod_kernel,
        grid=(depth, n // tn),
        in_specs=[
            pl.BlockSpec((d, bsz), lambda l, j: (0, 0)),
            pl.BlockSpec((1, d, tn), lambda l, j: (l, 0, j)),
            pl.BlockSpec((1, 1, tn), lambda l, j: (l, 0, j)),
        ],
        out_specs=pl.BlockSpec((1, bsz, tn), lambda l, j: (l, 0, j)),
        out_shape=jax.ShapeDtypeStruct((depth, bsz, n), F32),
        compiler_params=pltpu.CompilerParams(
            dimension_semantics=("arbitrary", "arbitrary"), vmem_limit_bytes=VMEM_LIMIT),
        name="adaln_mod",
    )(c.T, ada_w, ada_b.reshape(depth, 1, n))


def _in_kernel(x_ref, mod_ref, g1_ref, w_ref, wsgu_ref, wf_ref, bf_ref, qx_ref, qg_ref, kg_ref,
               sg_ref, sw_ref, sb_ref, ltri_ref, sel_ref, gmat_ref,
               qa_ref, ka_ref, vat_ref, qf_ref, kf_ref, vft_ref, og_ref, f_ref,
               carry_ref, proj_ref, *, sb_w, fox_w, sgu_w, chunk):
    @pl.when(pl.program_id(1) == 0)
    def _():
        carry_ref[...] = jnp.zeros_like(carry_ref)

    tm = x_ref.shape[1]
    x = x_ref[0]
    ms = jnp.mean(x * x, axis=-1, keepdims=True)
    h = x * lax.rsqrt(ms + EPS) * g1_ref[...]
    h = h * (1.0 + mod_ref[0, 1:2, :]) + mod_ref[0, 0:1, :]
    hb = h.astype(BF16)
    scale = HEAD_DIM ** -0.5
    n_attn = w_ref.shape[1]
    wn = n_attn + wsgu_ref.shape[1]
    n_heads = f_ref.shape[1]
    lane = lax.broadcasted_iota(jnp.int32, (1, LANES), 1)
    head_lane = lane < n_heads
    gmat = gmat_ref[...]

    def pack3(parts):
        a, b, c = (p.astype(F32) for p in parts)
        return (a + pltpu.roll(b, PART_STRIDE, 1) + pltpu.roll(c, 2 * PART_STRIDE, 1)).astype(BF16)

    def project(lo, hi):
        proj_ref[:, lo:hi] = _dot(hb, w_ref[:, lo:hi])

    o_q, o_k, o_v = 3 * sb_w, 3 * sb_w + fox_w, 3 * sb_w + 2 * fox_w
    proj_ref[:, n_attn:wn] = _dot(hb, wsgu_ref[...])
    fl = _dot(hb, wf_ref[...])
    project(o_q, o_k)
    gu = _gelu_tanh(proj_ref[:, n_attn:n_attn + sgu_w])
    gv = _gelu_tanh(proj_ref[:, n_attn + sgu_w:wn])
    project(o_k, o_v)
    logf = jnp.where(head_lane, _log_sigmoid(fl + bf_ref[...]), 0.0)
    cp = _dot(ltri_ref[...], pack3(_split3(logf)))
    vn_all = _head_rmsnorm(gv, gmat, sg_ref[...]).astype(BF16)
    q_tiles = [_head_rmsnorm(proj_ref[:, o_q + j * MXU_DIM:o_q + (j + 1) * MXU_DIM], gmat, qg_ref[...]) * scale
               for j in range(fox_w // MXU_DIM)]
    project(o_v, n_attn)

    cum = cp + pltpu.roll(cp, LANES - PART_STRIDE, 1) + pltpu.roll(cp, LANES - 2 * PART_STRIDE, 1)
    cum = jnp.where(head_lane, cum, 0.0) + carry_ref[0:1, :]
    carry_ref[0:1, :] = cum[tm - 1:tm, :]
    f_ref[0] = cum.T[0:n_heads, :]
    extras = _dot(pack3(_split3(cum)), sel_ref[...])
    k_tiles = [_head_rmsnorm(proj_ref[:, o_k + j * MXU_DIM:o_k + (j + 1) * MXU_DIM], gmat, kg_ref[...])
               for j in range(fox_w // MXU_DIM)]

    lane_c = lax.broadcasted_iota(jnp.int32, (chunk, LANES), 1)
    rr = lax.broadcasted_iota(jnp.int32, (chunk, chunk), 0)
    cs = lax.broadcasted_iota(jnp.int32, (chunk, chunk), 1)
    wt = [jnp.where(rr >= cs, sw_ref[g], 0.0).astype(BF16) for g in range(sw_ref.shape[0])]
    for p in range(sgu_w // LANES):
        vn = vn_all[:, p * LANES:(p + 1) * LANES]
        for ci in range(tm // chunk):
            vblk = vn[ci * chunk:(ci + 1) * chunk, :]
            mixed = jnp.where(lane_c < HEAD_DIM, _dot(wt[2 * p], vblk), _dot(wt[2 * p + 1], vblk))
            mixed = mixed + sb_ref[:, p * LANES:(p + 1) * LANES]
            og_ref[0, ci * chunk:(ci + 1) * chunk, p * LANES:(p + 1) * LANES] = (
                gu[ci * chunk:(ci + 1) * chunk, p * LANES:(p + 1) * LANES] * mixed).astype(BF16)

    project(0, o_q)
    qa_ref[0] = (proj_ref[:, 0:sb_w] * (scale * LOG2_E)).astype(BF16)
    ka_ref[0] = proj_ref[:, sb_w:2 * sb_w].astype(BF16)
    vat_ref[0] = proj_ref[:, 2 * sb_w:3 * sb_w].T.astype(BF16)

    is_head = lane < HEAD_DIM
    q_extra = qx_ref[...]
    k_extra = jnp.where((lane >= HEAD_DIM + 3) & (lane < HEAD_DIM + 7), 1.0, 0.0)
    heads_per_mxu = MXU_DIM // HEAD_DIM

    def head_tile(tn, s):
        half = tn[:, (s // 2) * LANES:(s // 2 + 1) * LANES]
        return half if s % 2 == 0 else pltpu.roll(half, HEAD_DIM, 1)

    for j in range(fox_w // MXU_DIM):
        for s in range(heads_per_mxu):
            hh = heads_per_mxu * j + s
            xq = extras[:, (n_heads + hh) * LANES:(n_heads + hh + 1) * LANES] + q_extra
            qf_ref[0, :, hh * LANES:(hh + 1) * LANES] = jnp.where(
                is_head, head_tile(q_tiles[j], s), xq).astype(BF16)
            xk = extras[:, hh * LANES:(hh + 1) * LANES] + k_extra
            kf_ref[0, :, hh * LANES:(hh + 1) * LANES] = jnp.where(
                is_head, head_tile(k_tiles[j], s), xk).astype(BF16)
    vt = proj_ref[:, o_v:n_attn].T.astype(BF16)
    ones = jnp.ones((V_ROWS - HEAD_DIM, tm), BF16)
    for hh in range(fox_w // HEAD_DIM):
        vft_ref[0, hh * V_ROWS:hh * V_ROWS + HEAD_DIM, :] = vt[hh * HEAD_DIM:(hh + 1) * HEAD_DIM, :]
        vft_ref[0, hh * V_ROWS + HEAD_DIM:(hh + 1) * V_ROWS, :] = ones


def _in_proj(x, mod, g1, w, wsgu, wf, bf, qx, qg, kg, sg, sw, sb, *, layer, tm, sb_w, fox_w, sgu_w):
    bsz, s, d = x.shape
    chunk = sw.shape[-1]
    wn = w.shape[2] + wsgu.shape[2]
    fox_heads = fox_w // HEAD_DIM

    def of_layer(a):
        nd = a.ndim - 1
        return pl.BlockSpec((None,) + a.shape[1:], lambda b, i: (layer,) + (0,) * nd)

    assert fox_heads <= PART_STRIDE and fox_w % MXU_DIM == 0 and sgu_w == MXU_DIM
    assert s % tm == 0 and tm % chunk == 0
    tok_idx = jnp.arange(tm)
    ltri = (tok_idx[:, None] >= tok_idx[None, :]).astype(BF16)
    const2 = lambda b, i: (0, 0)
    tok = lambda b, i: (b, i, 0)
    tok_t = lambda b, i: (b, 0, i)
    kern = functools.partial(_in_kernel, sb_w=sb_w, fox_w=fox_w, sgu_w=sgu_w, chunk=chunk)
    out_shape = [
        jax.ShapeDtypeStruct((bsz, s, sb_w), BF16),
        jax.ShapeDtypeStruct((bsz, s, sb_w), BF16),
        jax.ShapeDtypeStruct((bsz, sb_w, s), BF16),
        jax.ShapeDtypeStruct((bsz, s, fox_heads * LANES), BF16),
        jax.ShapeDtypeStruct((bsz, s, fox_heads * LANES), BF16),
        jax.ShapeDtypeStruct((bsz, fox_heads * V_ROWS, s), BF16),
        jax.ShapeDtypeStruct((bsz, s, sgu_w), BF16),
        jax.ShapeDtypeStruct((bsz, fox_heads, s), F32),
    ]
    out_specs = [
        pl.BlockSpec((1, tm, sb_w), tok),
        pl.BlockSpec((1, tm, sb_w), tok),
        pl.BlockSpec((1, sb_w, tm), tok_t),
        pl.BlockSpec((1, tm, fox_heads * LANES), tok),
        pl.BlockSpec((1, tm, fox_heads * LANES), tok),
        pl.BlockSpec((1, fox_heads * V_ROWS, tm), tok_t),
        pl.BlockSpec((1, tm, sgu_w), tok),
        pl.BlockSpec((1, fox_heads, tm), tok_t),
    ]
    return pl.pallas_call(
        kern,
        grid=(bsz, s // tm),
        in_specs=[
            pl.BlockSpec((1, tm, d), tok),
            pl.BlockSpec((None, 1, 6, d), lambda b, i: (layer, b, 0, 0)),
            of_layer(g1), of_layer(w), of_layer(wsgu), of_layer(wf), of_layer(bf), of_layer(qx), of_layer(qg),
            of_layer(kg), of_layer(sg), of_layer(sw), of_layer(sb),
            pl.BlockSpec((tm, tm), const2),
            pl.BlockSpec((LANES, 2 * fox_heads * LANES), const2),
            pl.BlockSpec((MXU_DIM, MXU_DIM), const2),
        ],
        out_specs=out_specs,
        out_shape=out_shape,
        scratch_shapes=[pltpu.VMEM((8, LANES), F32), pltpu.VMEM((tm, wn), F32)],
        compiler_params=pltpu.CompilerParams(
            dimension_semantics=("arbitrary", "arbitrary"), vmem_limit_bytes=VMEM_LIMIT),
        name="in_proj",
    )(x, mod, g1, w, wsgu, wf, bf, qx, qg, kg, sg, sw, sb, ltri, _placement_matrix(fox_heads),
      _group_mean_matrix())


def _sb_kernel(q_ref, k_ref, vt_ref, after_ref, o_ref, acc_ref, c_ref, *, tq):
    tk = tq
    n_blk = q_ref.shape[1] // tq
    lane = lax.broadcasted_iota(jnp.int32, (1, LANES), 1)
    key = lax.broadcasted_iota(jnp.int32, (tk, tq), 0)
    qry = lax.broadcasted_iota(jnp.int32, (tk, tq), 1)
    causal = key < qry
    def process(chains, size=tk):
        after = after_ref[0:size, 0:size]
        units, starts, qh = [], {}, {}
        for slot, qi, blocks, first in chains:
            q = q_ref[0, pl.ds(_block_start(qi, tq), tq), :]
            qh[slot] = (jnp.where(lane < HEAD_DIM, q, 0).astype(BF16),
                        jnp.where(lane >= HEAD_DIM, q, 0).astype(BF16))
            for b, j in enumerate(blocks):
                starts[slot, b] = _block_start(j, size)
                units += [(slot, b, h, first and b == 0) for h in range(2)]
        z = {(sl, b, h): _dot_nt(k_ref[0, pl.ds(starts[sl, b], size), :], qh[sl][h])
             for sl, b, h, _ in units}
        l1mb, head = {}, {}
        for sl, b, h, diag in units:
            zz = z[sl, b, h]
            nz = -zz
            lg = jnp.minimum(nz, 0.0) - jnp.log2(1.0 + jnp.exp2(jnp.minimum(zz, nz)))
            if diag:
                lg = jnp.where(causal, lg, 0.0)
            l1mb[sl, b, h] = lg.astype(BF16)
            head[sl, b, h] = (zz + lg, lg[0:1, :])
        between = {(sl, b, h): _dot(after, l1mb[sl, b, h]) for sl, b, h, _ in units}
        a = {}
        for slot, qi, blocks, first in chains:
            for h in range(2):
                c = None if first else c_ref[slot, h]
                for b in range(len(blocks)):
                    e = head[slot, b, h][0] + between[slot, b, h]
                    if c is not None:
                        e = e + c
                    w = jnp.exp2(e)
                    if first and b == 0:
                        w = jnp.where(causal, w, 0.0)
                    a[slot, b, h] = w.astype(BF16)
                    block_sum = between[slot, b, h][0:1, :] + head[slot, b, h][1]
                    c = block_sum if c is None else c + block_sum
                c_ref[slot, h] = c
        for slot, qi, blocks, first in chains:
            for h in range(2):
                pv = None
                for b in range(len(blocks)):
                    part = _dot(vt_ref[0, h * HEAD_DIM:(h + 1) * HEAD_DIM, pl.ds(starts[slot, b], size)],
                                a[slot, b, h])
                    pv = part if pv is None else pv + part
                if first:
                    acc_ref[slot, h] = pv
                else:
                    acc_ref[slot, h] += pv

    def carry_max(slot):
        cm = jnp.maximum(c_ref[slot, 0], c_ref[slot, 1])
        return jnp.max(cm, axis=1, keepdims=True)[0, 0]

    half = tk // 2
    live = EXP_ZERO * LOG2_E

    def first_half_block(qi):
        return jnp.asarray(2 * (qi - 1) - 1, jnp.int32)

    def finish(slot, qi, j_start=None, cmax=None):
        def cond(carry):
            j, cm = carry
            return jnp.logical_and(j >= 0, cm > live)

        def body(carry):
            j, _ = carry
            process([(slot, qi, [j], False)], size=half)
            return j - 1, carry_max(slot)

        lax.while_loop(cond, body, (first_half_block(qi) if j_start is None else j_start,
                                    carry_max(slot) if cmax is None else cmax))
        o_ref[0, pl.ds(_block_start(qi, tq), tq), :] = (
            jnp.concatenate([acc_ref[slot, 0], acc_ref[slot, 1]], axis=0).T.astype(o_ref.dtype))

    process([(0, 0, [0], True)])
    finish(0, 0)

    def two_blocks(g, carry):
        qa, qb = 1 + 2 * g, 2 + 2 * g
        process([(0, qa, [qa, qa - 1], True), (1, qb, [qb, qb - 1], True)])

        def both_live(c):
            j, ca, cb = c
            return jnp.logical_and(j >= 0, jnp.logical_and(ca > live, cb > live))

        def both_step(c):
            j = c[0]
            process([(0, qa, [j], False), (1, qb, [j + 2], False)], size=half)
            return j - 1, carry_max(0), carry_max(1)

        j, ca, cb = lax.while_loop(both_live, both_step, (first_half_block(qa), carry_max(0), carry_max(1)))
        finish(0, qa, j, ca)
        finish(1, qb, j + 2, cb)
        return carry

    lax.fori_loop(0, (n_blk - 1) // 2, two_blocks, 0)
    if (n_blk - 1) % 2:
        last = n_blk - 1
        process([(0, last, [last, last - 1], True)])
        finish(0, last)


def _sb_attention(q, k, vt, *, tq):
    bsz, s, w = q.shape
    assert s >= 2 * tq
    idx = jnp.arange(tq)
    after = (idx[None, :] > idx[:, None]).astype(BF16)
    kern = functools.partial(_sb_kernel, tq=tq)
    return pl.pallas_call(
        kern,
        grid=(bsz, w // LANES),
        in_specs=[
            pl.BlockSpec((1, s, LANES), lambda b, p: (b, 0, p)),
            pl.BlockSpec((1, s, LANES), lambda b, p: (b, 0, p)),
            pl.BlockSpec((1, LANES, s), lambda b, p: (b, p, 0)),
            pl.BlockSpec((tq, tq), lambda b, p: (0, 0)),
        ],
        out_specs=pl.BlockSpec((1, s, LANES), lambda b, p: (b, 0, p)),
        out_shape=jax.ShapeDtypeStruct((bsz, s, w), BF16),
        scratch_shapes=[pltpu.VMEM((2, 2, HEAD_DIM, tq), F32), pltpu.VMEM((2, 2, 1, tq), F32)],
        compiler_params=pltpu.CompilerParams(
            dimension_semantics=("arbitrary", "arbitrary"), vmem_limit_bytes=VMEM_LIMIT),
        name="sb_attn",
    )(q, k, vt, after)


def _fox_kernel(fend_ref, par_ref, order_ref, q0_ref, q1_ref, k0_ref, k1_ref, v0_ref, v1_ref, o_ref,
                acc_ref, m_ref, p_ref, *, tq, n_heads):
    tk = tq
    qs, ks, vs = (q0_ref, q1_ref), (k0_ref, k1_ref), (v0_ref, v1_ref)
    n_blk = q0_ref.shape[1] // tq
    pair = pl.program_id(1)
    heads = tuple(pl.program_id(0) * n_heads + order_ref[2 * pair + h] for h in range(2))

    def last_dead_block(head, qi, j_prev):
        base = head * n_blk
        f_q = fend_ref[base + jnp.maximum(qi - 1, 0)]

        def next_is_dead(j):
            jn = jnp.minimum(j + 1, n_blk - 1)
            return jnp.logical_and(j + 1 < qi, f_q - fend_ref[base + jn] < -par_ref[0])

        return lax.while_loop(next_is_dead, lambda j: j + 1, j_prev)

    def plan(qi, dead):
        dead = (last_dead_block(heads[0], qi, dead[0]), last_dead_block(heads[1], qi, dead[1]))
        j_dead = jnp.minimum(dead[0], dead[1])
        n_left = jnp.maximum(qi - 2 - j_dead, 0)
        odd = n_left % 2
        has_dead_below = j_dead >= 0
        n_pairs = n_left // 2 + jnp.where(has_dead_below, odd, 0)
        return n_pairs, jnp.logical_and(odd == 1, jnp.logical_not(has_dead_below)), dead

    def q_tile(qi, h):
        return qs[h][0, pl.ds(pl.multiple_of(qi * tq, tq), tq), :]

    def finalize(qi):
        outs = []
        for h in range(2):
            acc = acc_ref[h]
            outs.append(acc[0:HEAD_DIM, :] / acc[HEAD_DIM:HEAD_DIM + 1, :])
        o_ref[0, pl.ds(pl.multiple_of(qi * tq, tq), tq), :] = (
            jnp.concatenate(outs, axis=0).T.astype(o_ref.dtype))

    key = lax.broadcasted_iota(jnp.int32, (tk, tq), 0)
    qry = lax.broadcasted_iota(jnp.int32, (tk, tq), 1)
    causal = key <= qry

    def process(qi, blocks, first):
        starts = [pl.multiple_of(j * tk, tk) for j in blocks]
        units = [(b, h) for b in range(len(blocks)) for h in range(2)]
        s = {(b, h): _dot_nt(ks[h][0, pl.ds(starts[b], tk), :],
                             q_tile(qi, h)) for b, h in units}
        p = {}
        alpha = {}
        for h in range(2):
            if first:
                s[0, h] = jnp.where(causal, s[0, h], NEG_BIG)
            m_new = None if first else m_ref[h]
            for b in range(len(blocks)):
                mb = jnp.max(s[b, h], axis=0, keepdims=True)
                m_new = mb if m_new is None else jnp.maximum(m_new, mb)
            if not first:
                alpha[h] = jnp.exp(m_ref[h] - m_new)
            for b in range(len(blocks)):
                p[b, h] = jnp.exp(s[b, h] - m_new).astype(BF16)
            m_ref[h] = m_new
        for h in range(2):
            pv = None
            for b in range(len(blocks)):
                part = _dot(vs[h][0, :, pl.ds(starts[b], tk)], p[b, h])
                pv = part if pv is None else pv + part
            acc_ref[h] = pv if first else alpha[h] * acc_ref[h] + pv

    def online_path():
        def q_block(qi, dead):
            n_pairs, lone_block0, dead = plan(qi, dead)

            @pl.when(qi == 0)
            def _():
                process(qi, [qi], True)

            @pl.when(qi > 0)
            def _():
                process(qi, [qi, qi - 1], True)

            def body(i, c):
                j = qi - 2 - 2 * i
                process(qi, [j, j - 1], False)
                return c

            lax.fori_loop(0, n_pairs, body, 0)

            @pl.when(lone_block0)
            def _():
                process(qi, [0], False)

            finalize(qi)
            return dead

        lax.fori_loop(0, n_blk, q_block, (jnp.int32(-1), jnp.int32(-1)))

    win = 2 * tk
    wrow = lax.broadcasted_iota(jnp.int32, (win, tq), 0)
    wcol = lax.broadcasted_iota(jnp.int32, (win, tq), 1)

    def window(qi, i):
        jl = qi - 1 - 2 * i
        j_lo = jnp.maximum(jl, 0)
        return j_lo, pl.multiple_of(j_lo * tk, tk), jnp.where(jl < 0, tk, win)

    def stage_a(qi, i, slot, first):
        j_lo, start, row_lim = window(qi, i)
        keep = wrow < row_lim
        if first:
            keep = jnp.logical_and(keep, wrow <= wcol + (qi - j_lo) * tk)
        for h in range(2):
            s = _dot_nt(ks[h][0, pl.ds(start, win), :], q_tile(qi, h))
            p_ref[slot, h] = jnp.exp(jnp.where(keep, s, NEG_BIG)).astype(BF16)

    def stage_c(qi, i, slot):
        _, start, _ = window(qi, i)
        for h in range(2):
            acc_ref[h] += _dot(vs[h][0, :, pl.ds(start, win)], p_ref[slot, h])

    def bounded_path():
        acc_ref[...] = jnp.zeros_like(acc_ref)
        stage_a(0, 0, 0, True)

        def q_block(qi, carry):
            t, dead = carry[0], carry[1:]
            n_pairs, lone_block0, dead = plan(qi, dead)
            n_stages = 1 + n_pairs + jnp.where(lone_block0, 1, 0)

            def body(i, t):
                stage_c(qi, i - 1, t & 1)
                stage_a(qi, i, (t + 1) & 1, False)
                return t + 1

            t = lax.fori_loop(1, n_stages, body, t)
            stage_c(qi, n_stages - 1, t & 1)
            stage_a(jnp.minimum(qi + 1, n_blk - 1), 0, (t + 1) & 1, True)
            finalize(qi)
            acc_ref[...] = jnp.zeros_like(acc_ref)
            return (t + 1,) + dead

        lax.fori_loop(0, n_blk, q_block, (jnp.int32(0), jnp.int32(-1), jnp.int32(-1)))

    bounded = par_ref[1] > 0.5
    pl.when(bounded)(bounded_path)
    pl.when(jnp.logical_not(bounded))(online_path)


def _fox_attention(q, k, vt, f, qk_bound, order, *, tq):
    bsz, s, w = q.shape
    n_heads = w // LANES
    assert s >= 2 * tq
    f_end = f[:, :, tq - 1::tq].reshape(-1)
    bounded = qk_bound <= MAX_UNSTABILISED_LOGIT
    stabiliser_slack = 0.5
    par = jnp.stack([jnp.where(bounded, stabiliser_slack - EXP_ZERO, 2.0 * qk_bound - EXP_ZERO),
                     jnp.where(bounded, 1.0, 0.0)]).astype(F32)
    kern = functools.partial(_fox_kernel, tq=tq, n_heads=n_heads)
    tok_tile = lambda h: pl.BlockSpec((1, s, LANES), lambda b, p, fe, th, od: (b, 0, od[2 * p + h]))
    val_tile = lambda h: pl.BlockSpec((1, V_ROWS, s), lambda b, p, fe, th, od: (b, od[2 * p + h], 0))
    grid_spec = pltpu.PrefetchScalarGridSpec(
        num_scalar_prefetch=3,
        grid=(bsz, n_heads // 2),
        in_specs=[tok_tile(0), tok_tile(1), tok_tile(0), tok_tile(1), val_tile(0), val_tile(1)],
        out_specs=pl.BlockSpec((1, s, LANES), lambda b, p, fe, th, od: (b, 0, p)),
        scratch_shapes=[pltpu.VMEM((2, V_ROWS, tq), F32), pltpu.VMEM((2, 1, tq), F32),
                        pltpu.VMEM((2, 2, 2 * tq, tq), BF16)],
    )
    return pl.pallas_call(
        kern,
        grid_spec=grid_spec,
        out_shape=jax.ShapeDtypeStruct((bsz, s, w // 2), BF16),
        compiler_params=pltpu.CompilerParams(
            dimension_semantics=("arbitrary", "arbitrary"), vmem_limit_bytes=VMEM_LIMIT),
        name="fox_attn",
    )(f_end, par, order.astype(jnp.int32), q, q, k, k, vt, vt)


def _out_kernel(x_ref, osb_ref, ofox_ref, osgu_ref, mod_ref, g2_ref, wo_ref, w1_ref, w2_ref,
                o_ref, *, ff_chunk):
    sb_w = osb_ref.shape[2]
    fox_w = ofox_ref.shape[2]
    x = x_ref[0]
    mix = (_dot(osb_ref[0], wo_ref[0:sb_w, :])
           + _dot(ofox_ref[0], wo_ref[sb_w:sb_w + fox_w, :])
           + _dot(osgu_ref[0], wo_ref[sb_w + fox_w:, :]))
    x1 = x + mod_ref[0, 2:3, :] * mix
    ms = jnp.mean(x1 * x1, axis=-1, keepdims=True)
    h = x1 * lax.rsqrt(ms + EPS) * g2_ref[...]
    hb = (h * (1.0 + mod_ref[0, 4:5, :]) + mod_ref[0, 3:4, :]).astype(BF16)
    d_ff = w1_ref.shape[1]
    acc = None
    for c in range(d_ff // ff_chunk):
        hid = jnp.maximum(_dot(hb, w1_ref[:, c * ff_chunk:(c + 1) * ff_chunk]), 0.0)
        part = _dot((hid * hid).astype(BF16), w2_ref[c * ff_chunk:(c + 1) * ff_chunk, :])
        acc = part if acc is None else acc + part
    o_ref[0] = x1 + mod_ref[0, 5:6, :] * acc


def _out_mlp(x, osb, ofox, osgu, mod, g2, wo, w1, w2, *, layer, tm):
    bsz, s, d = x.shape
    tok = lambda b, i: (b, i, 0)

    def of_layer(a, **kw):
        nd = a.ndim - 1
        return pl.BlockSpec((None,) + a.shape[1:], lambda b, i: (layer,) + (0,) * nd, **kw)

    single = pl.Buffered(1)
    kern = functools.partial(_out_kernel, ff_chunk=FF_CHUNK)
    return pl.pallas_call(
        kern,
        grid=(bsz, s // tm),
        in_specs=[
            pl.BlockSpec((1, tm, d), tok),
            pl.BlockSpec((1, tm, osb.shape[2]), tok),
            pl.BlockSpec((1, tm, ofox.shape[2]), tok),
            pl.BlockSpec((1, tm, osgu.shape[2]), tok),
            pl.BlockSpec((None, 1, 6, d), lambda b, i: (layer, b, 0, 0)),
            of_layer(g2),
            of_layer(wo, pipeline_mode=single),
            of_layer(w1, pipeline_mode=single),
            of_layer(w2, pipeline_mode=single),
        ],
        out_specs=pl.BlockSpec((1, tm, d), tok),
        out_shape=jax.ShapeDtypeStruct((bsz, s, d), F32),
        compiler_params=pltpu.CompilerParams(
            dimension_semantics=("arbitrary", "arbitrary"), vmem_limit_bytes=VMEM_LIMIT),
        name="out_mlp",
    )(x, osb, ofox, osgu, mod, g2, wo, w1, w2)


def kernel(x, c, ada_w, ada_b, norm1_g, norm2_g, w_in, b_forget, q_norm_g, k_norm_g, sgu_norm_g,
           sgu_w, sgu_b, w_out, mlp_w1, mlp_w2):
    depth, d, _ = ada_w.shape
    bsz, s, _ = x.shape
    fox_heads = b_forget.shape[1]
    fox_w = fox_heads * HEAD_DIM
    sgu_groups, chunk = sgu_b.shape[1], sgu_b.shape[2]
    sgu_wd = sgu_groups * sgu_norm_g.shape[2]
    sb_w = (w_in.shape[2] - 3 * fox_w - fox_heads - 2 * sgu_wd) // 3
    f_lo = 3 * sb_w + 3 * fox_w
    assert s % TOKEN_TILE == 0 and s % (2 * ATTN_BLOCK) == 0 and mlp_w1.shape[2] % FF_CHUNK == 0
    assert sb_w % LANES == 0 and ada_w.shape[2] % MOD_COLS == 0

    mod = _modulation(c, ada_w, ada_b).reshape(depth, bsz, 6, d)

    perm = jnp.argsort(b_forget, axis=1)
    assert f_lo % LANES == 0 and f_lo + LANES <= w_in.shape[2]
    w = w_in[:, :, :f_lo].astype(BF16)
    wsgu = w_in[:, :, f_lo + fox_heads:].astype(BF16)
    wf = w_in[:, :, f_lo:f_lo + LANES].astype(BF16)
    bf = jnp.pad(b_forget, ((0, 0), (0, LANES - fox_heads))).reshape(depth, 1, LANES)
    qg = jnp.tile(q_norm_g, (1, MXU_DIM // HEAD_DIM)).reshape(depth, 1, MXU_DIM)
    kg = jnp.tile(k_norm_g, (1, MXU_DIM // HEAD_DIM)).reshape(depth, 1, MXU_DIM)
    sg = sgu_norm_g.reshape(depth, 1, sgu_wd)
    sb = jnp.repeat(jnp.swapaxes(sgu_b, 1, 2), sgu_norm_g.shape[2], axis=2)
    qk_bound = ((BF16_MARGIN * HEAD_DIM ** 0.5) * jnp.max(jnp.abs(q_norm_g), axis=1)
                * jnp.max(jnp.abs(k_norm_g), axis=1))
    qx = jnp.zeros((depth, 1, LANES), F32).at[:, 0, HEAD_DIM:HEAD_DIM + 3].set(-1.0)
    qx = qx.at[:, 0, HEAD_DIM + 6].set(-qk_bound)
    wo_fox = w_out[:, sb_w:sb_w + fox_w].reshape(depth, fox_heads, HEAD_DIM, d)
    wo_fox = jnp.take_along_axis(wo_fox, perm[:, :, None, None], axis=1).reshape(depth, fox_w, d)
    wo = jnp.concatenate([w_out[:, :sb_w], wo_fox, w_out[:, sb_w + fox_w:]], axis=1).astype(BF16)
    w1 = mlp_w1.astype(BF16)
    w2 = mlp_w2.astype(BF16)
    g1 = norm1_g.reshape(depth, 1, d)
    g2 = norm2_g.reshape(depth, 1, d)

    for l in range(depth):
        qa, ka, vat, qf, kf, vft, osgu, f = _in_proj(
            x, mod, g1, w, wsgu, wf, bf, qx, qg, kg, sg, sgu_w, sb,
            layer=l, tm=TOKEN_TILE, sb_w=sb_w, fox_w=fox_w, sgu_w=sgu_wd)
        osb = _sb_attention(qa, ka, vat, tq=ATTN_BLOCK)
        ofox = _fox_attention(qf, kf, vft, f, qk_bound[l], perm[l], tq=ATTN_BLOCK)
        x = _out_mlp(x, osb, ofox, osgu, mod, g2, wo, w1, w2, layer=l, tm=TOKEN_TILE)
    return x
```

```python
import functools
import math

import jax
import jax.numpy as jnp
from jax import lax
from jax.experimental import pallas as pl
from jax.experimental.pallas import tpu as pltpu

HEAD_DIM = 64
LANES = 128
MXU_DIM = 256
V_ROWS = 2 * HEAD_DIM
PART_STRIDE = 8
BF16_MARGIN = 1.03

TOKEN_TILE = 512
ATTN_BLOCK = MXU_DIM
MOD_COLS = 1536
FF_CHUNK = 1024
EPS = 1e-6
NEG_BIG = -1e30
EXP_ZERO = -104.0
LOG2_E = math.log2(math.e)
MAX_UNSTABILISED_LOGIT = 40.0
VMEM_LIMIT = 56 * 1024 * 1024

F32 = jnp.float32
BF16 = jnp.bfloat16


def _dot(a, b):
    return jnp.dot(a, b, preferred_element_type=F32)


def _dot_nt(a, b):
    return lax.dot_general(a, b, (((1,), (1,)), ((), ())), preferred_element_type=F32)


def _block_start(j, size):
    return j * size if isinstance(j, int) else pl.multiple_of(j * size, size)


def _split3(x):
    hi = x.astype(BF16)
    r = x - hi.astype(F32)
    mid = r.astype(BF16)
    lo = (r - mid.astype(F32)).astype(BF16)
    return hi, mid, lo


def _group_mean_matrix():
    head = jnp.arange(MXU_DIM) // HEAD_DIM
    return jnp.where(head[:, None] == head[None, :], 1.0 / HEAD_DIM, 0.0).astype(BF16)


def _placement_matrix(n_heads):
    row = jnp.arange(LANES)[:, None]
    col = jnp.arange(2 * n_heads * LANES)[None, :]
    tile = col // LANES
    is_q = (tile >= n_heads).astype(jnp.int32)
    part = col % LANES - HEAD_DIM - 3 * is_q
    hit = (part >= 0) & (part < 3) & (row == part * PART_STRIDE + tile - n_heads * is_q)
    return hit.astype(BF16)


def _head_rmsnorm(t, gmat, gain):
    ms = _dot((t * t).astype(BF16), gmat)
    return t * lax.rsqrt(ms + EPS) * gain


def _gelu_tanh(x):
    c = math.sqrt(2.0 / math.pi)
    return x * (0.5 * (1.0 + jnp.tanh(c * (x + 0.044715 * (x * x * x)))))


def _log_sigmoid(x):
    return jnp.minimum(x, 0.0) - jnp.log(1.0 + jnp.exp(-jnp.abs(x)))


def _mod_kernel(ct_ref, w_ref, b_ref, o_ref):
    ct = ct_ref[...]
    cond = ct * (1.0 / (1.0 + jnp.exp(-ct)))
    w = w_ref[0]
    rows = [jnp.sum(cond[:, b:b + 1] * w, axis=0, keepdims=True) for b in range(ct.shape[1])]
    o_ref[0] = jnp.concatenate(rows, axis=0) + b_ref[0]


def _modulation(c, ada_w, ada_b):
    depth, d, n = ada_w.shape
    bsz = c.shape[0]
    tn = MOD_COLS
    return pl.pallas_call(
        _mod_kernel,
        grid=(depth, n // tn),
        in_specs=[
            pl.BlockSpec((d, bsz), lambda l, j: (0, 0)),
            pl.BlockSpec((1, d, tn), lambda l, j: (l, 0, j)),
            pl.BlockSpec((1, 1, tn), lambda l, j: (l, 0, j)),
        ],
        out_specs=pl.BlockSpec((1, bsz, tn), lambda l, j: (l, 0, j)),
        out_shape=jax.ShapeDtypeStruct((depth, bsz, n), F32),
        compiler_params=pltpu.CompilerParams(
            dimension_semantics=("arbitrary", "arbitrary"), vmem_limit_bytes=VMEM_LIMIT),
        name="adaln_mod",
    )(c.T, ada_w, ada_b.reshape(depth, 1, n))


def _in_kernel(x_ref, mod_ref, g1_ref, w_ref, wsgu_ref, wf_ref, bf_ref, qx_ref, qg_ref, kg_ref,
               sg_ref, sw_ref, sb_ref, ltri_ref, sel_ref, gmat_ref,
               qa_ref, ka_ref, vat_ref, qf_ref, kf_ref, vft_ref, og_ref, f_ref,
               carry_ref, proj_ref, *, sb_w, fox_w, sgu_w, chunk):
    @pl.when(pl.program_id(1) == 0)
    def _():
        carry_ref[...] = jnp.zeros_like(carry_ref)

    tm = x_ref.shape[1]
    x = x_ref[0]
    ms = jnp.mean(x * x, axis=-1, keepdims=True)
    h = x * lax.rsqrt(ms + EPS) * g1_ref[...]
    h = h * (1.0 + mod_ref[0, 1:2, :]) + mod_ref[0, 0:1, :]
    hb = h.astype(BF16)
    scale = HEAD_DIM ** -0.5
    n_attn = w_ref.shape[1]
    wn = n_attn + wsgu_ref.shape[1]
    n_heads = f_ref.shape[1]
    lane = lax.broadcasted_iota(jnp.int32, (1, LANES), 1)
    head_lane = lane < n_heads
    gmat = gmat_ref[...]

    def pack3(parts):
        a, b, c = (p.astype(F32) for p in parts)
        return (a + pltpu.roll(b, PART_STRIDE, 1) + pltpu.roll(c, 2 * PART_STRIDE, 1)).astype(BF16)

    def project(lo, hi):
        proj_ref[:, lo:hi] = _dot(hb, w_ref[:, lo:hi])

    o_q, o_k, o_v = 3 * sb_w, 3 * sb_w + fox_w, 3 * sb_w + 2 * fox_w
    proj_ref[:, n_attn:wn] = _dot(hb, wsgu_ref[...])
    fl = _dot(hb, wf_ref[...])
    project(o_q, o_k)
    gu = _gelu_tanh(proj_ref[:, n_attn:n_attn + sgu_w])
    gv = _gelu_tanh(proj_ref[:, n_attn + sgu_w:wn])
    project(o_k, o_v)
    logf = jnp.where(head_lane, _log_sigmoid(fl + bf_ref[...]), 0.0)
    cp = _dot(ltri_ref[...], pack3(_split3(logf)))
    vn_all = _head_rmsnorm(gv, gmat, sg_ref[...]).astype(BF16)
    q_tiles = [_head_rmsnorm(proj_ref[:, o_q + j * MXU_DIM:o_q + (j + 1) * MXU_DIM], gmat, qg_ref[...]) * scale
               for j in range(fox_w // MXU_DIM)]
    project(o_v, n_attn)

    cum = cp + pltpu.roll(cp, LANES - PART_STRIDE, 1) + pltpu.roll(cp, LANES - 2 * PART_STRIDE, 1)
    cum = jnp.where(head_lane, cum, 0.0) + carry_ref[0:1, :]
    carry_ref[0:1, :] = cum[tm - 1:tm, :]
    f_ref[0] = cum.T[0:n_heads, :]
    extras = _dot(pack3(_split3(cum)), sel_ref[...])
    k_tiles = [_head_rmsnorm(proj_ref[:, o_k + j * MXU_DIM:o_k + (j + 1) * MXU_DIM], gmat, kg_ref[...])
               for j in range(fox_w // MXU_DIM)]

    lane_c = lax.broadcasted_iota(jnp.int32, (chunk, LANES), 1)
    rr = lax.broadcasted_iota(jnp.int32, (chunk, chunk), 0)
    cs = lax.broadcasted_iota(jnp.int32, (chunk, chunk), 1)
    wt = [jnp.where(rr >= cs, sw_ref[g], 0.0).astype(BF16) for g in range(sw_ref.shape[0])]
    for p in range(sgu_w // LANES):
        vn = vn_all[:, p * LANES:(p + 1) * LANES]
        for ci in range(tm // chunk):
            vblk = vn[ci * chunk:(ci + 1) * chunk, :]
            mixed = jnp.where(lane_c < HEAD_DIM, _dot(wt[2 * p], vblk), _dot(wt[2 * p + 1], vblk))
            mixed = mixed + sb_ref[:, p * LANES:(p + 1) * LANES]
            og_ref[0, ci * chunk:(ci + 1) * chunk, p * LANES:(p + 1) * LANES] = (
                gu[ci * chunk:(ci + 1) * chunk, p * LANES:(p + 1) * LANES] * mixed).astype(BF16)

    project(0, o_q)
    qa_ref[0] = (proj_ref[:, 0:sb_w] * (scale * LOG2_E)).astype(BF16)
    ka_ref[0] = proj_ref[:, sb_w:2 * sb_w].astype(BF16)
    vat_ref[0] = proj_ref[:, 2 * sb_w:3 * sb_w].T.astype(BF16)

    is_head = lane < HEAD_DIM
    q_extra = qx_ref[...]
    k_extra = jnp.where((lane >= HEAD_DIM + 3) & (lane < HEAD_DIM + 7), 1.0, 0.0)
    heads_per_mxu = MXU_DIM // HEAD_DIM

    def head_tile(tn, s):
        half = tn[:, (s // 2) * LANES:(s // 2 + 1) * LANES]
        return half if s % 2 == 0 else pltpu.roll(half, HEAD_DIM, 1)

    for j in range(fox_w // MXU_DIM):
        for s in range(heads_per_mxu):
            hh = heads_per_mxu * j + s
            xq = extras[:, (n_heads + hh) * LANES:(n_heads + hh + 1) * LANES] + q_extra
            qf_ref[0, :, hh * LANES:(hh + 1) * LANES] = jnp.where(
                is_head, head_tile(q_tiles[j], s), xq).astype(BF16)
            xk = extras[:, hh * LANES:(hh + 1) * LANES] + k_extra
            kf_ref[0, :, hh * LANES:(hh + 1) * LANES] = jnp.where(
                is_head, head_tile(k_tiles[j], s), xk).astype(BF16)
    vt = proj_ref[:, o_v:n_attn].T.astype(BF16)
    ones = jnp.ones((V_ROWS - HEAD_DIM, tm), BF16)
    for hh in range(fox_w // HEAD_DIM):
        vft_ref[0, hh * V_ROWS:hh * V_ROWS + HEAD_DIM, :] = vt[hh * HEAD_DIM:(hh + 1) * HEAD_DIM, :]
        vft_ref[0, hh * V_ROWS + HEAD_DIM:(hh + 1) * V_ROWS, :] = ones


def _in_proj(x, mod, g1, w, wsgu, wf, bf, qx, qg, kg, sg, sw, sb, *, layer, tm, sb_w, fox_w, sgu_w):
    bsz, s, d = x.shape
    chunk = sw.shape[-1]
    wn = w.shape[2] + wsgu.shape[2]
    fox_heads = fox_w // HEAD_DIM

    def of_layer(a):
        nd = a.ndim - 1
        return pl.BlockSpec((None,) + a.shape[1:], lambda b, i: (layer,) + (0,) * nd)

    assert fox_heads <= PART_STRIDE and fox_w % MXU_DIM == 0 and sgu_w == MXU_DIM
    assert s % tm == 0 and tm % chunk == 0
    tok_idx = jnp.arange(tm)
    ltri = (tok_idx[:, None] >= tok_idx[None, :]).astype(BF16)
    const2 = lambda b, i: (0, 0)
    tok = lambda b, i: (b, i, 0)
    tok_t = lambda b, i: (b, 0, i)
    kern = functools.partial(_in_kernel, sb_w=sb_w, fox_w=fox_w, sgu_w=sgu_w, chunk=chunk)
    out_shape = [
        jax.ShapeDtypeStruct((bsz, s, sb_w), BF16),
        jax.ShapeDtypeStruct((bsz, s, sb_w), BF16),
        jax.ShapeDtypeStruct((bsz, sb_w, s), BF16),
        jax.ShapeDtypeStruct((bsz, s, fox_heads * LANES), BF16),
        jax.ShapeDtypeStruct((bsz, s, fox_heads * LANES), BF16),
        jax.ShapeDtypeStruct((bsz, fox_heads * V_ROWS, s), BF16),
        jax.ShapeDtypeStruct((bsz, s, sgu_w), BF16),
        jax.ShapeDtypeStruct((bsz, fox_heads, s), F32),
    ]
    out_specs = [
        pl.BlockSpec((1, tm, sb_w), tok),
        pl.BlockSpec((1, tm, sb_w), tok),
        pl.BlockSpec((1, sb_w, tm), tok_t),
        pl.BlockSpec((1, tm, fox_heads * LANES), tok),
        pl.BlockSpec((1, tm, fox_heads * LANES), tok),
        pl.BlockSpec((1, fox_heads * V_ROWS, tm), tok_t),
        pl.BlockSpec((1, tm, sgu_w), tok),
        pl.BlockSpec((1, fox_heads, tm), tok_t),
    ]
    return pl.pallas_call(
        kern,
        grid=(bsz, s // tm),
        in_specs=[
            pl.BlockSpec((1, tm, d), tok),
            pl.BlockSpec((None, 1, 6, d), lambda b, i: (layer, b, 0, 0)),
            of_layer(g1), of_layer(w), of_layer(wsgu), of_layer(wf), of_layer(bf), of_layer(qx), of_layer(qg),
            of_layer(kg), of_layer(sg), of_layer(sw), of_layer(sb),
            pl.BlockSpec((tm, tm), const2),
            pl.BlockSpec((LANES, 2 * fox_heads * LANES), const2),
            pl.BlockSpec((MXU_DIM, MXU_DIM), const2),
        ],
        out_specs=out_specs,
        out_shape=out_shape,
        scratch_shapes=[pltpu.VMEM((8, LANES), F32), pltpu.VMEM((tm, wn), F32)],
        compiler_params=pltpu.CompilerParams(
            dimension_semantics=("arbitrary", "arbitrary"), vmem_limit_bytes=VMEM_LIMIT),
        name="in_proj",
    )(x, mod, g1, w, wsgu, wf, bf, qx, qg, kg, sg, sw, sb, ltri, _placement_matrix(fox_heads),
      _group_mean_matrix())


def _sb_kernel(q_ref, k_ref, vt_ref, after_ref, o_ref, acc_ref, c_ref, *, tq):
    tk = tq
    n_blk = q_ref.shape[1] // tq
    lane = lax.broadcasted_iota(jnp.int32, (1, LANES), 1)
    key = lax.broadcasted_iota(jnp.int32, (tk, tq), 0)
    qry = lax.broadcasted_iota(jnp.int32, (tk, tq), 1)
    causal = key < qry
    def process(chains, size=tk):
        after = after_ref[0:size, 0:size]
        units, starts, qh = [], {}, {}
        for slot, qi, blocks, first in chains:
            q = q_ref[0, pl.ds(_block_start(qi, tq), tq), :]
            qh[slot] = (jnp.where(lane < HEAD_DIM, q, 0).astype(BF16),
                        jnp.where(lane >= HEAD_DIM, q, 0).astype(BF16))
            for b, j in enumerate(blocks):
                starts[slot, b] = _block_start(j, size)
                units += [(slot, b, h, first and b == 0) for h in range(2)]
        z = {(sl, b, h): _dot_nt(k_ref[0, pl.ds(starts[sl, b], size), :], qh[sl][h])
             for sl, b, h, _ in units}
        l1mb, head = {}, {}
        for sl, b, h, diag in units:
            zz = z[sl, b, h]
            nz = -zz
            lg = jnp.minimum(nz, 0.0) - jnp.log2(1.0 + jnp.exp2(jnp.minimum(zz, nz)))
            if diag:
                lg = jnp.where(causal, lg, 0.0)
            l1mb[sl, b, h] = lg.astype(BF16)
            head[sl, b, h] = (zz + lg, lg[0:1, :])
        between = {(sl, b, h): _dot(after, l1mb[sl, b, h]) for sl, b, h, _ in units}
        a = {}
        for slot, qi, blocks, first in chains:
            for h in range(2):
                c = None if first else c_ref[slot, h]
                for b in range(len(blocks)):
                    e = head[slot, b, h][0] + between[slot, b, h]
                    if c is not None:
                        e = e + c
                    w = jnp.exp2(e)
                    if first and b == 0:
                        w = jnp.where(causal, w, 0.0)
                    a[slot, b, h] = w.astype(BF16)
                    block_sum = between[slot, b, h][0:1, :] + head[slot, b, h][1]
                    c = block_sum if c is None else c + block_sum
                c_ref[slot, h] = c
        for slot, qi, blocks, first in chains:
            for h in range(2):
                pv = None
                for b in range(len(blocks)):
                    part = _dot(vt_ref[0, h * HEAD_DIM:(h + 1) * HEAD_DIM, pl.ds(starts[slot, b], size)],
                                a[slot, b, h])
                    pv = part if pv is None else pv + part
                if first:
                    acc_ref[slot, h] = pv
                else:
                    acc_ref[slot, h] += pv

    def carry_max(slot):
        cm = jnp.maximum(c_ref[slot, 0], c_ref[slot, 1])
        return jnp.max(cm, axis=1, keepdims=True)[0, 0]

    half = tk // 2
    live = EXP_ZERO * LOG2_E

    def first_half_block(qi):
        return jnp.asarray(2 * (qi - 1) - 1, jnp.int32)

    def finish(slot, qi, j_start=None, cmax=None):
        def cond(carry):
            j, cm = carry
            return jnp.logical_and(j >= 0, cm > live)

        def body(carry):
            j, _ = carry
            process([(slot, qi, [j], False)], size=half)
            return j - 1, carry_max(slot)

        lax.while_loop(cond, body, (first_half_block(qi) if j_start is None else j_start,
                                    carry_max(slot) if cmax is None else cmax))
        o_ref[0, pl.ds(_block_start(qi, tq), tq), :] = (
            jnp.concatenate([acc_ref[slot, 0], acc_ref[slot, 1]], axis=0).T.astype(o_ref.dtype))

    process([(0, 0, [0], True), (1, 1, [1, 0], True)])
    finish(0, 0)
    finish(1, 1)

    def two_blocks(g, carry):
        qa, qb = 2 * g, 2 * g + 1
        process([(0, qa, [qa, qa - 1], True), (1, qb, [qb, qb - 1], True)])

        def both_live(c):
            j, ca, cb = c
            return jnp.logical_and(j >= 0, jnp.logical_and(ca > live, cb > live))

        def both_step(c):
            j = c[0]
            process([(0, qa, [j], False), (1, qb, [j + 2], False)], size=half)
            return j - 1, carry_max(0), carry_max(1)

        j, ca, cb = lax.while_loop(both_live, both_step, (first_half_block(qa), carry_max(0), carry_max(1)))
        finish(0, qa, j, ca)
        finish(1, qb, j + 2, cb)
        return carry

    lax.fori_loop(1, n_blk // 2, two_blocks, 0)
    if n_blk % 2:
        last = n_blk - 1
        process([(0, last, [last, last - 1], True)])
        finish(0, last)


def _sb_attention(q, k, vt, *, tq):
    bsz, s, w = q.shape
    assert s >= 2 * tq
    idx = jnp.arange(tq)
    after = (idx[None, :] > idx[:, None]).astype(BF16)
    kern = functools.partial(_sb_kernel, tq=tq)
    return pl.pallas_call(
        kern,
        grid=(bsz, w // LANES),
        in_specs=[
            pl.BlockSpec((1, s, LANES), lambda b, p: (b, 0, p)),
            pl.BlockSpec((1, s, LANES), lambda b, p: (b, 0, p)),
            pl.BlockSpec((1, LANES, s), lambda b, p: (b, p, 0)),
            pl.BlockSpec((tq, tq), lambda b, p: (0, 0)),
        ],
        out_specs=pl.BlockSpec((1, s, LANES), lambda b, p: (b, 0, p)),
        out_shape=jax.ShapeDtypeStruct((bsz, s, w), BF16),
        scratch_shapes=[pltpu.VMEM((2, 2, HEAD_DIM, tq), F32), pltpu.VMEM((2, 2, 1, tq), F32)],
        compiler_params=pltpu.CompilerParams(
            dimension_semantics=("arbitrary", "arbitrary"), vmem_limit_bytes=VMEM_LIMIT),
        name="sb_attn",
    )(q, k, vt, after)


def _fox_kernel(fend_ref, par_ref, order_ref, q0_ref, q1_ref, k0_ref, k1_ref, v0_ref, v1_ref, o_ref,
                acc_ref, m_ref, p_ref, *, tq, n_heads):
    tk = tq
    qs, ks, vs = (q0_ref, q1_ref), (k0_ref, k1_ref), (v0_ref, v1_ref)
    n_blk = q0_ref.shape[1] // tq
    pair = pl.program_id(1)
    heads = tuple(pl.program_id(0) * n_heads + order_ref[2 * pair + h] for h in range(2))

    def last_dead_block(head, qi, j_prev):
        base = head * n_blk
        f_q = fend_ref[base + jnp.maximum(qi - 1, 0)]

        def next_is_dead(j):
            jn = jnp.minimum(j + 1, n_blk - 1)
            return jnp.logical_and(j + 1 < qi, f_q - fend_ref[base + jn] < -par_ref[0])

        return lax.while_loop(next_is_dead, lambda j: j + 1, j_prev)

    def plan(qi, dead):
        dead = (last_dead_block(heads[0], qi, dead[0]), last_dead_block(heads[1], qi, dead[1]))
        j_dead = jnp.minimum(dead[0], dead[1])
        n_left = jnp.maximum(qi - 2 - j_dead, 0)
        odd = n_left % 2
        has_dead_below = j_dead >= 0
        n_pairs = n_left // 2 + jnp.where(has_dead_below, odd, 0)
        return n_pairs, jnp.logical_and(odd == 1, jnp.logical_not(has_dead_below)), dead

    def q_tile(qi, h):
        return qs[h][0, pl.ds(pl.multiple_of(qi * tq, tq), tq), :]

    def finalize(qi):
        outs = []
        for h in range(2):
            acc = acc_ref[h]
            outs.append(acc[0:HEAD_DIM, :] / acc[HEAD_DIM:HEAD_DIM + 1, :])
        o_ref[0, pl.ds(pl.multiple_of(qi * tq, tq), tq), :] = (
            jnp.concatenate(outs, axis=0).T.astype(o_ref.dtype))

    key = lax.broadcasted_iota(jnp.int32, (tk, tq), 0)
    qry = lax.broadcasted_iota(jnp.int32, (tk, tq), 1)
    causal = key <= qry

    def process(qi, blocks, first):
        starts = [pl.multiple_of(j * tk, tk) for j in blocks]
        units = [(b, h) for b in range(len(blocks)) for h in range(2)]
        s = {(b, h): _dot_nt(ks[h][0, pl.ds(starts[b], tk), :],
                             q_tile(qi, h)) for b, h in units}
        p = {}
        alpha = {}
        for h in range(2):
            if first:
                s[0, h] = jnp.where(causal, s[0, h], NEG_BIG)
            m_new = None if first else m_ref[h]
            for b in range(len(blocks)):
                mb = jnp.max(s[b, h], axis=0, keepdims=True)
                m_new = mb if m_new is None else jnp.maximum(m_new, mb)
            if not first:
                alpha[h] = jnp.exp(m_ref[h] - m_new)
            for b in range(len(blocks)):
                p[b, h] = jnp.exp(s[b, h] - m_new).astype(BF16)
            m_ref[h] = m_new
        for h in range(2):
            pv = None
            for b in range(len(blocks)):
                part = _dot(vs[h][0, :, pl.ds(starts[b], tk)], p[b, h])
                pv = part if pv is None else pv + part
            acc_ref[h] = pv if first else alpha[h] * acc_ref[h] + pv

    def online_path():
        def q_block(qi, dead):
            n_pairs, lone_block0, dead = plan(qi, dead)

            @pl.when(qi == 0)
            def _():
                process(qi, [qi], True)

            @pl.when(qi > 0)
            def _():
                process(qi, [qi, qi - 1], True)

            def body(i, c):
                j = qi - 2 - 2 * i
                process(qi, [j, j - 1], False)
                return c

            lax.fori_loop(0, n_pairs, body, 0)

            @pl.when(lone_block0)
            def _():
                process(qi, [0], False)

            finalize(qi)
            return dead

        lax.fori_loop(0, n_blk, q_block, (jnp.int32(-1), jnp.int32(-1)))

    win = 2 * tk
    wrow = lax.broadcasted_iota(jnp.int32, (win, tq), 0)
    wcol = lax.broadcasted_iota(jnp.int32, (win, tq), 1)

    def window(qi, i):
        jl = qi - 1 - 2 * i
        j_lo = jnp.maximum(jl, 0)
        return j_lo, pl.multiple_of(j_lo * tk, tk), jnp.where(jl < 0, tk, win)

    def stage_a(qi, i, slot, first):
        j_lo, start, row_lim = window(qi, i)
        keep = wrow < row_lim
        if first:
            keep = jnp.logical_and(keep, wrow <= wcol + (qi - j_lo) * tk)
        for h in range(2):
            s = _dot_nt(ks[h][0, pl.ds(start, win), :], q_tile(qi, h))
            p_ref[slot, h] = jnp.exp(jnp.where(keep, s, NEG_BIG)).astype(BF16)

    def stage_c(qi, i, slot):
        _, start, _ = window(qi, i)
        for h in range(2):
            acc_ref[h] += _dot(vs[h][0, :, pl.ds(start, win)], p_ref[slot, h])

    def bounded_path():
        acc_ref[...] = jnp.zeros_like(acc_ref)
        stage_a(0, 0, 0, True)

        def q_block(qi, carry):
            t, dead = carry[0], carry[1:]
            n_pairs, lone_block0, dead = plan(qi, dead)
            n_stages = 1 + n_pairs + jnp.where(lone_block0, 1, 0)

            def body(i, t):
                stage_c(qi, i - 1, t & 1)
                stage_a(qi, i, (t + 1) & 1, False)
                return t + 1

            t = lax.fori_loop(1, n_stages, body, t)
            stage_c(qi, n_stages - 1, t & 1)
            stage_a(jnp.minimum(qi + 1, n_blk - 1), 0, (t + 1) & 1, True)
            finalize(qi)
            acc_ref[...] = jnp.zeros_like(acc_ref)
            return (t + 1,) + dead

        lax.fori_loop(0, n_blk, q_block, (jnp.int32(0), jnp.int32(-1), jnp.int32(-1)))

    bounded = par_ref[1] > 0.5
    pl.when(bounded)(bounded_path)
    pl.when(jnp.logical_not(bounded))(online_path)


def _fox_attention(q, k, vt, f, qk_bound, order, *, tq):
    bsz, s, w = q.shape
    n_heads = w // LANES
    assert s >= 2 * tq
    f_end = f[:, :, tq - 1::tq].reshape(-1)
    bounded = qk_bound <= MAX_UNSTABILISED_LOGIT
    stabiliser_slack = 0.5
    par = jnp.stack([jnp.where(bounded, stabiliser_slack - EXP_ZERO, 2.0 * qk_bound - EXP_ZERO),
                     jnp.where(bounded, 1.0, 0.0)]).astype(F32)
    kern = functools.partial(_fox_kernel, tq=tq, n_heads=n_heads)
    tok_tile = lambda h: pl.BlockSpec((1, s, LANES), lambda b, p, fe, th, od: (b, 0, od[2 * p + h]))
    val_tile = lambda h: pl.BlockSpec((1, V_ROWS, s), lambda b, p, fe, th, od: (b, od[2 * p + h], 0))
    grid_spec = pltpu.PrefetchScalarGridSpec(
        num_scalar_prefetch=3,
        grid=(bsz, n_heads // 2),
        in_specs=[tok_tile(0), tok_tile(1), tok_tile(0), tok_tile(1), val_tile(0), val_tile(1)],
        out_specs=pl.BlockSpec((1, s, LANES), lambda b, p, fe, th, od: (b, 0, p)),
        scratch_shapes=[pltpu.VMEM((2, V_ROWS, tq), F32), pltpu.VMEM((2, 1, tq), F32),
                        pltpu.VMEM((2, 2, 2 * tq, tq), BF16)],
    )
    return pl.pallas_call(
        kern,
        grid_spec=grid_spec,
        out_shape=jax.ShapeDtypeStruct((bsz, s, w // 2), BF16),
        compiler_params=pltpu.CompilerParams(
            dimension_semantics=("arbitrary", "arbitrary"), vmem_limit_bytes=VMEM_LIMIT),
        name="fox_attn",
    )(f_end, par, order.astype(jnp.int32), q, q, k, k, vt, vt)


def _out_kernel(x_ref, osb_ref, ofox_ref, osgu_ref, mod_ref, g2_ref, wo_ref, w1_ref, w2_ref,
                o_ref, *, ff_chunk):
    sb_w = osb_ref.shape[2]
    fox_w = ofox_ref.shape[2]
    x = x_ref[0]
    mix = (_dot(osb_ref[0], wo_ref[0:sb_w, :])
           + _dot(ofox_ref[0], wo_ref[sb_w:sb_w + fox_w, :])
           + _dot(osgu_ref[0], wo_ref[sb_w + fox_w:, :]))
    x1 = x + mod_ref[0, 2:3, :] * mix
    ms = jnp.mean(x1 * x1, axis=-1, keepdims=True)
    h = x1 * lax.rsqrt(ms + EPS) * g2_ref[...]
    hb = (h * (1.0 + mod_ref[0, 4:5, :]) + mod_ref[0, 3:4, :]).astype(BF16)
    d_ff = w1_ref.shape[1]
    acc = None
    for c in range(d_ff // ff_chunk):
        hid = jnp.maximum(_dot(hb, w1_ref[:, c * ff_chunk:(c + 1) * ff_chunk]), 0.0)
        part = _dot((hid * hid).astype(BF16), w2_ref[c * ff_chunk:(c + 1) * ff_chunk, :])
        acc = part if acc is None else acc + part
    o_ref[0] = x1 + mod_ref[0, 5:6, :] * acc


def _out_mlp(x, osb, ofox, osgu, mod, g2, wo, w1, w2, *, layer, tm):
    bsz, s, d = x.shape
    tok = lambda b, i: (b, i, 0)

    def of_layer(a, **kw):
        nd = a.ndim - 1
        return pl.BlockSpec((None,) + a.shape[1:], lambda b, i: (layer,) + (0,) * nd, **kw)

    single = pl.Buffered(1)
    kern = functools.partial(_out_kernel, ff_chunk=FF_CHUNK)
    return pl.pallas_call(
        kern,
        grid=(bsz, s // tm),
        in_specs=[
            pl.BlockSpec((1, tm, d), tok),
            pl.BlockSpec((1, tm, osb.shape[2]), tok),
            pl.BlockSpec((1, tm, ofox.shape[2]), tok),
            pl.BlockSpec((1, tm, osgu.shape[2]), tok),
            pl.BlockSpec((None, 1, 6, d), lambda b, i: (layer, b, 0, 0)),
            of_layer(g2),
            of_layer(wo, pipeline_mode=single),
            of_layer(w1, pipeline_mode=single),
            of_layer(w2, pipeline_mode=single),
        ],
        out_specs=pl.BlockSpec((1, tm, d), tok),
        out_shape=jax.ShapeDtypeStruct((bsz, s, d), F32),
        compiler_params=pltpu.CompilerParams(
            dimension_semantics=("arbitrary", "arbitrary"), vmem_limit_bytes=VMEM_LIMIT),
        name="out_mlp",
    )(x, osb, ofox, osgu, mod, g2, wo, w1, w2)


def kernel(x, c, ada_w, ada_b, norm1_g, norm2_g, w_in, b_forget, q_norm_g, k_norm_g, sgu_norm_g,
           sgu_w, sgu_b, w_out, mlp_w1, mlp_w2):
    depth, d, _ = ada_w.shape
    bsz, s, _ = x.shape
    fox_heads = b_forget.shape[1]
    fox_w = fox_heads * HEAD_DIM
    sgu_groups, chunk = sgu_b.shape[1], sgu_b.shape[2]
    sgu_wd = sgu_groups * sgu_norm_g.shape[2]
    sb_w = (w_in.shape[2] - 3 * fox_w - fox_heads - 2 * sgu_wd) // 3
    f_lo = 3 * sb_w + 3 * fox_w
    assert s % TOKEN_TILE == 0 and s % (2 * ATTN_BLOCK) == 0 and mlp_w1.shape[2] % FF_CHUNK == 0
    assert sb_w % LANES == 0 and ada_w.shape[2] % MOD_COLS == 0

    mod = _modulation(c, ada_w, ada_b).reshape(depth, bsz, 6, d)

    perm = jnp.argsort(b_forget, axis=1)
    assert f_lo % LANES == 0 and f_lo + LANES <= w_in.shape[2]
    w = w_in[:, :, :f_lo].astype(BF16)
    wsgu = w_in[:, :, f_lo + fox_heads:].astype(BF16)
    wf = w_in[:, :, f_lo:f_lo + LANES].astype(BF16)
    bf = jnp.pad(b_forget, ((0, 0), (0, LANES - fox_heads))).reshape(depth, 1, LANES)
    qg = jnp.tile(q_norm_g, (1, MXU_DIM // HEAD_DIM)).reshape(depth, 1, MXU_DIM)
    kg = jnp.tile(k_norm_g, (1, MXU_DIM // HEAD_DIM)).reshape(depth, 1, MXU_DIM)
    sg = sgu_norm_g.reshape(depth, 1, sgu_wd)
    sb = jnp.repeat(jnp.swapaxes(sgu_b, 1, 2), sgu_norm_g.shape[2], axis=2)
    qk_bound = ((BF16_MARGIN * HEAD_DIM ** 0.5) * jnp.max(jnp.abs(q_norm_g), axis=1)
                * jnp.max(jnp.abs(k_norm_g), axis=1))
    qx = jnp.zeros((depth, 1, LANES), F32).at[:, 0, HEAD_DIM:HEAD_DIM + 3].set(-1.0)
    qx = qx.at[:, 0, HEAD_DIM + 6].set(-qk_bound)
    wo_fox = w_out[:, sb_w:sb_w + fox_w].reshape(depth, fox_heads, HEAD_DIM, d)
    wo_fox = jnp.take_along_axis(wo_fox, perm[:, :, None, None], axis=1).reshape(depth, fox_w, d)
    wo = jnp.concatenate([w_out[:, :sb_w], wo_fox, w_out[:, sb_w + fox_w:]], axis=1).astype(BF16)
    w1 = mlp_w1.astype(BF16)
    w2 = mlp_w2.astype(BF16)
    g1 = norm1_g.reshape(depth, 1, d)
    g2 = norm2_g.reshape(depth, 1, d)

    for l in range(depth):
        qa, ka, vat, qf, kf, vft, osgu, f = _in_proj(
            x, mod, g1, w, wsgu, wf, bf, qx, qg, kg, sg, sgu_w, sb,
            layer=l, tm=TOKEN_TILE, sb_w=sb_w, fox_w=fox_w, sgu_w=sgu_wd)
        osb = _sb_attention(qa, ka, vat, tq=ATTN_BLOCK)
        ofox = _fox_attention(qf, kf, vft, f, qk_bound[l], perm[l], tq=ATTN_BLOCK)
        x = _out_mlp(x, osb, ofox, osgu, mod, g2, wo, w1, w2, layer=l, tm=TOKEN_TILE)
    return x
```
